```python
import jax, jax.numpy as jnp
from jax import lax
import numpy as np

D_MODEL = 2048
BATCH = 8
SEQ = 8192
DEPTH = 4

HEAD_DIM = 128
N_MIX_HEADS = D_MODEL // HEAD_DIM
N_ATTN_HEADS = 3 * N_MIX_HEADS // 4
N_SGU_GROUPS = N_MIX_HEADS - N_ATTN_HEADS
SGU_GROUP = HEAD_DIM
ATTN_WIDTH = N_ATTN_HEADS * HEAD_DIM
SGU_WIDTH = N_SGU_GROUPS * SGU_GROUP
MIX_IN = 3 * ATTN_WIDTH + 2 * SGU_WIDTH
CHUNK = 128
DILATED_BRANCHES = ((128, 1), (512, 4), (2048, 16))
BRANCH_BLOCK = 128
ROPE_THETA = 500000.0
ROPE_DIM = HEAD_DIM // 4
CONV_WIDTH = 3
D_FF = -(-8 * D_MODEL // (3 * 256)) * 256
N_EVEN = (DEPTH + 1) // 2
N_ODD = DEPTH // 2
EPS = 1e-6

kernel_name = "hybrid_dilated_attn_sgu_shortconv_adaln"


def rmsnorm(x, g):
    xf = x.astype(jnp.float32)
    y = xf * lax.rsqrt(jnp.mean(xf * xf, axis=-1, keepdims=True) + EPS)
    return (y * g.astype(jnp.float32)).astype(x.dtype)


def rope_tables(positions):
    inv_freq = ROPE_THETA ** (-jnp.arange(0, ROPE_DIM, 2, dtype=jnp.float32) / ROPE_DIM)
    ang = positions.astype(jnp.float32)[..., None] * inv_freq
    return jnp.cos(ang)[:, :, None, :], jnp.sin(ang)[:, :, None, :]


def apply_partial_rope(t, cos, sin):
    half = ROPE_DIM // 2
    t1 = t[..., :half].astype(jnp.float32)
    t2 = t[..., half:ROPE_DIM].astype(jnp.float32)
    rot = jnp.concatenate([t1 * cos - t2 * sin, t2 * cos + t1 * sin], axis=-1).astype(t.dtype)
    return jnp.concatenate([rot, t[..., ROPE_DIM:]], axis=-1)


def _dilated_branch(q, k, v, span, dilation):
    b, s, h, d = q.shape
    n = s // dilation
    nb = -(-n // BRANCH_BLOCK)
    pad = nb * BRANCH_BLOCK - n

    def to_sub(t):
        t = t.reshape(b, n, dilation, h, d).transpose(0, 2, 1, 3, 4)
        t = jnp.pad(t, ((0, 0), (0, 0), (0, pad), (0, 0), (0, 0)))
        return t.reshape(b, dilation, nb, BRANCH_BLOCK, h, d)

    def with_prev(t):
        prev = jnp.pad(t, ((0, 0), (0, 0), (1, 0), (0, 0), (0, 0), (0, 0)))[:, :, :nb]
        return jnp.concatenate([prev, t], axis=3)

    qs = to_sub(q)
    kb = with_prev(to_sub(k))
    vb = with_prev(to_sub(v))
    scores = jnp.einsum("brnqhd,brnkhd->brnqhk", qs, kb).astype(jnp.float32) * (d ** -0.5)

    qi = jnp.arange(BRANCH_BLOCK)[:, None]
    kj = jnp.arange(2 * BRANCH_BLOCK)[None, :]
    dist = qi + BRANCH_BLOCK - kj
    band = (dist >= 0) & (dist <= span)
    not_first = jnp.arange(nb)[:, None, None] > 0
    valid = band[None] & (not_first | (kj >= BRANCH_BLOCK)[None])
    scores = jnp.where(valid[None, None, :, :, None, :], scores, -jnp.inf)

    m = jnp.max(scores, axis=-1, keepdims=True)
    p = jnp.exp(scores - m)
    l = jnp.sum(p, axis=-1)
    o = jnp.einsum("brnqhk,brnkhd->brnqhd", p, vb.astype(jnp.float32)) / l[..., None]
    lse = m[..., 0] + jnp.log(l)

    def from_sub(t):
        t = t.reshape(b, dilation, nb * BRANCH_BLOCK, *t.shape[4:])[:, :, :n]
        t = jnp.moveaxis(t, 1, 2)
        return t.reshape(b, s, *t.shape[3:])

    return from_sub(o), from_sub(lse)


def dilated_attention(q, k, v):
    outs, lses = zip(*[_dilated_branch(q, k, v, w // dil, dil) for w, dil in DILATED_BRANCHES])
    alpha = jax.nn.softmax(jnp.stack(lses, axis=-1), axis=-1)
    o = sum(alpha[..., i, None] * outs[i] for i in range(len(outs)))
    return o.astype(q.dtype)


def spatial_gating(u, v, w_s, b_s):
    b, s, _ = u.shape
    shp = (b, s // CHUNK, CHUNK, N_SGU_GROUPS, SGU_GROUP)
    u = jax.nn.gelu(u).reshape(shp)
    v = jax.nn.gelu(v).reshape(shp)
    w = w_s * jnp.tril(jnp.ones((CHUNK, CHUNK), w_s.dtype))
    mixed = jnp.einsum("gts,bnsgc->bntgc", w, v) + b_s.T[:, :, None]
    return (u * mixed).reshape(b, s, SGU_WIDTH)


def attn_sgu_mixer(h, w_in, w_s, b_s, w_out, cos, sin):
    b, s, _ = h.shape
    z = h @ w_in
    q, k, v, u, vg = jnp.split(
        z, [ATTN_WIDTH, 2 * ATTN_WIDTH, 3 * ATTN_WIDTH, 3 * ATTN_WIDTH + SGU_WIDTH], axis=-1)
    hs = (b, s, N_ATTN_HEADS, HEAD_DIM)
    q = apply_partial_rope(q.reshape(hs), cos, sin)
    k = apply_partial_rope(k.reshape(hs), cos, sin)
    attn = dilated_attention(q, k, v.reshape(hs)).reshape(b, s, ATTN_WIDTH)
    sgu = spatial_gating(u, vg, w_s, b_s)
    return jnp.concatenate([attn, sgu], axis=-1) @ w_out


def short_conv_mixer(h, w_in, conv_w, w_out):
    gb, gc, hx = jnp.split(h @ w_in, 3, axis=-1)
    y = gc * hx
    s = y.shape[1]
    y_pad = jnp.pad(y, ((0, 0), (CONV_WIDTH - 1, 0), (0, 0)))
    conv = sum(conv_w[j] * y_pad[:, j:j + s] for j in range(CONV_WIDTH))
    return (gb * conv) @ w_out


def swiglu(h, w_gate, w_up, w_down):
    return (jax.nn.silu(h @ w_gate) * (h @ w_up)) @ w_down


def _fwd_setup_inputs(seed: int = 0) -> dict:
    key = jax.random.key(seed)
    ks = jax.random.split(key, 20)
    nrm = jax.random.normal
    f32 = jnp.float32
    d = D_MODEL
    x = nrm(ks[0], (BATCH, SEQ, d), f32)
    c = nrm(ks[1], (BATCH, d), f32)
    positions = (jnp.arange(SEQ, dtype=jnp.int32)[None, :]
                 + jax.random.randint(ks[2], (BATCH, 1), 0, 4096, dtype=jnp.int32))
    ada_w = nrm(ks[3], (DEPTH, d, 6 * d), f32) * (0.5 * d ** -0.5)
    ada_b = nrm(ks[4], (DEPTH, 6 * d), f32) * 0.02
    norm_mix = 1.0 + 0.02 * nrm(ks[5], (DEPTH, d), f32)
    norm_ffn = 1.0 + 0.02 * nrm(ks[6], (DEPTH, d), f32)
    ab_w_in = nrm(ks[7], (N_EVEN, d, MIX_IN), f32) * d ** -0.5
    sgu_w = nrm(ks[8], (N_EVEN, N_SGU_GROUPS, CHUNK, CHUNK), f32) * CHUNK ** -0.5
    sgu_b = 1.0 + 0.1 * nrm(ks[9], (N_EVEN, N_SGU_GROUPS, CHUNK), f32)
    ab_w_out = nrm(ks[10], (N_EVEN, ATTN_WIDTH + SGU_WIDTH, d), f32) * (ATTN_WIDTH + SGU_WIDTH) ** -0.5
    conv_w_in = nrm(ks[11], (N_ODD, d, 3 * d), f32) * d ** -0.5
    conv_w = nrm(ks[12], (N_ODD, CONV_WIDTH, d), f32) * CONV_WIDTH ** -0.5
    conv_w_out = nrm(ks[13], (N_ODD, d, d), f32) * d ** -0.5
    ffn_w_gate = nrm(ks[14], (DEPTH, d, D_FF), f32) * d ** -0.5
    ffn_w_up = nrm(ks[15], (DEPTH, d, D_FF), f32) * d ** -0.5
    ffn_w_down = nrm(ks[16], (DEPTH, D_FF, d), f32) * D_FF ** -0.5
    final_norm = 1.0 + 0.02 * nrm(ks[17], (d,), f32)
    return {"x": x, "c": c, "positions": positions, "ada_w": ada_w, "ada_b": ada_b,
            "norm_mix": norm_mix, "norm_ffn": norm_ffn, "ab_w_in": ab_w_in,
            "sgu_w": sgu_w, "sgu_b": sgu_b, "ab_w_out": ab_w_out,
            "conv_w_in": conv_w_in, "conv_w": conv_w, "conv_w_out": conv_w_out,
            "ffn_w_gate": ffn_w_gate, "ffn_w_up": ffn_w_up, "ffn_w_down": ffn_w_down,
            "final_norm": final_norm}


def _fwd_reference(x, c, positions, ada_w, ada_b, norm_mix, norm_ffn, ab_w_in, sgu_w, sgu_b,
              ab_w_out, conv_w_in, conv_w, conv_w_out, ffn_w_gate, ffn_w_up, ffn_w_down,
              final_norm):
    cos, sin = rope_tables(positions)
    c_act = jax.nn.silu(c)
    for layer in range(DEPTH):
        mod = (c_act @ ada_w[layer] + ada_b[layer])[:, None, :]
        sh_m, sc_m, g_m, sh_f, sc_f, g_f = jnp.split(mod, 6, axis=-1)
        h = rmsnorm(x, norm_mix[layer]) * (1 + sc_m) + sh_m
        i = layer // 2
        if layer % 2 == 0:
            mix = attn_sgu_mixer(h, ab_w_in[i], sgu_w[i], sgu_b[i], ab_w_out[i], cos, sin)
        else:
            mix = short_conv_mixer(h, conv_w_in[i], conv_w[i], conv_w_out[i])
        x = x + g_m * mix
        h = rmsnorm(x, norm_ffn[layer]) * (1 + sc_f) + sh_f
        x = x + g_f * swiglu(h, ffn_w_gate[layer], ffn_w_up[layer], ffn_w_down[layer])
    return rmsnorm(x, final_norm)


import jax as _jax
import jax.numpy as _jnp

TWIN_FORMAT = 'train_step'
FWD_PARAMS = ['x', 'c', 'positions', 'ada_w', 'ada_b', 'norm_mix', 'norm_ffn', 'ab_w_in', 'sgu_w', 'sgu_b', 'ab_w_out', 'conv_w_in', 'conv_w', 'conv_w_out', 'ffn_w_gate', 'ffn_w_up', 'ffn_w_down', 'final_norm']
TWIN_WEIGHTS = ['ada_w', 'ada_b', 'norm_mix', 'norm_ffn', 'ab_w_in', 'sgu_w', 'sgu_b', 'ab_w_out', 'conv_w_in', 'conv_w', 'conv_w_out', 'ffn_w_gate', 'ffn_w_up', 'ffn_w_down', 'final_norm']
TWIN_DIFF_INPUT = 'x'
TWIN_INPUTS = ['x', 'c', 'positions', 'ada_w', 'ada_b', 'norm_mix', 'norm_ffn', 'ab_w_in', 'sgu_w', 'sgu_b', 'ab_w_out', 'conv_w_in', 'conv_w', 'conv_w_out', 'ffn_w_gate', 'ffn_w_up', 'ffn_w_down', 'final_norm', 'loss_target', 'm_ada_w', 'm_ada_b', 'm_norm_mix', 'm_norm_ffn', 'm_ab_w_in', 'm_sgu_w', 'm_sgu_b', 'm_ab_w_out', 'm_conv_w_in', 'm_conv_w', 'm_conv_w_out', 'm_ffn_w_gate', 'm_ffn_w_up', 'm_ffn_w_down', 'm_final_norm', 'v_ada_w', 'v_ada_b', 'v_norm_mix', 'v_norm_ffn', 'v_ab_w_in', 'v_sgu_w', 'v_sgu_b', 'v_ab_w_out', 'v_conv_w_in', 'v_conv_w', 'v_conv_w_out', 'v_ffn_w_gate', 'v_ffn_w_up', 'v_ffn_w_down', 'v_final_norm']
TWIN_OUTPUTS = ['loss', 'grad_x', 'grad_ada_w', 'grad_ada_b', 'grad_norm_mix', 'grad_norm_ffn', 'grad_ab_w_in', 'grad_sgu_w', 'grad_sgu_b', 'grad_ab_w_out', 'grad_conv_w_in', 'grad_conv_w', 'grad_conv_w_out', 'grad_ffn_w_gate', 'grad_ffn_w_up', 'grad_ffn_w_down', 'grad_final_norm', 'delta_ada_w', 'delta_ada_b', 'delta_norm_mix', 'delta_norm_ffn', 'delta_ab_w_in', 'delta_sgu_w', 'delta_sgu_b', 'delta_ab_w_out', 'delta_conv_w_in', 'delta_conv_w', 'delta_conv_w_out', 'delta_ffn_w_gate', 'delta_ffn_w_up', 'delta_ffn_w_down', 'delta_final_norm', 'new_m_ada_w', 'new_m_ada_b', 'new_m_norm_mix', 'new_m_norm_ffn', 'new_m_ab_w_in', 'new_m_sgu_w', 'new_m_sgu_b', 'new_m_ab_w_out', 'new_m_conv_w_in', 'new_m_conv_w', 'new_m_conv_w_out', 'new_m_ffn_w_gate', 'new_m_ffn_w_up', 'new_m_ffn_w_down', 'new_m_final_norm', 'new_v_ada_w', 'new_v_ada_b', 'new_v_norm_mix', 'new_v_norm_ffn', 'new_v_ab_w_in', 'new_v_sgu_w', 'new_v_sgu_b', 'new_v_ab_w_out', 'new_v_conv_w_in', 'new_v_conv_w', 'new_v_conv_w_out', 'new_v_ffn_w_gate', 'new_v_ffn_w_up', 'new_v_ffn_w_down', 'new_v_final_norm']
TWIN_LEAF_KINDS = {'loss': 'loss', 'grad_x': 'grad_x', 'grad_ada_w': 'grad_w', 'grad_ada_b': 'grad_w', 'grad_norm_mix': 'grad_w', 'grad_norm_ffn': 'grad_w', 'grad_ab_w_in': 'grad_w', 'grad_sgu_w': 'grad_w', 'grad_sgu_b': 'grad_w', 'grad_ab_w_out': 'grad_w', 'grad_conv_w_in': 'grad_w', 'grad_conv_w': 'grad_w', 'grad_conv_w_out': 'grad_w', 'grad_ffn_w_gate': 'grad_w', 'grad_ffn_w_up': 'grad_w', 'grad_ffn_w_down': 'grad_w', 'grad_final_norm': 'grad_w', 'delta_ada_w': 'delta_w', 'delta_ada_b': 'delta_w', 'delta_norm_mix': 'delta_w', 'delta_norm_ffn': 'delta_w', 'delta_ab_w_in': 'delta_w', 'delta_sgu_w': 'delta_w', 'delta_sgu_b': 'delta_w', 'delta_ab_w_out': 'delta_w', 'delta_conv_w_in': 'delta_w', 'delta_conv_w': 'delta_w', 'delta_conv_w_out': 'delta_w', 'delta_ffn_w_gate': 'delta_w', 'delta_ffn_w_up': 'delta_w', 'delta_ffn_w_down': 'delta_w', 'delta_final_norm': 'delta_w', 'new_m_ada_w': 'new_m', 'new_m_ada_b': 'new_m', 'new_m_norm_mix': 'new_m', 'new_m_norm_ffn': 'new_m', 'new_m_ab_w_in': 'new_m', 'new_m_sgu_w': 'new_m', 'new_m_sgu_b': 'new_m', 'new_m_ab_w_out': 'new_m', 'new_m_conv_w_in': 'new_m', 'new_m_conv_w': 'new_m', 'new_m_conv_w_out': 'new_m', 'new_m_ffn_w_gate': 'new_m', 'new_m_ffn_w_up': 'new_m', 'new_m_ffn_w_down': 'new_m', 'new_m_final_norm': 'new_m', 'new_v_ada_w': 'new_v', 'new_v_ada_b': 'new_v', 'new_v_norm_mix': 'new_v', 'new_v_norm_ffn': 'new_v', 'new_v_ab_w_in': 'new_v', 'new_v_sgu_w': 'new_v', 'new_v_sgu_b': 'new_v', 'new_v_ab_w_out': 'new_v', 'new_v_conv_w_in': 'new_v', 'new_v_conv_w': 'new_v', 'new_v_conv_w_out': 'new_v', 'new_v_ffn_w_gate': 'new_v', 'new_v_ffn_w_up': 'new_v', 'new_v_ffn_w_down': 'new_v', 'new_v_final_norm': 'new_v'}


def _forward(args):
    return _fwd_reference(*[args[k] for k in FWD_PARAMS])


def _output_shape():
    def fwd():
        inp = _fwd_setup_inputs(0)
        return _fwd_reference(*[inp[k] for k in FWD_PARAMS])
    out = _jax.eval_shape(fwd)
    return out.shape, out.dtype

N_MICROBATCH = 1
ADAM_LR = 0.001
ADAM_B1 = 0.9
ADAM_B2 = 0.999
ADAM_EPS = 1e-08
ADAM_WD = 0.01
ADAM_STEP = 10
PER_EXAMPLE_BATCH_AXIS = {'x': 0, 'c': 0, 'positions': 0, 'loss_target': 0}
SHARED_INPUTS = []
_WEIGHT_DTYPES = {'ada_w': _jnp.float32, 'ada_b': _jnp.float32, 'norm_mix': _jnp.float32, 'norm_ffn': _jnp.float32, 'ab_w_in': _jnp.float32, 'sgu_w': _jnp.float32, 'sgu_b': _jnp.float32, 'ab_w_out': _jnp.float32, 'conv_w_in': _jnp.float32, 'conv_w': _jnp.float32, 'conv_w_out': _jnp.float32, 'ffn_w_gate': _jnp.float32, 'ffn_w_up': _jnp.float32, 'ffn_w_down': _jnp.float32, 'final_norm': _jnp.float32}
MOMENT_SCALE = {'ada_w': 4.522140e-02, 'ada_b': 7.878035e-02, 'norm_mix': 5.633946e-02, 'norm_ffn': 3.775151e-02, 'ab_w_in': 1.649746e-02, 'sgu_w': 1.567956e-02, 'sgu_b': 3.173182e-02, 'ab_w_out': 2.275753e-02, 'conv_w_in': 4.535173e-02, 'conv_w': 4.639095e-02, 'conv_w_out': 4.517466e-02, 'ffn_w_gate': 1.683309e-02, 'ffn_w_up': 1.630642e-02, 'ffn_w_down': 2.705515e-02, 'final_norm': 3.207721e+01}


def _to_microbatches(a, axis):
    t = _jnp.moveaxis(a, axis, 0)
    t = t.reshape((N_MICROBATCH, t.shape[0] // N_MICROBATCH) + t.shape[1:])
    return _jnp.moveaxis(t, 1, axis + 1)


def setup_inputs(seed: int = 0) -> dict:
    inp = _fwd_setup_inputs(seed)
    key = _jax.random.fold_in(_jax.random.key(seed), 7919)
    shape, _ = _output_shape()
    out = dict(inp)
    out["loss_target"] = _jax.random.normal(_jax.random.fold_in(key, 0), shape, _jnp.float32)
    for i, name in enumerate(TWIN_WEIGHTS):
        w = inp[name].astype(_jnp.float32)
        if MOMENT_SCALE is None:
            s = _jnp.sqrt(_jnp.mean(_jnp.square(w)) + 1e-30)
        else:
            s = MOMENT_SCALE[name]
        km, kv = _jax.random.split(_jax.random.fold_in(key, i + 1))
        out[name] = w
        out["m_" + name] = s * _jax.random.normal(km, w.shape, _jnp.float32)
        out["v_" + name] = (s * s) * _jax.random.uniform(kv, w.shape, _jnp.float32, 0.5, 1.5)
    if N_MICROBATCH > 1:
        for name, axis in PER_EXAMPLE_BATCH_AXIS.items():
            out[name] = _to_microbatches(out[name], axis)
    return {'x': out['x'], 'c': out['c'], 'positions': out['positions'], 'ada_w': out['ada_w'], 'ada_b': out['ada_b'], 'norm_mix': out['norm_mix'], 'norm_ffn': out['norm_ffn'], 'ab_w_in': out['ab_w_in'], 'sgu_w': out['sgu_w'], 'sgu_b': out['sgu_b'], 'ab_w_out': out['ab_w_out'], 'conv_w_in': out['conv_w_in'], 'conv_w': out['conv_w'], 'conv_w_out': out['conv_w_out'], 'ffn_w_gate': out['ffn_w_gate'], 'ffn_w_up': out['ffn_w_up'], 'ffn_w_down': out['ffn_w_down'], 'final_norm': out['final_norm'], 'loss_target': out['loss_target'], 'm_ada_w': out['m_ada_w'], 'm_ada_b': out['m_ada_b'], 'm_norm_mix': out['m_norm_mix'], 'm_norm_ffn': out['m_norm_ffn'], 'm_ab_w_in': out['m_ab_w_in'], 'm_sgu_w': out['m_sgu_w'], 'm_sgu_b': out['m_sgu_b'], 'm_ab_w_out': out['m_ab_w_out'], 'm_conv_w_in': out['m_conv_w_in'], 'm_conv_w': out['m_conv_w'], 'm_conv_w_out': out['m_conv_w_out'], 'm_ffn_w_gate': out['m_ffn_w_gate'], 'm_ffn_w_up': out['m_ffn_w_up'], 'm_ffn_w_down': out['m_ffn_w_down'], 'm_final_norm': out['m_final_norm'], 'v_ada_w': out['v_ada_w'], 'v_ada_b': out['v_ada_b'], 'v_norm_mix': out['v_norm_mix'], 'v_norm_ffn': out['v_norm_ffn'], 'v_ab_w_in': out['v_ab_w_in'], 'v_sgu_w': out['v_sgu_w'], 'v_sgu_b': out['v_sgu_b'], 'v_ab_w_out': out['v_ab_w_out'], 'v_conv_w_in': out['v_conv_w_in'], 'v_conv_w': out['v_conv_w'], 'v_conv_w_out': out['v_conv_w_out'], 'v_ffn_w_gate': out['v_ffn_w_gate'], 'v_ffn_w_up': out['v_ffn_w_up'], 'v_ffn_w_down': out['v_ffn_w_down'], 'v_final_norm': out['v_final_norm']}


def _loss(weights, diff, rest, loss_target):
    with _jax.named_scope("forward"):
        args = {**rest, TWIN_DIFF_INPUT: diff, **{k: w.astype(_WEIGHT_DTYPES[k]) for k, w in weights.items()}}
        y = _forward(args)
    with _jax.named_scope("loss_head"):
        err = _jnp.square(y.astype(_jnp.float32) - loss_target)
        return 0.5 * _jnp.sum(_jnp.mean(err, axis=-1)) if err.ndim else 0.5 * err


def _adamw(w, g, m, v):
    m = ADAM_B1 * m + (1.0 - ADAM_B1) * g
    v = ADAM_B2 * v + (1.0 - ADAM_B2) * _jnp.square(g)
    m_hat = m / (1.0 - ADAM_B1 ** ADAM_STEP)
    v_hat = v / (1.0 - ADAM_B2 ** ADAM_STEP)
    delta = -ADAM_LR * (m_hat / (_jnp.sqrt(v_hat) + ADAM_EPS) + ADAM_WD * w)
    return delta, m, v


def reference(x, c, positions, ada_w, ada_b, norm_mix, norm_ffn, ab_w_in, sgu_w, sgu_b, ab_w_out, conv_w_in, conv_w, conv_w_out, ffn_w_gate, ffn_w_up, ffn_w_down, final_norm, loss_target, m_ada_w, m_ada_b, m_norm_mix, m_norm_ffn, m_ab_w_in, m_sgu_w, m_sgu_b, m_ab_w_out, m_conv_w_in, m_conv_w, m_conv_w_out, m_ffn_w_gate, m_ffn_w_up, m_ffn_w_down, m_final_norm, v_ada_w, v_ada_b, v_norm_mix, v_norm_ffn, v_ab_w_in, v_sgu_w, v_sgu_b, v_ab_w_out, v_conv_w_in, v_conv_w, v_conv_w_out, v_ffn_w_gate, v_ffn_w_up, v_ffn_w_down, v_final_norm):
    given = dict(x=x, c=c, positions=positions, ada_w=ada_w, ada_b=ada_b, norm_mix=norm_mix, norm_ffn=norm_ffn, ab_w_in=ab_w_in, sgu_w=sgu_w, sgu_b=sgu_b, ab_w_out=ab_w_out, conv_w_in=conv_w_in, conv_w=conv_w, conv_w_out=conv_w_out, ffn_w_gate=ffn_w_gate, ffn_w_up=ffn_w_up, ffn_w_down=ffn_w_down, final_norm=final_norm, loss_target=loss_target, m_ada_w=m_ada_w, m_ada_b=m_ada_b, m_norm_mix=m_norm_mix, m_norm_ffn=m_norm_ffn, m_ab_w_in=m_ab_w_in, m_sgu_w=m_sgu_w, m_sgu_b=m_sgu_b, m_ab_w_out=m_ab_w_out, m_conv_w_in=m_conv_w_in, m_conv_w=m_conv_w, m_conv_w_out=m_conv_w_out, m_ffn_w_gate=m_ffn_w_gate, m_ffn_w_up=m_ffn_w_up, m_ffn_w_down=m_ffn_w_down, m_final_norm=m_final_norm, v_ada_w=v_ada_w, v_ada_b=v_ada_b, v_norm_mix=v_norm_mix, v_norm_ffn=v_norm_ffn, v_ab_w_in=v_ab_w_in, v_sgu_w=v_sgu_w, v_sgu_b=v_sgu_b, v_ab_w_out=v_ab_w_out, v_conv_w_in=v_conv_w_in, v_conv_w=v_conv_w, v_conv_w_out=v_conv_w_out, v_ffn_w_gate=v_ffn_w_gate, v_ffn_w_up=v_ffn_w_up, v_ffn_w_down=v_ffn_w_down, v_final_norm=v_final_norm)
    weights = {n: given[n] for n in TWIN_WEIGHTS}
    shared = {n: given[n] for n in SHARED_INPUTS}
    per_example = {n: given[n] for n in ['x', 'c', 'positions']}
    grad_fn = _jax.value_and_grad(_loss, argnums=(0, 1))

    def one_microbatch(ex, loss_target):
        ex = dict(ex)
        diff = ex.pop(TWIN_DIFF_INPUT)
        return grad_fn(weights, diff, {**shared, **ex}, loss_target)

    if N_MICROBATCH == 1:
        loss, (grad_w, grad_x) = one_microbatch(per_example, given["loss_target"])
    else:
        def body(carry, xs):
            loss_sum, grad_sum = carry
            l_k, (gw_k, gx_k) = one_microbatch(xs[0], xs[1])
            with _jax.named_scope("update"):
                return (loss_sum + l_k, _jax.tree.map(_jnp.add, grad_sum, gw_k)), gx_k

        init = (_jnp.zeros((), _jnp.float32), _jax.tree.map(_jnp.zeros_like, weights))
        (loss, grad_w), grad_x = _jax.lax.scan(body, init, (per_example, given["loss_target"]))
    with _jax.named_scope("update"):
        delta_w, new_m, new_v = {}, {}, {}
        for n in TWIN_WEIGHTS:
            delta_w[n], new_m[n], new_v[n] = _adamw(weights[n], grad_w[n], given["m_" + n], given["v_" + n])
    return (loss, grad_x, *[grad_w[n] for n in TWIN_WEIGHTS], *[delta_w[n] for n in TWIN_WEIGHTS],
            *[new_m[n] for n in TWIN_WEIGHTS], *[new_v[n] for n in TWIN_WEIGHTS])
```

```python
import functools
import math

import numpy as np
import jax
import jax.numpy as jnp
from jax import lax
from jax.experimental import pallas as pl
from jax.experimental.pallas import tpu as pltpu

F32, BF16 = jnp.float32, jnp.bfloat16
MESH = pl.DeviceIdType.MESH
N_DEV = 8
EPS = 1e-6
HEAD = 128
CHUNK = 128
DILATIONS = (1, 4, 16)
ATT_CHUNK = CHUNK * DILATIONS[-1]
ROPE_THETA = 500000.0
ROPE_DIM = HEAD // 4
NEG = -1e30
ADAM_LR, ADAM_B1, ADAM_B2, ADAM_EPS, ADAM_WD, ADAM_STEP = 0.001, 0.9, 0.999, 1e-08, 0.01, 10
VMEM_LIMIT = 56 * 1024 * 1024


def _cparams(*sem):
    return pltpu.CompilerParams(dimension_semantics=sem or None, vmem_limit_bytes=VMEM_LIMIT)


def _tile(n, pref, unit):
    t = (min(pref, n) // unit) * unit
    while t >= unit:
        if n % t == 0:
            return t
        t -= unit
    return n


def _dot(a, b, dims):
    return lax.dot_general(a, b, (dims, ((), ())), preferred_element_type=F32)


def _dot_nn(a, b):
    return _dot(a, b, ((1,), (0,)))


def _dot_nt(a, b):
    return _dot(a, b, ((1,), (1,)))


def _dot_tn(a, b):
    return _dot(a, b, ((0,), (0,)))


def _matmul(name, mode, a, b, outs, *, extras=(), epilogue=None, tm=1024, tn=512, tk=512):
    if mode == "nn":
        (m, kk), (_, n) = a.shape, b.shape
    elif mode == "nt":
        (m, kk), (n, _) = a.shape, b.shape
    else:
        (kk, m), (_, n) = a.shape, b.shape
    tm, tn, tk = _tile(m, tm, 128), _tile(n, tn, 128), _tile(kk, tk, 128)
    nk = kk // tk
    dotf = {"nn": _dot_nn, "nt": _dot_nt, "tn": _dot_tn}[mode]
    a_spec = pl.BlockSpec((tk, tm), lambda i, j, k: (k, i)) if mode == "tn" else pl.BlockSpec((tm, tk), lambda i, j, k: (i, k))
    b_spec = pl.BlockSpec((tn, tk), lambda i, j, k: (j, k)) if mode == "nt" else pl.BlockSpec((tk, tn), lambda i, j, k: (k, j))
    e_specs = [pl.BlockSpec((tm, tn), lambda i, j, k: (i, j)) if kind == "mn" else pl.BlockSpec((1, tn), lambda i, j, k: (0, j))
               for _, kind in extras]
    ne, no = len(extras), len(outs)
    epi = epilogue or (lambda acc: (acc,))

    def body(a_ref, b_ref, *rest):
        e_refs, o_refs = rest[:ne], rest[ne:ne + no]

        def finish(acc):
            for o_ref, o in zip(o_refs, epi(acc, *[r[...] for r in e_refs])):
                o_ref[...] = o.astype(o_ref.dtype)

        if nk == 1:
            finish(dotf(a_ref[...], b_ref[...]))
            return
        acc_ref = rest[-1]
        k = pl.program_id(2)

        @pl.when(k == 0)
        def _():
            acc_ref[...] = jnp.zeros_like(acc_ref)

        acc_ref[...] += dotf(a_ref[...], b_ref[...])

        @pl.when(k == nk - 1)
        def _():
            finish(acc_ref[...])

    res = pl.pallas_call(
        body, name=name, grid=(m // tm, n // tn, nk),
        in_specs=[a_spec, b_spec] + e_specs,
        out_specs=[pl.BlockSpec((tm, tn), lambda i, j, k: (i, j)) for _ in outs],
        out_shape=[jax.ShapeDtypeStruct((m, n), dt) for dt in outs],
        scratch_shapes=[pltpu.VMEM((tm, tn), F32)] if nk > 1 else [],
        compiler_params=_cparams("parallel", "parallel", "arbitrary"),
    )(a, b, *[e for e, _ in extras])
    return res[0] if no == 1 else res


def _norm_mod_fwd(x, w, sc, sh):
    s, d = x.shape
    tm = _tile(s, 512, 8)

    def body(x_ref, w_ref, sc_ref, sh_ref, h_ref):
        xv = x_ref[...]
        r = lax.rsqrt(jnp.mean(xv * xv, axis=-1, keepdims=True) + EPS)
        h_ref[...] = ((xv * r) * w_ref[...] * (1.0 + sc_ref[...]) + sh_ref[...]).astype(BF16)

    row = pl.BlockSpec((1, d), lambda i: (0, 0))
    return pl.pallas_call(
        body, name="norm_mod_fwd", grid=(s // tm,),
        in_specs=[pl.BlockSpec((tm, d), lambda i: (i, 0)), row, row, row],
        out_specs=pl.BlockSpec((tm, d), lambda i: (i, 0)),
        out_shape=jax.ShapeDtypeStruct((s, d), BF16),
        compiler_params=_cparams("parallel"),
    )(x, w, sc, sh)


def _colsum8(t):
    tm, d = t.shape
    return jnp.sum(t.reshape(tm // 8, 8, d), axis=0)


def _norm_mod_bwd(x, dh, w, sc, dres, branch=None, g=None):
    s, d = x.shape
    tm = _tile(s, 256, 8)
    nsteps = s // tm
    gated = branch is not None

    def body(*refs):
        if gated:
            x_ref, dh_ref, w_ref, sc_ref, dres_ref, br_ref, g_ref, dx_ref, dbr_ref, dsh_ref, dsc_ref, dw_ref, dg_ref, acc = refs
        else:
            x_ref, dh_ref, w_ref, sc_ref, dres_ref, dx_ref, dsh_ref, dsc_ref, dw_ref, acc = refs
        i = pl.program_id(0)

        @pl.when(i == 0)
        def _():
            acc[...] = jnp.zeros_like(acc)

        xv, dhv, wv, scv = x_ref[...], dh_ref[...].astype(F32), w_ref[...], sc_ref[...]
        r = lax.rsqrt(jnp.mean(xv * xv, axis=-1, keepdims=True) + EPS)
        xn = xv * r
        dxn = dhv * (wv * (1.0 + scv))
        dx = dres_ref[...] + r * (dxn - xn * jnp.mean(dxn * xn, axis=-1, keepdims=True))
        dx_ref[...] = dx
        acc[0] += _colsum8(dhv)
        acc[1] += _colsum8(dhv * xn)
        if gated:
            dbr_ref[...] = (dx * g_ref[...]).astype(BF16)
            acc[2] += _colsum8(dx * br_ref[...].astype(F32))

        @pl.when(i == nsteps - 1)
        def _():
            a0 = jnp.sum(acc[0], axis=0, keepdims=True)
            a1 = jnp.sum(acc[1], axis=0, keepdims=True)
            dsh_ref[...] = a0
            dsc_ref[...] = a1 * wv
            dw_ref[...] = a1 * (1.0 + scv)
            if gated:
                dg_ref[...] = jnp.sum(acc[2], axis=0, keepdims=True)

    big = pl.BlockSpec((tm, d), lambda i: (i, 0))
    row = pl.BlockSpec((1, d), lambda i: (0, 0))
    rowo = jax.ShapeDtypeStruct((1, d), F32)
    in_specs = [big, big, row, row, big] + ([big, row] if gated else [])
    out_specs = [big] + ([big] if gated else []) + [row, row, row] + ([row] if gated else [])
    out_shape = ([jax.ShapeDtypeStruct((s, d), F32)] + ([jax.ShapeDtypeStruct((s, d), BF16)] if gated else [])
                 + [rowo, rowo, rowo] + ([rowo] if gated else []))
    args = [x, dh, w, sc, dres] + ([branch, g] if gated else [])
    return pl.pallas_call(
        body, name="norm_mod_bwd_gated" if gated else "norm_mod_bwd", grid=(nsteps,),
        in_specs=in_specs, out_specs=out_specs, out_shape=out_shape,
        scratch_shapes=[pltpu.VMEM((3, 8, d), F32)],
        compiler_params=_cparams("arbitrary"),
    )(*args)


def _final_loss(x, target, w, branch, g):
    s, d = x.shape
    tm = _tile(s, 256, 8)
    nsteps = s // tm

    def body(x_ref, t_ref, w_ref, br_ref, g_ref, loss_ref, dx_ref, dbr_ref, dw_ref, dg_ref, acc):
        i = pl.program_id(0)

        @pl.when(i == 0)
        def _():
            acc[...] = jnp.zeros_like(acc)

        xv, wv = x_ref[...], w_ref[...]
        r = lax.rsqrt(jnp.mean(xv * xv, axis=-1, keepdims=True) + EPS)
        xn = xv * r
        err = xn * wv - t_ref[...]
        dy = err * (1.0 / d)
        dxn = dy * wv
        dx = r * (dxn - xn * jnp.mean(dxn * xn, axis=-1, keepdims=True))
        dx_ref[...] = dx
        dbr_ref[...] = (dx * g_ref[...]).astype(BF16)
        acc[0] += _colsum8(err * err)
        acc[1] += _colsum8(dy * xn)
        acc[2] += _colsum8(dx * br_ref[...].astype(F32))

        @pl.when(i == nsteps - 1)
        def _():
            loss_ref[...] = jnp.sum(jnp.sum(acc[0], axis=0, keepdims=True), axis=1, keepdims=True) * (0.5 / d)
            dw_ref[...] = jnp.sum(acc[1], axis=0, keepdims=True)
            dg_ref[...] = jnp.sum(acc[2], axis=0, keepdims=True)

    big = pl.BlockSpec((tm, d), lambda i: (i, 0))
    row = pl.BlockSpec((1, d), lambda i: (0, 0))
    rowo = jax.ShapeDtypeStruct((1, d), F32)
    return pl.pallas_call(
        body, name="final_loss", grid=(nsteps,),
        in_specs=[big, big, row, big, row],
        out_specs=[pl.BlockSpec((1, 1), lambda i: (0, 0)), big, big, row, row],
        out_shape=[jax.ShapeDtypeStruct((1, 1), F32), jax.ShapeDtypeStruct((s, d), F32),
                   jax.ShapeDtypeStruct((s, d), BF16), rowo, rowo],
        scratch_shapes=[pltpu.VMEM((3, 8, d), F32)],
        compiler_params=_cparams("arbitrary"),
    )(x, target, w, branch, g)


def _rope_tables(pos_col):
    s = pos_col.shape[0]
    tq = _tile(s, 1024, 8)
    half = ROPE_DIM // 2
    inv = np.float32(ROPE_THETA) ** (-np.arange(0, ROPE_DIM, 2, dtype=np.float32) / np.float32(ROPE_DIM))
    inv_row = jnp.asarray(np.tile(inv.astype(np.float32), HEAD // half)[None, :])

    def body(p_ref, inv_ref, ct_ref, st_ref):
        lane = lax.broadcasted_iota(jnp.int32, (tq, HEAD), 1)
        ang = p_ref[...].astype(F32) * inv_ref[...]
        cs, sn = jnp.cos(ang), jnp.sin(ang)
        ct_ref[...] = jnp.where(lane < ROPE_DIM, cs, 1.0)
        st_ref[...] = jnp.where(lane < half, -sn, jnp.where(lane < ROPE_DIM, sn, 0.0))

    blk = pl.BlockSpec((tq, HEAD), lambda i: (i, 0))
    return pl.pallas_call(
        body, name="rope_tables", grid=(s // tq,),
        in_specs=[pl.BlockSpec((tq, 1), lambda i: (i, 0)), pl.BlockSpec((1, HEAD), lambda i: (0, 0))],
        out_specs=[blk, blk],
        out_shape=[jax.ShapeDtypeStruct((s, HEAD), F32)] * 2,
        compiler_params=_cparams("parallel"),
    )(pos_col, inv_row)


def _swap_halves(x):
    lane = lax.broadcasted_iota(jnp.int32, x.shape, 1)
    half = ROPE_DIM // 2
    return jnp.where(lane < half, pltpu.roll(x, HEAD - half, 1), pltpu.roll(x, half, 1))


def _rope(x, ct, st):
    return x * ct + _swap_halves(x) * st


def _rope_t(dy, ct, st):
    return dy * ct - _swap_halves(dy) * st


def _band_bias(bias_ref):
    qi = lax.broadcasted_iota(jnp.int32, (CHUNK, 2 * CHUNK), 0)
    kj = lax.broadcasted_iota(jnp.int32, (CHUNK, 2 * CHUNK), 1)
    band = (kj >= qi) & (kj <= qi + CHUNK)
    bias_ref[0] = jnp.where(band, 0.0, NEG)
    bias_ref[1] = jnp.where(band & (kj >= CHUNK), 0.0, NEG)


def _unit_starts(u, d):
    shift = int(math.log2(d))
    sb, r = u >> shift, u & (d - 1)
    qs = sb * (CHUNK * d) + r
    ks = ATT_CHUNK + (sb - 1) * (CHUNK * d) + r
    if d == 1:
        qs, ks = pl.multiple_of(qs, CHUNK), pl.multiple_of(ks, CHUNK)
    return sb, qs, ks


def _rows(start, size, d):
    return pl.ds(start, size) if d == 1 else pl.ds(start, size, stride=d)


def _attn_specs(nh, nc):
    cur = lambda off: pl.BlockSpec((ATT_CHUNK, HEAD), lambda h, c: (jnp.minimum(c, nc - 1), off + h))
    prev = lambda off: pl.BlockSpec((ATT_CHUNK, HEAD), lambda h, c: (jnp.maximum(c - 1, 0), off + h))
    tcur = pl.BlockSpec((ATT_CHUNK, HEAD), lambda h, c: (jnp.minimum(c, nc - 1), 0))
    tprev = pl.BlockSpec((ATT_CHUNK, HEAD), lambda h, c: (jnp.maximum(c - 1, 0), 0))
    return [cur(0), prev(nh), cur(nh), prev(2 * nh), cur(2 * nh), tcur, tcur, tprev, tprev]


def _attn_fwd(z, ct, st, nh):
    s = z.shape[0]
    nc = s // ATT_CHUNK
    scale = HEAD ** -0.5

    def body(q_ref, kp_ref, kc_ref, vp_ref, vc_ref, ctc, stc, ctp, stp, o_ref, lse_ref, qf, kf, vf, ob, lb, bias):
        c = pl.program_id(1)
        qf[...] = _rope(q_ref[...].astype(F32), ctc[...], stc[...])
        kf[0:ATT_CHUNK] = _rope(kp_ref[...].astype(F32), ctp[...], stp[...])
        kf[ATT_CHUNK:] = _rope(kc_ref[...].astype(F32), ctc[...], stc[...])
        vf[0:ATT_CHUNK] = vp_ref[...].astype(F32)
        vf[ATT_CHUNK:] = vc_ref[...].astype(F32)
        _band_bias(bias)
        for b, d in enumerate(DILATIONS):
            def unit(u, carry, b=b, d=d):
                sb, qs, ks = _unit_starts(u, d)
                q_u = qf[_rows(qs, CHUNK, d), :].astype(BF16)
                k_u = kf[_rows(ks, 2 * CHUNK, d), :].astype(BF16)
                v_u = vf[_rows(ks, 2 * CHUNK, d), :].astype(BF16)
                first = jnp.where((c == 0) & (sb == 0), 1, 0)
                sc = _dot_nt(q_u, k_u) * scale + bias[first]
                m = jnp.max(sc, axis=1, keepdims=True)
                p = jnp.exp(sc - m)
                l = jnp.sum(p, axis=1, keepdims=True)
                ob[b, _rows(qs, CHUNK, d), :] = _dot_nn(p.astype(BF16), v_u) / l
                lb[b, _rows(qs, CHUNK, d), :] = jnp.broadcast_to(m + jnp.log(l), (CHUNK, HEAD))
                return carry
            lax.fori_loop(0, ATT_CHUNK // CHUNK, unit, 0)
        mx = jnp.maximum(jnp.maximum(lb[0], lb[1]), lb[2])
        e0, e1, e2 = jnp.exp(lb[0] - mx), jnp.exp(lb[1] - mx), jnp.exp(lb[2] - mx)
        den = e0 + e1 + e2
        o_ref[...] = ((e0 * ob[0] + e1 * ob[1] + e2 * ob[2]) / den).astype(BF16)
        lse_ref[...] = mx + jnp.log(den)

    blk = pl.BlockSpec((ATT_CHUNK, HEAD), lambda h, c: (c, h))
    return pl.pallas_call(
        body, name="attn_fwd", grid=(nh, nc),
        in_specs=_attn_specs(nh, nc), out_specs=[blk, blk],
        out_shape=[jax.ShapeDtypeStruct((s, nh * HEAD), BF16), jax.ShapeDtypeStruct((s, nh * HEAD), F32)],
        scratch_shapes=[pltpu.VMEM((ATT_CHUNK, HEAD), F32), pltpu.VMEM((2 * ATT_CHUNK, HEAD), F32),
                        pltpu.VMEM((2 * ATT_CHUNK, HEAD), F32), pltpu.VMEM((3, ATT_CHUNK, HEAD), F32),
                        pltpu.VMEM((3, ATT_CHUNK, HEAD), F32), pltpu.VMEM((2, CHUNK, 2 * CHUNK), F32)],
        compiler_params=_cparams("parallel", "arbitrary"),
    )(z, z, z, z, z, ct, st, ct, st)


def _attn_bwd(z, ct, st, da, o, lse, nh):
    s = z.shape[0]
    nc = s // ATT_CHUNK
    scale = HEAD ** -0.5

    def body(q_ref, kp_ref, kc_ref, vp_ref, vc_ref, ctc, stc, ctp, stp, do_ref, o_ref, lse_ref,
             dq_ref, dk_ref, dv_ref, qf, kf, vf, dof, dbar, dqa, dkf, dvf, bias):
        c = pl.program_id(1)

        @pl.when(c == 0)
        def _():
            dkf[...] = jnp.zeros_like(dkf)
            dvf[...] = jnp.zeros_like(dvf)

        @pl.when(c > 0)
        def _():
            dkf[0:ATT_CHUNK] = dkf[ATT_CHUNK:]
            dvf[0:ATT_CHUNK] = dvf[ATT_CHUNK:]
            dkf[ATT_CHUNK:] = jnp.zeros((ATT_CHUNK, HEAD), F32)
            dvf[ATT_CHUNK:] = jnp.zeros((ATT_CHUNK, HEAD), F32)

        @pl.when(c < nc)
        def _():
            qf[...] = _rope(q_ref[...].astype(F32), ctc[...], stc[...])
            kf[0:ATT_CHUNK] = _rope(kp_ref[...].astype(F32), ctp[...], stp[...])
            kf[ATT_CHUNK:] = _rope(kc_ref[...].astype(F32), ctc[...], stc[...])
            vf[0:ATT_CHUNK] = vp_ref[...].astype(F32)
            vf[ATT_CHUNK:] = vc_ref[...].astype(F32)
            dov = do_ref[...].astype(F32)
            dof[...] = dov
            dbar[...] = jnp.broadcast_to(jnp.sum(dov * o_ref[...].astype(F32), axis=1, keepdims=True), (ATT_CHUNK, HEAD))
            dqa[...] = jnp.zeros_like(dqa)
            _band_bias(bias)
            for d in DILATIONS:
                def unit(u, carry, d=d):
                    sb, qs, ks = _unit_starts(u, d)
                    qr, kr = _rows(qs, CHUNK, d), _rows(ks, 2 * CHUNK, d)
                    q_u = qf[qr, :].astype(BF16)
                    k_u = kf[kr, :].astype(BF16)
                    v_u = vf[kr, :].astype(BF16)
                    do_u = dof[qr, :].astype(BF16)
                    lse_u = lse_ref[qr, :]
                    dbar_u = dbar[qr, :]
                    first = jnp.where((c == 0) & (sb == 0), 1, 0)
                    sc = _dot_nt(q_u, k_u) * scale + bias[first]
                    p = jnp.exp(sc - jnp.concatenate([lse_u, lse_u], axis=1))
                    dp = _dot_nt(do_u, v_u)
                    ds = (p * (dp - jnp.concatenate([dbar_u, dbar_u], axis=1)) * scale).astype(BF16)
                    dqa[qr, :] += _dot_nn(ds, k_u)
                    dkf[kr, :] += _dot_tn(ds, q_u)
                    dvf[kr, :] += _dot_tn(p.astype(BF16), do_u)
                    return carry
                lax.fori_loop(0, ATT_CHUNK // CHUNK, unit, 0)
            dq_ref[...] = _rope_t(dqa[...], ctc[...], stc[...]).astype(BF16)

        @pl.when(c > 0)
        def _():
            dk_ref[...] = _rope_t(dkf[0:ATT_CHUNK], ctp[...], stp[...]).astype(BF16)
            dv_ref[...] = dvf[0:ATT_CHUNK].astype(BF16)

    cur = pl.BlockSpec((ATT_CHUNK, HEAD), lambda h, c: (jnp.minimum(c, nc - 1), h))
    late = pl.BlockSpec((ATT_CHUNK, HEAD), lambda h, c: (jnp.maximum(c - 1, 0), h))
    shp = jax.ShapeDtypeStruct((s, nh * HEAD), BF16)
    big = pltpu.VMEM((2 * ATT_CHUNK, HEAD), F32)
    one = pltpu.VMEM((ATT_CHUNK, HEAD), F32)
    return pl.pallas_call(
        body, name="attn_bwd", grid=(nh, nc + 1),
        in_specs=_attn_specs(nh, nc) + [cur, cur, cur], out_specs=[cur, late, late],
        out_shape=[shp, shp, shp],
        scratch_shapes=[one, big, big, one, one, one, big, big, pltpu.VMEM((2, CHUNK, 2 * CHUNK), F32)],
        compiler_params=_cparams("parallel", "arbitrary"),
    )(z, z, z, z, z, ct, st, ct, st, da, o, lse)


_GELU_K = math.sqrt(2.0 / math.pi)


def _gelu(x):
    return 0.5 * x * (1.0 + jnp.tanh(_GELU_K * (x + 0.044715 * x * x * x)))


def _gelu_and_grad(x):
    t = jnp.tanh(_GELU_K * (x + 0.044715 * x * x * x))
    g = 0.5 * x * (1.0 + t)
    dg = 0.5 * (1.0 + t) + 0.5 * x * (1.0 - t * t) * (_GELU_K * (1.0 + 3 * 0.044715 * x * x))
    return g, dg


def _tril(w):
    ti = lax.broadcasted_iota(jnp.int32, (CHUNK, CHUNK), 0)
    si = lax.broadcasted_iota(jnp.int32, (CHUNK, CHUNK), 1)
    return jnp.where(si <= ti, w, 0.0)


def _sgu_fwd(z, w_s, b_col, ng, u_blk):
    s = z.shape[0]
    gw = ng * HEAD
    tq = _tile(s, 1024, CHUNK)

    def body(u_ref, v_ref, w_ref, b_ref, o_ref):
        for g in range(ng):
            wg = _tril(w_ref[g]).astype(BF16)
            cols = slice(g * HEAD, (g + 1) * HEAD)
            for n in range(tq // CHUNK):
                rows = slice(n * CHUNK, (n + 1) * CHUNK)
                gv = _gelu(v_ref[rows, cols].astype(F32)).astype(BF16)
                mixed = _dot_nn(wg, gv) + b_ref[g]
                o_ref[rows, cols] = (_gelu(u_ref[rows, cols].astype(F32)) * mixed).astype(BF16)

    full = pl.BlockSpec((ng, CHUNK, CHUNK), lambda i: (0, 0, 0))
    return pl.pallas_call(
        body, name="sgu_fwd", grid=(s // tq,),
        in_specs=[pl.BlockSpec((tq, gw), lambda i: (i, u_blk)), pl.BlockSpec((tq, gw), lambda i: (i, u_blk + 1)), full, full],
        out_specs=pl.BlockSpec((tq, gw), lambda i: (i, 0)),
        out_shape=jax.ShapeDtypeStruct((s, gw), BF16),
        compiler_params=_cparams("parallel"),
    )(z, z, w_s, b_col)


def _sgu_bwd(z, w_s, b_col, da, ng, u_blk, da_blk):
    s = z.shape[0]
    gw = ng * HEAD
    tq = _tile(s, 1024, CHUNK)
    nsteps = s // tq

    def body(u_ref, v_ref, w_ref, b_ref, do_ref, du_ref, dv_ref, dw_ref, db_ref):
        i = pl.program_id(0)

        @pl.when(i == 0)
        def _():
            dw_ref[...] = jnp.zeros_like(dw_ref)
            db_ref[...] = jnp.zeros_like(db_ref)

        for g in range(ng):
            wg = _tril(w_ref[g]).astype(BF16)
            cols = slice(g * HEAD, (g + 1) * HEAD)
            dw_acc = jnp.zeros((CHUNK, CHUNK), F32)
            db_acc = jnp.zeros((CHUNK, 1), F32)
            for n in range(tq // CHUNK):
                rows = slice(n * CHUNK, (n + 1) * CHUNK)
                gu, dgu = _gelu_and_grad(u_ref[rows, cols].astype(F32))
                gv, dgv = _gelu_and_grad(v_ref[rows, cols].astype(F32))
                gvb = gv.astype(BF16)
                mixed = _dot_nn(wg, gvb) + b_ref[g]
                dout = do_ref[rows, cols].astype(F32)
                du_ref[rows, cols] = (dout * mixed * dgu).astype(BF16)
                dmix = dout * gu
                dmb = dmix.astype(BF16)
                dv_ref[rows, cols] = (_dot_tn(wg, dmb) * dgv).astype(BF16)
                dw_acc += _dot_nt(dmb, gvb)
                db_acc += jnp.sum(dmix, axis=1, keepdims=True)
            dw_ref[g] += _tril(dw_acc)
            db_ref[g] += jnp.broadcast_to(db_acc, (CHUNK, CHUNK))

    full = pl.BlockSpec((ng, CHUNK, CHUNK), lambda i: (0, 0, 0))
    out = pl.BlockSpec((tq, gw), lambda i: (i, 0))
    return pl.pallas_call(
        body, name="sgu_bwd", grid=(nsteps,),
        in_specs=[pl.BlockSpec((tq, gw), lambda i: (i, u_blk)), pl.BlockSpec((tq, gw), lambda i: (i, u_blk + 1)), full, full,
                  pl.BlockSpec((tq, gw), lambda i: (i, da_blk))],
        out_specs=[out, out, full, full],
        out_shape=[jax.ShapeDtypeStruct((s, gw), BF16)] * 2 + [jax.ShapeDtypeStruct((ng, CHUNK, CHUNK), F32)] * 2,
        compiler_params=_cparams("arbitrary"),
    )(z, z, w_s, b_col, da)


def _shift_down(y, halo, k):
    rolled = pltpu.roll(y, k, 0)
    row = lax.broadcasted_iota(jnp.int32, y.shape, 0)
    for j in range(k):
        rolled = jnp.where(row == j, halo[8 - k + j:8 - k + j + 1, :], rolled)
    return rolled


def _shift_up(y, halo, k):
    n = y.shape[0]
    rolled = pltpu.roll(y, n - k, 0)
    row = lax.broadcasted_iota(jnp.int32, y.shape, 0)
    for j in range(k):
        rolled = jnp.where(row == n - k + j, halo[j:j + 1, :], rolled)
    return rolled


def _conv_fwd(z, cw):
    s, d3 = z.shape
    d = d3 // 3
    tq = _tile(s, 256, 8)

    def body(z_ref, zh_ref, cw_ref, a_ref):
        i = pl.program_id(0)
        zv = z_ref[...].astype(F32)
        zh = jnp.where(i > 0, zh_ref[...].astype(F32), 0.0)
        y = zv[:, d:2 * d] * zv[:, 2 * d:]
        yh = zh[:, d:2 * d] * zh[:, 2 * d:]
        cwv = cw_ref[...]
        conv = cwv[0:1] * _shift_down(y, yh, 2) + cwv[1:2] * _shift_down(y, yh, 1) + cwv[2:3] * y
        a_ref[...] = (zv[:, :d] * conv).astype(BF16)

    return pl.pallas_call(
        body, name="conv_fwd", grid=(s // tq,),
        in_specs=[pl.BlockSpec((tq, d3), lambda i: (i, 0)),
                  pl.BlockSpec((8, d3), lambda i: (jnp.maximum(i * (tq // 8) - 1, 0), 0)),
                  pl.BlockSpec((3, d), lambda i: (0, 0))],
        out_specs=pl.BlockSpec((tq, d), lambda i: (i, 0)),
        out_shape=jax.ShapeDtypeStruct((s, d), BF16),
        compiler_params=_cparams("parallel"),
    )(z, z, cw)


def _conv_bwd(z, cw, da):
    s, d3 = z.shape
    d = d3 // 3
    tq = _tile(s, 128, 8)
    nsteps = s // tq
    nblk8 = s // 8

    def body(z_ref, zp_ref, zn_ref, da_ref, dan_ref, cw_ref, dz_ref, dcw_ref, acc):
        i = pl.program_id(0)

        @pl.when(i == 0)
        def _():
            acc[...] = jnp.zeros_like(acc)

        zv = z_ref[...].astype(F32)
        zp = jnp.where(i > 0, zp_ref[...].astype(F32), 0.0)
        zn = jnp.where(i < nsteps - 1, zn_ref[...].astype(F32), 0.0)
        dav = da_ref[...].astype(F32)
        dan = jnp.where(i < nsteps - 1, dan_ref[...].astype(F32), 0.0)
        gb, gc, hx = zv[:, :d], zv[:, d:2 * d], zv[:, 2 * d:]
        y = gc * hx
        yp = zp[:, d:2 * d] * zp[:, 2 * d:]
        cwv = cw_ref[...]
        y1, y2 = _shift_down(y, yp, 1), _shift_down(y, yp, 2)
        conv = cwv[0:1] * y2 + cwv[1:2] * y1 + cwv[2:3] * y
        dconv = dav * gb
        dconv_n = dan * zn[:, :d]
        dy = cwv[2:3] * dconv + cwv[1:2] * _shift_up(dconv, dconv_n, 1) + cwv[0:1] * _shift_up(dconv, dconv_n, 2)
        dz_ref[:, :d] = (dav * conv).astype(BF16)
        dz_ref[:, d:2 * d] = (dy * hx).astype(BF16)
        dz_ref[:, 2 * d:] = (dy * gc).astype(BF16)
        acc[0] += _colsum8(dconv * y2)
        acc[1] += _colsum8(dconv * y1)
        acc[2] += _colsum8(dconv * y)

        @pl.when(i == nsteps - 1)
        def _():
            for j in range(3):
                dcw_ref[j:j + 1, :] = jnp.sum(acc[j], axis=0, keepdims=True)

    return pl.pallas_call(
        body, name="conv_bwd", grid=(nsteps,),
        in_specs=[pl.BlockSpec((tq, d3), lambda i: (i, 0)),
                  pl.BlockSpec((8, d3), lambda i: (jnp.maximum(i * (tq // 8) - 1, 0), 0)),
                  pl.BlockSpec((8, d3), lambda i: (jnp.minimum((i + 1) * (tq // 8), nblk8 - 1), 0)),
                  pl.BlockSpec((tq, d), lambda i: (i, 0)),
                  pl.BlockSpec((8, d), lambda i: (jnp.minimum((i + 1) * (tq // 8), nblk8 - 1), 0)),
                  pl.BlockSpec((3, d), lambda i: (0, 0))],
        out_specs=[pl.BlockSpec((tq, d3), lambda i: (i, 0)), pl.BlockSpec((3, d), lambda i: (0, 0))],
        out_shape=[jax.ShapeDtypeStruct((s, d3), BF16), jax.ShapeDtypeStruct((3, d), F32)],
        scratch_shapes=[pltpu.VMEM((3, 8, d), F32)],
        compiler_params=_cparams("arbitrary"),
    )(z, z, z, da, da, cw)


def _swiglu_fwd(gt, up):
    s, f = gt.shape
    tm, tn = _tile(s, 512, 8), _tile(f, 2816, 128)

    def body(g_ref, u_ref, a_ref):
        g = g_ref[...].astype(F32)
        a_ref[...] = (g / (1.0 + jnp.exp(-g)) * u_ref[...].astype(F32)).astype(BF16)

    blk = pl.BlockSpec((tm, tn), lambda i, j: (i, j))
    return pl.pallas_call(
        body, name="swiglu_fwd", grid=(s // tm, f // tn), in_specs=[blk, blk], out_specs=blk,
        out_shape=jax.ShapeDtypeStruct((s, f), BF16), compiler_params=_cparams("parallel", "parallel"),
    )(gt, up)


def _swiglu_bwd_epilogue(dact, gt, up):
    g, u = gt.astype(F32), up.astype(F32)
    sg = 1.0 / (1.0 + jnp.exp(-g))
    return dact * u * (sg * (1.0 + g * (1.0 - sg))), dact * (g * sg)


def _ada_fwd(c_all, ada_w):
    nl, d, n8 = ada_w.shape
    tn = _tile(n8, 768, 128)

    def body(c_ref, w_ref, o_ref):
        cv = c_ref[...]
        act = (cv / (1.0 + jnp.exp(-cv))).astype(BF16)
        o_ref[0] = _dot_nn(act, w_ref[0].astype(BF16))

    return pl.pallas_call(
        body, name="ada_fwd", grid=(nl, n8 // tn),
        in_specs=[pl.BlockSpec((N_DEV, d), lambda l, j: (0, 0)), pl.BlockSpec((1, d, tn), lambda l, j: (l, 0, j))],
        out_specs=pl.BlockSpec((1, N_DEV, tn), lambda l, j: (l, 0, j)),
        out_shape=jax.ShapeDtypeStruct((nl, N_DEV, n8), F32),
        compiler_params=_cparams("parallel", "parallel"),
    )(c_all, ada_w)


def _ada_wgrad(c_all, dmod_cols):
    nl, _, n8 = dmod_cols.shape
    d = c_all.shape[1]
    tn = _tile(n8, 768, 128)

    def body(c_ref, g_ref, o_ref):
        cv = c_ref[...]
        act = (cv / (1.0 + jnp.exp(-cv))).astype(BF16)
        o_ref[0] = _dot_tn(act, g_ref[0].astype(BF16))

    return pl.pallas_call(
        body, name="ada_wgrad", grid=(nl, n8 // tn),
        in_specs=[pl.BlockSpec((N_DEV, d), lambda l, j: (0, 0)), pl.BlockSpec((1, N_DEV, tn), lambda l, j: (l, 0, j))],
        out_specs=pl.BlockSpec((1, d, tn), lambda l, j: (l, 0, j)),
        out_shape=jax.ShapeDtypeStruct((nl, d, n8), F32),
        compiler_params=_cparams("parallel", "parallel"),
    )(c_all, dmod_cols)


def _adamw(name, pieces, w, m, v):
    npc, r, c = pieces.shape
    tr = _tile(r, max(8, (1 << 19) // c // 8 * 8), 8)
    bc1, bc2 = 1.0 - ADAM_B1 ** ADAM_STEP, 1.0 - ADAM_B2 ** ADAM_STEP

    def body(p_ref, w_ref, m_ref, v_ref, g_ref, d_ref, nm_ref, nv_ref):
        g = p_ref[0].astype(F32)
        for i in range(1, npc):
            g = g + p_ref[i].astype(F32)
        nm = ADAM_B1 * m_ref[...] + (1.0 - ADAM_B1) * g
        nv = ADAM_B2 * v_ref[...] + (1.0 - ADAM_B2) * (g * g)
        g_ref[...] = g
        nm_ref[...] = nm
        nv_ref[...] = nv
        d_ref[...] = -ADAM_LR * ((nm / bc1) / (jnp.sqrt(nv / bc2) + ADAM_EPS) + ADAM_WD * w_ref[...])

    blk = pl.BlockSpec((tr, c), lambda i: (i, 0))
    return pl.pallas_call(
        body, name=name, grid=(r // tr,),
        in_specs=[pl.BlockSpec((npc, tr, c), lambda i: (0, i, 0)), blk, blk, blk],
        out_specs=[blk] * 4, out_shape=[jax.ShapeDtypeStruct((r, c), F32)] * 4,
        compiler_params=_cparams("parallel"),
    )(pieces, w, m, v)


def _place():
    x, y, c = lax.axis_index("x"), lax.axis_index("y"), lax.axis_index("c")
    return x, y, c


def _all_gather_small(name, x_shard):
    m_per, n = x_shard.shape

    def body(x_ref, out_ref, send_sems, recv_sems, local_sem):
        x, y, c = _place()
        me, sibling = (x, y, c), (x, y, 1 - c)
        chips = [(1 - x, y), (x, 1 - y), (1 - x, 1 - y)]

        def rows(px, py, pc):
            return out_ref.at[pl.ds((4 * px + 2 * py + pc) * m_per, m_per), :]

        def copy(k, block, to, src=None):
            return pltpu.make_async_remote_copy(
                src_ref=rows(*block) if src is None else src, dst_ref=rows(*block),
                send_sem=send_sems.at[k], recv_sem=recv_sems.at[k], device_id=to, device_id_type=MESH)

        mine = pltpu.make_async_copy(x_ref, rows(*me), local_sem)
        mine.start()
        first = [copy(0, me, sibling, src=x_ref)]
        first += [copy(1 + j, me, (*chip, c), src=x_ref) for j, chip in enumerate(chips)]
        for cp in first:
            cp.start()
        passed = [copy(4 + j, (*chip, c), sibling) for j, chip in enumerate(chips)]
        for j, chip in enumerate(chips):
            copy(1 + j, (*chip, c), me).wait_recv()
            passed[j].start()
        copy(0, sibling, me).wait_recv()
        for j, chip in enumerate(chips):
            copy(4 + j, (*chip, 1 - c), me).wait_recv()
        for cp in first + passed:
            cp.wait_send()
        mine.wait()

    return pl.pallas_call(
        body, name=name,
        out_shape=jax.ShapeDtypeStruct((N_DEV * m_per, n), x_shard.dtype),
        in_specs=[pl.BlockSpec(memory_space=pltpu.VMEM)], out_specs=pl.BlockSpec(memory_space=pltpu.VMEM),
        scratch_shapes=[pltpu.SemaphoreType.DMA((7,)), pltpu.SemaphoreType.DMA((7,)), pltpu.SemaphoreType.DMA],
        compiler_params=pltpu.CompilerParams(vmem_limit_bytes=VMEM_LIMIT),
    )(x_shard)


def _all_gather_big(name, shards):
    na = len(shards)

    def body(*refs):
        x_refs, out_refs = refs[:na], refs[na:2 * na]
        send_sems, recv_sems, local_sems = refs[2 * na:]
        x, y, c = _place()
        me, sibling = (x, y, c), (x, y, 1 - c)
        chips = [(1 - x, y), (x, 1 - y), (1 - x, 1 - y)]

        def slot(a, px, py, pc):
            return out_refs[a].at[4 * px + 2 * py + pc]

        def copy(a, k, block, to, src=None):
            return pltpu.make_async_remote_copy(
                src_ref=slot(a, *block) if src is None else src, dst_ref=slot(a, *block),
                send_sem=send_sems.at[a, k], recv_sem=recv_sems.at[a, k], device_id=to, device_id_type=MESH)

        mine = [pltpu.make_async_copy(x_refs[a], slot(a, *me), local_sems.at[a]) for a in range(na)]
        for cp in mine:
            cp.start()
        first = []
        for a in range(na):
            first.append(copy(a, 0, me, sibling, src=x_refs[a]))
            first += [copy(a, 1 + j, me, (*chip, c), src=x_refs[a]) for j, chip in enumerate(chips)]
        for cp in first:
            cp.start()
        passed = []
        for a in range(na):
            for j, chip in enumerate(chips):
                copy(a, 1 + j, (*chip, c), me).wait_recv()
                fwd = copy(a, 4 + j, (*chip, c), sibling)
                fwd.start()
                passed.append(fwd)
        for a in range(na):
            copy(a, 0, sibling, me).wait_recv()
            for j, chip in enumerate(chips):
                copy(a, 4 + j, (*chip, 1 - c), me).wait_recv()
        for cp in first + passed:
            cp.wait_send()
        for cp in mine:
            cp.wait()

    hbm = pl.BlockSpec(memory_space=pl.ANY)
    return pl.pallas_call(
        body, name=name,
        out_shape=[jax.ShapeDtypeStruct((N_DEV,) + s.shape, s.dtype) for s in shards],
        in_specs=[hbm] * na, out_specs=[hbm] * na,
        scratch_shapes=[pltpu.SemaphoreType.DMA((na, 7)), pltpu.SemaphoreType.DMA((na, 7)), pltpu.SemaphoreType.DMA((na,))],
    )(*shards)


def _exchange_shards(name, grads):
    na = len(grads)

    def body(*refs):
        g_refs, out_refs = refs[:na], refs[na:2 * na]
        send_sems, recv_sems, local_sems = refs[2 * na:]
        x, y, c = _place()
        me = 4 * x + 2 * y + c

        def peer(k):
            px = 1 - x if k & 4 else x
            py = 1 - y if k & 2 else y
            pc = 1 - c if k & 1 else c
            return (px, py, pc), 4 * px + 2 * py + pc

        def copy(a, k):
            dev, idx = peer(k)
            return pltpu.make_async_remote_copy(
                src_ref=g_refs[a].at[idx], dst_ref=out_refs[a].at[me],
                send_sem=send_sems.at[a, k - 1], recv_sem=recv_sems.at[a, k - 1], device_id=dev, device_id_type=MESH)

        def arrival(a, k):
            dev, idx = peer(k)
            return pltpu.make_async_remote_copy(
                src_ref=g_refs[a].at[idx], dst_ref=out_refs[a].at[idx],
                send_sem=send_sems.at[a, k - 1], recv_sem=recv_sems.at[a, k - 1], device_id=dev, device_id_type=MESH)

        mine = [pltpu.make_async_copy(g_refs[a].at[me], out_refs[a].at[me], local_sems.at[a]) for a in range(na)]
        for cp in mine:
            cp.start()
        sends = [copy(a, k) for a in range(na) for k in range(1, N_DEV)]
        for cp in sends:
            cp.start()
        for a in range(na):
            for k in range(1, N_DEV):
                arrival(a, k).wait_recv()
        for cp in sends:
            cp.wait_send()
        for cp in mine:
            cp.wait()

    hbm = pl.BlockSpec(memory_space=pl.ANY)
    return pl.pallas_call(
        body, name=name,
        out_shape=[jax.ShapeDtypeStruct(g.shape, g.dtype) for g in grads],
        in_specs=[hbm] * na, out_specs=[hbm] * na,
        scratch_shapes=[pltpu.SemaphoreType.DMA((na, 7)), pltpu.SemaphoreType.DMA((na, 7)), pltpu.SemaphoreType.DMA((na,))],
    )(*grads)


def _cols_to_plain(g):
    n, k, n8 = g.shape
    return jnp.transpose(g, (1, 0, 2)).reshape(k, n * n8)


def _plain_to_cols(w):
    k, n = w.shape
    return jnp.transpose(w.reshape(k, N_DEV, n // N_DEV), (1, 0, 2))


def _pack_rows(parts, width):
    rows, offs, r = [], [], 0
    for p in parts:
        flat = p.reshape(-1).astype(F32)
        nr = -(-flat.shape[0] // width)
        rows.append(jnp.pad(flat, (0, nr * width - flat.shape[0])).reshape(nr, width))
        offs.append((r, flat.shape[0], p.shape))
        r += nr
    pad = -r % 8
    if pad:
        rows.append(jnp.zeros((pad, width), F32))
    return jnp.concatenate(rows, axis=0), offs, r + pad


def _unpack_rows(slab, offs, width):
    lead = slab.shape[:-2]
    out = []
    for r0, n, shape in offs:
        nr = -(-n // width)
        out.append(slab[..., r0:r0 + nr, :].reshape(lead + (nr * width,))[..., :n].reshape(lead + tuple(shape)))
    return out


def kernel(x, c, positions, ada_w, ada_b, norm_mix, norm_ffn, ab_w_in, sgu_w, sgu_b, ab_w_out, conv_w_in, conv_w, conv_w_out, ffn_w_gate, ffn_w_up, ffn_w_down, final_norm, loss_target, m_ada_w, m_ada_b, m_norm_mix, m_norm_ffn, m_ab_w_in, m_sgu_w, m_sgu_b, m_ab_w_out, m_conv_w_in, m_conv_w, m_conv_w_out, m_ffn_w_gate, m_ffn_w_up, m_ffn_w_down, m_final_norm, v_ada_w, v_ada_b, v_norm_mix, v_norm_ffn, v_ab_w_in, v_sgu_w, v_sgu_b, v_ab_w_out, v_conv_w_in, v_conv_w, v_conv_w_out, v_ffn_w_gate, v_ffn_w_up, v_ffn_w_down, v_final_norm):
    xi, yi, ci = _place()
    me = 4 * xi + 2 * yi + ci
    s, d = x.shape[1], x.shape[2]
    depth = ada_w.shape[0]
    n_even = ab_w_in.shape[0]
    nh_mix = d // HEAD
    nh = 3 * nh_mix // 4
    ng = nh_mix - nh
    aw, gw = nh * HEAD, ng * HEAD
    x0 = x[0]
    target = loss_target[0]

    width = 512
    slab, offs, _ = _pack_rows([c, conv_w], width)
    gathered = _all_gather_small("gather_cond", slab).reshape(N_DEV, -1, width)
    c_parts, cw_parts = _unpack_rows(gathered, offs, width)
    c_all = c_parts.reshape(N_DEV, d)
    n_odd, cwid, d8 = conv_w.shape
    conv_w_full = jnp.transpose(cw_parts, (1, 2, 0, 3)).reshape(n_odd, cwid, d)

    mod_cols = _ada_fwd(c_all, ada_w)
    n8 = mod_cols.shape[2]
    mod_all = _all_gather_small("gather_mod", mod_cols.reshape(depth * N_DEV, n8)).reshape(N_DEV, depth, N_DEV, n8)
    mod_mine = lax.dynamic_index_in_dim(mod_all, me, axis=2, keepdims=False)
    mod = jnp.transpose(mod_mine, (1, 0, 2)).reshape(depth, N_DEV * n8) + ada_b
    mods = mod.reshape(depth, 6, 1, d)

    big = [ab_w_in, ab_w_out, conv_w_in, conv_w_out, ffn_w_gate, ffn_w_up, ffn_w_down]
    g_in, g_out, g_cin, g_cout, g_gate, g_up, g_down = _all_gather_big("gather_weights", [w.astype(BF16) for w in big])
    w_in = [_cols_to_plain(g_in[:, i]) for i in range(n_even)]
    w_out = [g_out[:, i].reshape(d, d) for i in range(n_even)]
    w_cin = [_cols_to_plain(g_cin[:, i]) for i in range(n_odd)]
    w_cout = [g_cout[:, i].reshape(d, d) for i in range(n_odd)]
    w_gate = [_cols_to_plain(g_gate[:, l]) for l in range(depth)]
    w_up = [_cols_to_plain(g_up[:, l]) for l in range(depth)]
    ff = w_gate[0].shape[1]
    w_down = [g_down[:, l].reshape(ff, d) for l in range(depth)]

    ct, st = _rope_tables(positions.reshape(s, 1))
    b_col = jnp.broadcast_to(sgu_b[..., None], sgu_b.shape + (CHUNK,))
    u_blk = 3 * aw // gw

    stream = [x0]
    saved = []
    xcur = x0
    for l in range(depth):
        sh_m, sc_m, g_m, sh_f, sc_f, g_f = [mods[l, j] for j in range(6)]
        i = l // 2
        h = _norm_mod_fwd(xcur, norm_mix[l][None], sc_m, sh_m)
        if l % 2 == 0:
            z = _matmul("mix_in", "nn", h, w_in[i], [BF16])
            attn, lse = _attn_fwd(z, ct, st, nh)
            sgu = _sgu_fwd(z, sgu_w[i], b_col[i], ng, u_blk)
            a = jnp.concatenate([attn, sgu], axis=1)
            x1, mix = _matmul("mix_out", "nn", a, w_out[i], [F32, BF16], extras=[(xcur, "mn"), (g_m, "n")],
                              epilogue=lambda acc, r, gv: (r + gv * acc, acc))
            mixer_saved = (z, a, lse)
        else:
            z = _matmul("conv_in", "nn", h, w_cin[i], [BF16])
            a = _conv_fwd(z, conv_w_full[i])
            x1, mix = _matmul("mix_out", "nn", a, w_cout[i], [F32, BF16], extras=[(xcur, "mn"), (g_m, "n")],
                              epilogue=lambda acc, r, gv: (r + gv * acc, acc))
            mixer_saved = (z, a, None)
        h2 = _norm_mod_fwd(x1, norm_ffn[l][None], sc_f, sh_f)
        gt = _matmul("ffn_gate", "nn", h2, w_gate[l], [BF16])
        up = _matmul("ffn_up", "nn", h2, w_up[l], [BF16])
        act = _swiglu_fwd(gt, up)
        x2, f = _matmul("ffn_down", "nn", act, w_down[l], [F32, BF16], extras=[(x1, "mn"), (g_f, "n")],
                        epilogue=lambda acc, r, gv: (r + gv * acc, acc))
        saved.append((h, mixer_saved, mix, x1, h2, gt, up, act, f))
        stream.append(x2)
        xcur = x2

    f_last = saved[-1][8]
    loss_part, dx, dbr, d_final, dg = _final_loss(xcur, target, final_norm[None], f_last, mods[depth - 1, 5])
    loss = lax.psum(loss_part[0, 0], ("x", "y", "c"))

    dmod = [[None] * 6 for _ in range(depth)]
    d_norm_mix, d_norm_ffn = [None] * depth, [None] * depth
    d_sgu_w, d_sgu_b, d_conv_w = [None] * n_even, [None] * n_even, [None] * n_odd
    gw_in, gw_out, gw_cin, gw_cout = [None] * n_even, [None] * n_even, [None] * n_odd, [None] * n_odd
    gw_gate, gw_up, gw_down = [None] * depth, [None] * depth, [None] * depth
    for l in reversed(range(depth)):
        sh_m, sc_m, g_m, sh_f, sc_f, g_f = [mods[l, j] for j in range(6)]
        h, (z, a, lse), mix, x1, h2, gt, up, act, f = saved[l]
        i = l // 2
        dmod[l][5] = dg
        gw_down[l] = _matmul("ffn_down_wgrad", "tn", act, dbr, [BF16])
        dgt, dup = _matmul("ffn_down_dgrad", "nt", dbr, w_down[l], [BF16, BF16], extras=[(gt, "mn"), (up, "mn")],
                           epilogue=_swiglu_bwd_epilogue)
        gw_gate[l] = _matmul("ffn_in_wgrad", "tn", h2, dgt, [BF16])
        gw_up[l] = _matmul("ffn_in_wgrad", "tn", h2, dup, [BF16])
        dh2 = _matmul("ffn_gate_dgrad", "nt", dgt, w_gate[l], [F32])
        dh2 = _matmul("ffn_up_dgrad", "nt", dup, w_up[l], [F32], extras=[(dh2, "mn")], epilogue=lambda acc, prev: (acc + prev,))
        dx, dbr, dmod[l][3], dmod[l][4], d_norm_ffn[l], dg = _norm_mod_bwd(x1, dh2, norm_ffn[l][None], sc_f, dx, mix, g_m)
        dmod[l][2] = dg
        if l % 2 == 0:
            gw_out[i] = _matmul("mix_out_wgrad", "tn", a, dbr, [BF16])
            da = _matmul("mix_out_dgrad", "nt", dbr, w_out[i], [BF16])
            dq, dk, dv = _attn_bwd(z, ct, st, da, a, lse, nh)
            du, dvg, d_sgu_w[i], dbb = _sgu_bwd(z, sgu_w[i], b_col[i], da, ng, u_blk, aw // gw)
            d_sgu_b[i] = dbb[:, :, 0]
            dz = jnp.concatenate([dq, dk, dv, du, dvg], axis=1)
            gw_in[i] = _matmul("mix_in_wgrad", "tn", h, dz, [BF16])
            dh = _matmul("mix_in_dgrad", "nt", dz, w_in[i], [F32])
        else:
            gw_cout[i] = _matmul("mix_out_wgrad", "tn", a, dbr, [BF16])
            da = _matmul("mix_out_dgrad", "nt", dbr, w_cout[i], [BF16])
            dz, d_conv_w[i] = _conv_bwd(z, conv_w_full[i], da)
            gw_cin[i] = _matmul("conv_in_wgrad", "tn", h, dz, [BF16])
            dh = _matmul("conv_in_dgrad", "nt", dz, w_cin[i], [F32])
        if l > 0:
            f_prev, g_prev = saved[l - 1][8], mods[l - 1, 5]
            dx, dbr, dmod[l][0], dmod[l][1], d_norm_mix[l], dg = _norm_mod_bwd(stream[l], dh, norm_mix[l][None], sc_m, dx, f_prev, g_prev)
        else:
            dx, dmod[l][0], dmod[l][1], d_norm_mix[l] = _norm_mod_bwd(stream[l], dh, norm_mix[l][None], sc_m, dx)
    grad_x = dx[None]

    dmod_mine = jnp.stack([jnp.concatenate([v.reshape(d) for v in dmod[l]]) for l in range(depth)])
    small = [dmod_mine, jnp.concatenate(d_norm_mix), jnp.concatenate(d_norm_ffn), jnp.stack(d_sgu_w), jnp.stack(d_sgu_b),
             d_final, jnp.stack(d_conv_w)]
    slab, offs, _ = _pack_rows(small, width)
    gathered = _all_gather_small("gather_small_grads", slab).reshape(N_DEV, -1, width)
    p_dmod, p_nmix, p_nffn, p_sguw, p_sgub, p_final, p_convw = _unpack_rows(gathered, offs, width)

    outs = {}

    def update(name, pieces, w, m, v):
        shape = w.shape
        cdim = shape[-1]
        res = _adamw("adamw_" + name, pieces.reshape(pieces.shape[0], -1, cdim), w.reshape(-1, cdim),
                     m.reshape(-1, cdim), v.reshape(-1, cdim))
        outs[name] = [r.reshape(shape) for r in res]

    update("ada_b", p_dmod.reshape(N_DEV, depth, 6 * d), ada_b, m_ada_b, v_ada_b)
    update("norm_mix", p_nmix.reshape(N_DEV, depth, d), norm_mix, m_norm_mix, v_norm_mix)
    update("norm_ffn", p_nffn.reshape(N_DEV, depth, d), norm_ffn, m_norm_ffn, v_norm_ffn)
    update("sgu_w", p_sguw, sgu_w, m_sgu_w, v_sgu_w)
    update("sgu_b", p_sgub.reshape(N_DEV, 1, -1), sgu_b.reshape(1, -1), m_sgu_b.reshape(1, -1), v_sgu_b.reshape(1, -1))
    outs["sgu_b"] = [r.reshape(sgu_b.shape) for r in outs["sgu_b"]]
    update("final_norm", p_final.reshape(N_DEV, 1, d), final_norm[None], m_final_norm[None], v_final_norm[None])
    outs["final_norm"] = [r.reshape(final_norm.shape) for r in outs["final_norm"]]
    cw_mine = lax.dynamic_slice_in_dim(p_convw.reshape(N_DEV, n_odd, cwid, d), me * d8, d8, axis=3)
    update("conv_w", cw_mine, conv_w, m_conv_w, v_conv_w)

    dmod_cols = lax.dynamic_slice_in_dim(p_dmod.reshape(N_DEV, depth, 6 * d), me * n8, n8, axis=2)
    g_ada = _ada_wgrad(c_all, jnp.transpose(dmod_cols, (1, 0, 2)))
    update("ada_w", g_ada[None], ada_w, m_ada_w, v_ada_w)

    def col_shards(gs):
        return jnp.stack([_plain_to_cols(g) for g in gs], axis=1)

    def row_shards(gs):
        return jnp.stack([g.reshape(N_DEV, g.shape[0] // N_DEV, g.shape[1]) for g in gs], axis=1)

    to_send = [col_shards(gw_in), row_shards(gw_out), col_shards(gw_cin), row_shards(gw_cout),
               col_shards(gw_gate), col_shards(gw_up), row_shards(gw_down)]
    received = _exchange_shards("exchange_grads", to_send)
    names = ["ab_w_in", "ab_w_out", "conv_w_in", "conv_w_out", "ffn_w_gate", "ffn_w_up", "ffn_w_down"]
    moments = [(m_ab_w_in, v_ab_w_in), (m_ab_w_out, v_ab_w_out), (m_conv_w_in, v_conv_w_in), (m_conv_w_out, v_conv_w_out),
               (m_ffn_w_gate, v_ffn_w_gate), (m_ffn_w_up, v_ffn_w_up), (m_ffn_w_down, v_ffn_w_down)]
    for name, w, (m, v), rec in zip(names, big, moments, received):
        update(name, rec, w, m, v)

    order = ["ada_w", "ada_b", "norm_mix", "norm_ffn", "ab_w_in", "sgu_w", "sgu_b", "ab_w_out", "conv_w_in", "conv_w",
             "conv_w_out", "ffn_w_gate", "ffn_w_up", "ffn_w_down", "final_norm"]
    return (loss, grad_x, *[outs[n][0] for n in order], *[outs[n][1] for n in order],
            *[outs[n][2] for n in order], *[outs[n][3] for n in order])
```

```python
import functools
import math

import numpy as np
import jax
import jax.numpy as jnp
from jax import lax
from jax.experimental import pallas as pl
from jax.experimental.pallas import tpu as pltpu

F32, BF16 = jnp.float32, jnp.bfloat16
MESH = pl.DeviceIdType.MESH
N_DEV = 8
EPS = 1e-6
HEAD = 128
CHUNK = 128
DILATIONS = (1, 4, 16)
ATT_CHUNK = CHUNK * DILATIONS[-1]
ATT_UNROLL_FWD, ATT_UNROLL_BWD = 8, 4
ROPE_THETA = 500000.0
ROPE_DIM = HEAD // 4
NEG = -1e30
ADAM_LR, ADAM_B1, ADAM_B2, ADAM_EPS, ADAM_WD, ADAM_STEP = 0.001, 0.9, 0.999, 1e-08, 0.01, 10
VMEM_LIMIT = 56 * 1024 * 1024


def _cparams(*sem):
    return pltpu.CompilerParams(dimension_semantics=sem or None, vmem_limit_bytes=VMEM_LIMIT)


def _tile(n, pref, unit):
    t = (min(pref, n) // unit) * unit
    while t >= unit:
        if n % t == 0:
            return t
        t -= unit
    return n


def _dot(a, b, dims):
    return lax.dot_general(a, b, (dims, ((), ())), preferred_element_type=F32)


def _dot_nn(a, b):
    return _dot(a, b, ((1,), (0,)))


def _dot_nt(a, b):
    return _dot(a, b, ((1,), (1,)))


def _dot_tn(a, b):
    return _dot(a, b, ((0,), (0,)))


def _matmul(name, mode, a, b, outs, *, extras=(), epilogue=None, tm=1024, tn=512, tk=2816):
    if mode == "nn":
        (m, kk), (_, n) = a.shape, b.shape
    elif mode == "nt":
        (m, kk), (n, _) = a.shape, b.shape
    else:
        (kk, m), (_, n) = a.shape, b.shape
    tm, tn, tk = _tile(m, tm, 128), _tile(n, tn, 128), _tile(kk, tk, 128)
    nk = kk // tk
    dotf = {"nn": _dot_nn, "nt": _dot_nt, "tn": _dot_tn}[mode]
    a_spec = pl.BlockSpec((tk, tm), lambda i, j, k: (k, i)) if mode == "tn" else pl.BlockSpec((tm, tk), lambda i, j, k: (i, k))
    b_spec = pl.BlockSpec((tn, tk), lambda i, j, k: (j, k)) if mode == "nt" else pl.BlockSpec((tk, tn), lambda i, j, k: (k, j))
    e_specs = [pl.BlockSpec((tm, tn), lambda i, j, k: (i, j)) if kind == "mn" else pl.BlockSpec((1, tn), lambda i, j, k: (0, j))
               for _, kind in extras]
    ne, no = len(extras), len(outs)
    epi = epilogue or (lambda acc: (acc,))

    def body(a_ref, b_ref, *rest):
        e_refs, o_refs = rest[:ne], rest[ne:ne + no]

        def finish(acc):
            for o_ref, o in zip(o_refs, epi(acc, *[r[...] for r in e_refs])):
                o_ref[...] = o.astype(o_ref.dtype)

        if nk == 1:
            finish(dotf(a_ref[...], b_ref[...]))
            return
        acc_ref = rest[-1]
        k = pl.program_id(2)

        @pl.when(k == 0)
        def _():
            acc_ref[...] = jnp.zeros_like(acc_ref)

        acc_ref[...] += dotf(a_ref[...], b_ref[...])

        @pl.when(k == nk - 1)
        def _():
            finish(acc_ref[...])

    res = pl.pallas_call(
        body, name=name, grid=(m // tm, n // tn, nk),
        in_specs=[a_spec, b_spec] + e_specs,
        out_specs=[pl.BlockSpec((tm, tn), lambda i, j, k: (i, j)) for _ in outs],
        out_shape=[jax.ShapeDtypeStruct((m, n), dt) for dt in outs],
        scratch_shapes=[pltpu.VMEM((tm, tn), F32)] if nk > 1 else [],
        compiler_params=_cparams("parallel", "parallel", "arbitrary"),
    )(a, b, *[e for e, _ in extras])
    return res[0] if no == 1 else res


def _norm_mod_fwd(x, w, sc, sh):
    s, d = x.shape
    tm = _tile(s, 512, 8)

    def body(x_ref, w_ref, sc_ref, sh_ref, h_ref):
        xv = x_ref[...]
        r = lax.rsqrt(jnp.mean(xv * xv, axis=-1, keepdims=True) + EPS)
        h_ref[...] = ((xv * r) * w_ref[...] * (1.0 + sc_ref[...]) + sh_ref[...]).astype(BF16)

    row = pl.BlockSpec((1, d), lambda i: (0, 0))
    return pl.pallas_call(
        body, name="norm_mod_fwd", grid=(s // tm,),
        in_specs=[pl.BlockSpec((tm, d), lambda i: (i, 0)), row, row, row],
        out_specs=pl.BlockSpec((tm, d), lambda i: (i, 0)),
        out_shape=jax.ShapeDtypeStruct((s, d), BF16),
        compiler_params=_cparams("parallel"),
    )(x, w, sc, sh)


def _colsum8(t):
    tm, d = t.shape
    return jnp.sum(t.reshape(tm // 8, 8, d), axis=0)


def _norm_mod_bwd(x, dh, w, sc, dres, branch=None, g=None):
    s, d = x.shape
    tm = _tile(s, 256, 8)
    nsteps = s // tm
    gated = branch is not None

    def body(*refs):
        if gated:
            x_ref, dh_ref, w_ref, sc_ref, dres_ref, br_ref, g_ref, dx_ref, dbr_ref, dsh_ref, dsc_ref, dw_ref, dg_ref, acc = refs
        else:
            x_ref, dh_ref, w_ref, sc_ref, dres_ref, dx_ref, dsh_ref, dsc_ref, dw_ref, acc = refs
        i = pl.program_id(0)

        @pl.when(i == 0)
        def _():
            acc[...] = jnp.zeros_like(acc)

        xv, dhv, wv, scv = x_ref[...], dh_ref[...].astype(F32), w_ref[...], sc_ref[...]
        r = lax.rsqrt(jnp.mean(xv * xv, axis=-1, keepdims=True) + EPS)
        xn = xv * r
        dxn = dhv * (wv * (1.0 + scv))
        dx = dres_ref[...] + r * (dxn - xn * jnp.mean(dxn * xn, axis=-1, keepdims=True))
        dx_ref[...] = dx
        acc[0] += _colsum8(dhv)
        acc[1] += _colsum8(dhv * xn)
        if gated:
            dbr_ref[...] = (dx * g_ref[...]).astype(BF16)
            acc[2] += _colsum8(dx * br_ref[...].astype(F32))

        @pl.when(i == nsteps - 1)
        def _():
            a0 = jnp.sum(acc[0], axis=0, keepdims=True)
            a1 = jnp.sum(acc[1], axis=0, keepdims=True)
            dsh_ref[...] = a0
            dsc_ref[...] = a1 * wv
            dw_ref[...] = a1 * (1.0 + scv)
            if gated:
                dg_ref[...] = jnp.sum(acc[2], axis=0, keepdims=True)

    big = pl.BlockSpec((tm, d), lambda i: (i, 0))
    row = pl.BlockSpec((1, d), lambda i: (0, 0))
    rowo = jax.ShapeDtypeStruct((1, d), F32)
    in_specs = [big, big, row, row, big] + ([big, row] if gated else [])
    out_specs = [big] + ([big] if gated else []) + [row, row, row] + ([row] if gated else [])
    out_shape = ([jax.ShapeDtypeStruct((s, d), F32)] + ([jax.ShapeDtypeStruct((s, d), BF16)] if gated else [])
                 + [rowo, rowo, rowo] + ([rowo] if gated else []))
    args = [x, dh, w, sc, dres] + ([branch, g] if gated else [])
    return pl.pallas_call(
        body, name="norm_mod_bwd_gated" if gated else "norm_mod_bwd", grid=(nsteps,),
        in_specs=in_specs, out_specs=out_specs, out_shape=out_shape,
        scratch_shapes=[pltpu.VMEM((3, 8, d), F32)],
        compiler_params=_cparams("arbitrary"),
    )(*args)


def _final_loss(x, target, w, branch, g):
    s, d = x.shape
    tm = _tile(s, 256, 8)
    nsteps = s // tm

    def body(x_ref, t_ref, w_ref, br_ref, g_ref, loss_ref, dx_ref, dbr_ref, dw_ref, dg_ref, acc):
        i = pl.program_id(0)

        @pl.when(i == 0)
        def _():
            acc[...] = jnp.zeros_like(acc)

        xv, wv = x_ref[...], w_ref[...]
        r = lax.rsqrt(jnp.mean(xv * xv, axis=-1, keepdims=True) + EPS)
        xn = xv * r
        err = xn * wv - t_ref[...]
        dy = err * (1.0 / d)
        dxn = dy * wv
        dx = r * (dxn - xn * jnp.mean(dxn * xn, axis=-1, keepdims=True))
        dx_ref[...] = dx
        dbr_ref[...] = (dx * g_ref[...]).astype(BF16)
        acc[0] += _colsum8(err * err)
        acc[1] += _colsum8(dy * xn)
        acc[2] += _colsum8(dx * br_ref[...].astype(F32))

        @pl.when(i == nsteps - 1)
        def _():
            loss_ref[...] = jnp.sum(jnp.sum(acc[0], axis=0, keepdims=True), axis=1, keepdims=True) * (0.5 / d)
            dw_ref[...] = jnp.sum(acc[1], axis=0, keepdims=True)
            dg_ref[...] = jnp.sum(acc[2], axis=0, keepdims=True)

    big = pl.BlockSpec((tm, d), lambda i: (i, 0))
    row = pl.BlockSpec((1, d), lambda i: (0, 0))
    rowo = jax.ShapeDtypeStruct((1, d), F32)
    return pl.pallas_call(
        body, name="final_loss", grid=(nsteps,),
        in_specs=[big, big, row, big, row],
        out_specs=[pl.BlockSpec((1, 1), lambda i: (0, 0)), big, big, row, row],
        out_shape=[jax.ShapeDtypeStruct((1, 1), F32), jax.ShapeDtypeStruct((s, d), F32),
                   jax.ShapeDtypeStruct((s, d), BF16), rowo, rowo],
        scratch_shapes=[pltpu.VMEM((3, 8, d), F32)],
        compiler_params=_cparams("arbitrary"),
    )(x, target, w, branch, g)


def _rope_tables(pos_col):
    s = pos_col.shape[0]
    tq = _tile(s, 1024, 8)
    half = ROPE_DIM // 2
    inv = np.float32(ROPE_THETA) ** (-np.arange(0, ROPE_DIM, 2, dtype=np.float32) / np.float32(ROPE_DIM))
    inv_row = jnp.asarray(np.tile(inv.astype(np.float32), HEAD // half)[None, :])

    def body(p_ref, inv_ref, ct_ref, st_ref):
        lane = lax.broadcasted_iota(jnp.int32, (tq, HEAD), 1)
        ang = p_ref[...].astype(F32) * inv_ref[...]
        cs, sn = jnp.cos(ang), jnp.sin(ang)
        ct_ref[...] = jnp.where(lane < ROPE_DIM, cs, 1.0)
        st_ref[...] = jnp.where(lane < half, -sn, jnp.where(lane < ROPE_DIM, sn, 0.0))

    blk = pl.BlockSpec((tq, HEAD), lambda i: (i, 0))
    return pl.pallas_call(
        body, name="rope_tables", grid=(s // tq,),
        in_specs=[pl.BlockSpec((tq, 1), lambda i: (i, 0)), pl.BlockSpec((1, HEAD), lambda i: (0, 0))],
        out_specs=[blk, blk],
        out_shape=[jax.ShapeDtypeStruct((s, HEAD), F32)] * 2,
        compiler_params=_cparams("parallel"),
    )(pos_col, inv_row)


def _swap_halves(x):
    lane = lax.broadcasted_iota(jnp.int32, x.shape, 1)
    half = ROPE_DIM // 2
    return jnp.where(lane < half, pltpu.roll(x, HEAD - half, 1), pltpu.roll(x, half, 1))


def _rope(x, ct, st):
    return x * ct + _swap_halves(x) * st


def _rope_t(dy, ct, st):
    return dy * ct - _swap_halves(dy) * st


def _band_bias(bias_ref):
    qi = lax.broadcasted_iota(jnp.int32, (CHUNK, 2 * CHUNK), 0)
    kj = lax.broadcasted_iota(jnp.int32, (CHUNK, 2 * CHUNK), 1)
    band = (kj >= qi) & (kj <= qi + CHUNK)
    bias_ref[0] = jnp.where(band, 0.0, NEG)
    bias_ref[1] = jnp.where(band & (kj >= CHUNK), 0.0, NEG)


def _unit_starts(u, d):
    shift = int(math.log2(d))
    sb, r = u >> shift, u & (d - 1)
    qs = sb * (CHUNK * d) + r
    ks = ATT_CHUNK + (sb - 1) * (CHUNK * d) + r
    if d == 1:
        qs, ks = pl.multiple_of(qs, CHUNK), pl.multiple_of(ks, CHUNK)
    return sb, qs, ks


def _rows(start, size, d):
    return pl.ds(start, size) if d == 1 else pl.ds(start, size, stride=d)


def _attn_specs(nh, nc):
    cur = lambda off: pl.BlockSpec((ATT_CHUNK, HEAD), lambda h, c: (jnp.minimum(c, nc - 1), off + h))
    prev = lambda off: pl.BlockSpec((ATT_CHUNK, HEAD), lambda h, c: (jnp.maximum(c - 1, 0), off + h))
    tcur = pl.BlockSpec((ATT_CHUNK, HEAD), lambda h, c: (jnp.minimum(c, nc - 1), 0))
    tprev = pl.BlockSpec((ATT_CHUNK, HEAD), lambda h, c: (jnp.maximum(c - 1, 0), 0))
    return [cur(0), prev(nh), cur(nh), prev(2 * nh), cur(2 * nh), tcur, tcur, tprev, tprev]


def _attn_fwd(z, ct, st, nh):
    s = z.shape[0]
    nc = s // ATT_CHUNK
    scale = HEAD ** -0.5

    def body(q_ref, kp_ref, kc_ref, vp_ref, vc_ref, ctc, stc, ctp, stp, o_ref, lse_ref, qf, kf, vf, ob, lb, bias):
        c = pl.program_id(1)
        qf[...] = _rope(q_ref[...].astype(F32), ctc[...], stc[...])
        kf[0:ATT_CHUNK] = _rope(kp_ref[...].astype(F32), ctp[...], stp[...])
        kf[ATT_CHUNK:] = _rope(kc_ref[...].astype(F32), ctc[...], stc[...])
        vf[0:ATT_CHUNK] = vp_ref[...].astype(F32)
        vf[ATT_CHUNK:] = vc_ref[...].astype(F32)
        _band_bias(bias)
        for b, d in enumerate(DILATIONS):
            def unit(u, carry, b=b, d=d):
                sb, qs, ks = _unit_starts(u, d)
                q_u = qf[_rows(qs, CHUNK, d), :].astype(BF16)
                k_u = kf[_rows(ks, 2 * CHUNK, d), :].astype(BF16)
                v_u = vf[_rows(ks, 2 * CHUNK, d), :].astype(BF16)
                first = jnp.where((c == 0) & (sb == 0), 1, 0)
                sc = _dot_nt(q_u, k_u) * scale + bias[first]
                m = jnp.max(sc, axis=1, keepdims=True)
                p = jnp.exp(sc - m)
                l = jnp.sum(p, axis=1, keepdims=True)
                ob[b, _rows(qs, CHUNK, d), :] = _dot_nn(p.astype(BF16), v_u) / l
                lb[b, _rows(qs, CHUNK, d), :] = jnp.broadcast_to(m + jnp.log(l), (CHUNK, HEAD))
                return carry
            lax.fori_loop(0, ATT_CHUNK // CHUNK, unit, 0, unroll=ATT_UNROLL_FWD)
        mx = jnp.maximum(jnp.maximum(lb[0], lb[1]), lb[2])
        e0, e1, e2 = jnp.exp(lb[0] - mx), jnp.exp(lb[1] - mx), jnp.exp(lb[2] - mx)
        den = e0 + e1 + e2
        o_ref[...] = ((e0 * ob[0] + e1 * ob[1] + e2 * ob[2]) / den).astype(BF16)
        lse_ref[...] = mx + jnp.log(den)

    blk = pl.BlockSpec((ATT_CHUNK, HEAD), lambda h, c: (c, h))
    return pl.pallas_call(
        body, name="attn_fwd", grid=(nh, nc),
        in_specs=_attn_specs(nh, nc), out_specs=[blk, blk],
        out_shape=[jax.ShapeDtypeStruct((s, nh * HEAD), BF16), jax.ShapeDtypeStruct((s, nh * HEAD), F32)],
        scratch_shapes=[pltpu.VMEM((ATT_CHUNK, HEAD), F32), pltpu.VMEM((2 * ATT_CHUNK, HEAD), F32),
                        pltpu.VMEM((2 * ATT_CHUNK, HEAD), F32), pltpu.VMEM((3, ATT_CHUNK, HEAD), F32),
                        pltpu.VMEM((3, ATT_CHUNK, HEAD), F32), pltpu.VMEM((2, CHUNK, 2 * CHUNK), F32)],
        compiler_params=_cparams("parallel", "arbitrary"),
    )(z, z, z, z, z, ct, st, ct, st)


def _attn_bwd(z, ct, st, da, o, lse, nh):
    s = z.shape[0]
    nc = s // ATT_CHUNK
    scale = HEAD ** -0.5

    def body(q_ref, kp_ref, kc_ref, vp_ref, vc_ref, ctc, stc, ctp, stp, do_ref, o_ref, lse_ref,
             dq_ref, dk_ref, dv_ref, qf, kf, vf, dof, dbar, dqa, dkf, dvf, bias):
        c = pl.program_id(1)

        @pl.when(c == 0)
        def _():
            dkf[...] = jnp.zeros_like(dkf)
            dvf[...] = jnp.zeros_like(dvf)

        @pl.when(c > 0)
        def _():
            dkf[0:ATT_CHUNK] = dkf[ATT_CHUNK:]
            dvf[0:ATT_CHUNK] = dvf[ATT_CHUNK:]
            dkf[ATT_CHUNK:] = jnp.zeros((ATT_CHUNK, HEAD), F32)
            dvf[ATT_CHUNK:] = jnp.zeros((ATT_CHUNK, HEAD), F32)

        @pl.when(c < nc)
        def _():
            qf[...] = _rope(q_ref[...].astype(F32), ctc[...], stc[...])
            kf[0:ATT_CHUNK] = _rope(kp_ref[...].astype(F32), ctp[...], stp[...])
            kf[ATT_CHUNK:] = _rope(kc_ref[...].astype(F32), ctc[...], stc[...])
            vf[0:ATT_CHUNK] = vp_ref[...].astype(F32)
            vf[ATT_CHUNK:] = vc_ref[...].astype(F32)
            dov = do_ref[...].astype(F32)
            dof[...] = dov
            dbar[...] = jnp.broadcast_to(jnp.sum(dov * o_ref[...].astype(F32), axis=1, keepdims=True), (ATT_CHUNK, HEAD))
            dqa[...] = jnp.zeros_like(dqa)
            _band_bias(bias)
            for d in DILATIONS:
                def unit(u, carry, d=d):
                    sb, qs, ks = _unit_starts(u, d)
                    qr, kr = _rows(qs, CHUNK, d), _rows(ks, 2 * CHUNK, d)
                    q_u = qf[qr, :].astype(BF16)
                    k_u = kf[kr, :].astype(BF16)
                    v_u = vf[kr, :].astype(BF16)
                    do_u = dof[qr, :].astype(BF16)
                    lse_u = lse_ref[qr, :]
                    dbar_u = dbar[qr, :]
                    first = jnp.where((c == 0) & (sb == 0), 1, 0)
                    sc = _dot_nt(q_u, k_u) * scale + bias[first]
                    p = jnp.exp(sc - jnp.concatenate([lse_u, lse_u], axis=1))
                    dp = _dot_nt(do_u, v_u)
                    ds = (p * (dp - jnp.concatenate([dbar_u, dbar_u], axis=1)) * scale).astype(BF16)
                    dqa[qr, :] += _dot_nn(ds, k_u)
                    dkf[kr, :] += _dot_tn(ds, q_u)
                    dvf[kr, :] += _dot_tn(p.astype(BF16), do_u)
                    return carry
                lax.fori_loop(0, ATT_CHUNK // CHUNK, unit, 0, unroll=ATT_UNROLL_BWD)
            dq_ref[...] = _rope_t(dqa[...], ctc[...], stc[...]).astype(BF16)

        @pl.when(c > 0)
        def _():
            dk_ref[...] = _rope_t(dkf[0:ATT_CHUNK], ctp[...], stp[...]).astype(BF16)
            dv_ref[...] = dvf[0:ATT_CHUNK].astype(BF16)

    cur = pl.BlockSpec((ATT_CHUNK, HEAD), lambda h, c: (jnp.minimum(c, nc - 1), h))
    late = pl.BlockSpec((ATT_CHUNK, HEAD), lambda h, c: (jnp.maximum(c - 1, 0), h))
    shp = jax.ShapeDtypeStruct((s, nh * HEAD), BF16)
    big = pltpu.VMEM((2 * ATT_CHUNK, HEAD), F32)
    one = pltpu.VMEM((ATT_CHUNK, HEAD), F32)
    return pl.pallas_call(
        body, name="attn_bwd", grid=(nh, nc + 1),
        in_specs=_attn_specs(nh, nc) + [cur, cur, cur], out_specs=[cur, late, late],
        out_shape=[shp, shp, shp],
        scratch_shapes=[one, big, big, one, one, one, big, big, pltpu.VMEM((2, CHUNK, 2 * CHUNK), F32)],
        compiler_params=_cparams("parallel", "arbitrary"),
    )(z, z, z, z, z, ct, st, ct, st, da, o, lse)


_GELU_K = math.sqrt(2.0 / math.pi)


def _gelu(x):
    return 0.5 * x * (1.0 + jnp.tanh(_GELU_K * (x + 0.044715 * x * x * x)))


def _gelu_and_grad(x):
    t = jnp.tanh(_GELU_K * (x + 0.044715 * x * x * x))
    g = 0.5 * x * (1.0 + t)
    dg = 0.5 * (1.0 + t) + 0.5 * x * (1.0 - t * t) * (_GELU_K * (1.0 + 3 * 0.044715 * x * x))
    return g, dg


def _tril(w):
    ti = lax.broadcasted_iota(jnp.int32, (CHUNK, CHUNK), 0)
    si = lax.broadcasted_iota(jnp.int32, (CHUNK, CHUNK), 1)
    return jnp.where(si <= ti, w, 0.0)


def _sgu_fwd(z, w_s, b_col, ng, u_blk):
    s = z.shape[0]
    gw = ng * HEAD
    tq = _tile(s, 1024, CHUNK)

    def body(u_ref, v_ref, w_ref, b_ref, o_ref):
        for g in range(ng):
            wg = _tril(w_ref[g]).astype(BF16)
            cols = slice(g * HEAD, (g + 1) * HEAD)
            for n in range(tq // CHUNK):
                rows = slice(n * CHUNK, (n + 1) * CHUNK)
                gv = _gelu(v_ref[rows, cols].astype(F32)).astype(BF16)
                mixed = _dot_nn(wg, gv) + b_ref[g]
                o_ref[rows, cols] = (_gelu(u_ref[rows, cols].astype(F32)) * mixed).astype(BF16)

    full = pl.BlockSpec((ng, CHUNK, CHUNK), lambda i: (0, 0, 0))
    return pl.pallas_call(
        body, name="sgu_fwd", grid=(s // tq,),
        in_specs=[pl.BlockSpec((tq, gw), lambda i: (i, u_blk)), pl.BlockSpec((tq, gw), lambda i: (i, u_blk + 1)), full, full],
        out_specs=pl.BlockSpec((tq, gw), lambda i: (i, 0)),
        out_shape=jax.ShapeDtypeStruct((s, gw), BF16),
        compiler_params=_cparams("parallel"),
    )(z, z, w_s, b_col)


def _sgu_bwd(z, w_s, b_col, da, ng, u_blk, da_blk):
    s = z.shape[0]
    gw = ng * HEAD
    tq = _tile(s, 1024, CHUNK)
    nsteps = s // tq

    def body(u_ref, v_ref, w_ref, b_ref, do_ref, du_ref, dv_ref, dw_ref, db_ref):
        i = pl.program_id(0)

        @pl.when(i == 0)
        def _():
            dw_ref[...] = jnp.zeros_like(dw_ref)
            db_ref[...] = jnp.zeros_like(db_ref)

        for g in range(ng):
            wg = _tril(w_ref[g]).astype(BF16)
            cols = slice(g * HEAD, (g + 1) * HEAD)
            dw_acc = jnp.zeros((CHUNK, CHUNK), F32)
            db_acc = jnp.zeros((CHUNK, 1), F32)
            for n in range(tq // CHUNK):
                rows = slice(n * CHUNK, (n + 1) * CHUNK)
                gu, dgu = _gelu_and_grad(u_ref[rows, cols].astype(F32))
                gv, dgv = _gelu_and_grad(v_ref[rows, cols].astype(F32))
                gvb = gv.astype(BF16)
                mixed = _dot_nn(wg, gvb) + b_ref[g]
                dout = do_ref[rows, cols].astype(F32)
                du_ref[rows, cols] = (dout * mixed * dgu).astype(BF16)
                dmix = dout * gu
                dmb = dmix.astype(BF16)
                dv_ref[rows, cols] = (_dot_tn(wg, dmb) * dgv).astype(BF16)
                dw_acc += _dot_nt(dmb, gvb)
                db_acc += jnp.sum(dmix, axis=1, keepdims=True)
            dw_ref[g] += _tril(dw_acc)
            db_ref[g] += jnp.broadcast_to(db_acc, (CHUNK, CHUNK))

    full = pl.BlockSpec((ng, CHUNK, CHUNK), lambda i: (0, 0, 0))
    out = pl.BlockSpec((tq, gw), lambda i: (i, 0))
    return pl.pallas_call(
        body, name="sgu_bwd", grid=(nsteps,),
        in_specs=[pl.BlockSpec((tq, gw), lambda i: (i, u_blk)), pl.BlockSpec((tq, gw), lambda i: (i, u_blk + 1)), full, full,
                  pl.BlockSpec((tq, gw), lambda i: (i, da_blk))],
        out_specs=[out, out, full, full],
        out_shape=[jax.ShapeDtypeStruct((s, gw), BF16)] * 2 + [jax.ShapeDtypeStruct((ng, CHUNK, CHUNK), F32)] * 2,
        compiler_params=_cparams("arbitrary"),
    )(z, z, w_s, b_col, da)


def _shift_down(y, halo, k):
    rolled = pltpu.roll(y, k, 0)
    row = lax.broadcasted_iota(jnp.int32, y.shape, 0)
    for j in range(k):
        rolled = jnp.where(row == j, halo[8 - k + j:8 - k + j + 1, :], rolled)
    return rolled


def _shift_up(y, halo, k):
    n = y.shape[0]
    rolled = pltpu.roll(y, n - k, 0)
    row = lax.broadcasted_iota(jnp.int32, y.shape, 0)
    for j in range(k):
        rolled = jnp.where(row == n - k + j, halo[j:j + 1, :], rolled)
    return rolled


def _conv_fwd(z, cw):
    s, d3 = z.shape
    d = d3 // 3
    tq = _tile(s, 256, 8)

    def body(z_ref, zh_ref, cw_ref, a_ref):
        i = pl.program_id(0)
        zv = z_ref[...].astype(F32)
        zh = jnp.where(i > 0, zh_ref[...].astype(F32), 0.0)
        y = zv[:, d:2 * d] * zv[:, 2 * d:]
        yh = zh[:, d:2 * d] * zh[:, 2 * d:]
        cwv = cw_ref[...]
        conv = cwv[0:1] * _shift_down(y, yh, 2) + cwv[1:2] * _shift_down(y, yh, 1) + cwv[2:3] * y
        a_ref[...] = (zv[:, :d] * conv).astype(BF16)

    return pl.pallas_call(
        body, name="conv_fwd", grid=(s // tq,),
        in_specs=[pl.BlockSpec((tq, d3), lambda i: (i, 0)),
                  pl.BlockSpec((8, d3), lambda i: (jnp.maximum(i * (tq // 8) - 1, 0), 0)),
                  pl.BlockSpec((3, d), lambda i: (0, 0))],
        out_specs=pl.BlockSpec((tq, d), lambda i: (i, 0)),
        out_shape=jax.ShapeDtypeStruct((s, d), BF16),
        compiler_params=_cparams("parallel"),
    )(z, z, cw)


def _conv_bwd(z, cw, da):
    s, d3 = z.shape
    d = d3 // 3
    tq = _tile(s, 128, 8)
    nsteps = s // tq
    nblk8 = s // 8

    def body(z_ref, zp_ref, zn_ref, da_ref, dan_ref, cw_ref, dz_ref, dcw_ref, acc):
        i = pl.program_id(0)

        @pl.when(i == 0)
        def _():
            acc[...] = jnp.zeros_like(acc)

        zv = z_ref[...].astype(F32)
        zp = jnp.where(i > 0, zp_ref[...].astype(F32), 0.0)
        zn = jnp.where(i < nsteps - 1, zn_ref[...].astype(F32), 0.0)
        dav = da_ref[...].astype(F32)
        dan = jnp.where(i < nsteps - 1, dan_ref[...].astype(F32), 0.0)
        gb, gc, hx = zv[:, :d], zv[:, d:2 * d], zv[:, 2 * d:]
        y = gc * hx
        yp = zp[:, d:2 * d] * zp[:, 2 * d:]
        cwv = cw_ref[...]
        y1, y2 = _shift_down(y, yp, 1), _shift_down(y, yp, 2)
        conv = cwv[0:1] * y2 + cwv[1:2] * y1 + cwv[2:3] * y
        dconv = dav * gb
        dconv_n = dan * zn[:, :d]
        dy = cwv[2:3] * dconv + cwv[1:2] * _shift_up(dconv, dconv_n, 1) + cwv[0:1] * _shift_up(dconv, dconv_n, 2)
        dz_ref[:, :d] = (dav * conv).astype(BF16)
        dz_ref[:, d:2 * d] = (dy * hx).astype(BF16)
        dz_ref[:, 2 * d:] = (dy * gc).astype(BF16)
        acc[0] += _colsum8(dconv * y2)
        acc[1] += _colsum8(dconv * y1)
        acc[2] += _colsum8(dconv * y)

        @pl.when(i == nsteps - 1)
        def _():
            for j in range(3):
                dcw_ref[j:j + 1, :] = jnp.sum(acc[j], axis=0, keepdims=True)

    return pl.pallas_call(
        body, name="conv_bwd", grid=(nsteps,),
        in_specs=[pl.BlockSpec((tq, d3), lambda i: (i, 0)),
                  pl.BlockSpec((8, d3), lambda i: (jnp.maximum(i * (tq // 8) - 1, 0), 0)),
                  pl.BlockSpec((8, d3), lambda i: (jnp.minimum((i + 1) * (tq // 8), nblk8 - 1), 0)),
                  pl.BlockSpec((tq, d), lambda i: (i, 0)),
                  pl.BlockSpec((8, d), lambda i: (jnp.minimum((i + 1) * (tq // 8), nblk8 - 1), 0)),
                  pl.BlockSpec((3, d), lambda i: (0, 0))],
        out_specs=[pl.BlockSpec((tq, d3), lambda i: (i, 0)), pl.BlockSpec((3, d), lambda i: (0, 0))],
        out_shape=[jax.ShapeDtypeStruct((s, d3), BF16), jax.ShapeDtypeStruct((3, d), F32)],
        scratch_shapes=[pltpu.VMEM((3, 8, d), F32)],
        compiler_params=_cparams("arbitrary"),
    )(z, z, z, da, da, cw)


def _swiglu_fwd(gt, up):
    s, f = gt.shape
    tm, tn = _tile(s, 512, 8), _tile(f, 2816, 128)

    def body(g_ref, u_ref, a_ref):
        g = g_ref[...].astype(F32)
        a_ref[...] = (g / (1.0 + jnp.exp(-g)) * u_ref[...].astype(F32)).astype(BF16)

    blk = pl.BlockSpec((tm, tn), lambda i, j: (i, j))
    return pl.pallas_call(
        body, name="swiglu_fwd", grid=(s // tm, f // tn), in_specs=[blk, blk], out_specs=blk,
        out_shape=jax.ShapeDtypeStruct((s, f), BF16), compiler_params=_cparams("parallel", "parallel"),
    )(gt, up)


def _swiglu_bwd_epilogue(dact, gt, up):
    g, u = gt.astype(F32), up.astype(F32)
    sg = 1.0 / (1.0 + jnp.exp(-g))
    return dact * u * (sg * (1.0 + g * (1.0 - sg))), dact * (g * sg)


def _ada_fwd(c_all, ada_w):
    nl, d, n8 = ada_w.shape
    tn = _tile(n8, 768, 128)

    def body(c_ref, w_ref, o_ref):
        cv = c_ref[...]
        act = (cv / (1.0 + jnp.exp(-cv))).astype(BF16)
        o_ref[0] = _dot_nn(act, w_ref[0].astype(BF16))

    return pl.pallas_call(
        body, name="ada_fwd", grid=(nl, n8 // tn),
        in_specs=[pl.BlockSpec((N_DEV, d), lambda l, j: (0, 0)), pl.BlockSpec((1, d, tn), lambda l, j: (l, 0, j))],
        out_specs=pl.BlockSpec((1, N_DEV, tn), lambda l, j: (l, 0, j)),
        out_shape=jax.ShapeDtypeStruct((nl, N_DEV, n8), F32),
        compiler_params=_cparams("parallel", "parallel"),
    )(c_all, ada_w)


def _ada_wgrad(c_all, dmod_cols):
    nl, _, n8 = dmod_cols.shape
    d = c_all.shape[1]
    tn = _tile(n8, 768, 128)

    def body(c_ref, g_ref, o_ref):
        cv = c_ref[...]
        act = (cv / (1.0 + jnp.exp(-cv))).astype(BF16)
        o_ref[0] = _dot_tn(act, g_ref[0].astype(BF16))

    return pl.pallas_call(
        body, name="ada_wgrad", grid=(nl, n8 // tn),
        in_specs=[pl.BlockSpec((N_DEV, d), lambda l, j: (0, 0)), pl.BlockSpec((1, N_DEV, tn), lambda l, j: (l, 0, j))],
        out_specs=pl.BlockSpec((1, d, tn), lambda l, j: (l, 0, j)),
        out_shape=jax.ShapeDtypeStruct((nl, d, n8), F32),
        compiler_params=_cparams("parallel", "parallel"),
    )(c_all, dmod_cols)


def _adamw(name, pieces, w, m, v):
    npc, r, c = pieces.shape
    tr = _tile(r, max(8, (1 << 19) // c // 8 * 8), 8)
    bc1, bc2 = 1.0 - ADAM_B1 ** ADAM_STEP, 1.0 - ADAM_B2 ** ADAM_STEP

    def body(p_ref, w_ref, m_ref, v_ref, g_ref, d_ref, nm_ref, nv_ref):
        g = p_ref[0].astype(F32)
        for i in range(1, npc):
            g = g + p_ref[i].astype(F32)
        nm = ADAM_B1 * m_ref[...] + (1.0 - ADAM_B1) * g
        nv = ADAM_B2 * v_ref[...] + (1.0 - ADAM_B2) * (g * g)
        g_ref[...] = g
        nm_ref[...] = nm
        nv_ref[...] = nv
        d_ref[...] = -ADAM_LR * ((nm / bc1) / (jnp.sqrt(nv / bc2) + ADAM_EPS) + ADAM_WD * w_ref[...])

    blk = pl.BlockSpec((tr, c), lambda i: (i, 0))
    return pl.pallas_call(
        body, name=name, grid=(r // tr,),
        in_specs=[pl.BlockSpec((npc, tr, c), lambda i: (0, i, 0)), blk, blk, blk],
        out_specs=[blk] * 4, out_shape=[jax.ShapeDtypeStruct((r, c), F32)] * 4,
        compiler_params=_cparams("parallel"),
    )(pieces, w, m, v)


def _place():
    x, y, c = lax.axis_index("x"), lax.axis_index("y"), lax.axis_index("c")
    return x, y, c


def _all_gather_small(name, x_shard):
    m_per, n = x_shard.shape

    def body(x_ref, out_ref, send_sems, recv_sems, local_sem):
        x, y, c = _place()
        me, sibling = (x, y, c), (x, y, 1 - c)
        chips = [(1 - x, y), (x, 1 - y), (1 - x, 1 - y)]

        def rows(px, py, pc):
            return out_ref.at[pl.ds((4 * px + 2 * py + pc) * m_per, m_per), :]

        def copy(k, block, to, src=None):
            return pltpu.make_async_remote_copy(
                src_ref=rows(*block) if src is None else src, dst_ref=rows(*block),
                send_sem=send_sems.at[k], recv_sem=recv_sems.at[k], device_id=to, device_id_type=MESH)

        mine = pltpu.make_async_copy(x_ref, rows(*me), local_sem)
        mine.start()
        first = [copy(0, me, sibling, src=x_ref)]
        first += [copy(1 + j, me, (*chip, c), src=x_ref) for j, chip in enumerate(chips)]
        for cp in first:
            cp.start()
        passed = [copy(4 + j, (*chip, c), sibling) for j, chip in enumerate(chips)]
        for j, chip in enumerate(chips):
            copy(1 + j, (*chip, c), me).wait_recv()
            passed[j].start()
        copy(0, sibling, me).wait_recv()
        for j, chip in enumerate(chips):
            copy(4 + j, (*chip, 1 - c), me).wait_recv()
        for cp in first + passed:
            cp.wait_send()
        mine.wait()

    return pl.pallas_call(
        body, name=name,
        out_shape=jax.ShapeDtypeStruct((N_DEV * m_per, n), x_shard.dtype),
        in_specs=[pl.BlockSpec(memory_space=pltpu.VMEM)], out_specs=pl.BlockSpec(memory_space=pltpu.VMEM),
        scratch_shapes=[pltpu.SemaphoreType.DMA((7,)), pltpu.SemaphoreType.DMA((7,)), pltpu.SemaphoreType.DMA],
        compiler_params=pltpu.CompilerParams(vmem_limit_bytes=VMEM_LIMIT),
    )(x_shard)


def _all_gather_big(name, shards):
    na = len(shards)

    def body(*refs):
        x_refs, out_refs = refs[:na], refs[na:2 * na]
        send_sems, recv_sems, local_sems = refs[2 * na:]
        x, y, c = _place()
        me, sibling = (x, y, c), (x, y, 1 - c)
        chips = [(1 - x, y), (x, 1 - y), (1 - x, 1 - y)]

        def slot(a, px, py, pc):
            return out_refs[a].at[4 * px + 2 * py + pc]

        def copy(a, k, block, to, src=None):
            return pltpu.make_async_remote_copy(
                src_ref=slot(a, *block) if src is None else src, dst_ref=slot(a, *block),
                send_sem=send_sems.at[a, k], recv_sem=recv_sems.at[a, k], device_id=to, device_id_type=MESH)

        mine = [pltpu.make_async_copy(x_refs[a], slot(a, *me), local_sems.at[a]) for a in range(na)]
        for cp in mine:
            cp.start()
        first = []
        for a in range(na):
            first.append(copy(a, 0, me, sibling, src=x_refs[a]))
            first += [copy(a, 1 + j, me, (*chip, c), src=x_refs[a]) for j, chip in enumerate(chips)]
        for cp in first:
            cp.start()
        passed = []
        for a in range(na):
            for j, chip in enumerate(chips):
                copy(a, 1 + j, (*chip, c), me).wait_recv()
                fwd = copy(a, 4 + j, (*chip, c), sibling)
                fwd.start()
                passed.append(fwd)
        for a in range(na):
            copy(a, 0, sibling, me).wait_recv()
            for j, chip in enumerate(chips):
                copy(a, 4 + j, (*chip, 1 - c), me).wait_recv()
        for cp in first + passed:
            cp.wait_send()
        for cp in mine:
            cp.wait()

    hbm = pl.BlockSpec(memory_space=pl.ANY)
    return pl.pallas_call(
        body, name=name,
        out_shape=[jax.ShapeDtypeStruct((N_DEV,) + s.shape, s.dtype) for s in shards],
        in_specs=[hbm] * na, out_specs=[hbm] * na,
        scratch_shapes=[pltpu.SemaphoreType.DMA((na, 7)), pltpu.SemaphoreType.DMA((na, 7)), pltpu.SemaphoreType.DMA((na,))],
    )(*shards)


def _exchange_shards(name, grads):
    na = len(grads)

    def body(*refs):
        g_refs, out_refs = refs[:na], refs[na:2 * na]
        send_sems, recv_sems, local_sems = refs[2 * na:]
        x, y, c = _place()
        me = 4 * x + 2 * y + c

        def peer(k):
            px = 1 - x if k & 4 else x
            py = 1 - y if k & 2 else y
            pc = 1 - c if k & 1 else c
            return (px, py, pc), 4 * px + 2 * py + pc

        def copy(a, k):
            dev, idx = peer(k)
            return pltpu.make_async_remote_copy(
                src_ref=g_refs[a].at[idx], dst_ref=out_refs[a].at[me],
                send_sem=send_sems.at[a, k - 1], recv_sem=recv_sems.at[a, k - 1], device_id=dev, device_id_type=MESH)

        def arrival(a, k):
            dev, idx = peer(k)
            return pltpu.make_async_remote_copy(
                src_ref=g_refs[a].at[idx], dst_ref=out_refs[a].at[idx],
                send_sem=send_sems.at[a, k - 1], recv_sem=recv_sems.at[a, k - 1], device_id=dev, device_id_type=MESH)

        mine = [pltpu.make_async_copy(g_refs[a].at[me], out_refs[a].at[me], local_sems.at[a]) for a in range(na)]
        for cp in mine:
            cp.start()
        sends = [copy(a, k) for a in range(na) for k in range(1, N_DEV)]
        for cp in sends:
            cp.start()
        for a in range(na):
            for k in range(1, N_DEV):
                arrival(a, k).wait_recv()
        for cp in sends:
            cp.wait_send()
        for cp in mine:
            cp.wait()

    hbm = pl.BlockSpec(memory_space=pl.ANY)
    return pl.pallas_call(
        body, name=name,
        out_shape=[jax.ShapeDtypeStruct(g.shape, g.dtype) for g in grads],
        in_specs=[hbm] * na, out_specs=[hbm] * na,
        scratch_shapes=[pltpu.SemaphoreType.DMA((na, 7)), pltpu.SemaphoreType.DMA((na, 7)), pltpu.SemaphoreType.DMA((na,))],
    )(*grads)


def _cols_to_plain(g):
    n, k, n8 = g.shape
    return jnp.transpose(g, (1, 0, 2)).reshape(k, n * n8)


def _plain_to_cols(w):
    k, n = w.shape
    return jnp.transpose(w.reshape(k, N_DEV, n // N_DEV), (1, 0, 2))


def _pack_rows(parts, width):
    rows, offs, r = [], [], 0
    for p in parts:
        flat = p.reshape(-1).astype(F32)
        nr = -(-flat.shape[0] // (8 * width)) * 8
        rows.append(jnp.pad(flat, (0, nr * width - flat.shape[0])).reshape(nr, width))
        offs.append((r, flat.shape[0], p.shape))
        r += nr
    return jnp.concatenate(rows, axis=0), offs, r


def _unpack_rows(slab, offs, width):
    lead = slab.shape[:-2]
    out = []
    for r0, n, shape in offs:
        nr = -(-n // width)
        out.append(slab[..., r0:r0 + nr, :].reshape(lead + (nr * width,))[..., :n].reshape(lead + tuple(shape)))
    return out


def kernel(x, c, positions, ada_w, ada_b, norm_mix, norm_ffn, ab_w_in, sgu_w, sgu_b, ab_w_out, conv_w_in, conv_w, conv_w_out, ffn_w_gate, ffn_w_up, ffn_w_down, final_norm, loss_target, m_ada_w, m_ada_b, m_norm_mix, m_norm_ffn, m_ab_w_in, m_sgu_w, m_sgu_b, m_ab_w_out, m_conv_w_in, m_conv_w, m_conv_w_out, m_ffn_w_gate, m_ffn_w_up, m_ffn_w_down, m_final_norm, v_ada_w, v_ada_b, v_norm_mix, v_norm_ffn, v_ab_w_in, v_sgu_w, v_sgu_b, v_ab_w_out, v_conv_w_in, v_conv_w, v_conv_w_out, v_ffn_w_gate, v_ffn_w_up, v_ffn_w_down, v_final_norm):
    xi, yi, ci = _place()
    me = 4 * xi + 2 * yi + ci
    s, d = x.shape[1], x.shape[2]
    depth = ada_w.shape[0]
    n_even = ab_w_in.shape[0]
    nh_mix = d // HEAD
    nh = 3 * nh_mix // 4
    ng = nh_mix - nh
    aw, gw = nh * HEAD, ng * HEAD
    x0 = x[0]
    target = loss_target[0]

    width = 512
    slab, offs, _ = _pack_rows([c, conv_w], width)
    gathered = _all_gather_small("gather_cond", slab).reshape(N_DEV, -1, width)
    c_parts, cw_parts = _unpack_rows(gathered, offs, width)
    c_all = c_parts.reshape(N_DEV, d)
    n_odd, cwid, d8 = conv_w.shape
    conv_w_full = jnp.transpose(cw_parts, (1, 2, 0, 3)).reshape(n_odd, cwid, d)

    mod_cols = _ada_fwd(c_all, ada_w)
    n8 = mod_cols.shape[2]
    mod_all = _all_gather_small("gather_mod", mod_cols.reshape(depth * N_DEV, n8)).reshape(N_DEV, depth, N_DEV, n8)
    mod_mine = lax.dynamic_index_in_dim(mod_all, me, axis=2, keepdims=False)
    mod = jnp.transpose(mod_mine, (1, 0, 2)).reshape(depth, N_DEV * n8) + ada_b
    mods = mod.reshape(depth, 6, 1, d)

    big = [ab_w_in, ab_w_out, conv_w_in, conv_w_out, ffn_w_gate, ffn_w_up, ffn_w_down]
    g_in, g_out, g_cin, g_cout, g_gate, g_up, g_down = _all_gather_big("gather_weights", [w.astype(BF16) for w in big])
    w_in = [_cols_to_plain(g_in[:, i]) for i in range(n_even)]
    w_out = [g_out[:, i].reshape(d, d) for i in range(n_even)]
    w_cin = [_cols_to_plain(g_cin[:, i]) for i in range(n_odd)]
    w_cout = [g_cout[:, i].reshape(d, d) for i in range(n_odd)]
    w_gate = [_cols_to_plain(g_gate[:, l]) for l in range(depth)]
    w_up = [_cols_to_plain(g_up[:, l]) for l in range(depth)]
    ff = w_gate[0].shape[1]
    w_down = [g_down[:, l].reshape(ff, d) for l in range(depth)]

    ct, st = _rope_tables(positions.reshape(s, 1))
    b_col = jnp.broadcast_to(sgu_b[..., None], sgu_b.shape + (CHUNK,))
    u_blk = 3 * aw // gw

    stream = [x0]
    saved = []
    xcur = x0
    for l in range(depth):
        sh_m, sc_m, g_m, sh_f, sc_f, g_f = [mods[l, j] for j in range(6)]
        i = l // 2
        h = _norm_mod_fwd(xcur, norm_mix[l][None], sc_m, sh_m)
        if l % 2 == 0:
            z = _matmul("mix_in", "nn", h, w_in[i], [BF16])
            attn, lse = _attn_fwd(z, ct, st, nh)
            sgu = _sgu_fwd(z, sgu_w[i], b_col[i], ng, u_blk)
            a = jnp.concatenate([attn, sgu], axis=1)
            x1, mix = _matmul("mix_out", "nn", a, w_out[i], [F32, BF16], extras=[(xcur, "mn"), (g_m, "n")],
                              epilogue=lambda acc, r, gv: (r + gv * acc, acc))
            mixer_saved = (z, a, lse)
        else:
            z = _matmul("conv_in", "nn", h, w_cin[i], [BF16])
            a = _conv_fwd(z, conv_w_full[i])
            x1, mix = _matmul("mix_out", "nn", a, w_cout[i], [F32, BF16], extras=[(xcur, "mn"), (g_m, "n")],
                              epilogue=lambda acc, r, gv: (r + gv * acc, acc))
            mixer_saved = (z, a, None)
        h2 = _norm_mod_fwd(x1, norm_ffn[l][None], sc_f, sh_f)
        gt = _matmul("ffn_gate", "nn", h2, w_gate[l], [BF16])
        up = _matmul("ffn_up", "nn", h2, w_up[l], [BF16])
        act = _swiglu_fwd(gt, up)
        x2, f = _matmul("ffn_down", "nn", act, w_down[l], [F32, BF16], extras=[(x1, "mn"), (g_f, "n")],
                        epilogue=lambda acc, r, gv: (r + gv * acc, acc))
        saved.append((h, mixer_saved, mix, x1, h2, gt, up, act, f))
        stream.append(x2)
        xcur = x2

    f_last = saved[-1][8]
    loss_part, dx, dbr, d_final, dg = _final_loss(xcur, target, final_norm[None], f_last, mods[depth - 1, 5])
    loss = lax.psum(loss_part[0, 0], ("x", "y", "c"))

    dmod = [[None] * 6 for _ in range(depth)]
    d_norm_mix, d_norm_ffn = [None] * depth, [None] * depth
    d_sgu_w, d_sgu_b, d_conv_w = [None] * n_even, [None] * n_even, [None] * n_odd
    gw_in, gw_out, gw_cin, gw_cout = [None] * n_even, [None] * n_even, [None] * n_odd, [None] * n_odd
    gw_gate, gw_up, gw_down = [None] * depth, [None] * depth, [None] * depth
    for l in reversed(range(depth)):
        sh_m, sc_m, g_m, sh_f, sc_f, g_f = [mods[l, j] for j in range(6)]
        h, (z, a, lse), mix, x1, h2, gt, up, act, f = saved[l]
        i = l // 2
        dmod[l][5] = dg
        gw_down[l] = _matmul("ffn_down_wgrad", "tn", act, dbr, [BF16], tn=1024)
        dgt, dup = _matmul("ffn_down_dgrad", "nt", dbr, w_down[l], [BF16, BF16], extras=[(gt, "mn"), (up, "mn")],
                           epilogue=_swiglu_bwd_epilogue)
        gw_gate[l] = _matmul("ffn_in_wgrad", "tn", h2, dgt, [BF16])
        gw_up[l] = _matmul("ffn_in_wgrad", "tn", h2, dup, [BF16])
        dh2 = _matmul("ffn_gate_dgrad", "nt", dgt, w_gate[l], [F32])
        dh2 = _matmul("ffn_up_dgrad", "nt", dup, w_up[l], [F32], extras=[(dh2, "mn")], epilogue=lambda acc, prev: (acc + prev,))
        dx, dbr, dmod[l][3], dmod[l][4], d_norm_ffn[l], dg = _norm_mod_bwd(x1, dh2, norm_ffn[l][None], sc_f, dx, mix, g_m)
        dmod[l][2] = dg
        if l % 2 == 0:
            gw_out[i] = _matmul("mix_out_wgrad", "tn", a, dbr, [BF16])
            da = _matmul("mix_out_dgrad", "nt", dbr, w_out[i], [BF16])
            dq, dk, dv = _attn_bwd(z, ct, st, da, a, lse, nh)
            du, dvg, d_sgu_w[i], dbb = _sgu_bwd(z, sgu_w[i], b_col[i], da, ng, u_blk, aw // gw)
            d_sgu_b[i] = dbb[:, :, 0]
            dz = jnp.concatenate([dq, dk, dv, du, dvg], axis=1)
            gw_in[i] = _matmul("mix_in_wgrad", "tn", h, dz, [BF16])
            dh = _matmul("mix_in_dgrad", "nt", dz, w_in[i], [F32])
        else:
            gw_cout[i] = _matmul("mix_out_wgrad", "tn", a, dbr, [BF16])
            da = _matmul("mix_out_dgrad", "nt", dbr, w_cout[i], [BF16])
            dz, d_conv_w[i] = _conv_bwd(z, conv_w_full[i], da)
            gw_cin[i] = _matmul("conv_in_wgrad", "tn", h, dz, [BF16])
            dh = _matmul("conv_in_dgrad", "nt", dz, w_cin[i], [F32])
        if l > 0:
            f_prev, g_prev = saved[l - 1][8], mods[l - 1, 5]
            dx, dbr, dmod[l][0], dmod[l][1], d_norm_mix[l], dg = _norm_mod_bwd(stream[l], dh, norm_mix[l][None], sc_m, dx, f_prev, g_prev)
        else:
            dx, dmod[l][0], dmod[l][1], d_norm_mix[l] = _norm_mod_bwd(stream[l], dh, norm_mix[l][None], sc_m, dx)
    grad_x = dx[None]

    dmod_mine = jnp.stack([jnp.concatenate([v.reshape(d) for v in dmod[l]]) for l in range(depth)])
    small = [dmod_mine, jnp.concatenate(d_norm_mix), jnp.concatenate(d_norm_ffn), jnp.stack(d_sgu_w), jnp.stack(d_sgu_b),
             d_final, jnp.stack(d_conv_w)]
    slab, offs, _ = _pack_rows(small, width)
    gathered = _all_gather_small("gather_small_grads", slab).reshape(N_DEV, -1, width)
    p_dmod, p_nmix, p_nffn, p_sguw, p_sgub, p_final, p_convw = _unpack_rows(gathered, offs, width)

    outs = {}

    def update(name, pieces, w, m, v):
        shape = w.shape
        cdim = shape[-1]
        res = _adamw("adamw_" + name, pieces.reshape(pieces.shape[0], -1, cdim), w.reshape(-1, cdim),
                     m.reshape(-1, cdim), v.reshape(-1, cdim))
        outs[name] = [r.reshape(shape) for r in res]

    update("ada_b", p_dmod.reshape(N_DEV, depth, 6 * d), ada_b, m_ada_b, v_ada_b)
    update("norm_mix", p_nmix.reshape(N_DEV, depth, d), norm_mix, m_norm_mix, v_norm_mix)
    update("norm_ffn", p_nffn.reshape(N_DEV, depth, d), norm_ffn, m_norm_ffn, v_norm_ffn)
    update("sgu_w", p_sguw, sgu_w, m_sgu_w, v_sgu_w)
    update("sgu_b", p_sgub.reshape(N_DEV, 1, -1), sgu_b.reshape(1, -1), m_sgu_b.reshape(1, -1), v_sgu_b.reshape(1, -1))
    outs["sgu_b"] = [r.reshape(sgu_b.shape) for r in outs["sgu_b"]]
    update("final_norm", p_final.reshape(N_DEV, 1, d), final_norm[None], m_final_norm[None], v_final_norm[None])
    outs["final_norm"] = [r.reshape(final_norm.shape) for r in outs["final_norm"]]
    cw_mine = lax.dynamic_slice_in_dim(p_convw.reshape(N_DEV, n_odd, cwid, d), me * d8, d8, axis=3)
    update("conv_w", cw_mine, conv_w, m_conv_w, v_conv_w)

    dmod_cols = lax.dynamic_slice_in_dim(p_dmod.reshape(N_DEV, depth, 6 * d), me * n8, n8, axis=2)
    g_ada = _ada_wgrad(c_all, jnp.transpose(dmod_cols, (1, 0, 2)))
    update("ada_w", g_ada[None], ada_w, m_ada_w, v_ada_w)

    def col_shards(gs):
        return jnp.stack([_plain_to_cols(g) for g in gs], axis=1)

    def row_shards(gs):
        return jnp.stack([g.reshape(N_DEV, g.shape[0] // N_DEV, g.shape[1]) for g in gs], axis=1)

    to_send = [col_shards(gw_in), row_shards(gw_out), col_shards(gw_cin), row_shards(gw_cout),
               col_shards(gw_gate), col_shards(gw_up), row_shards(gw_down)]
    received = _exchange_shards("exchange_grads", to_send)
    names = ["ab_w_in", "ab_w_out", "conv_w_in", "conv_w_out", "ffn_w_gate", "ffn_w_up", "ffn_w_down"]
    moments = [(m_ab_w_in, v_ab_w_in), (m_ab_w_out, v_ab_w_out), (m_conv_w_in, v_conv_w_in), (m_conv_w_out, v_conv_w_out),
               (m_ffn_w_gate, v_ffn_w_gate), (m_ffn_w_up, v_ffn_w_up), (m_ffn_w_down, v_ffn_w_down)]
    for name, w, (m, v), rec in zip(names, big, moments, received):
        update(name, rec, w, m, v)

    order = ["ada_w", "ada_b", "norm_mix", "norm_ffn", "ab_w_in", "sgu_w", "sgu_b", "ab_w_out", "conv_w_in", "conv_w",
             "conv_w_out", "ffn_w_gate", "ffn_w_up", "ffn_w_down", "final_norm"]
    return (loss, grad_x, *[outs[n][0] for n in order], *[outs[n][1] for n in order],
            *[outs[n][2] for n in order], *[outs[n][3] for n in order])
```

```python
import functools
import math

import numpy as np
import jax
import jax.numpy as jnp
from jax import lax
from jax.experimental import pallas as pl
from jax.experimental.pallas import tpu as pltpu

F32, BF16 = jnp.float32, jnp.bfloat16
MESH = pl.DeviceIdType.MESH
N_DEV = 8
EPS = 1e-6
HEAD = 128
CHUNK = 128
DILATIONS = (1, 4, 16)
ATT_CHUNK = CHUNK * DILATIONS[-1]
ATT_UNROLL_FWD, ATT_UNROLL_BWD = 8, 4
ROPE_THETA = 500000.0
ROPE_DIM = HEAD // 4
NEG = -1e30
ADAM_LR, ADAM_B1, ADAM_B2, ADAM_EPS, ADAM_WD, ADAM_STEP = 0.001, 0.9, 0.999, 1e-08, 0.01, 10
VMEM_LIMIT = 56 * 1024 * 1024


def _cparams(*sem):
    return pltpu.CompilerParams(dimension_semantics=sem or None, vmem_limit_bytes=VMEM_LIMIT)


def _tile(n, pref, unit):
    t = (min(pref, n) // unit) * unit
    while t >= unit:
        if n % t == 0:
            return t
        t -= unit
    return n


def _dot(a, b, dims):
    return lax.dot_general(a, b, (dims, ((), ())), preferred_element_type=F32)


def _dot_nn(a, b):
    return _dot(a, b, ((1,), (0,)))


def _dot_nt(a, b):
    return _dot(a, b, ((1,), (1,)))


def _dot_tn(a, b):
    return _dot(a, b, ((0,), (0,)))


def _matmul(name, mode, a, b, outs, *, extras=(), epilogue=None, tm=1024, tn=512, tk=2816):
    if mode == "nn":
        (m, kk), (_, n) = a.shape, b.shape
    elif mode == "nt":
        (m, kk), (n, _) = a.shape, b.shape
    else:
        (kk, m), (_, n) = a.shape, b.shape
    tm, tn, tk = _tile(m, tm, 128), _tile(n, tn, 128), _tile(kk, tk, 128)
    nk = kk // tk
    dotf = {"nn": _dot_nn, "nt": _dot_nt, "tn": _dot_tn}[mode]
    a_spec = pl.BlockSpec((tk, tm), lambda i, j, k: (k, i)) if mode == "tn" else pl.BlockSpec((tm, tk), lambda i, j, k: (i, k))
    b_spec = pl.BlockSpec((tn, tk), lambda i, j, k: (j, k)) if mode == "nt" else pl.BlockSpec((tk, tn), lambda i, j, k: (k, j))
    e_specs = [pl.BlockSpec((tm, tn), lambda i, j, k: (i, j)) if kind == "mn" else pl.BlockSpec((1, tn), lambda i, j, k: (0, j))
               for _, kind in extras]
    ne, no = len(extras), len(outs)
    epi = epilogue or (lambda acc: (acc,))

    def body(a_ref, b_ref, *rest):
        e_refs, o_refs = rest[:ne], rest[ne:ne + no]

        def finish(acc):
            for o_ref, o in zip(o_refs, epi(acc, *[r[...] for r in e_refs])):
                o_ref[...] = o.astype(o_ref.dtype)

        if nk == 1:
            finish(dotf(a_ref[...], b_ref[...]))
            return
        acc_ref = rest[-1]
        k = pl.program_id(2)

        @pl.when(k == 0)
        def _():
            acc_ref[...] = jnp.zeros_like(acc_ref)

        acc_ref[...] += dotf(a_ref[...], b_ref[...])

        @pl.when(k == nk - 1)
        def _():
            finish(acc_ref[...])

    res = pl.pallas_call(
        body, name=name, grid=(m // tm, n // tn, nk),
        in_specs=[a_spec, b_spec] + e_specs,
        out_specs=[pl.BlockSpec((tm, tn), lambda i, j, k: (i, j)) for _ in outs],
        out_shape=[jax.ShapeDtypeStruct((m, n), dt) for dt in outs],
        scratch_shapes=[pltpu.VMEM((tm, tn), F32)] if nk > 1 else [],
        compiler_params=_cparams("parallel", "parallel", "arbitrary"),
    )(a, b, *[e for e, _ in extras])
    return res[0] if no == 1 else res


def _norm_mod_fwd(x, w, sc, sh):
    s, d = x.shape
    tm = _tile(s, 512, 8)

    def body(x_ref, w_ref, sc_ref, sh_ref, h_ref):
        xv = x_ref[...]
        r = lax.rsqrt(jnp.mean(xv * xv, axis=-1, keepdims=True) + EPS)
        h_ref[...] = ((xv * r) * w_ref[...] * (1.0 + sc_ref[...]) + sh_ref[...]).astype(BF16)

    row = pl.BlockSpec((1, d), lambda i: (0, 0))
    return pl.pallas_call(
        body, name="norm_mod_fwd", grid=(s // tm,),
        in_specs=[pl.BlockSpec((tm, d), lambda i: (i, 0)), row, row, row],
        out_specs=pl.BlockSpec((tm, d), lambda i: (i, 0)),
        out_shape=jax.ShapeDtypeStruct((s, d), BF16),
        compiler_params=_cparams("parallel"),
    )(x, w, sc, sh)


def _colsum8(t):
    tm, d = t.shape
    return jnp.sum(t.reshape(tm // 8, 8, d), axis=0)


def _norm_mod_bwd(x, dh, w, sc, dres, branch=None, g=None):
    s, d = x.shape
    tm = _tile(s, 256, 8)
    nsteps = s // tm
    gated = branch is not None

    def body(*refs):
        if gated:
            x_ref, dh_ref, w_ref, sc_ref, dres_ref, br_ref, g_ref, dx_ref, dbr_ref, dsh_ref, dsc_ref, dw_ref, dg_ref, acc = refs
        else:
            x_ref, dh_ref, w_ref, sc_ref, dres_ref, dx_ref, dsh_ref, dsc_ref, dw_ref, acc = refs
        i = pl.program_id(0)

        @pl.when(i == 0)
        def _():
            acc[...] = jnp.zeros_like(acc)

        xv, dhv, wv, scv = x_ref[...], dh_ref[...].astype(F32), w_ref[...], sc_ref[...]
        r = lax.rsqrt(jnp.mean(xv * xv, axis=-1, keepdims=True) + EPS)
        xn = xv * r
        dxn = dhv * (wv * (1.0 + scv))
        dx = dres_ref[...] + r * (dxn - xn * jnp.mean(dxn * xn, axis=-1, keepdims=True))
        dx_ref[...] = dx
        acc[0] += _colsum8(dhv)
        acc[1] += _colsum8(dhv * xn)
        if gated:
            dbr_ref[...] = (dx * g_ref[...]).astype(BF16)
            acc[2] += _colsum8(dx * br_ref[...].astype(F32))

        @pl.when(i == nsteps - 1)
        def _():
            a0 = jnp.sum(acc[0], axis=0, keepdims=True)
            a1 = jnp.sum(acc[1], axis=0, keepdims=True)
            dsh_ref[...] = a0
            dsc_ref[...] = a1 * wv
            dw_ref[...] = a1 * (1.0 + scv)
            if gated:
                dg_ref[...] = jnp.sum(acc[2], axis=0, keepdims=True)

    big = pl.BlockSpec((tm, d), lambda i: (i, 0))
    row = pl.BlockSpec((1, d), lambda i: (0, 0))
    rowo = jax.ShapeDtypeStruct((1, d), F32)
    in_specs = [big, big, row, row, big] + ([big, row] if gated else [])
    out_specs = [big] + ([big] if gated else []) + [row, row, row] + ([row] if gated else [])
    out_shape = ([jax.ShapeDtypeStruct((s, d), F32)] + ([jax.ShapeDtypeStruct((s, d), BF16)] if gated else [])
                 + [rowo, rowo, rowo] + ([rowo] if gated else []))
    args = [x, dh, w, sc, dres] + ([branch, g] if gated else [])
    return pl.pallas_call(
        body, name="norm_mod_bwd_gated" if gated else "norm_mod_bwd", grid=(nsteps,),
        in_specs=in_specs, out_specs=out_specs, out_shape=out_shape,
        scratch_shapes=[pltpu.VMEM((3, 8, d), F32)],
        compiler_params=_cparams("arbitrary"),
    )(*args)


def _final_loss(x, target, w, branch, g):
    s, d = x.shape
    tm = _tile(s, 256, 8)
    nsteps = s // tm

    def body(x_ref, t_ref, w_ref, br_ref, g_ref, loss_ref, dx_ref, dbr_ref, dw_ref, dg_ref, acc):
        i = pl.program_id(0)

        @pl.when(i == 0)
        def _():
            acc[...] = jnp.zeros_like(acc)

        xv, wv = x_ref[...], w_ref[...]
        r = lax.rsqrt(jnp.mean(xv * xv, axis=-1, keepdims=True) + EPS)
        xn = xv * r
        err = xn * wv - t_ref[...]
        dy = err * (1.0 / d)
        dxn = dy * wv
        dx = r * (dxn - xn * jnp.mean(dxn * xn, axis=-1, keepdims=True))
        dx_ref[...] = dx
        dbr_ref[...] = (dx * g_ref[...]).astype(BF16)
        acc[0] += _colsum8(err * err)
        acc[1] += _colsum8(dy * xn)
        acc[2] += _colsum8(dx * br_ref[...].astype(F32))

        @pl.when(i == nsteps - 1)
        def _():
            loss_ref[...] = jnp.sum(jnp.sum(acc[0], axis=0, keepdims=True), axis=1, keepdims=True) * (0.5 / d)
            dw_ref[...] = jnp.sum(acc[1], axis=0, keepdims=True)
            dg_ref[...] = jnp.sum(acc[2], axis=0, keepdims=True)

    big = pl.BlockSpec((tm, d), lambda i: (i, 0))
    row = pl.BlockSpec((1, d), lambda i: (0, 0))
    rowo = jax.ShapeDtypeStruct((1, d), F32)
    return pl.pallas_call(
        body, name="final_loss", grid=(nsteps,),
        in_specs=[big, big, row, big, row],
        out_specs=[pl.BlockSpec((1, 1), lambda i: (0, 0)), big, big, row, row],
        out_shape=[jax.ShapeDtypeStruct((1, 1), F32), jax.ShapeDtypeStruct((s, d), F32),
                   jax.ShapeDtypeStruct((s, d), BF16), rowo, rowo],
        scratch_shapes=[pltpu.VMEM((3, 8, d), F32)],
        compiler_params=_cparams("arbitrary"),
    )(x, target, w, branch, g)


def _rope_tables(pos_col):
    s = pos_col.shape[0]
    tq = _tile(s, 1024, 8)
    half = ROPE_DIM // 2
    inv = np.float32(ROPE_THETA) ** (-np.arange(0, ROPE_DIM, 2, dtype=np.float32) / np.float32(ROPE_DIM))
    inv_row = jnp.asarray(np.tile(inv.astype(np.float32), HEAD // half)[None, :])

    def body(p_ref, inv_ref, ct_ref, st_ref):
        lane = lax.broadcasted_iota(jnp.int32, (tq, HEAD), 1)
        ang = p_ref[...].astype(F32) * inv_ref[...]
        cs, sn = jnp.cos(ang), jnp.sin(ang)
        ct_ref[...] = jnp.where(lane < ROPE_DIM, cs, 1.0)
        st_ref[...] = jnp.where(lane < half, -sn, jnp.where(lane < ROPE_DIM, sn, 0.0))

    blk = pl.BlockSpec((tq, HEAD), lambda i: (i, 0))
    return pl.pallas_call(
        body, name="rope_tables", grid=(s // tq,),
        in_specs=[pl.BlockSpec((tq, 1), lambda i: (i, 0)), pl.BlockSpec((1, HEAD), lambda i: (0, 0))],
        out_specs=[blk, blk],
        out_shape=[jax.ShapeDtypeStruct((s, HEAD), F32)] * 2,
        compiler_params=_cparams("parallel"),
    )(pos_col, inv_row)


def _swap_halves(x):
    lane = lax.broadcasted_iota(jnp.int32, x.shape, 1)
    half = ROPE_DIM // 2
    return jnp.where(lane < half, pltpu.roll(x, HEAD - half, 1), pltpu.roll(x, half, 1))


def _rope(x, ct, st):
    return x * ct + _swap_halves(x) * st


def _rope_t(dy, ct, st):
    return dy * ct - _swap_halves(dy) * st


def _band_bias(bias_ref):
    qi = lax.broadcasted_iota(jnp.int32, (CHUNK, 2 * CHUNK), 0)
    kj = lax.broadcasted_iota(jnp.int32, (CHUNK, 2 * CHUNK), 1)
    band = (kj >= qi) & (kj <= qi + CHUNK)
    bias_ref[0] = jnp.where(band, 0.0, NEG)
    bias_ref[1] = jnp.where(band & (kj >= CHUNK), 0.0, NEG)


def _unit_starts(u, d):
    shift = int(math.log2(d))
    sb, r = u >> shift, u & (d - 1)
    qs = sb * (CHUNK * d) + r
    ks = ATT_CHUNK + (sb - 1) * (CHUNK * d) + r
    if d == 1:
        qs, ks = pl.multiple_of(qs, CHUNK), pl.multiple_of(ks, CHUNK)
    return sb, qs, ks


def _rows(start, size, d):
    return pl.ds(start, size) if d == 1 else pl.ds(start, size, stride=d)


def _attn_specs(nh, nc):
    cur = lambda off: pl.BlockSpec((ATT_CHUNK, HEAD), lambda h, c: (jnp.minimum(c, nc - 1), off + h))
    prev = lambda off: pl.BlockSpec((ATT_CHUNK, HEAD), lambda h, c: (jnp.maximum(c - 1, 0), off + h))
    tcur = pl.BlockSpec((ATT_CHUNK, HEAD), lambda h, c: (jnp.minimum(c, nc - 1), 0))
    tprev = pl.BlockSpec((ATT_CHUNK, HEAD), lambda h, c: (jnp.maximum(c - 1, 0), 0))
    return [cur(0), prev(nh), cur(nh), prev(2 * nh), cur(2 * nh), tcur, tcur, tprev, tprev]


def _attn_fwd(z, ct, st, nh):
    s = z.shape[0]
    nc = s // ATT_CHUNK
    scale = HEAD ** -0.5

    def body(q_ref, kp_ref, kc_ref, vp_ref, vc_ref, ctc, stc, ctp, stp, o_ref, lse_ref, qf, kf, vf, ob, lb, bias):
        c = pl.program_id(1)
        qf[...] = _rope(q_ref[...].astype(F32), ctc[...], stc[...])
        kf[0:ATT_CHUNK] = _rope(kp_ref[...].astype(F32), ctp[...], stp[...])
        kf[ATT_CHUNK:] = _rope(kc_ref[...].astype(F32), ctc[...], stc[...])
        vf[0:ATT_CHUNK] = vp_ref[...].astype(F32)
        vf[ATT_CHUNK:] = vc_ref[...].astype(F32)
        _band_bias(bias)
        for b, d in enumerate(DILATIONS):
            def unit(u, carry, b=b, d=d):
                sb, qs, ks = _unit_starts(u, d)
                q_u = qf[_rows(qs, CHUNK, d), :].astype(BF16)
                k_u = kf[_rows(ks, 2 * CHUNK, d), :].astype(BF16)
                v_u = vf[_rows(ks, 2 * CHUNK, d), :].astype(BF16)
                first = jnp.where((c == 0) & (sb == 0), 1, 0)
                sc = _dot_nt(q_u, k_u) * scale + bias[first]
                m = jnp.max(sc, axis=1, keepdims=True)
                p = jnp.exp(sc - m)
                l = jnp.sum(p, axis=1, keepdims=True)
                ob[b, _rows(qs, CHUNK, d), :] = _dot_nn(p.astype(BF16), v_u) / l
                lb[b, _rows(qs, CHUNK, d), :] = jnp.broadcast_to(m + jnp.log(l), (CHUNK, HEAD))
                return carry
            lax.fori_loop(0, ATT_CHUNK // CHUNK, unit, 0, unroll=ATT_UNROLL_FWD)
        mx = jnp.maximum(jnp.maximum(lb[0], lb[1]), lb[2])
        e0, e1, e2 = jnp.exp(lb[0] - mx), jnp.exp(lb[1] - mx), jnp.exp(lb[2] - mx)
        den = e0 + e1 + e2
        o_ref[...] = ((e0 * ob[0] + e1 * ob[1] + e2 * ob[2]) / den).astype(BF16)
        lse_ref[...] = mx + jnp.log(den)

    blk = pl.BlockSpec((ATT_CHUNK, HEAD), lambda h, c: (c, h))
    return pl.pallas_call(
        body, name="attn_fwd", grid=(nh, nc),
        in_specs=_attn_specs(nh, nc), out_specs=[blk, blk],
        out_shape=[jax.ShapeDtypeStruct((s, nh * HEAD), BF16), jax.ShapeDtypeStruct((s, nh * HEAD), F32)],
        scratch_shapes=[pltpu.VMEM((ATT_CHUNK, HEAD), F32), pltpu.VMEM((2 * ATT_CHUNK, HEAD), F32),
                        pltpu.VMEM((2 * ATT_CHUNK, HEAD), F32), pltpu.VMEM((3, ATT_CHUNK, HEAD), F32),
                        pltpu.VMEM((3, ATT_CHUNK, HEAD), F32), pltpu.VMEM((2, CHUNK, 2 * CHUNK), F32)],
        compiler_params=_cparams("parallel", "arbitrary"),
    )(z, z, z, z, z, ct, st, ct, st)


def _attn_bwd(z, ct, st, da, o, lse, nh):
    s = z.shape[0]
    nc = s // ATT_CHUNK
    scale = HEAD ** -0.5

    def body(q_ref, kp_ref, kc_ref, vp_ref, vc_ref, ctc, stc, ctp, stp, do_ref, o_ref, lse_ref,
             dq_ref, dk_ref, dv_ref, qf, kf, vf, dof, dbar, dqa, dkf, dvf, bias):
        c = pl.program_id(1)

        @pl.when(c == 0)
        def _():
            dkf[...] = jnp.zeros_like(dkf)
            dvf[...] = jnp.zeros_like(dvf)

        @pl.when(c > 0)
        def _():
            dkf[0:ATT_CHUNK] = dkf[ATT_CHUNK:]
            dvf[0:ATT_CHUNK] = dvf[ATT_CHUNK:]
            dkf[ATT_CHUNK:] = jnp.zeros((ATT_CHUNK, HEAD), F32)
            dvf[ATT_CHUNK:] = jnp.zeros((ATT_CHUNK, HEAD), F32)

        @pl.when(c < nc)
        def _():
            qf[...] = _rope(q_ref[...].astype(F32), ctc[...], stc[...])
            kf[0:ATT_CHUNK] = _rope(kp_ref[...].astype(F32), ctp[...], stp[...])
            kf[ATT_CHUNK:] = _rope(kc_ref[...].astype(F32), ctc[...], stc[...])
            vf[0:ATT_CHUNK] = vp_ref[...].astype(F32)
            vf[ATT_CHUNK:] = vc_ref[...].astype(F32)
            dov = do_ref[...].astype(F32)
            dof[...] = dov
            dbar[...] = jnp.broadcast_to(jnp.sum(dov * o_ref[...].astype(F32), axis=1, keepdims=True), (ATT_CHUNK, HEAD))
            dqa[...] = jnp.zeros_like(dqa)
            _band_bias(bias)
            for d in DILATIONS:
                def unit(u, carry, d=d):
                    sb, qs, ks = _unit_starts(u, d)
                    qr, kr = _rows(qs, CHUNK, d), _rows(ks, 2 * CHUNK, d)
                    q_u = qf[qr, :].astype(BF16)
                    k_u = kf[kr, :].astype(BF16)
                    v_u = vf[kr, :].astype(BF16)
                    do_u = dof[qr, :].astype(BF16)
                    lse_u = lse_ref[qr, :]
                    dbar_u = dbar[qr, :]
                    first = jnp.where((c == 0) & (sb == 0), 1, 0)
                    sc = _dot_nt(q_u, k_u) * scale + bias[first]
                    p = jnp.exp(sc - jnp.concatenate([lse_u, lse_u], axis=1))
                    dp = _dot_nt(do_u, v_u)
                    ds = (p * (dp - jnp.concatenate([dbar_u, dbar_u], axis=1)) * scale).astype(BF16)
                    dqa[qr, :] += _dot_nn(ds, k_u)
                    dkf[kr, :] += _dot_tn(ds, q_u)
                    dvf[kr, :] += _dot_tn(p.astype(BF16), do_u)
                    return carry
                lax.fori_loop(0, ATT_CHUNK // CHUNK, unit, 0, unroll=ATT_UNROLL_BWD)
            dq_ref[...] = _rope_t(dqa[...], ctc[...], stc[...]).astype(BF16)

        @pl.when(c > 0)
        def _():
            dk_ref[...] = _rope_t(dkf[0:ATT_CHUNK], ctp[...], stp[...]).astype(BF16)
            dv_ref[...] = dvf[0:ATT_CHUNK].astype(BF16)

    cur = pl.BlockSpec((ATT_CHUNK, HEAD), lambda h, c: (jnp.minimum(c, nc - 1), h))
    late = pl.BlockSpec((ATT_CHUNK, HEAD), lambda h, c: (jnp.maximum(c - 1, 0), h))
    shp = jax.ShapeDtypeStruct((s, nh * HEAD), BF16)
    big = pltpu.VMEM((2 * ATT_CHUNK, HEAD), F32)
    one = pltpu.VMEM((ATT_CHUNK, HEAD), F32)
    return pl.pallas_call(
        body, name="attn_bwd", grid=(nh, nc + 1),
        in_specs=_attn_specs(nh, nc) + [cur, cur, cur], out_specs=[cur, late, late],
        out_shape=[shp, shp, shp],
        scratch_shapes=[one, big, big, one, one, one, big, big, pltpu.VMEM((2, CHUNK, 2 * CHUNK), F32)],
        compiler_params=_cparams("parallel", "arbitrary"),
    )(z, z, z, z, z, ct, st, ct, st, da, o, lse)


_GELU_K = math.sqrt(2.0 / math.pi)


def _gelu(x):
    return 0.5 * x * (1.0 + jnp.tanh(_GELU_K * (x + 0.044715 * x * x * x)))


def _gelu_and_grad(x):
    t = jnp.tanh(_GELU_K * (x + 0.044715 * x * x * x))
    g = 0.5 * x * (1.0 + t)
    dg = 0.5 * (1.0 + t) + 0.5 * x * (1.0 - t * t) * (_GELU_K * (1.0 + 3 * 0.044715 * x * x))
    return g, dg


def _tril(w):
    ti = lax.broadcasted_iota(jnp.int32, (CHUNK, CHUNK), 0)
    si = lax.broadcasted_iota(jnp.int32, (CHUNK, CHUNK), 1)
    return jnp.where(si <= ti, w, 0.0)


def _sgu_fwd(z, w_s, b_col, ng, u_blk):
    s = z.shape[0]
    gw = ng * HEAD
    tq = _tile(s, 1024, CHUNK)

    def body(u_ref, v_ref, w_ref, b_ref, o_ref):
        for g in range(ng):
            wg = _tril(w_ref[g]).astype(BF16)
            cols = slice(g * HEAD, (g + 1) * HEAD)
            for n in range(tq // CHUNK):
                rows = slice(n * CHUNK, (n + 1) * CHUNK)
                gv = _gelu(v_ref[rows, cols].astype(F32)).astype(BF16)
                mixed = _dot_nn(wg, gv) + b_ref[g]
                o_ref[rows, cols] = (_gelu(u_ref[rows, cols].astype(F32)) * mixed).astype(BF16)

    full = pl.BlockSpec((ng, CHUNK, CHUNK), lambda i: (0, 0, 0))
    return pl.pallas_call(
        body, name="sgu_fwd", grid=(s // tq,),
        in_specs=[pl.BlockSpec((tq, gw), lambda i: (i, u_blk)), pl.BlockSpec((tq, gw), lambda i: (i, u_blk + 1)), full, full],
        out_specs=pl.BlockSpec((tq, gw), lambda i: (i, 0)),
        out_shape=jax.ShapeDtypeStruct((s, gw), BF16),
        compiler_params=_cparams("parallel"),
    )(z, z, w_s, b_col)


def _sgu_bwd(z, w_s, b_col, da, ng, u_blk, da_blk):
    s = z.shape[0]
    gw = ng * HEAD
    tq = _tile(s, 1024, CHUNK)
    nsteps = s // tq

    def body(u_ref, v_ref, w_ref, b_ref, do_ref, du_ref, dv_ref, dw_ref, db_ref):
        i = pl.program_id(0)

        @pl.when(i == 0)
        def _():
            dw_ref[...] = jnp.zeros_like(dw_ref)
            db_ref[...] = jnp.zeros_like(db_ref)

        for g in range(ng):
            wg = _tril(w_ref[g]).astype(BF16)
            cols = slice(g * HEAD, (g + 1) * HEAD)
            dw_acc = jnp.zeros((CHUNK, CHUNK), F32)
            db_acc = jnp.zeros((CHUNK, 1), F32)
            for n in range(tq // CHUNK):
                rows = slice(n * CHUNK, (n + 1) * CHUNK)
                gu, dgu = _gelu_and_grad(u_ref[rows, cols].astype(F32))
                gv, dgv = _gelu_and_grad(v_ref[rows, cols].astype(F32))
                gvb = gv.astype(BF16)
                mixed = _dot_nn(wg, gvb) + b_ref[g]
                dout = do_ref[rows, cols].astype(F32)
                du_ref[rows, cols] = (dout * mixed * dgu).astype(BF16)
                dmix = dout * gu
                dmb = dmix.astype(BF16)
                dv_ref[rows, cols] = (_dot_tn(wg, dmb) * dgv).astype(BF16)
                dw_acc += _dot_nt(dmb, gvb)
                db_acc += jnp.sum(dmix, axis=1, keepdims=True)
            dw_ref[g] += _tril(dw_acc)
            db_ref[g] += jnp.broadcast_to(db_acc, (CHUNK, CHUNK))

    full = pl.BlockSpec((ng, CHUNK, CHUNK), lambda i: (0, 0, 0))
    out = pl.BlockSpec((tq, gw), lambda i: (i, 0))
    return pl.pallas_call(
        body, name="sgu_bwd", grid=(nsteps,),
        in_specs=[pl.BlockSpec((tq, gw), lambda i: (i, u_blk)), pl.BlockSpec((tq, gw), lambda i: (i, u_blk + 1)), full, full,
                  pl.BlockSpec((tq, gw), lambda i: (i, da_blk))],
        out_specs=[out, out, full, full],
        out_shape=[jax.ShapeDtypeStruct((s, gw), BF16)] * 2 + [jax.ShapeDtypeStruct((ng, CHUNK, CHUNK), F32)] * 2,
        compiler_params=_cparams("arbitrary"),
    )(z, z, w_s, b_col, da)


def _shift_down(y, halo, k):
    rolled = pltpu.roll(y, k, 0)
    row = lax.broadcasted_iota(jnp.int32, y.shape, 0)
    for j in range(k):
        rolled = jnp.where(row == j, halo[8 - k + j:8 - k + j + 1, :], rolled)
    return rolled


def _shift_up(y, halo, k):
    n = y.shape[0]
    rolled = pltpu.roll(y, n - k, 0)
    row = lax.broadcasted_iota(jnp.int32, y.shape, 0)
    for j in range(k):
        rolled = jnp.where(row == n - k + j, halo[j:j + 1, :], rolled)
    return rolled


def _conv_fwd(z, cw):
    s, d3 = z.shape
    d = d3 // 3
    tq = _tile(s, 256, 8)

    def body(z_ref, zh_ref, cw_ref, a_ref):
        i = pl.program_id(0)
        zv = z_ref[...].astype(F32)
        zh = jnp.where(i > 0, zh_ref[...].astype(F32), 0.0)
        y = zv[:, d:2 * d] * zv[:, 2 * d:]
        yh = zh[:, d:2 * d] * zh[:, 2 * d:]
        cwv = cw_ref[...]
        conv = cwv[0:1] * _shift_down(y, yh, 2) + cwv[1:2] * _shift_down(y, yh, 1) + cwv[2:3] * y
        a_ref[...] = (zv[:, :d] * conv).astype(BF16)

    return pl.pallas_call(
        body, name="conv_fwd", grid=(s // tq,),
        in_specs=[pl.BlockSpec((tq, d3), lambda i: (i, 0)),
                  pl.BlockSpec((8, d3), lambda i: (jnp.maximum(i * (tq // 8) - 1, 0), 0)),
                  pl.BlockSpec((3, d), lambda i: (0, 0))],
        out_specs=pl.BlockSpec((tq, d), lambda i: (i, 0)),
        out_shape=jax.ShapeDtypeStruct((s, d), BF16),
        compiler_params=_cparams("parallel"),
    )(z, z, cw)


def _conv_bwd(z, cw, da):
    s, d3 = z.shape
    d = d3 // 3
    tq = _tile(s, 128, 8)
    nsteps = s // tq
    nblk8 = s // 8

    def body(z_ref, zp_ref, zn_ref, da_ref, dan_ref, cw_ref, dz_ref, dcw_ref, acc):
        i = pl.program_id(0)

        @pl.when(i == 0)
        def _():
            acc[...] = jnp.zeros_like(acc)

        zv = z_ref[...].astype(F32)
        zp = jnp.where(i > 0, zp_ref[...].astype(F32), 0.0)
        zn = jnp.where(i < nsteps - 1, zn_ref[...].astype(F32), 0.0)
        dav = da_ref[...].astype(F32)
        dan = jnp.where(i < nsteps - 1, dan_ref[...].astype(F32), 0.0)
        gb, gc, hx = zv[:, :d], zv[:, d:2 * d], zv[:, 2 * d:]
        y = gc * hx
        yp = zp[:, d:2 * d] * zp[:, 2 * d:]
        cwv = cw_ref[...]
        y1, y2 = _shift_down(y, yp, 1), _shift_down(y, yp, 2)
        conv = cwv[0:1] * y2 + cwv[1:2] * y1 + cwv[2:3] * y
        dconv = dav * gb
        dconv_n = dan * zn[:, :d]
        dy = cwv[2:3] * dconv + cwv[1:2] * _shift_up(dconv, dconv_n, 1) + cwv[0:1] * _shift_up(dconv, dconv_n, 2)
        dz_ref[:, :d] = (dav * conv).astype(BF16)
        dz_ref[:, d:2 * d] = (dy * hx).astype(BF16)
        dz_ref[:, 2 * d:] = (dy * gc).astype(BF16)
        acc[0] += _colsum8(dconv * y2)
        acc[1] += _colsum8(dconv * y1)
        acc[2] += _colsum8(dconv * y)

        @pl.when(i == nsteps - 1)
        def _():
            for j in range(3):
                dcw_ref[j:j + 1, :] = jnp.sum(acc[j], axis=0, keepdims=True)

    return pl.pallas_call(
        body, name="conv_bwd", grid=(nsteps,),
        in_specs=[pl.BlockSpec((tq, d3), lambda i: (i, 0)),
                  pl.BlockSpec((8, d3), lambda i: (jnp.maximum(i * (tq // 8) - 1, 0), 0)),
                  pl.BlockSpec((8, d3), lambda i: (jnp.minimum((i + 1) * (tq // 8), nblk8 - 1), 0)),
                  pl.BlockSpec((tq, d), lambda i: (i, 0)),
                  pl.BlockSpec((8, d), lambda i: (jnp.minimum((i + 1) * (tq // 8), nblk8 - 1), 0)),
                  pl.BlockSpec((3, d), lambda i: (0, 0))],
        out_specs=[pl.BlockSpec((tq, d3), lambda i: (i, 0)), pl.BlockSpec((3, d), lambda i: (0, 0))],
        out_shape=[jax.ShapeDtypeStruct((s, d3), BF16), jax.ShapeDtypeStruct((3, d), F32)],
        scratch_shapes=[pltpu.VMEM((3, 8, d), F32)],
        compiler_params=_cparams("arbitrary"),
    )(z, z, z, da, da, cw)


def _swiglu_fwd(gt, up):
    s, f = gt.shape
    tm, tn = _tile(s, 512, 8), _tile(f, 2816, 128)

    def body(g_ref, u_ref, a_ref):
        g = g_ref[...].astype(F32)
        a_ref[...] = (g / (1.0 + jnp.exp(-g)) * u_ref[...].astype(F32)).astype(BF16)

    blk = pl.BlockSpec((tm, tn), lambda i, j: (i, j))
    return pl.pallas_call(
        body, name="swiglu_fwd", grid=(s // tm, f // tn), in_specs=[blk, blk], out_specs=blk,
        out_shape=jax.ShapeDtypeStruct((s, f), BF16), compiler_params=_cparams("parallel", "parallel"),
    )(gt, up)


def _swiglu_bwd_epilogue(dact, gt, up):
    g, u = gt.astype(F32), up.astype(F32)
    sg = 1.0 / (1.0 + jnp.exp(-g))
    return dact * u * (sg * (1.0 + g * (1.0 - sg))), dact * (g * sg)


def _ada_fwd(c_all, ada_w):
    nl, d, n8 = ada_w.shape
    tn = _tile(n8, 768, 128)

    def body(c_ref, w_ref, o_ref):
        cv = c_ref[...]
        act = (cv / (1.0 + jnp.exp(-cv))).astype(BF16)
        o_ref[0] = _dot_nn(act, w_ref[0].astype(BF16))

    return pl.pallas_call(
        body, name="ada_fwd", grid=(nl, n8 // tn),
        in_specs=[pl.BlockSpec((N_DEV, d), lambda l, j: (0, 0)), pl.BlockSpec((1, d, tn), lambda l, j: (l, 0, j))],
        out_specs=pl.BlockSpec((1, N_DEV, tn), lambda l, j: (l, 0, j)),
        out_shape=jax.ShapeDtypeStruct((nl, N_DEV, n8), F32),
        compiler_params=_cparams("parallel", "parallel"),
    )(c_all, ada_w)


def _ada_wgrad(c_all, dmod_cols):
    nl, _, n8 = dmod_cols.shape
    d = c_all.shape[1]
    tn = _tile(n8, 768, 128)

    def body(c_ref, g_ref, o_ref):
        cv = c_ref[...]
        act = (cv / (1.0 + jnp.exp(-cv))).astype(BF16)
        o_ref[0] = _dot_tn(act, g_ref[0].astype(BF16))

    return pl.pallas_call(
        body, name="ada_wgrad", grid=(nl, n8 // tn),
        in_specs=[pl.BlockSpec((N_DEV, d), lambda l, j: (0, 0)), pl.BlockSpec((1, N_DEV, tn), lambda l, j: (l, 0, j))],
        out_specs=pl.BlockSpec((1, d, tn), lambda l, j: (l, 0, j)),
        out_shape=jax.ShapeDtypeStruct((nl, d, n8), F32),
        compiler_params=_cparams("parallel", "parallel"),
    )(c_all, dmod_cols)


def _adamw(name, pieces, w, m, v):
    npc, r, c = pieces.shape
    tr = _tile(r, max(8, (1 << 19) // c // 8 * 8), 8)
    bc1, bc2 = 1.0 - ADAM_B1 ** ADAM_STEP, 1.0 - ADAM_B2 ** ADAM_STEP

    def body(p_ref, w_ref, m_ref, v_ref, g_ref, d_ref, nm_ref, nv_ref):
        g = p_ref[0].astype(F32)
        for i in range(1, npc):
            g = g + p_ref[i].astype(F32)
        nm = ADAM_B1 * m_ref[...] + (1.0 - ADAM_B1) * g
        nv = ADAM_B2 * v_ref[...] + (1.0 - ADAM_B2) * (g * g)
        g_ref[...] = g
        nm_ref[...] = nm
        nv_ref[...] = nv
        d_ref[...] = -ADAM_LR * ((nm / bc1) / (jnp.sqrt(nv / bc2) + ADAM_EPS) + ADAM_WD * w_ref[...])

    blk = pl.BlockSpec((tr, c), lambda i: (i, 0))
    return pl.pallas_call(
        body, name=name, grid=(r // tr,),
        in_specs=[pl.BlockSpec((npc, tr, c), lambda i: (0, i, 0)), blk, blk, blk],
        out_specs=[blk] * 4, out_shape=[jax.ShapeDtypeStruct((r, c), F32)] * 4,
        compiler_params=_cparams("parallel"),
    )(pieces, w, m, v)


def _place():
    x, y, c = lax.axis_index("x"), lax.axis_index("y"), lax.axis_index("c")
    return x, y, c


def _all_gather_small(name, x_shard):
    m_per, n = x_shard.shape

    def body(x_ref, out_ref, send_sems, recv_sems, local_sem):
        x, y, c = _place()
        me, sibling = (x, y, c), (x, y, 1 - c)
        chips = [(1 - x, y), (x, 1 - y), (1 - x, 1 - y)]

        def rows(px, py, pc):
            return out_ref.at[pl.ds((4 * px + 2 * py + pc) * m_per, m_per), :]

        def copy(k, block, to, src=None):
            return pltpu.make_async_remote_copy(
                src_ref=rows(*block) if src is None else src, dst_ref=rows(*block),
                send_sem=send_sems.at[k], recv_sem=recv_sems.at[k], device_id=to, device_id_type=MESH)

        mine = pltpu.make_async_copy(x_ref, rows(*me), local_sem)
        mine.start()
        first = [copy(0, me, sibling, src=x_ref)]
        first += [copy(1 + j, me, (*chip, c), src=x_ref) for j, chip in enumerate(chips)]
        for cp in first:
            cp.start()
        passed = [copy(4 + j, (*chip, c), sibling) for j, chip in enumerate(chips)]
        for j, chip in enumerate(chips):
            copy(1 + j, (*chip, c), me).wait_recv()
            passed[j].start()
        copy(0, sibling, me).wait_recv()
        for j, chip in enumerate(chips):
            copy(4 + j, (*chip, 1 - c), me).wait_recv()
        for cp in first + passed:
            cp.wait_send()
        mine.wait()

    return pl.pallas_call(
        body, name=name,
        out_shape=jax.ShapeDtypeStruct((N_DEV * m_per, n), x_shard.dtype),
        in_specs=[pl.BlockSpec(memory_space=pltpu.VMEM)], out_specs=pl.BlockSpec(memory_space=pltpu.VMEM),
        scratch_shapes=[pltpu.SemaphoreType.DMA((7,)), pltpu.SemaphoreType.DMA((7,)), pltpu.SemaphoreType.DMA],
        compiler_params=pltpu.CompilerParams(vmem_limit_bytes=VMEM_LIMIT),
    )(x_shard)


_HBM = pl.BlockSpec(memory_space=pltpu.HBM)
_SEM = pl.BlockSpec(memory_space=pltpu.SEMAPHORE)
_EFFECT = pltpu.SideEffectType.DATAFLOW_SIDE_EFFECTING


def _peers():
    x, y, c = _place()
    peers = []
    for k in range(1, N_DEV):
        px = 1 - x if k & 4 else x
        py = 1 - y if k & 2 else y
        pc = 1 - c if k & 1 else c
        peers.append(((px, py, pc), 4 * px + 2 * py + pc))
    return 4 * x + 2 * y + c, peers


def _push_start(name, srcs, lands, src_view, dst_view):
    na = len(srcs)
    n = na * (N_DEV - 1)

    def body(*refs):
        s_refs, l_refs = refs[:na], refs[na:2 * na]
        send_sems, recv_sems, token = refs[2 * na], refs[2 * na + 1], refs[-1]
        me, peers = _peers()
        for a in range(na):
            for k, (dev, idx) in enumerate(peers):
                pltpu.make_async_remote_copy(
                    src_ref=src_view(s_refs[a], idx), dst_ref=dst_view(l_refs[a], me),
                    send_sem=send_sems.at[a * (N_DEV - 1) + k], recv_sem=recv_sems.at[a * (N_DEV - 1) + k],
                    device_id=dev, device_id_type=MESH).start()
        token[...] = jnp.zeros_like(token)

    outs = pl.pallas_call(
        body, name=name,
        out_shape=(pltpu.SemaphoreType.DMA((n,)), pltpu.SemaphoreType.DMA((n,)),
                   *[pltpu.HBM(t.shape, t.dtype) for t in list(srcs) + list(lands)], jax.ShapeDtypeStruct((8, HEAD), F32)),
        in_specs=[_HBM] * (2 * na),
        out_specs=(_SEM, _SEM, *[_HBM] * (2 * na), pl.BlockSpec(memory_space=pltpu.VMEM)),
        input_output_aliases={i: 2 + i for i in range(2 * na)},
        compiler_params=pltpu.CompilerParams(has_side_effects=_EFFECT),
    )(*[pltpu.with_memory_space_constraint(t, pltpu.HBM) for t in list(srcs) + list(lands)])
    return outs[0], outs[1], list(outs[2:2 + na]), list(outs[2 + na:2 + 2 * na]), outs[-1]


def _push_wait(name, send_sems, recv_sems, srcs, lands, after, src_view, dst_view):
    na = len(srcs)

    def body(*refs):
        s_refs, l_refs = refs[:na], refs[na:2 * na]
        send_sems, recv_sems = refs[2 * na], refs[2 * na + 1]
        me, peers = _peers()
        for a in range(na):
            for k, (dev, idx) in enumerate(peers):
                cp = pltpu.make_async_remote_copy(
                    src_ref=src_view(s_refs[a], idx), dst_ref=dst_view(l_refs[a], idx),
                    send_sem=send_sems.at[a * (N_DEV - 1) + k], recv_sem=recv_sems.at[a * (N_DEV - 1) + k],
                    device_id=dev, device_id_type=MESH)
                cp.wait_send()
                cp.wait_recv()

    outs = pl.pallas_call(
        body, name=name,
        out_shape=[pltpu.HBM(t.shape, t.dtype) for t in list(srcs) + list(lands)],
        in_specs=[_HBM] * (2 * na) + [_SEM, _SEM, pl.BlockSpec(memory_space=pl.ANY)],
        out_specs=[_HBM] * (2 * na),
        input_output_aliases={i: i for i in range(2 * na)},
        compiler_params=pltpu.CompilerParams(has_side_effects=_EFFECT),
    )(*srcs, *lands, send_sems, recv_sems, after)
    return list(outs[na:])


def _gather_start(name, shards):
    lands = [lax.empty((N_DEV,) + t.shape, t.dtype) for t in shards]
    return _push_start(name, shards, lands, lambda ref, idx: ref, lambda ref, slot: ref.at[slot])


def _gather_wait(name, started, shards, after, me):
    send_sems, recv_sems, srcs, lands, _ = started
    lands = _push_wait(name, send_sems, recv_sems, srcs, lands, after, lambda ref, idx: ref, lambda ref, slot: ref.at[slot])
    return [lax.dynamic_update_index_in_dim(g, t, me, 0) for g, t in zip(lands, shards)]


def _cols_to_plain(g):
    n, k, n8 = g.shape
    return jnp.transpose(g, (1, 0, 2)).reshape(k, n * n8)


def _plain_to_cols(w):
    k, n = w.shape
    return jnp.transpose(w.reshape(k, N_DEV, n // N_DEV), (1, 0, 2))


def _pack_rows(parts, width):
    rows, offs, r = [], [], 0
    for p in parts:
        flat = p.reshape(-1).astype(F32)
        nr = -(-flat.shape[0] // (8 * width)) * 8
        rows.append(jnp.pad(flat, (0, nr * width - flat.shape[0])).reshape(nr, width))
        offs.append((r, flat.shape[0], p.shape))
        r += nr
    return jnp.concatenate(rows, axis=0), offs, r


def _unpack_rows(slab, offs, width):
    lead = slab.shape[:-2]
    out = []
    for r0, n, shape in offs:
        nr = -(-n // width)
        out.append(slab[..., r0:r0 + nr, :].reshape(lead + (nr * width,))[..., :n].reshape(lead + tuple(shape)))
    return out


def kernel(x, c, positions, ada_w, ada_b, norm_mix, norm_ffn, ab_w_in, sgu_w, sgu_b, ab_w_out, conv_w_in, conv_w, conv_w_out, ffn_w_gate, ffn_w_up, ffn_w_down, final_norm, loss_target, m_ada_w, m_ada_b, m_norm_mix, m_norm_ffn, m_ab_w_in, m_sgu_w, m_sgu_b, m_ab_w_out, m_conv_w_in, m_conv_w, m_conv_w_out, m_ffn_w_gate, m_ffn_w_up, m_ffn_w_down, m_final_norm, v_ada_w, v_ada_b, v_norm_mix, v_norm_ffn, v_ab_w_in, v_sgu_w, v_sgu_b, v_ab_w_out, v_conv_w_in, v_conv_w, v_conv_w_out, v_ffn_w_gate, v_ffn_w_up, v_ffn_w_down, v_final_norm):
    xi, yi, ci = _place()
    me = 4 * xi + 2 * yi + ci
    s, d = x.shape[1], x.shape[2]
    depth = ada_w.shape[0]
    n_even = ab_w_in.shape[0]
    nh_mix = d // HEAD
    nh = 3 * nh_mix // 4
    ng = nh_mix - nh
    aw, gw = nh * HEAD, ng * HEAD
    x0 = x[0]
    target = loss_target[0]
    n_odd, cwid, d8 = conv_w.shape

    def mixer_weights(l):
        return (ab_w_in[l // 2], ab_w_out[l // 2]) if l % 2 == 0 else (conv_w_in[l // 2], conv_w_out[l // 2])

    groups = [[mixer_weights(0)[0]], [mixer_weights(0)[1]], [ffn_w_gate[0], ffn_w_up[0], ffn_w_down[0]]]
    groups += [[*mixer_weights(l), ffn_w_gate[l], ffn_w_up[l], ffn_w_down[l]] for l in range(1, depth)]
    gathers, tok = [], jnp.zeros((), F32)
    for n, ws in enumerate(groups):
        shards = [(w + tok).astype(BF16) for w in ws]
        started = _gather_start(f"gather_start_{n}", shards)
        gathers.append((started, shards))
        tok = started[4][0, 0]

    def weights_of_group(n, after):
        started, shards = gathers[n]
        return _gather_wait(f"gather_wait_{n}", started, shards, after, me)

    def plain_rows(g):
        return g.reshape(g.shape[0] * g.shape[1], g.shape[2])

    width = 512
    slab, offs, _ = _pack_rows([c + tok, conv_w], width)
    gathered = _all_gather_small("gather_cond", slab).reshape(N_DEV, -1, width)
    c_parts, cw_parts = _unpack_rows(gathered, offs, width)
    c_all = c_parts.reshape(N_DEV, d)
    conv_w_full = jnp.transpose(cw_parts, (1, 2, 0, 3)).reshape(n_odd, cwid, d)

    mod_cols = _ada_fwd(c_all, ada_w)
    n8 = mod_cols.shape[2]
    mod_all = _all_gather_small("gather_mod", mod_cols.reshape(depth * N_DEV, n8)).reshape(N_DEV, depth, N_DEV, n8)
    mod_mine = lax.dynamic_index_in_dim(mod_all, me, axis=2, keepdims=False)
    mod = jnp.transpose(mod_mine, (1, 0, 2)).reshape(depth, N_DEV * n8) + ada_b
    mods = mod.reshape(depth, 6, 1, d)

    ct, st = _rope_tables(positions.reshape(s, 1))
    b_col = jnp.broadcast_to(sgu_b[..., None], sgu_b.shape + (CHUNK,))
    u_blk = 3 * aw // gw

    stream = [x0]
    saved = []
    w_in, w_out, w_gate, w_up, w_down = [[None] * depth for _ in range(5)]
    xcur = x0
    for l in range(depth):
        sh_m, sc_m, g_m, sh_f, sc_f, g_f = [mods[l, j] for j in range(6)]
        i = l // 2
        if l == 0:
            (g_in,) = weights_of_group(0, ct)
        else:
            g_in, g_out, g_gate, g_up, g_down = weights_of_group(2 + l, xcur)
        w_in[l] = _cols_to_plain(g_in)
        h = _norm_mod_fwd(xcur, norm_mix[l][None], sc_m, sh_m)
        if l % 2 == 0:
            z = _matmul("mix_in", "nn", h, w_in[l], [BF16])
            attn, lse = _attn_fwd(z, ct, st, nh)
            sgu = _sgu_fwd(z, sgu_w[i], b_col[i], ng, u_blk)
            a = jnp.concatenate([attn, sgu], axis=1)
            mixer_saved = (z, a, lse)
        else:
            z = _matmul("conv_in", "nn", h, w_in[l], [BF16])
            a = _conv_fwd(z, conv_w_full[i])
            mixer_saved = (z, a, None)
        if l == 0:
            (g_out,) = weights_of_group(1, a)
        w_out[l] = plain_rows(g_out)
        x1, mix = _matmul("mix_out", "nn", a, w_out[l], [F32, BF16], extras=[(xcur, "mn"), (g_m, "n")],
                          epilogue=lambda acc, r, gv: (r + gv * acc, acc))
        if l == 0:
            g_gate, g_up, g_down = weights_of_group(2, x1)
        w_gate[l], w_up[l], w_down[l] = _cols_to_plain(g_gate), _cols_to_plain(g_up), plain_rows(g_down)
        h2 = _norm_mod_fwd(x1, norm_ffn[l][None], sc_f, sh_f)
        gt = _matmul("ffn_gate", "nn", h2, w_gate[l], [BF16])
        up = _matmul("ffn_up", "nn", h2, w_up[l], [BF16])
        act = _swiglu_fwd(gt, up)
        x2, f = _matmul("ffn_down", "nn", act, w_down[l], [F32, BF16], extras=[(x1, "mn"), (g_f, "n")],
                        epilogue=lambda acc, r, gv: (r + gv * acc, acc))
        saved.append((h, mixer_saved, mix, x1, h2, gt, up, act, f))
        stream.append(x2)
        xcur = x2

    f_last = saved[-1][8]
    loss_part, dx, dbr, d_final, dg = _final_loss(xcur, target, final_norm[None], f_last, mods[depth - 1, 5])
    loss = lax.psum(loss_part[0, 0], ("x", "y", "c"))

    dmod = [[None] * 6 for _ in range(depth)]
    d_norm_mix, d_norm_ffn = [None] * depth, [None] * depth
    d_sgu_w, d_sgu_b, d_conv_w = [None] * n_even, [None] * n_even, [None] * n_odd

    big = {"in": ab_w_in, "out": ab_w_out, "cin": conv_w_in, "cout": conv_w_out,
           "gate": ffn_w_gate, "up": ffn_w_up, "down": ffn_w_down}
    lands = {k: lax.empty((N_DEV,) + w.shape, BF16) for k, w in big.items()}
    own = {k: [None] * w.shape[0] for k, w in big.items()}
    pending = {"ffn": None, "mix": None}

    def exchange_finish(tag, after):
        (send_sems, recv_sems, srcs, lds, _), keys, li, layer = pending[tag]
        lds = _push_wait(f"exchange_wait_{tag}_{layer}", send_sems, recv_sems, srcs, lds, after,
                         lambda ref, idx: ref.at[idx], lambda ref, slot: ref.at[slot, li])
        for k, ld in zip(keys, lds):
            lands[k] = ld
        pending[tag] = None

    def exchange_start(tag, layer, keys, li, grads):
        if pending[tag] is not None:
            exchange_finish(tag, grads[0])
        for k, g in zip(keys, grads):
            own[k][li] = lax.dynamic_index_in_dim(g, me, 0, keepdims=False)
        started = _push_start(f"exchange_start_{tag}_{layer}", grads, [lands[k] for k in keys],
                              lambda ref, idx: ref.at[idx], lambda ref, slot: ref.at[slot, li])
        pending[tag] = (started, keys, li, layer)

    def row_shards(g):
        return g.reshape(N_DEV, g.shape[0] // N_DEV, g.shape[1])

    for l in reversed(range(depth)):
        sh_m, sc_m, g_m, sh_f, sc_f, g_f = [mods[l, j] for j in range(6)]
        h, (z, a, lse), mix, x1, h2, gt, up, act, f = saved[l]
        i = l // 2
        dmod[l][5] = dg
        gw_down = _matmul("ffn_down_wgrad", "tn", act, dbr, [BF16], tn=1024)
        dgt, dup = _matmul("ffn_down_dgrad", "nt", dbr, w_down[l], [BF16, BF16], extras=[(gt, "mn"), (up, "mn")],
                           epilogue=_swiglu_bwd_epilogue)
        gw_gate = _matmul("ffn_in_wgrad", "tn", h2, dgt, [BF16])
        gw_up = _matmul("ffn_in_wgrad", "tn", h2, dup, [BF16])
        exchange_start("ffn", l, ("gate", "up", "down"), l, [_plain_to_cols(gw_gate), _plain_to_cols(gw_up), row_shards(gw_down)])
        dh2 = _matmul("ffn_gate_dgrad", "nt", dgt, w_gate[l], [F32])
        dh2 = _matmul("ffn_up_dgrad", "nt", dup, w_up[l], [F32], extras=[(dh2, "mn")], epilogue=lambda acc, prev: (acc + prev,))
        dx, dbr, dmod[l][3], dmod[l][4], d_norm_ffn[l], dg = _norm_mod_bwd(x1, dh2, norm_ffn[l][None], sc_f, dx, mix, g_m)
        dmod[l][2] = dg
        gw_out = _matmul("mix_out_wgrad", "tn", a, dbr, [BF16])
        da = _matmul("mix_out_dgrad", "nt", dbr, w_out[l], [BF16])
        if l % 2 == 0:
            dq, dk, dv = _attn_bwd(z, ct, st, da, a, lse, nh)
            du, dvg, d_sgu_w[i], dbb = _sgu_bwd(z, sgu_w[i], b_col[i], da, ng, u_blk, aw // gw)
            d_sgu_b[i] = dbb[:, :, 0]
            dz = jnp.concatenate([dq, dk, dv, du, dvg], axis=1)
            gw_in = _matmul("mix_in_wgrad", "tn", h, dz, [BF16])
            dh = _matmul("mix_in_dgrad", "nt", dz, w_in[l], [F32])
        else:
            dz, d_conv_w[i] = _conv_bwd(z, conv_w_full[i], da)
            gw_in = _matmul("conv_in_wgrad", "tn", h, dz, [BF16])
            dh = _matmul("conv_in_dgrad", "nt", dz, w_in[l], [F32])
        exchange_start("mix", l, ("in", "out") if l % 2 == 0 else ("cin", "cout"), i, [_plain_to_cols(gw_in), row_shards(gw_out)])
        if l > 0:
            f_prev, g_prev = saved[l - 1][8], mods[l - 1, 5]
            dx, dbr, dmod[l][0], dmod[l][1], d_norm_mix[l], dg = _norm_mod_bwd(stream[l], dh, norm_mix[l][None], sc_m, dx, f_prev, g_prev)
        else:
            dx, dmod[l][0], dmod[l][1], d_norm_mix[l] = _norm_mod_bwd(stream[l], dh, norm_mix[l][None], sc_m, dx)
    grad_x = dx[None]

    dmod_mine = jnp.stack([jnp.concatenate([v.reshape(d) for v in dmod[l]]) for l in range(depth)])
    small = [dmod_mine, jnp.concatenate(d_norm_mix), jnp.concatenate(d_norm_ffn), jnp.stack(d_sgu_w), jnp.stack(d_sgu_b),
             d_final, jnp.stack(d_conv_w)]
    slab, offs, _ = _pack_rows(small, width)
    gathered = _all_gather_small("gather_small_grads", slab).reshape(N_DEV, -1, width)
    p_dmod, p_nmix, p_nffn, p_sguw, p_sgub, p_final, p_convw = _unpack_rows(gathered, offs, width)

    outs = {}

    def update(name, pieces, w, m, v):
        shape = w.shape
        cdim = shape[-1]
        res = _adamw("adamw_" + name, pieces.reshape(pieces.shape[0], -1, cdim), w.reshape(-1, cdim),
                     m.reshape(-1, cdim), v.reshape(-1, cdim))
        outs[name] = [r.reshape(shape) for r in res]

    update("ada_b", p_dmod.reshape(N_DEV, depth, 6 * d), ada_b, m_ada_b, v_ada_b)
    update("norm_mix", p_nmix.reshape(N_DEV, depth, d), norm_mix, m_norm_mix, v_norm_mix)
    update("norm_ffn", p_nffn.reshape(N_DEV, depth, d), norm_ffn, m_norm_ffn, v_norm_ffn)
    update("sgu_w", p_sguw, sgu_w, m_sgu_w, v_sgu_w)
    update("sgu_b", p_sgub.reshape(N_DEV, 1, -1), sgu_b.reshape(1, -1), m_sgu_b.reshape(1, -1), v_sgu_b.reshape(1, -1))
    outs["sgu_b"] = [r.reshape(sgu_b.shape) for r in outs["sgu_b"]]
    update("final_norm", p_final.reshape(N_DEV, 1, d), final_norm[None], m_final_norm[None], v_final_norm[None])
    outs["final_norm"] = [r.reshape(final_norm.shape) for r in outs["final_norm"]]
    cw_mine = lax.dynamic_slice_in_dim(p_convw.reshape(N_DEV, n_odd, cwid, d), me * d8, d8, axis=3)
    update("conv_w", cw_mine, conv_w, m_conv_w, v_conv_w)

    dmod_cols = lax.dynamic_slice_in_dim(p_dmod.reshape(N_DEV, depth, 6 * d), me * n8, n8, axis=2)
    g_ada = _ada_wgrad(c_all, jnp.transpose(dmod_cols, (1, 0, 2)))
    update("ada_w", g_ada[None], ada_w, m_ada_w, v_ada_w)

    exchange_finish("ffn", g_ada)
    exchange_finish("mix", g_ada)
    names = {"in": "ab_w_in", "out": "ab_w_out", "cin": "conv_w_in", "cout": "conv_w_out",
             "gate": "ffn_w_gate", "up": "ffn_w_up", "down": "ffn_w_down"}
    moments = {"in": (m_ab_w_in, v_ab_w_in), "out": (m_ab_w_out, v_ab_w_out), "cin": (m_conv_w_in, v_conv_w_in),
               "cout": (m_conv_w_out, v_conv_w_out), "gate": (m_ffn_w_gate, v_ffn_w_gate), "up": (m_ffn_w_up, v_ffn_w_up),
               "down": (m_ffn_w_down, v_ffn_w_down)}
    for k, w in big.items():
        pieces = lax.dynamic_update_slice(lands[k], jnp.stack(own[k])[None], (me, 0, 0, 0))
        update(names[k], pieces, w, *moments[k])

    order = ["ada_w", "ada_b", "norm_mix", "norm_ffn", "ab_w_in", "sgu_w", "sgu_b", "ab_w_out", "conv_w_in", "conv_w",
             "conv_w_out", "ffn_w_gate", "ffn_w_up", "ffn_w_down", "final_norm"]
    return (loss, grad_x, *[outs[n][0] for n in order], *[outs[n][1] for n in order],
            *[outs[n][2] for n in order], *[outs[n][3] for n in order])
```

```python
import functools
import math

import numpy as np
import jax
import jax.numpy as jnp
from jax import lax
from jax.experimental import pallas as pl
from jax.experimental.pallas import tpu as pltpu

F32, BF16 = jnp.float32, jnp.bfloat16
MESH = pl.DeviceIdType.MESH
N_DEV = 8
EPS = 1e-6
HEAD = 128
CHUNK = 128
DILATIONS = (1, 4, 16)
ATT_CHUNK = CHUNK * DILATIONS[-1]
ATT_UNROLL_FWD, ATT_UNROLL_BWD = 8, 4
ROPE_THETA = 500000.0
ROPE_DIM = HEAD // 4
NEG = -1e30
ADAM_LR, ADAM_B1, ADAM_B2, ADAM_EPS, ADAM_WD, ADAM_STEP = 0.001, 0.9, 0.999, 1e-08, 0.01, 10
VMEM_LIMIT = 56 * 1024 * 1024


def _cparams(*sem):
    return pltpu.CompilerParams(dimension_semantics=sem or None, vmem_limit_bytes=VMEM_LIMIT)


def _tile(n, pref, unit):
    t = (min(pref, n) // unit) * unit
    while t >= unit:
        if n % t == 0:
            return t
        t -= unit
    return n


def _dot(a, b, dims):
    return lax.dot_general(a, b, (dims, ((), ())), preferred_element_type=F32)


def _dot_nn(a, b):
    return _dot(a, b, ((1,), (0,)))


def _dot_nt(a, b):
    return _dot(a, b, ((1,), (1,)))


def _dot_tn(a, b):
    return _dot(a, b, ((0,), (0,)))


def _matmul(name, mode, a, b, outs, *, extras=(), epilogue=None, after=None, tm=1024, tn=512, tk=2816):
    if mode == "nn":
        (m, kk), (_, n) = a.shape, b.shape
    elif mode == "nt":
        (m, kk), (n, _) = a.shape, b.shape
    else:
        (kk, m), (_, n) = a.shape, b.shape
    if after is not None:
        inner = epilogue or (lambda acc: (acc,))
        extras = list(extras) + [(jnp.zeros((1, n), F32) + after, "n")]
        epilogue = lambda acc, *tiles: inner(acc, *tiles[:-1])
    tm, tn, tk = _tile(m, tm, 128), _tile(n, tn, 128), _tile(kk, tk, 128)
    nk = kk // tk
    dotf = {"nn": _dot_nn, "nt": _dot_nt, "tn": _dot_tn}[mode]
    a_spec = pl.BlockSpec((tk, tm), lambda i, j, k: (k, i)) if mode == "tn" else pl.BlockSpec((tm, tk), lambda i, j, k: (i, k))
    b_spec = pl.BlockSpec((tn, tk), lambda i, j, k: (j, k)) if mode == "nt" else pl.BlockSpec((tk, tn), lambda i, j, k: (k, j))
    e_specs = [pl.BlockSpec((tm, tn), lambda i, j, k: (i, j)) if kind == "mn" else pl.BlockSpec((1, tn), lambda i, j, k: (0, j))
               for _, kind in extras]
    ne, no = len(extras), len(outs)
    epi = epilogue or (lambda acc: (acc,))

    def body(a_ref, b_ref, *rest):
        e_refs, o_refs = rest[:ne], rest[ne:ne + no]

        def finish(acc):
            for o_ref, o in zip(o_refs, epi(acc, *[r[...] for r in e_refs])):
                o_ref[...] = o.astype(o_ref.dtype)

        if nk == 1:
            finish(dotf(a_ref[...], b_ref[...]))
            return
        acc_ref = rest[-1]
        k = pl.program_id(2)

        @pl.when(k == 0)
        def _():
            acc_ref[...] = jnp.zeros_like(acc_ref)

        acc_ref[...] += dotf(a_ref[...], b_ref[...])

        @pl.when(k == nk - 1)
        def _():
            finish(acc_ref[...])

    res = pl.pallas_call(
        body, name=name, grid=(m // tm, n // tn, nk),
        in_specs=[a_spec, b_spec] + e_specs,
        out_specs=[pl.BlockSpec((tm, tn), lambda i, j, k: (i, j)) for _ in outs],
        out_shape=[jax.ShapeDtypeStruct((m, n), dt) for dt in outs],
        scratch_shapes=[pltpu.VMEM((tm, tn), F32)] if nk > 1 else [],
        compiler_params=_cparams("parallel", "parallel", "arbitrary"),
    )(a, b, *[e for e, _ in extras])
    return res[0] if no == 1 else res


def _norm_mod_fwd(x, w, sc, sh):
    s, d = x.shape
    tm = _tile(s, 512, 8)

    def body(x_ref, w_ref, sc_ref, sh_ref, h_ref):
        xv = x_ref[...]
        r = lax.rsqrt(jnp.mean(xv * xv, axis=-1, keepdims=True) + EPS)
        h_ref[...] = ((xv * r) * w_ref[...] * (1.0 + sc_ref[...]) + sh_ref[...]).astype(BF16)

    row = pl.BlockSpec((1, d), lambda i: (0, 0))
    return pl.pallas_call(
        body, name="norm_mod_fwd", grid=(s // tm,),
        in_specs=[pl.BlockSpec((tm, d), lambda i: (i, 0)), row, row, row],
        out_specs=pl.BlockSpec((tm, d), lambda i: (i, 0)),
        out_shape=jax.ShapeDtypeStruct((s, d), BF16),
        compiler_params=_cparams("parallel"),
    )(x, w, sc, sh)


def _colsum8(t):
    tm, d = t.shape
    return jnp.sum(t.reshape(tm // 8, 8, d), axis=0)


def _norm_mod_bwd(x, dh, w, sc, dres, branch=None, g=None):
    s, d = x.shape
    tm = _tile(s, 256, 8)
    nsteps = s // tm
    gated = branch is not None

    def body(*refs):
        if gated:
            x_ref, dh_ref, w_ref, sc_ref, dres_ref, br_ref, g_ref, dx_ref, dbr_ref, dsh_ref, dsc_ref, dw_ref, dg_ref, acc = refs
        else:
            x_ref, dh_ref, w_ref, sc_ref, dres_ref, dx_ref, dsh_ref, dsc_ref, dw_ref, acc = refs
        i = pl.program_id(0)

        @pl.when(i == 0)
        def _():
            acc[...] = jnp.zeros_like(acc)

        xv, dhv, wv, scv = x_ref[...], dh_ref[...].astype(F32), w_ref[...], sc_ref[...]
        r = lax.rsqrt(jnp.mean(xv * xv, axis=-1, keepdims=True) + EPS)
        xn = xv * r
        dxn = dhv * (wv * (1.0 + scv))
        dx = dres_ref[...] + r * (dxn - xn * jnp.mean(dxn * xn, axis=-1, keepdims=True))
        dx_ref[...] = dx
        acc[0] += _colsum8(dhv)
        acc[1] += _colsum8(dhv * xn)
        if gated:
            dbr_ref[...] = (dx * g_ref[...]).astype(BF16)
            acc[2] += _colsum8(dx * br_ref[...].astype(F32))

        @pl.when(i == nsteps - 1)
        def _():
            a0 = jnp.sum(acc[0], axis=0, keepdims=True)
            a1 = jnp.sum(acc[1], axis=0, keepdims=True)
            dsh_ref[...] = a0
            dsc_ref[...] = a1 * wv
            dw_ref[...] = a1 * (1.0 + scv)
            if gated:
                dg_ref[...] = jnp.sum(acc[2], axis=0, keepdims=True)

    big = pl.BlockSpec((tm, d), lambda i: (i, 0))
    row = pl.BlockSpec((1, d), lambda i: (0, 0))
    rowo = jax.ShapeDtypeStruct((1, d), F32)
    in_specs = [big, big, row, row, big] + ([big, row] if gated else [])
    out_specs = [big] + ([big] if gated else []) + [row, row, row] + ([row] if gated else [])
    out_shape = ([jax.ShapeDtypeStruct((s, d), F32)] + ([jax.ShapeDtypeStruct((s, d), BF16)] if gated else [])
                 + [rowo, rowo, rowo] + ([rowo] if gated else []))
    args = [x, dh, w, sc, dres] + ([branch, g] if gated else [])
    return pl.pallas_call(
        body, name="norm_mod_bwd_gated" if gated else "norm_mod_bwd", grid=(nsteps,),
        in_specs=in_specs, out_specs=out_specs, out_shape=out_shape,
        scratch_shapes=[pltpu.VMEM((3, 8, d), F32)],
        compiler_params=_cparams("arbitrary"),
    )(*args)


def _final_loss(x, target, w, branch, g):
    s, d = x.shape
    tm = _tile(s, 256, 8)
    nsteps = s // tm

    def body(x_ref, t_ref, w_ref, br_ref, g_ref, loss_ref, dx_ref, dbr_ref, dw_ref, dg_ref, acc):
        i = pl.program_id(0)

        @pl.when(i == 0)
        def _():
            acc[...] = jnp.zeros_like(acc)

        xv, wv = x_ref[...], w_ref[...]
        r = lax.rsqrt(jnp.mean(xv * xv, axis=-1, keepdims=True) + EPS)
        xn = xv * r
        err = xn * wv - t_ref[...]
        dy = err * (1.0 / d)
        dxn = dy * wv
        dx = r * (dxn - xn * jnp.mean(dxn * xn, axis=-1, keepdims=True))
        dx_ref[...] = dx
        dbr_ref[...] = (dx * g_ref[...]).astype(BF16)
        acc[0] += _colsum8(err * err)
        acc[1] += _colsum8(dy * xn)
        acc[2] += _colsum8(dx * br_ref[...].astype(F32))

        @pl.when(i == nsteps - 1)
        def _():
            loss_ref[...] = jnp.sum(jnp.sum(acc[0], axis=0, keepdims=True), axis=1, keepdims=True) * (0.5 / d)
            dw_ref[...] = jnp.sum(acc[1], axis=0, keepdims=True)
            dg_ref[...] = jnp.sum(acc[2], axis=0, keepdims=True)

    big = pl.BlockSpec((tm, d), lambda i: (i, 0))
    row = pl.BlockSpec((1, d), lambda i: (0, 0))
    rowo = jax.ShapeDtypeStruct((1, d), F32)
    return pl.pallas_call(
        body, name="final_loss", grid=(nsteps,),
        in_specs=[big, big, row, big, row],
        out_specs=[pl.BlockSpec((1, 1), lambda i: (0, 0)), big, big, row, row],
        out_shape=[jax.ShapeDtypeStruct((1, 1), F32), jax.ShapeDtypeStruct((s, d), F32),
                   jax.ShapeDtypeStruct((s, d), BF16), rowo, rowo],
        scratch_shapes=[pltpu.VMEM((3, 8, d), F32)],
        compiler_params=_cparams("arbitrary"),
    )(x, target, w, branch, g)


def _rope_tables(pos_col):
    s = pos_col.shape[0]
    tq = _tile(s, 1024, 8)
    half = ROPE_DIM // 2
    inv = np.float32(ROPE_THETA) ** (-np.arange(0, ROPE_DIM, 2, dtype=np.float32) / np.float32(ROPE_DIM))
    inv_row = jnp.asarray(np.tile(inv.astype(np.float32), HEAD // half)[None, :])

    def body(p_ref, inv_ref, ct_ref, st_ref):
        lane = lax.broadcasted_iota(jnp.int32, (tq, HEAD), 1)
        ang = p_ref[...].astype(F32) * inv_ref[...]
        cs, sn = jnp.cos(ang), jnp.sin(ang)
        ct_ref[...] = jnp.where(lane < ROPE_DIM, cs, 1.0)
        st_ref[...] = jnp.where(lane < half, -sn, jnp.where(lane < ROPE_DIM, sn, 0.0))

    blk = pl.BlockSpec((tq, HEAD), lambda i: (i, 0))
    return pl.pallas_call(
        body, name="rope_tables", grid=(s // tq,),
        in_specs=[pl.BlockSpec((tq, 1), lambda i: (i, 0)), pl.BlockSpec((1, HEAD), lambda i: (0, 0))],
        out_specs=[blk, blk],
        out_shape=[jax.ShapeDtypeStruct((s, HEAD), F32)] * 2,
        compiler_params=_cparams("parallel"),
    )(pos_col, inv_row)


def _swap_halves(x):
    lane = lax.broadcasted_iota(jnp.int32, x.shape, 1)
    half = ROPE_DIM // 2
    return jnp.where(lane < half, pltpu.roll(x, HEAD - half, 1), pltpu.roll(x, half, 1))


def _rope(x, ct, st):
    return x * ct + _swap_halves(x) * st


def _rope_t(dy, ct, st):
    return dy * ct - _swap_halves(dy) * st


def _band_bias(bias_ref):
    qi = lax.broadcasted_iota(jnp.int32, (CHUNK, 2 * CHUNK), 0)
    kj = lax.broadcasted_iota(jnp.int32, (CHUNK, 2 * CHUNK), 1)
    band = (kj >= qi) & (kj <= qi + CHUNK)
    bias_ref[0] = jnp.where(band, 0.0, NEG)
    bias_ref[1] = jnp.where(band & (kj >= CHUNK), 0.0, NEG)


def _unit_starts(u, d):
    shift = int(math.log2(d))
    sb, r = u >> shift, u & (d - 1)
    qs = sb * (CHUNK * d) + r
    ks = ATT_CHUNK + (sb - 1) * (CHUNK * d) + r
    if d == 1:
        qs, ks = pl.multiple_of(qs, CHUNK), pl.multiple_of(ks, CHUNK)
    return sb, qs, ks


def _rows(start, size, d):
    return pl.ds(start, size) if d == 1 else pl.ds(start, size, stride=d)


def _attn_specs(nh, nc):
    cur = lambda off: pl.BlockSpec((ATT_CHUNK, HEAD), lambda h, c: (jnp.minimum(c, nc - 1), off + h))
    prev = lambda off: pl.BlockSpec((ATT_CHUNK, HEAD), lambda h, c: (jnp.maximum(c - 1, 0), off + h))
    tcur = pl.BlockSpec((ATT_CHUNK, HEAD), lambda h, c: (jnp.minimum(c, nc - 1), 0))
    tprev = pl.BlockSpec((ATT_CHUNK, HEAD), lambda h, c: (jnp.maximum(c - 1, 0), 0))
    return [cur(0), prev(nh), cur(nh), prev(2 * nh), cur(2 * nh), tcur, tcur, tprev, tprev]


def _attn_fwd(z, ct, st, nh):
    s = z.shape[0]
    nc = s // ATT_CHUNK
    scale = HEAD ** -0.5

    def body(q_ref, kp_ref, kc_ref, vp_ref, vc_ref, ctc, stc, ctp, stp, o_ref, lse_ref, qf, kf, vf, ob, lb, bias):
        c = pl.program_id(1)
        qf[...] = _rope(q_ref[...].astype(F32), ctc[...], stc[...])
        kf[0:ATT_CHUNK] = _rope(kp_ref[...].astype(F32), ctp[...], stp[...])
        kf[ATT_CHUNK:] = _rope(kc_ref[...].astype(F32), ctc[...], stc[...])
        vf[0:ATT_CHUNK] = vp_ref[...].astype(F32)
        vf[ATT_CHUNK:] = vc_ref[...].astype(F32)
        _band_bias(bias)
        for b, d in enumerate(DILATIONS):
            def unit(u, carry, b=b, d=d):
                sb, qs, ks = _unit_starts(u, d)
                q_u = qf[_rows(qs, CHUNK, d), :].astype(BF16)
                k_u = kf[_rows(ks, 2 * CHUNK, d), :].astype(BF16)
                v_u = vf[_rows(ks, 2 * CHUNK, d), :].astype(BF16)
                first = jnp.where((c == 0) & (sb == 0), 1, 0)
                sc = _dot_nt(q_u, k_u) * scale + bias[first]
                m = jnp.max(sc, axis=1, keepdims=True)
                p = jnp.exp(sc - m)
                l = jnp.sum(p, axis=1, keepdims=True)
                ob[b, _rows(qs, CHUNK, d), :] = _dot_nn(p.astype(BF16), v_u) / l
                lb[b, _rows(qs, CHUNK, d), :] = jnp.broadcast_to(m + jnp.log(l), (CHUNK, HEAD))
                return carry
            lax.fori_loop(0, ATT_CHUNK // CHUNK, unit, 0, unroll=ATT_UNROLL_FWD)
        mx = jnp.maximum(jnp.maximum(lb[0], lb[1]), lb[2])
        e0, e1, e2 = jnp.exp(lb[0] - mx), jnp.exp(lb[1] - mx), jnp.exp(lb[2] - mx)
        den = e0 + e1 + e2
        o_ref[...] = ((e0 * ob[0] + e1 * ob[1] + e2 * ob[2]) / den).astype(BF16)
        lse_ref[...] = mx + jnp.log(den)

    blk = pl.BlockSpec((ATT_CHUNK, HEAD), lambda h, c: (c, h))
    return pl.pallas_call(
        body, name="attn_fwd", grid=(nh, nc),
        in_specs=_attn_specs(nh, nc), out_specs=[blk, blk],
        out_shape=[jax.ShapeDtypeStruct((s, nh * HEAD), BF16), jax.ShapeDtypeStruct((s, nh * HEAD), F32)],
        scratch_shapes=[pltpu.VMEM((ATT_CHUNK, HEAD), F32), pltpu.VMEM((2 * ATT_CHUNK, HEAD), F32),
                        pltpu.VMEM((2 * ATT_CHUNK, HEAD), F32), pltpu.VMEM((3, ATT_CHUNK, HEAD), F32),
                        pltpu.VMEM((3, ATT_CHUNK, HEAD), F32), pltpu.VMEM((2, CHUNK, 2 * CHUNK), F32)],
        compiler_params=_cparams("parallel", "arbitrary"),
    )(z, z, z, z, z, ct, st, ct, st)


def _attn_bwd(z, ct, st, da, o, lse, nh):
    s = z.shape[0]
    nc = s // ATT_CHUNK
    scale = HEAD ** -0.5

    def body(q_ref, kp_ref, kc_ref, vp_ref, vc_ref, ctc, stc, ctp, stp, do_ref, o_ref, lse_ref,
             dq_ref, dk_ref, dv_ref, qf, kf, vf, dof, dbar, dqa, dkf, dvf, bias):
        c = pl.program_id(1)

        @pl.when(c == 0)
        def _():
            dkf[...] = jnp.zeros_like(dkf)
            dvf[...] = jnp.zeros_like(dvf)

        @pl.when(c > 0)
        def _():
            dkf[0:ATT_CHUNK] = dkf[ATT_CHUNK:]
            dvf[0:ATT_CHUNK] = dvf[ATT_CHUNK:]
            dkf[ATT_CHUNK:] = jnp.zeros((ATT_CHUNK, HEAD), F32)
            dvf[ATT_CHUNK:] = jnp.zeros((ATT_CHUNK, HEAD), F32)

        @pl.when(c < nc)
        def _():
            qf[...] = _rope(q_ref[...].astype(F32), ctc[...], stc[...])
            kf[0:ATT_CHUNK] = _rope(kp_ref[...].astype(F32), ctp[...], stp[...])
            kf[ATT_CHUNK:] = _rope(kc_ref[...].astype(F32), ctc[...], stc[...])
            vf[0:ATT_CHUNK] = vp_ref[...].astype(F32)
            vf[ATT_CHUNK:] = vc_ref[...].astype(F32)
            dov = do_ref[...].astype(F32)
            dof[...] = dov
            dbar[...] = jnp.broadcast_to(jnp.sum(dov * o_ref[...].astype(F32), axis=1, keepdims=True), (ATT_CHUNK, HEAD))
            dqa[...] = jnp.zeros_like(dqa)
            _band_bias(bias)
            for d in DILATIONS:
                def unit(u, carry, d=d):
                    sb, qs, ks = _unit_starts(u, d)
                    qr, kr = _rows(qs, CHUNK, d), _rows(ks, 2 * CHUNK, d)
                    q_u = qf[qr, :].astype(BF16)
                    k_u = kf[kr, :].astype(BF16)
                    v_u = vf[kr, :].astype(BF16)
                    do_u = dof[qr, :].astype(BF16)
                    lse_u = lse_ref[qr, :]
                    dbar_u = dbar[qr, :]
                    first = jnp.where((c == 0) & (sb == 0), 1, 0)
                    sc = _dot_nt(q_u, k_u) * scale + bias[first]
                    p = jnp.exp(sc - jnp.concatenate([lse_u, lse_u], axis=1))
                    dp = _dot_nt(do_u, v_u)
                    ds = (p * (dp - jnp.concatenate([dbar_u, dbar_u], axis=1)) * scale).astype(BF16)
                    dqa[qr, :] += _dot_nn(ds, k_u)
                    dkf[kr, :] += _dot_tn(ds, q_u)
                    dvf[kr, :] += _dot_tn(p.astype(BF16), do_u)
                    return carry
                lax.fori_loop(0, ATT_CHUNK // CHUNK, unit, 0, unroll=ATT_UNROLL_BWD)
            dq_ref[...] = _rope_t(dqa[...], ctc[...], stc[...]).astype(BF16)

        @pl.when(c > 0)
        def _():
            dk_ref[...] = _rope_t(dkf[0:ATT_CHUNK], ctp[...], stp[...]).astype(BF16)
            dv_ref[...] = dvf[0:ATT_CHUNK].astype(BF16)

    cur = pl.BlockSpec((ATT_CHUNK, HEAD), lambda h, c: (jnp.minimum(c, nc - 1), h))
    late = pl.BlockSpec((ATT_CHUNK, HEAD), lambda h, c: (jnp.maximum(c - 1, 0), h))
    shp = jax.ShapeDtypeStruct((s, nh * HEAD), BF16)
    big = pltpu.VMEM((2 * ATT_CHUNK, HEAD), F32)
    one = pltpu.VMEM((ATT_CHUNK, HEAD), F32)
    return pl.pallas_call(
        body, name="attn_bwd", grid=(nh, nc + 1),
        in_specs=_attn_specs(nh, nc) + [cur, cur, cur], out_specs=[cur, late, late],
        out_shape=[shp, shp, shp],
        scratch_shapes=[one, big, big, one, one, one, big, big, pltpu.VMEM((2, CHUNK, 2 * CHUNK), F32)],
        compiler_params=_cparams("parallel", "arbitrary"),
    )(z, z, z, z, z, ct, st, ct, st, da, o, lse)


_GELU_K = math.sqrt(2.0 / math.pi)


def _gelu(x):
    return 0.5 * x * (1.0 + jnp.tanh(_GELU_K * (x + 0.044715 * x * x * x)))


def _gelu_and_grad(x):
    t = jnp.tanh(_GELU_K * (x + 0.044715 * x * x * x))
    g = 0.5 * x * (1.0 + t)
    dg = 0.5 * (1.0 + t) + 0.5 * x * (1.0 - t * t) * (_GELU_K * (1.0 + 3 * 0.044715 * x * x))
    return g, dg


def _tril(w):
    ti = lax.broadcasted_iota(jnp.int32, (CHUNK, CHUNK), 0)
    si = lax.broadcasted_iota(jnp.int32, (CHUNK, CHUNK), 1)
    return jnp.where(si <= ti, w, 0.0)


def _sgu_fwd(z, w_s, b_col, ng, u_blk):
    s = z.shape[0]
    gw = ng * HEAD
    tq = _tile(s, 1024, CHUNK)

    def body(u_ref, v_ref, w_ref, b_ref, o_ref):
        for g in range(ng):
            wg = _tril(w_ref[g]).astype(BF16)
            cols = slice(g * HEAD, (g + 1) * HEAD)
            for n in range(tq // CHUNK):
                rows = slice(n * CHUNK, (n + 1) * CHUNK)
                gv = _gelu(v_ref[rows, cols].astype(F32)).astype(BF16)
                mixed = _dot_nn(wg, gv) + b_ref[g]
                o_ref[rows, cols] = (_gelu(u_ref[rows, cols].astype(F32)) * mixed).astype(BF16)

    full = pl.BlockSpec((ng, CHUNK, CHUNK), lambda i: (0, 0, 0))
    return pl.pallas_call(
        body, name="sgu_fwd", grid=(s // tq,),
        in_specs=[pl.BlockSpec((tq, gw), lambda i: (i, u_blk)), pl.BlockSpec((tq, gw), lambda i: (i, u_blk + 1)), full, full],
        out_specs=pl.BlockSpec((tq, gw), lambda i: (i, 0)),
        out_shape=jax.ShapeDtypeStruct((s, gw), BF16),
        compiler_params=_cparams("parallel"),
    )(z, z, w_s, b_col)


def _sgu_bwd(z, w_s, b_col, da, ng, u_blk, da_blk):
    s = z.shape[0]
    gw = ng * HEAD
    tq = _tile(s, 1024, CHUNK)
    nsteps = s // tq

    def body(u_ref, v_ref, w_ref, b_ref, do_ref, du_ref, dv_ref, dw_ref, db_ref):
        i = pl.program_id(0)

        @pl.when(i == 0)
        def _():
            dw_ref[...] = jnp.zeros_like(dw_ref)
            db_ref[...] = jnp.zeros_like(db_ref)

        for g in range(ng):
            wg = _tril(w_ref[g]).astype(BF16)
            cols = slice(g * HEAD, (g + 1) * HEAD)
            dw_acc = jnp.zeros((CHUNK, CHUNK), F32)
            db_acc = jnp.zeros((CHUNK, 1), F32)
            for n in range(tq // CHUNK):
                rows = slice(n * CHUNK, (n + 1) * CHUNK)
                gu, dgu = _gelu_and_grad(u_ref[rows, cols].astype(F32))
                gv, dgv = _gelu_and_grad(v_ref[rows, cols].astype(F32))
                gvb = gv.astype(BF16)
                mixed = _dot_nn(wg, gvb) + b_ref[g]
                dout = do_ref[rows, cols].astype(F32)
                du_ref[rows, cols] = (dout * mixed * dgu).astype(BF16)
                dmix = dout * gu
                dmb = dmix.astype(BF16)
                dv_ref[rows, cols] = (_dot_tn(wg, dmb) * dgv).astype(BF16)
                dw_acc += _dot_nt(dmb, gvb)
                db_acc += jnp.sum(dmix, axis=1, keepdims=True)
            dw_ref[g] += _tril(dw_acc)
            db_ref[g] += jnp.broadcast_to(db_acc, (CHUNK, CHUNK))

    full = pl.BlockSpec((ng, CHUNK, CHUNK), lambda i: (0, 0, 0))
    out = pl.BlockSpec((tq, gw), lambda i: (i, 0))
    return pl.pallas_call(
        body, name="sgu_bwd", grid=(nsteps,),
        in_specs=[pl.BlockSpec((tq, gw), lambda i: (i, u_blk)), pl.BlockSpec((tq, gw), lambda i: (i, u_blk + 1)), full, full,
                  pl.BlockSpec((tq, gw), lambda i: (i, da_blk))],
        out_specs=[out, out, full, full],
        out_shape=[jax.ShapeDtypeStruct((s, gw), BF16)] * 2 + [jax.ShapeDtypeStruct((ng, CHUNK, CHUNK), F32)] * 2,
        compiler_params=_cparams("arbitrary"),
    )(z, z, w_s, b_col, da)


def _shift_down(y, halo, k):
    rolled = pltpu.roll(y, k, 0)
    row = lax.broadcasted_iota(jnp.int32, y.shape, 0)
    for j in range(k):
        rolled = jnp.where(row == j, halo[8 - k + j:8 - k + j + 1, :], rolled)
    return rolled


def _shift_up(y, halo, k):
    n = y.shape[0]
    rolled = pltpu.roll(y, n - k, 0)
    row = lax.broadcasted_iota(jnp.int32, y.shape, 0)
    for j in range(k):
        rolled = jnp.where(row == n - k + j, halo[j:j + 1, :], rolled)
    return rolled


def _conv_fwd(z, cw):
    s, d3 = z.shape
    d = d3 // 3
    tq = _tile(s, 256, 8)

    def body(z_ref, zh_ref, cw_ref, a_ref):
        i = pl.program_id(0)
        zv = z_ref[...].astype(F32)
        zh = jnp.where(i > 0, zh_ref[...].astype(F32), 0.0)
        y = zv[:, d:2 * d] * zv[:, 2 * d:]
        yh = zh[:, d:2 * d] * zh[:, 2 * d:]
        cwv = cw_ref[...]
        conv = cwv[0:1] * _shift_down(y, yh, 2) + cwv[1:2] * _shift_down(y, yh, 1) + cwv[2:3] * y
        a_ref[...] = (zv[:, :d] * conv).astype(BF16)

    return pl.pallas_call(
        body, name="conv_fwd", grid=(s // tq,),
        in_specs=[pl.BlockSpec((tq, d3), lambda i: (i, 0)),
                  pl.BlockSpec((8, d3), lambda i: (jnp.maximum(i * (tq // 8) - 1, 0), 0)),
                  pl.BlockSpec((3, d), lambda i: (0, 0))],
        out_specs=pl.BlockSpec((tq, d), lambda i: (i, 0)),
        out_shape=jax.ShapeDtypeStruct((s, d), BF16),
        compiler_params=_cparams("parallel"),
    )(z, z, cw)


def _conv_bwd(z, cw, da):
    s, d3 = z.shape
    d = d3 // 3
    tq = _tile(s, 128, 8)
    nsteps = s // tq
    nblk8 = s // 8

    def body(z_ref, zp_ref, zn_ref, da_ref, dan_ref, cw_ref, dz_ref, dcw_ref, acc):
        i = pl.program_id(0)

        @pl.when(i == 0)
        def _():
            acc[...] = jnp.zeros_like(acc)

        zv = z_ref[...].astype(F32)
        zp = jnp.where(i > 0, zp_ref[...].astype(F32), 0.0)
        zn = jnp.where(i < nsteps - 1, zn_ref[...].astype(F32), 0.0)
        dav = da_ref[...].astype(F32)
        dan = jnp.where(i < nsteps - 1, dan_ref[...].astype(F32), 0.0)
        gb, gc, hx = zv[:, :d], zv[:, d:2 * d], zv[:, 2 * d:]
        y = gc * hx
        yp = zp[:, d:2 * d] * zp[:, 2 * d:]
        cwv = cw_ref[...]
        y1, y2 = _shift_down(y, yp, 1), _shift_down(y, yp, 2)
        conv = cwv[0:1] * y2 + cwv[1:2] * y1 + cwv[2:3] * y
        dconv = dav * gb
        dconv_n = dan * zn[:, :d]
        dy = cwv[2:3] * dconv + cwv[1:2] * _shift_up(dconv, dconv_n, 1) + cwv[0:1] * _shift_up(dconv, dconv_n, 2)
        dz_ref[:, :d] = (dav * conv).astype(BF16)
        dz_ref[:, d:2 * d] = (dy * hx).astype(BF16)
        dz_ref[:, 2 * d:] = (dy * gc).astype(BF16)
        acc[0] += _colsum8(dconv * y2)
        acc[1] += _colsum8(dconv * y1)
        acc[2] += _colsum8(dconv * y)

        @pl.when(i == nsteps - 1)
        def _():
            for j in range(3):
                dcw_ref[j:j + 1, :] = jnp.sum(acc[j], axis=0, keepdims=True)

    return pl.pallas_call(
        body, name="conv_bwd", grid=(nsteps,),
        in_specs=[pl.BlockSpec((tq, d3), lambda i: (i, 0)),
                  pl.BlockSpec((8, d3), lambda i: (jnp.maximum(i * (tq // 8) - 1, 0), 0)),
                  pl.BlockSpec((8, d3), lambda i: (jnp.minimum((i + 1) * (tq // 8), nblk8 - 1), 0)),
                  pl.BlockSpec((tq, d), lambda i: (i, 0)),
                  pl.BlockSpec((8, d), lambda i: (jnp.minimum((i + 1) * (tq // 8), nblk8 - 1), 0)),
                  pl.BlockSpec((3, d), lambda i: (0, 0))],
        out_specs=[pl.BlockSpec((tq, d3), lambda i: (i, 0)), pl.BlockSpec((3, d), lambda i: (0, 0))],
        out_shape=[jax.ShapeDtypeStruct((s, d3), BF16), jax.ShapeDtypeStruct((3, d), F32)],
        scratch_shapes=[pltpu.VMEM((3, 8, d), F32)],
        compiler_params=_cparams("arbitrary"),
    )(z, z, z, da, da, cw)


def _swiglu_fwd(gt, up):
    s, f = gt.shape
    tm, tn = _tile(s, 512, 8), _tile(f, 2816, 128)

    def body(g_ref, u_ref, a_ref):
        g = g_ref[...].astype(F32)
        a_ref[...] = (g / (1.0 + jnp.exp(-g)) * u_ref[...].astype(F32)).astype(BF16)

    blk = pl.BlockSpec((tm, tn), lambda i, j: (i, j))
    return pl.pallas_call(
        body, name="swiglu_fwd", grid=(s // tm, f // tn), in_specs=[blk, blk], out_specs=blk,
        out_shape=jax.ShapeDtypeStruct((s, f), BF16), compiler_params=_cparams("parallel", "parallel"),
    )(gt, up)


def _swiglu_bwd_epilogue(dact, gt, up):
    g, u = gt.astype(F32), up.astype(F32)
    sg = 1.0 / (1.0 + jnp.exp(-g))
    return dact * u * (sg * (1.0 + g * (1.0 - sg))), dact * (g * sg)


def _ada_fwd(c_all, ada_w):
    nl, d, n8 = ada_w.shape
    tn = _tile(n8, 768, 128)

    def body(c_ref, w_ref, o_ref):
        cv = c_ref[...]
        act = (cv / (1.0 + jnp.exp(-cv))).astype(BF16)
        o_ref[0] = _dot_nn(act, w_ref[0].astype(BF16))

    return pl.pallas_call(
        body, name="ada_fwd", grid=(nl, n8 // tn),
        in_specs=[pl.BlockSpec((N_DEV, d), lambda l, j: (0, 0)), pl.BlockSpec((1, d, tn), lambda l, j: (l, 0, j))],
        out_specs=pl.BlockSpec((1, N_DEV, tn), lambda l, j: (l, 0, j)),
        out_shape=jax.ShapeDtypeStruct((nl, N_DEV, n8), F32),
        compiler_params=_cparams("parallel", "parallel"),
    )(c_all, ada_w)


def _ada_wgrad(c_all, dmod_cols):
    nl, _, n8 = dmod_cols.shape
    d = c_all.shape[1]
    tn = _tile(n8, 768, 128)

    def body(c_ref, g_ref, o_ref):
        cv = c_ref[...]
        act = (cv / (1.0 + jnp.exp(-cv))).astype(BF16)
        o_ref[0] = _dot_tn(act, g_ref[0].astype(BF16))

    return pl.pallas_call(
        body, name="ada_wgrad", grid=(nl, n8 // tn),
        in_specs=[pl.BlockSpec((N_DEV, d), lambda l, j: (0, 0)), pl.BlockSpec((1, N_DEV, tn), lambda l, j: (l, 0, j))],
        out_specs=pl.BlockSpec((1, d, tn), lambda l, j: (l, 0, j)),
        out_shape=jax.ShapeDtypeStruct((nl, d, n8), F32),
        compiler_params=_cparams("parallel", "parallel"),
    )(c_all, dmod_cols)


def _adamw(name, pieces, w, m, v):
    npc, r, c = pieces.shape
    tr = _tile(r, max(8, (1 << 19) // c // 8 * 8), 8)
    bc1, bc2 = 1.0 - ADAM_B1 ** ADAM_STEP, 1.0 - ADAM_B2 ** ADAM_STEP

    def body(p_ref, w_ref, m_ref, v_ref, g_ref, d_ref, nm_ref, nv_ref):
        g = p_ref[0].astype(F32)
        for i in range(1, npc):
            g = g + p_ref[i].astype(F32)
        nm = ADAM_B1 * m_ref[...] + (1.0 - ADAM_B1) * g
        nv = ADAM_B2 * v_ref[...] + (1.0 - ADAM_B2) * (g * g)
        g_ref[...] = g
        nm_ref[...] = nm
        nv_ref[...] = nv
        d_ref[...] = -ADAM_LR * ((nm / bc1) / (jnp.sqrt(nv / bc2) + ADAM_EPS) + ADAM_WD * w_ref[...])

    blk = pl.BlockSpec((tr, c), lambda i: (i, 0))
    return pl.pallas_call(
        body, name=name, grid=(r // tr,),
        in_specs=[pl.BlockSpec((npc, tr, c), lambda i: (0, i, 0)), blk, blk, blk],
        out_specs=[blk] * 4, out_shape=[jax.ShapeDtypeStruct((r, c), F32)] * 4,
        compiler_params=_cparams("parallel"),
    )(pieces, w, m, v)


def _place():
    x, y, c = lax.axis_index("x"), lax.axis_index("y"), lax.axis_index("c")
    return x, y, c


def _all_gather_small(name, x_shard):
    m_per, n = x_shard.shape

    def body(x_ref, out_ref, token_ref, send_sems, recv_sems, local_sem):
        token_ref[...] = jnp.zeros_like(token_ref)
        x, y, c = _place()
        me, sibling = (x, y, c), (x, y, 1 - c)
        chips = [(1 - x, y), (x, 1 - y), (1 - x, 1 - y)]

        def rows(px, py, pc):
            return out_ref.at[pl.ds((4 * px + 2 * py + pc) * m_per, m_per), :]

        def copy(k, block, to, src=None):
            return pltpu.make_async_remote_copy(
                src_ref=rows(*block) if src is None else src, dst_ref=rows(*block),
                send_sem=send_sems.at[k], recv_sem=recv_sems.at[k], device_id=to, device_id_type=MESH)

        mine = pltpu.make_async_copy(x_ref, rows(*me), local_sem)
        mine.start()
        first = [copy(0, me, sibling, src=x_ref)]
        first += [copy(1 + j, me, (*chip, c), src=x_ref) for j, chip in enumerate(chips)]
        for cp in first:
            cp.start()
        passed = [copy(4 + j, (*chip, c), sibling) for j, chip in enumerate(chips)]
        for j, chip in enumerate(chips):
            copy(1 + j, (*chip, c), me).wait_recv()
            passed[j].start()
        copy(0, sibling, me).wait_recv()
        for j, chip in enumerate(chips):
            copy(4 + j, (*chip, 1 - c), me).wait_recv()
        for cp in first + passed:
            cp.wait_send()
        mine.wait()

    vmem = pl.BlockSpec(memory_space=pltpu.VMEM)
    return pl.pallas_call(
        body, name=name,
        out_shape=[jax.ShapeDtypeStruct((N_DEV * m_per, n), x_shard.dtype), jax.ShapeDtypeStruct((8, HEAD), F32)],
        in_specs=[vmem], out_specs=[vmem, vmem],
        scratch_shapes=[pltpu.SemaphoreType.DMA((7,)), pltpu.SemaphoreType.DMA((7,)), pltpu.SemaphoreType.DMA],
        compiler_params=pltpu.CompilerParams(vmem_limit_bytes=VMEM_LIMIT),
    )(x_shard)


_HBM = pl.BlockSpec(memory_space=pltpu.HBM)
_SEM = pl.BlockSpec(memory_space=pltpu.SEMAPHORE)
_EFFECT = pltpu.SideEffectType.DATAFLOW_SIDE_EFFECTING


def _peers():
    x, y, c = _place()
    peers = []
    for k in range(1, N_DEV):
        px = 1 - x if k & 4 else x
        py = 1 - y if k & 2 else y
        pc = 1 - c if k & 1 else c
        peers.append(((px, py, pc), 4 * px + 2 * py + pc))
    return 4 * x + 2 * y + c, peers


def _push_start(name, srcs, lands, src_view, dst_view):
    na = len(srcs)
    n = na * (N_DEV - 1)

    def body(*refs):
        s_refs, l_refs = refs[:na], refs[na:2 * na]
        send_sems, recv_sems, token = refs[2 * na], refs[2 * na + 1], refs[-1]
        me, peers = _peers()
        for a in range(na):
            for k, (dev, idx) in enumerate(peers):
                pltpu.make_async_remote_copy(
                    src_ref=src_view(s_refs[a], idx), dst_ref=dst_view(l_refs[a], me),
                    send_sem=send_sems.at[a * (N_DEV - 1) + k], recv_sem=recv_sems.at[a * (N_DEV - 1) + k],
                    device_id=dev, device_id_type=MESH).start()
        token[...] = jnp.zeros_like(token)

    outs = pl.pallas_call(
        body, name=name,
        out_shape=(pltpu.SemaphoreType.DMA((n,)), pltpu.SemaphoreType.DMA((n,)),
                   *[pltpu.HBM(t.shape, t.dtype) for t in list(srcs) + list(lands)], jax.ShapeDtypeStruct((8, HEAD), F32)),
        in_specs=[_HBM] * (2 * na),
        out_specs=(_SEM, _SEM, *[_HBM] * (2 * na), pl.BlockSpec(memory_space=pltpu.VMEM)),
        input_output_aliases={i: 2 + i for i in range(2 * na)},
        compiler_params=pltpu.CompilerParams(has_side_effects=_EFFECT),
    )(*[pltpu.with_memory_space_constraint(t, pltpu.HBM) for t in list(srcs) + list(lands)])
    return outs[0], outs[1], list(outs[2:2 + na]), list(outs[2 + na:2 + 2 * na]), outs[-1]


def _push_wait(name, send_sems, recv_sems, srcs, lands, after, src_view, dst_view):
    na = len(srcs)

    def body(*refs):
        s_refs, l_refs = refs[:na], refs[na:2 * na]
        send_sems, recv_sems = refs[2 * na], refs[2 * na + 1]
        me, peers = _peers()
        for a in range(na):
            for k, (dev, idx) in enumerate(peers):
                cp = pltpu.make_async_remote_copy(
                    src_ref=src_view(s_refs[a], idx), dst_ref=dst_view(l_refs[a], idx),
                    send_sem=send_sems.at[a * (N_DEV - 1) + k], recv_sem=recv_sems.at[a * (N_DEV - 1) + k],
                    device_id=dev, device_id_type=MESH)
                cp.wait_send()
                cp.wait_recv()

    outs = pl.pallas_call(
        body, name=name,
        out_shape=[pltpu.HBM(t.shape, t.dtype) for t in list(srcs) + list(lands)],
        in_specs=[_HBM] * (2 * na) + [_SEM, _SEM, pl.BlockSpec(memory_space=pl.ANY)],
        out_specs=[_HBM] * (2 * na),
        input_output_aliases={i: i for i in range(2 * na)},
        compiler_params=pltpu.CompilerParams(has_side_effects=_EFFECT),
    )(*srcs, *lands, send_sems, recv_sems, after)
    return list(outs[na:])


def _gather_start(name, shards):
    lands = [lax.empty((N_DEV,) + t.shape, t.dtype) for t in shards]
    return _push_start(name, shards, lands, lambda ref, idx: ref, lambda ref, slot: ref.at[slot])


def _gather_wait(name, started, shards, after, me):
    send_sems, recv_sems, srcs, lands, _ = started
    lands = _push_wait(name, send_sems, recv_sems, srcs, lands, after, lambda ref, idx: ref, lambda ref, slot: ref.at[slot])
    return [lax.dynamic_update_index_in_dim(g, t, me, 0) for g, t in zip(lands, shards)]


def _cols_to_plain(g):
    n, k, n8 = g.shape
    return jnp.transpose(g, (1, 0, 2)).reshape(k, n * n8)


def _plain_to_cols(w):
    k, n = w.shape
    return jnp.transpose(w.reshape(k, N_DEV, n // N_DEV), (1, 0, 2))


def _pack_rows(parts, width):
    rows, offs, r = [], [], 0
    for p in parts:
        flat = p.reshape(-1).astype(F32)
        nr = -(-flat.shape[0] // (8 * width)) * 8
        rows.append(jnp.pad(flat, (0, nr * width - flat.shape[0])).reshape(nr, width))
        offs.append((r, flat.shape[0], p.shape))
        r += nr
    return jnp.concatenate(rows, axis=0), offs, r


def _unpack_rows(slab, offs, width):
    lead = slab.shape[:-2]
    out = []
    for r0, n, shape in offs:
        nr = -(-n // width)
        out.append(slab[..., r0:r0 + nr, :].reshape(lead + (nr * width,))[..., :n].reshape(lead + tuple(shape)))
    return out


def kernel(x, c, positions, ada_w, ada_b, norm_mix, norm_ffn, ab_w_in, sgu_w, sgu_b, ab_w_out, conv_w_in, conv_w, conv_w_out, ffn_w_gate, ffn_w_up, ffn_w_down, final_norm, loss_target, m_ada_w, m_ada_b, m_norm_mix, m_norm_ffn, m_ab_w_in, m_sgu_w, m_sgu_b, m_ab_w_out, m_conv_w_in, m_conv_w, m_conv_w_out, m_ffn_w_gate, m_ffn_w_up, m_ffn_w_down, m_final_norm, v_ada_w, v_ada_b, v_norm_mix, v_norm_ffn, v_ab_w_in, v_sgu_w, v_sgu_b, v_ab_w_out, v_conv_w_in, v_conv_w, v_conv_w_out, v_ffn_w_gate, v_ffn_w_up, v_ffn_w_down, v_final_norm):
    xi, yi, ci = _place()
    me = 4 * xi + 2 * yi + ci
    s, d = x.shape[1], x.shape[2]
    depth = ada_w.shape[0]
    n_even = ab_w_in.shape[0]
    nh_mix = d // HEAD
    nh = 3 * nh_mix // 4
    ng = nh_mix - nh
    aw, gw = nh * HEAD, ng * HEAD
    x0 = x[0]
    target = loss_target[0]
    n_odd, cwid, d8 = conv_w.shape

    width = 512
    slab, offs, _ = _pack_rows([c, conv_w], width)
    gathered, _ = _all_gather_small("gather_cond", slab)
    c_parts, cw_parts = _unpack_rows(gathered.reshape(N_DEV, -1, width), offs, width)
    c_all = c_parts.reshape(N_DEV, d)
    conv_w_full = jnp.transpose(cw_parts, (1, 2, 0, 3)).reshape(n_odd, cwid, d)

    mod_cols = _ada_fwd(c_all, ada_w)
    n8 = mod_cols.shape[2]
    mod_all, token = _all_gather_small("gather_mod", mod_cols.reshape(depth * N_DEV, n8))
    mod_mine = lax.dynamic_index_in_dim(mod_all.reshape(N_DEV, depth, N_DEV, n8), me, axis=2, keepdims=False)

    def mixer_weights(l):
        return (ab_w_in[l // 2], ab_w_out[l // 2]) if l % 2 == 0 else (conv_w_in[l // 2], conv_w_out[l // 2])

    groups = [[mixer_weights(0)[0]], [mixer_weights(0)[1]], [ffn_w_gate[0], ffn_w_up[0], ffn_w_down[0]]]
    groups += [[*mixer_weights(l), ffn_w_gate[l], ffn_w_up[l], ffn_w_down[l]] for l in range(1, depth)]
    gathers, tok = [], token[0, 0]
    for n, ws in enumerate(groups):
        shards = [(w + tok).astype(BF16) for w in ws]
        started = _gather_start(f"gather_start_{n}", shards)
        gathers.append((started, shards))
        tok = started[4][0, 0]

    def weights_of_group(n, after):
        started, shards = gathers[n]
        return _gather_wait(f"gather_wait_{n}", started, shards, after, me)

    def plain_rows(g):
        return g.reshape(g.shape[0] * g.shape[1], g.shape[2])

    mod = jnp.transpose(mod_mine, (1, 0, 2)).reshape(depth, N_DEV * n8) + ada_b + tok
    mods = mod.reshape(depth, 6, 1, d)

    ct, st = _rope_tables(positions.reshape(s, 1))
    b_col = jnp.broadcast_to(sgu_b[..., None], sgu_b.shape + (CHUNK,))
    u_blk = 3 * aw // gw

    stream = [x0]
    saved = []
    w_in, w_out, w_gate, w_up, w_down = [[None] * depth for _ in range(5)]
    xcur = x0
    for l in range(depth):
        sh_m, sc_m, g_m, sh_f, sc_f, g_f = [mods[l, j] for j in range(6)]
        i = l // 2
        if l == 0:
            (g_in,) = weights_of_group(0, ct)
        else:
            g_in, g_out, g_gate, g_up, g_down = weights_of_group(2 + l, xcur)
        w_in[l] = _cols_to_plain(g_in)
        h = _norm_mod_fwd(xcur, norm_mix[l][None], sc_m, sh_m)
        if l % 2 == 0:
            z = _matmul("mix_in", "nn", h, w_in[l], [BF16])
            attn, lse = _attn_fwd(z, ct, st, nh)
            sgu = _sgu_fwd(z, sgu_w[i], b_col[i], ng, u_blk)
            a = jnp.concatenate([attn, sgu], axis=1)
            mixer_saved = (z, a, lse)
        else:
            z = _matmul("conv_in", "nn", h, w_in[l], [BF16])
            a = _conv_fwd(z, conv_w_full[i])
            mixer_saved = (z, a, None)
        if l == 0:
            (g_out,) = weights_of_group(1, a)
        w_out[l] = plain_rows(g_out)
        x1, mix = _matmul("mix_out", "nn", a, w_out[l], [F32, BF16], extras=[(xcur, "mn"), (g_m, "n")],
                          epilogue=lambda acc, r, gv: (r + gv * acc, acc))
        if l == 0:
            g_gate, g_up, g_down = weights_of_group(2, x1)
        w_gate[l], w_up[l], w_down[l] = _cols_to_plain(g_gate), _cols_to_plain(g_up), plain_rows(g_down)
        h2 = _norm_mod_fwd(x1, norm_ffn[l][None], sc_f, sh_f)
        gt = _matmul("ffn_gate", "nn", h2, w_gate[l], [BF16])
        up = _matmul("ffn_up", "nn", h2, w_up[l], [BF16])
        act = _swiglu_fwd(gt, up)
        x2, f = _matmul("ffn_down", "nn", act, w_down[l], [F32, BF16], extras=[(x1, "mn"), (g_f, "n")],
                        epilogue=lambda acc, r, gv: (r + gv * acc, acc))
        saved.append((h, mixer_saved, mix, x1, h2, gt, up, act, f))
        stream.append(x2)
        xcur = x2

    f_last = saved[-1][8]
    loss_part, dx, dbr, d_final, dg = _final_loss(xcur, target, final_norm[None], f_last, mods[depth - 1, 5])
    loss = lax.psum(loss_part[0, 0], ("x", "y", "c"))

    dmod = [[None] * 6 for _ in range(depth)]
    d_norm_mix, d_norm_ffn = [None] * depth, [None] * depth
    d_sgu_w, d_sgu_b, d_conv_w = [None] * n_even, [None] * n_even, [None] * n_odd

    big = {"in": ab_w_in, "out": ab_w_out, "cin": conv_w_in, "cout": conv_w_out,
           "gate": ffn_w_gate, "up": ffn_w_up, "down": ffn_w_down}
    lands = {k: lax.empty((N_DEV,) + w.shape, BF16) for k, w in big.items()}
    own = {k: [None] * w.shape[0] for k, w in big.items()}
    pending = {"ffn": None, "mix": None}

    def exchange_finish(tag, after):
        (send_sems, recv_sems, srcs, lds, _), keys, li, layer = pending[tag]
        lds = _push_wait(f"exchange_wait_{tag}_{layer}", send_sems, recv_sems, srcs, lds, after,
                         lambda ref, idx: ref.at[idx], lambda ref, slot: ref.at[slot, li])
        for k, ld in zip(keys, lds):
            lands[k] = ld
        pending[tag] = None

    def exchange_start(tag, layer, keys, li, grads):
        if pending[tag] is not None:
            exchange_finish(tag, grads[0])
        for k, g in zip(keys, grads):
            own[k][li] = lax.dynamic_index_in_dim(g, me, 0, keepdims=False)
        started = _push_start(f"exchange_start_{tag}_{layer}", grads, [lands[k] for k in keys],
                              lambda ref, idx: ref.at[idx], lambda ref, slot: ref.at[slot, li])
        pending[tag] = (started, keys, li, layer)
        return started[4][0, 0]

    def row_shards(g):
        return g.reshape(N_DEV, g.shape[0] // N_DEV, g.shape[1])

    for l in reversed(range(depth)):
        sh_m, sc_m, g_m, sh_f, sc_f, g_f = [mods[l, j] for j in range(6)]
        h, (z, a, lse), mix, x1, h2, gt, up, act, f = saved[l]
        i = l // 2
        dmod[l][5] = dg
        gw_down = _matmul("ffn_down_wgrad", "tn", act, dbr, [BF16], tn=1024)
        dgt, dup = _matmul("ffn_down_dgrad", "nt", dbr, w_down[l], [BF16, BF16], extras=[(gt, "mn"), (up, "mn")],
                           epilogue=_swiglu_bwd_epilogue)
        gw_gate = _matmul("ffn_in_wgrad", "tn", h2, dgt, [BF16])
        gw_up = _matmul("ffn_in_wgrad", "tn", h2, dup, [BF16])
        tok = exchange_start("ffn", l, ("gate", "up", "down"), l, [_plain_to_cols(gw_gate), _plain_to_cols(gw_up), row_shards(gw_down)])
        dh2 = _matmul("ffn_gate_dgrad", "nt", dgt, w_gate[l], [F32], after=tok)
        dh2 = _matmul("ffn_up_dgrad", "nt", dup, w_up[l], [F32], extras=[(dh2, "mn")], epilogue=lambda acc, prev: (acc + prev,))
        dx, dbr, dmod[l][3], dmod[l][4], d_norm_ffn[l], dg = _norm_mod_bwd(x1, dh2, norm_ffn[l][None], sc_f, dx, mix, g_m)
        dmod[l][2] = dg
        gw_out = _matmul("mix_out_wgrad", "tn", a, dbr, [BF16])
        da = _matmul("mix_out_dgrad", "nt", dbr, w_out[l], [BF16])
        if l % 2 == 0:
            dq, dk, dv = _attn_bwd(z, ct, st, da, a, lse, nh)
            du, dvg, d_sgu_w[i], dbb = _sgu_bwd(z, sgu_w[i], b_col[i], da, ng, u_blk, aw // gw)
            d_sgu_b[i] = dbb[:, :, 0]
            dz = jnp.concatenate([dq, dk, dv, du, dvg], axis=1)
            gw_in = _matmul("mix_in_wgrad", "tn", h, dz, [BF16])
            dh = _matmul("mix_in_dgrad", "nt", dz, w_in[l], [F32])
        else:
            dz, d_conv_w[i] = _conv_bwd(z, conv_w_full[i], da)
            gw_in = _matmul("conv_in_wgrad", "tn", h, dz, [BF16])
            dh = _matmul("conv_in_dgrad", "nt", dz, w_in[l], [F32])
        mix_group = ("mix", l, ("in", "out") if l % 2 == 0 else ("cin", "cout"), i, [_plain_to_cols(gw_in), row_shards(gw_out)])
        w_norm = norm_mix[l][None]
        if l > 0:
            w_norm = w_norm + exchange_start(*mix_group)
            f_prev, g_prev = saved[l - 1][8], mods[l - 1, 5]
            dx, dbr, dmod[l][0], dmod[l][1], d_norm_mix[l], dg = _norm_mod_bwd(stream[l], dh, w_norm, sc_m, dx, f_prev, g_prev)
        else:
            dx, dmod[l][0], dmod[l][1], d_norm_mix[l] = _norm_mod_bwd(stream[l], dh, w_norm, sc_m, dx)
    grad_x = dx[None]

    dmod_mine = jnp.stack([jnp.concatenate([v.reshape(d) for v in dmod[l]]) for l in range(depth)])
    small = [dmod_mine, jnp.concatenate(d_norm_mix), jnp.concatenate(d_norm_ffn), jnp.stack(d_sgu_w), jnp.stack(d_sgu_b),
             d_final, jnp.stack(d_conv_w)]
    slab, offs, _ = _pack_rows(small, width)
    gathered, token = _all_gather_small("gather_small_grads", slab)
    p_dmod, p_nmix, p_nffn, p_sguw, p_sgub, p_final, p_convw = _unpack_rows(gathered.reshape(N_DEV, -1, width), offs, width)
    mix_group[4][0] = mix_group[4][0] + token[0, 0].astype(BF16)
    p_dmod = p_dmod + exchange_start(*mix_group)

    outs = {}

    def update(name, pieces, w, m, v):
        shape = w.shape
        cdim = shape[-1]
        res = _adamw("adamw_" + name, pieces.reshape(pieces.shape[0], -1, cdim), w.reshape(-1, cdim),
                     m.reshape(-1, cdim), v.reshape(-1, cdim))
        outs[name] = [r.reshape(shape) for r in res]

    update("ada_b", p_dmod.reshape(N_DEV, depth, 6 * d), ada_b, m_ada_b, v_ada_b)
    update("norm_mix", p_nmix.reshape(N_DEV, depth, d), norm_mix, m_norm_mix, v_norm_mix)
    update("norm_ffn", p_nffn.reshape(N_DEV, depth, d), norm_ffn, m_norm_ffn, v_norm_ffn)
    update("sgu_w", p_sguw, sgu_w, m_sgu_w, v_sgu_w)
    update("sgu_b", p_sgub.reshape(N_DEV, 1, -1), sgu_b.reshape(1, -1), m_sgu_b.reshape(1, -1), v_sgu_b.reshape(1, -1))
    outs["sgu_b"] = [r.reshape(sgu_b.shape) for r in outs["sgu_b"]]
    update("final_norm", p_final.reshape(N_DEV, 1, d), final_norm[None], m_final_norm[None], v_final_norm[None])
    outs["final_norm"] = [r.reshape(final_norm.shape) for r in outs["final_norm"]]
    cw_mine = lax.dynamic_slice_in_dim(p_convw.reshape(N_DEV, n_odd, cwid, d), me * d8, d8, axis=3)
    update("conv_w", cw_mine, conv_w, m_conv_w, v_conv_w)

    dmod_cols = lax.dynamic_slice_in_dim(p_dmod.reshape(N_DEV, depth, 6 * d), me * n8, n8, axis=2)
    g_ada = _ada_wgrad(c_all, jnp.transpose(dmod_cols, (1, 0, 2)))
    update("ada_w", g_ada[None], ada_w, m_ada_w, v_ada_w)

    names = {"in": "ab_w_in", "out": "ab_w_out", "cin": "conv_w_in", "cout": "conv_w_out",
             "gate": "ffn_w_gate", "up": "ffn_w_up", "down": "ffn_w_down"}
    moments = {"in": (m_ab_w_in, v_ab_w_in), "out": (m_ab_w_out, v_ab_w_out), "cin": (m_conv_w_in, v_conv_w_in),
               "cout": (m_conv_w_out, v_conv_w_out), "gate": (m_ffn_w_gate, v_ffn_w_gate), "up": (m_ffn_w_up, v_ffn_w_up),
               "down": (m_ffn_w_down, v_ffn_w_down)}

    def update_big(k):
        pieces = lax.dynamic_update_slice(lands[k], jnp.stack(own[k])[None], (me, 0, 0, 0))
        update(names[k], pieces, big[k], *moments[k])

    exchange_finish("ffn", g_ada)
    for k in ("gate", "up", "down"):
        update_big(k)
    exchange_finish("mix", outs["ffn_w_down"][0])
    for k in ("cin", "cout", "in", "out"):
        update_big(k)

    order = ["ada_w", "ada_b", "norm_mix", "norm_ffn", "ab_w_in", "sgu_w", "sgu_b", "ab_w_out", "conv_w_in", "conv_w",
             "conv_w_out", "ffn_w_gate", "ffn_w_up", "ffn_w_down", "final_norm"]
    return (loss, grad_x, *[outs[n][0] for n in order], *[outs[n][1] for n in order],
            *[outs[n][2] for n in order], *[outs[n][3] for n in order])
```

```python
import functools
import math

import numpy as np
import jax
import jax.numpy as jnp
from jax import lax
from jax.experimental import pallas as pl
from jax.experimental.pallas import tpu as pltpu

F32, BF16 = jnp.float32, jnp.bfloat16
MESH = pl.DeviceIdType.MESH
N_DEV = 8
EPS = 1e-6
HEAD = 128
CHUNK = 128
DILATIONS = (1, 4, 16)
ATT_CHUNK = CHUNK * DILATIONS[-1]
ATT_BATCH = 8
ROPE_THETA = 500000.0
ROPE_DIM = HEAD // 4
NEG = -1e30
ADAM_LR, ADAM_B1, ADAM_B2, ADAM_EPS, ADAM_WD, ADAM_STEP = 0.001, 0.9, 0.999, 1e-08, 0.01, 10
VMEM_LIMIT = 56 * 1024 * 1024


def _cparams(*sem):
    return pltpu.CompilerParams(dimension_semantics=sem or None, vmem_limit_bytes=VMEM_LIMIT)


def _tile(n, pref, unit):
    t = (min(pref, n) // unit) * unit
    while t >= unit:
        if n % t == 0:
            return t
        t -= unit
    return n


def _dot(a, b, dims):
    return lax.dot_general(a, b, (dims, ((), ())), preferred_element_type=F32)


def _dot_nn(a, b):
    return _dot(a, b, ((1,), (0,)))


def _dot_nt(a, b):
    return _dot(a, b, ((1,), (1,)))


def _dot_tn(a, b):
    return _dot(a, b, ((0,), (0,)))


def _matmul(name, mode, a, b, outs, *, extras=(), epilogue=None, after=None, tm=1024, tn=512, tk=2816):
    a_list = list(a) if isinstance(a, (list, tuple)) else [a]
    b_list = list(b) if isinstance(b, (list, tuple)) else [b]
    na, nb = len(a_list), len(b_list)
    paired = na > 1
    assert na == nb if paired else na == 1
    nacc = 1 if paired else nb
    a0, b0 = a_list[0], b_list[0]
    if mode == "nn":
        (m, kk), (_, n) = a0.shape, b0.shape
    elif mode == "nt":
        (m, kk), (n, _) = a0.shape, b0.shape
    else:
        (kk, m), (_, n) = a0.shape, b0.shape
    if after is not None:
        inner = epilogue or (lambda *accs: accs)
        extras = list(extras) + [(jnp.zeros((1, n), F32) + after, "n")]
        epilogue = lambda *tiles: inner(*tiles[:-1])
    tm, tn, tk = _tile(m, tm, 128), _tile(n, tn, 128), _tile(kk, tk, 128)
    nk = kk // tk
    dotf = {"nn": _dot_nn, "nt": _dot_nt, "tn": _dot_tn}[mode]
    a_spec = pl.BlockSpec((tk, tm), lambda i, j, k: (k, i)) if mode == "tn" else pl.BlockSpec((tm, tk), lambda i, j, k: (i, k))
    b_spec = pl.BlockSpec((tn, tk), lambda i, j, k: (j, k)) if mode == "nt" else pl.BlockSpec((tk, tn), lambda i, j, k: (k, j))
    e_specs = [pl.BlockSpec((tm, tn), lambda i, j, k: (i, j)) if kind == "mn" else pl.BlockSpec((1, tn), lambda i, j, k: (0, j))
               for _, kind in extras]
    ne, no = len(extras), len(outs)
    epi = epilogue or (lambda *accs: accs)

    def body(*refs):
        a_refs, b_refs, rest = refs[:na], refs[na:na + nb], refs[na + nb:]
        e_refs, o_refs, acc_refs = rest[:ne], rest[ne:ne + no], rest[ne + no:]

        def products():
            if paired:
                p = dotf(a_refs[0][...], b_refs[0][...])
                for a_ref, b_ref in zip(a_refs[1:], b_refs[1:]):
                    p = p + dotf(a_ref[...], b_ref[...])
                return [p]
            av = a_refs[0][...]
            return [dotf(av, b_ref[...]) for b_ref in b_refs]

        def finish(accs):
            for o_ref, o in zip(o_refs, epi(*accs, *[r[...] for r in e_refs])):
                o_ref[...] = o.astype(o_ref.dtype)

        if nk == 1:
            finish(products())
            return
        k = pl.program_id(2)

        @pl.when(k == 0)
        def _():
            for acc_ref in acc_refs:
                acc_ref[...] = jnp.zeros_like(acc_ref)

        for acc_ref, p in zip(acc_refs, products()):
            acc_ref[...] += p

        @pl.when(k == nk - 1)
        def _():
            finish([acc_ref[...] for acc_ref in acc_refs])

    res = pl.pallas_call(
        body, name=name, grid=(m // tm, n // tn, nk),
        in_specs=[a_spec] * na + [b_spec] * nb + e_specs,
        out_specs=[pl.BlockSpec((tm, tn), lambda i, j, k: (i, j)) for _ in outs],
        out_shape=[jax.ShapeDtypeStruct((m, n), dt) for dt in outs],
        scratch_shapes=[pltpu.VMEM((tm, tn), F32)] * nacc if nk > 1 else [],
        compiler_params=_cparams("parallel", "parallel", "arbitrary"),
    )(*a_list, *b_list, *[e for e, _ in extras])
    return res[0] if no == 1 else res


def _norm_mod_fwd(x, w, sc, sh):
    s, d = x.shape
    tm = _tile(s, 512, 8)

    def body(x_ref, w_ref, sc_ref, sh_ref, h_ref):
        xv = x_ref[...]
        r = lax.rsqrt(jnp.mean(xv * xv, axis=-1, keepdims=True) + EPS)
        h_ref[...] = ((xv * r) * w_ref[...] * (1.0 + sc_ref[...]) + sh_ref[...]).astype(BF16)

    row = pl.BlockSpec((1, d), lambda i: (0, 0))
    return pl.pallas_call(
        body, name="norm_mod_fwd", grid=(s // tm,),
        in_specs=[pl.BlockSpec((tm, d), lambda i: (i, 0)), row, row, row],
        out_specs=pl.BlockSpec((tm, d), lambda i: (i, 0)),
        out_shape=jax.ShapeDtypeStruct((s, d), BF16),
        compiler_params=_cparams("parallel"),
    )(x, w, sc, sh)


def _colsum8(t):
    tm, d = t.shape
    return jnp.sum(t.reshape(tm // 8, 8, d), axis=0)


def _norm_mod_bwd(x, dh, w, sc, dres, branch=None, g=None):
    s, d = x.shape
    tm = _tile(s, 256, 8)
    nsteps = s // tm
    gated = branch is not None

    def body(*refs):
        if gated:
            x_ref, dh_ref, w_ref, sc_ref, dres_ref, br_ref, g_ref, dx_ref, dbr_ref, dsh_ref, dsc_ref, dw_ref, dg_ref, acc = refs
        else:
            x_ref, dh_ref, w_ref, sc_ref, dres_ref, dx_ref, dsh_ref, dsc_ref, dw_ref, acc = refs
        i = pl.program_id(0)

        @pl.when(i == 0)
        def _():
            acc[...] = jnp.zeros_like(acc)

        xv, dhv, wv, scv = x_ref[...], dh_ref[...].astype(F32), w_ref[...], sc_ref[...]
        r = lax.rsqrt(jnp.mean(xv * xv, axis=-1, keepdims=True) + EPS)
        xn = xv * r
        dxn = dhv * (wv * (1.0 + scv))
        dx = dres_ref[...] + r * (dxn - xn * jnp.mean(dxn * xn, axis=-1, keepdims=True))
        dx_ref[...] = dx
        acc[0] += _colsum8(dhv)
        acc[1] += _colsum8(dhv * xn)
        if gated:
            dbr_ref[...] = (dx * g_ref[...]).astype(BF16)
            acc[2] += _colsum8(dx * br_ref[...].astype(F32))

        @pl.when(i == nsteps - 1)
        def _():
            a0 = jnp.sum(acc[0], axis=0, keepdims=True)
            a1 = jnp.sum(acc[1], axis=0, keepdims=True)
            dsh_ref[...] = a0
            dsc_ref[...] = a1 * wv
            dw_ref[...] = a1 * (1.0 + scv)
            if gated:
                dg_ref[...] = jnp.sum(acc[2], axis=0, keepdims=True)

    big = pl.BlockSpec((tm, d), lambda i: (i, 0))
    row = pl.BlockSpec((1, d), lambda i: (0, 0))
    rowo = jax.ShapeDtypeStruct((1, d), F32)
    in_specs = [big, big, row, row, big] + ([big, row] if gated else [])
    out_specs = [big] + ([big] if gated else []) + [row, row, row] + ([row] if gated else [])
    out_shape = ([jax.ShapeDtypeStruct((s, d), F32)] + ([jax.ShapeDtypeStruct((s, d), BF16)] if gated else [])
                 + [rowo, rowo, rowo] + ([rowo] if gated else []))
    args = [x, dh, w, sc, dres] + ([branch, g] if gated else [])
    return pl.pallas_call(
        body, name="norm_mod_bwd_gated" if gated else "norm_mod_bwd", grid=(nsteps,),
        in_specs=in_specs, out_specs=out_specs, out_shape=out_shape,
        scratch_shapes=[pltpu.VMEM((3, 8, d), F32)],
        compiler_params=_cparams("arbitrary"),
    )(*args)


def _final_loss(x, target, w, branch, g):
    s, d = x.shape
    tm = _tile(s, 256, 8)
    nsteps = s // tm

    def body(x_ref, t_ref, w_ref, br_ref, g_ref, loss_ref, dx_ref, dbr_ref, dw_ref, dg_ref, acc):
        i = pl.program_id(0)

        @pl.when(i == 0)
        def _():
            acc[...] = jnp.zeros_like(acc)

        xv, wv = x_ref[...], w_ref[...]
        r = lax.rsqrt(jnp.mean(xv * xv, axis=-1, keepdims=True) + EPS)
        xn = xv * r
        err = xn * wv - t_ref[...]
        dy = err * (1.0 / d)
        dxn = dy * wv
        dx = r * (dxn - xn * jnp.mean(dxn * xn, axis=-1, keepdims=True))
        dx_ref[...] = dx
        dbr_ref[...] = (dx * g_ref[...]).astype(BF16)
        acc[0] += _colsum8(err * err)
        acc[1] += _colsum8(dy * xn)
        acc[2] += _colsum8(dx * br_ref[...].astype(F32))

        @pl.when(i == nsteps - 1)
        def _():
            loss_ref[...] = jnp.sum(jnp.sum(acc[0], axis=0, keepdims=True), axis=1, keepdims=True) * (0.5 / d)
            dw_ref[...] = jnp.sum(acc[1], axis=0, keepdims=True)
            dg_ref[...] = jnp.sum(acc[2], axis=0, keepdims=True)

    big = pl.BlockSpec((tm, d), lambda i: (i, 0))
    row = pl.BlockSpec((1, d), lambda i: (0, 0))
    rowo = jax.ShapeDtypeStruct((1, d), F32)
    return pl.pallas_call(
        body, name="final_loss", grid=(nsteps,),
        in_specs=[big, big, row, big, row],
        out_specs=[pl.BlockSpec((1, 1), lambda i: (0, 0)), big, big, row, row],
        out_shape=[jax.ShapeDtypeStruct((1, 1), F32), jax.ShapeDtypeStruct((s, d), F32),
                   jax.ShapeDtypeStruct((s, d), BF16), rowo, rowo],
        scratch_shapes=[pltpu.VMEM((3, 8, d), F32)],
        compiler_params=_cparams("arbitrary"),
    )(x, target, w, branch, g)


def _rope_tables(pos_col):
    s = pos_col.shape[0]
    tq = _tile(s, 1024, 8)
    half = ROPE_DIM // 2
    inv = np.float32(ROPE_THETA) ** (-np.arange(0, ROPE_DIM, 2, dtype=np.float32) / np.float32(ROPE_DIM))
    inv_row = jnp.asarray(np.tile(inv.astype(np.float32), HEAD // half)[None, :])

    def body(p_ref, inv_ref, ct_ref, st_ref):
        lane = lax.broadcasted_iota(jnp.int32, (tq, HEAD), 1)
        ang = p_ref[...].astype(F32) * inv_ref[...]
        cs, sn = jnp.cos(ang), jnp.sin(ang)
        ct_ref[...] = jnp.where(lane < ROPE_DIM, cs, 1.0)
        st_ref[...] = jnp.where(lane < half, -sn, jnp.where(lane < ROPE_DIM, sn, 0.0))

    blk = pl.BlockSpec((tq, HEAD), lambda i: (i, 0))
    return pl.pallas_call(
        body, name="rope_tables", grid=(s // tq,),
        in_specs=[pl.BlockSpec((tq, 1), lambda i: (i, 0)), pl.BlockSpec((1, HEAD), lambda i: (0, 0))],
        out_specs=[blk, blk],
        out_shape=[jax.ShapeDtypeStruct((s, HEAD), F32)] * 2,
        compiler_params=_cparams("parallel"),
    )(pos_col, inv_row)


def _swap_halves(x):
    lane = lax.broadcasted_iota(jnp.int32, x.shape, 1)
    half = ROPE_DIM // 2
    return jnp.where(lane < half, pltpu.roll(x, HEAD - half, 1), pltpu.roll(x, half, 1))


def _rope(x, ct, st):
    return x * ct + _swap_halves(x) * st


def _rope_t(dy, ct, st):
    return dy * ct - _swap_halves(dy) * st


def _band_bias(bias_ref):
    qi = lax.broadcasted_iota(jnp.int32, (CHUNK, 2 * CHUNK), 0)
    kj = lax.broadcasted_iota(jnp.int32, (CHUNK, 2 * CHUNK), 1)
    band = (kj >= qi) & (kj <= qi + CHUNK)
    bias_ref[0] = jnp.where(band, 0.0, NEG)
    bias_ref[1] = jnp.where(band & (kj >= CHUNK), 0.0, NEG)


def _rows(start, size, d):
    return pl.ds(start, size) if d == 1 else pl.ds(start, size, stride=d)


def _batch_units(ub, d, c):
    if d >= ATT_BATCH:
        per_sb = d // ATT_BATCH
        units = [(ub // per_sb, (ub % per_sb) * ATT_BATCH + j) for j in range(ATT_BATCH)]
    else:
        n_sb = ATT_BATCH // d
        units = [(ub * n_sb + t, r) for t in range(n_sb) for r in range(d)]
    out = []
    for sb, r in units:
        base = sb * (CHUNK * d)
        if not isinstance(base, int):
            base = pl.multiple_of(base, CHUNK)
        first = jnp.where((c == 0) & (sb == 0), 1, 0)
        out.append((_rows(base + r, CHUNK, d), _rows(base + (ATT_CHUNK - CHUNK * d + r), 2 * CHUNK, d), first))
    return out


def _for_batches(d, fn):
    n = ATT_CHUNK // CHUNK // ATT_BATCH
    if d >= ATT_BATCH:
        for ub in range(n):
            fn(ub)
    else:
        def step(ub, carry):
            fn(ub)
            return carry
        lax.fori_loop(0, n, step, 0)


def _bdot(a, b, ca, cb):
    return lax.dot_general(a, b, (((ca,), (cb,)), ((0,), (0,))), preferred_element_type=F32)


def _attn_specs(nh, nc):
    cur = lambda off: pl.BlockSpec((ATT_CHUNK, HEAD), lambda h, c: (jnp.minimum(c, nc - 1), off + h))
    prev = lambda off: pl.BlockSpec((ATT_CHUNK, HEAD), lambda h, c: (jnp.maximum(c - 1, 0), off + h))
    tcur = pl.BlockSpec((ATT_CHUNK, HEAD), lambda h, c: (jnp.minimum(c, nc - 1), 0))
    tprev = pl.BlockSpec((ATT_CHUNK, HEAD), lambda h, c: (jnp.maximum(c - 1, 0), 0))
    return [cur(0), prev(nh), cur(nh), prev(2 * nh), cur(2 * nh), tcur, tcur, tprev, tprev]


def _attn_fwd(z, ct, st, nh):
    s = z.shape[0]
    nc = s // ATT_CHUNK
    scale = HEAD ** -0.5

    def body(q_ref, kp_ref, kc_ref, vp_ref, vc_ref, ctc, stc, ctp, stp, o_ref, lse_ref, qf, kf, vf, ob, lb, bias):
        c = pl.program_id(1)
        qf[...] = _rope(q_ref[...].astype(F32), ctc[...], stc[...])
        kf[0:ATT_CHUNK] = _rope(kp_ref[...].astype(F32), ctp[...], stp[...])
        kf[ATT_CHUNK:] = _rope(kc_ref[...].astype(F32), ctc[...], stc[...])
        vf[0:ATT_CHUNK] = vp_ref[...].astype(F32)
        vf[ATT_CHUNK:] = vc_ref[...].astype(F32)
        _band_bias(bias)
        for b, d in enumerate(DILATIONS):
            def batch(ub, b=b, d=d):
                units = _batch_units(ub, d, c)
                q_b = jnp.stack([qf[qr, :] for qr, _, _ in units]).astype(BF16)
                k_b = jnp.stack([kf[kr, :] for _, kr, _ in units]).astype(BF16)
                v_b = jnp.stack([vf[kr, :] for _, kr, _ in units]).astype(BF16)
                bias_b = jnp.stack([bias[first] for _, _, first in units])
                sc = _bdot(q_b, k_b, 2, 2) * scale + bias_b
                m = jnp.max(sc, axis=2, keepdims=True)
                p = jnp.exp(sc - m)
                l = jnp.sum(p, axis=2, keepdims=True)
                o = _bdot(p.astype(BF16), v_b, 2, 1) / l
                lse = m + jnp.log(l)
                for j, (qr, _, _) in enumerate(units):
                    ob[b, qr, :] = o[j]
                    lb[b, qr, :] = jnp.broadcast_to(lse[j], (CHUNK, HEAD))
            _for_batches(d, batch)
        mx = jnp.maximum(jnp.maximum(lb[0], lb[1]), lb[2])
        e0, e1, e2 = jnp.exp(lb[0] - mx), jnp.exp(lb[1] - mx), jnp.exp(lb[2] - mx)
        den = e0 + e1 + e2
        o_ref[...] = ((e0 * ob[0] + e1 * ob[1] + e2 * ob[2]) / den).astype(BF16)
        lse_ref[...] = mx + jnp.log(den)

    blk = pl.BlockSpec((ATT_CHUNK, HEAD), lambda h, c: (c, h))
    return pl.pallas_call(
        body, name="attn_fwd", grid=(nh, nc),
        in_specs=_attn_specs(nh, nc), out_specs=[blk, blk],
        out_shape=[jax.ShapeDtypeStruct((s, nh * HEAD), BF16), jax.ShapeDtypeStruct((s, nh * HEAD), F32)],
        scratch_shapes=[pltpu.VMEM((ATT_CHUNK, HEAD), F32), pltpu.VMEM((2 * ATT_CHUNK, HEAD), F32),
                        pltpu.VMEM((2 * ATT_CHUNK, HEAD), F32), pltpu.VMEM((3, ATT_CHUNK, HEAD), F32),
                        pltpu.VMEM((3, ATT_CHUNK, HEAD), F32), pltpu.VMEM((2, CHUNK, 2 * CHUNK), F32)],
        compiler_params=_cparams("parallel", "arbitrary"),
    )(z, z, z, z, z, ct, st, ct, st)


def _attn_bwd(z, ct, st, da, o, lse, nh):
    s = z.shape[0]
    nc = s // ATT_CHUNK
    scale = HEAD ** -0.5

    def body(q_ref, kp_ref, kc_ref, vp_ref, vc_ref, ctc, stc, ctp, stp, do_ref, o_ref, lse_ref,
             dq_ref, dk_ref, dv_ref, qf, kf, vf, dof, dbar, dqa, dkf, dvf, bias):
        c = pl.program_id(1)

        @pl.when(c == 0)
        def _():
            dkf[...] = jnp.zeros_like(dkf)
            dvf[...] = jnp.zeros_like(dvf)

        @pl.when(c > 0)
        def _():
            dkf[0:ATT_CHUNK] = dkf[ATT_CHUNK:]
            dvf[0:ATT_CHUNK] = dvf[ATT_CHUNK:]
            dkf[ATT_CHUNK:] = jnp.zeros((ATT_CHUNK, HEAD), F32)
            dvf[ATT_CHUNK:] = jnp.zeros((ATT_CHUNK, HEAD), F32)

        @pl.when(c < nc)
        def _():
            qf[...] = _rope(q_ref[...].astype(F32), ctc[...], stc[...])
            kf[0:ATT_CHUNK] = _rope(kp_ref[...].astype(F32), ctp[...], stp[...])
            kf[ATT_CHUNK:] = _rope(kc_ref[...].astype(F32), ctc[...], stc[...])
            vf[0:ATT_CHUNK] = vp_ref[...].astype(F32)
            vf[ATT_CHUNK:] = vc_ref[...].astype(F32)
            dov = do_ref[...].astype(F32)
            dof[...] = dov
            dbar[...] = jnp.broadcast_to(jnp.sum(dov * o_ref[...].astype(F32), axis=1, keepdims=True), (ATT_CHUNK, HEAD))
            dqa[...] = jnp.zeros_like(dqa)
            _band_bias(bias)
            for d in DILATIONS:
                def batch(ub, d=d):
                    units = _batch_units(ub, d, c)
                    q_b = jnp.stack([qf[qr, :] for qr, _, _ in units]).astype(BF16)
                    k_b = jnp.stack([kf[kr, :] for _, kr, _ in units]).astype(BF16)
                    v_b = jnp.stack([vf[kr, :] for _, kr, _ in units]).astype(BF16)
                    do_b = jnp.stack([dof[qr, :] for qr, _, _ in units]).astype(BF16)
                    lse_b = jnp.stack([lse_ref[qr, :] for qr, _, _ in units])
                    dbar_b = jnp.stack([dbar[qr, :] for qr, _, _ in units])
                    bias_b = jnp.stack([bias[first] for _, _, first in units])
                    sc = _bdot(q_b, k_b, 2, 2) * scale + bias_b
                    p = jnp.exp(sc - jnp.concatenate([lse_b, lse_b], axis=2))
                    dp = _bdot(do_b, v_b, 2, 2)
                    ds = (p * (dp - jnp.concatenate([dbar_b, dbar_b], axis=2)) * scale).astype(BF16)
                    dq = _bdot(ds, k_b, 2, 1)
                    dk = _bdot(ds, q_b, 1, 1)
                    dv = _bdot(p.astype(BF16), do_b, 1, 1)
                    for j, (qr, kr, _) in enumerate(units):
                        dqa[qr, :] += dq[j]
                        dkf[kr, :] += dk[j]
                        dvf[kr, :] += dv[j]
                _for_batches(d, batch)
            dq_ref[...] = _rope_t(dqa[...], ctc[...], stc[...]).astype(BF16)

        @pl.when(c > 0)
        def _():
            dk_ref[...] = _rope_t(dkf[0:ATT_CHUNK], ctp[...], stp[...]).astype(BF16)
            dv_ref[...] = dvf[0:ATT_CHUNK].astype(BF16)

    cur = pl.BlockSpec((ATT_CHUNK, HEAD), lambda h, c: (jnp.minimum(c, nc - 1), h))
    late = pl.BlockSpec((ATT_CHUNK, HEAD), lambda h, c: (jnp.maximum(c - 1, 0), h))
    shp = jax.ShapeDtypeStruct((s, nh * HEAD), BF16)
    big = pltpu.VMEM((2 * ATT_CHUNK, HEAD), F32)
    one = pltpu.VMEM((ATT_CHUNK, HEAD), F32)
    return pl.pallas_call(
        body, name="attn_bwd", grid=(nh, nc + 1),
        in_specs=_attn_specs(nh, nc) + [cur, cur, cur], out_specs=[cur, late, late],
        out_shape=[shp, shp, shp],
        scratch_shapes=[one, big, big, one, one, one, big, big, pltpu.VMEM((2, CHUNK, 2 * CHUNK), F32)],
        compiler_params=_cparams("parallel", "arbitrary"),
    )(z, z, z, z, z, ct, st, ct, st, da, o, lse)


_GELU_K = math.sqrt(2.0 / math.pi)


def _gelu(x):
    return 0.5 * x * (1.0 + jnp.tanh(_GELU_K * (x + 0.044715 * x * x * x)))


def _gelu_and_grad(x):
    t = jnp.tanh(_GELU_K * (x + 0.044715 * x * x * x))
    g = 0.5 * x * (1.0 + t)
    dg = 0.5 * (1.0 + t) + 0.5 * x * (1.0 - t * t) * (_GELU_K * (1.0 + 3 * 0.044715 * x * x))
    return g, dg


def _tril(w):
    ti = lax.broadcasted_iota(jnp.int32, (CHUNK, CHUNK), 0)
    si = lax.broadcasted_iota(jnp.int32, (CHUNK, CHUNK), 1)
    return jnp.where(si <= ti, w, 0.0)


def _sgu_fwd(z, w_s, b_col, ng, u_blk):
    s = z.shape[0]
    gw = ng * HEAD
    tq = _tile(s, 1024, CHUNK)

    def body(u_ref, v_ref, w_ref, b_ref, o_ref):
        for g in range(ng):
            wg = _tril(w_ref[g]).astype(BF16)
            cols = slice(g * HEAD, (g + 1) * HEAD)
            for n in range(tq // CHUNK):
                rows = slice(n * CHUNK, (n + 1) * CHUNK)
                gv = _gelu(v_ref[rows, cols].astype(F32)).astype(BF16)
                mixed = _dot_nn(wg, gv) + b_ref[g]
                o_ref[rows, cols] = (_gelu(u_ref[rows, cols].astype(F32)) * mixed).astype(BF16)

    full = pl.BlockSpec((ng, CHUNK, CHUNK), lambda i: (0, 0, 0))
    return pl.pallas_call(
        body, name="sgu_fwd", grid=(s // tq,),
        in_specs=[pl.BlockSpec((tq, gw), lambda i: (i, u_blk)), pl.BlockSpec((tq, gw), lambda i: (i, u_blk + 1)), full, full],
        out_specs=pl.BlockSpec((tq, gw), lambda i: (i, 0)),
        out_shape=jax.ShapeDtypeStruct((s, gw), BF16),
        compiler_params=_cparams("parallel"),
    )(z, z, w_s, b_col)


def _sgu_bwd(z, w_s, b_col, da, ng, u_blk, da_blk):
    s = z.shape[0]
    gw = ng * HEAD
    tq = _tile(s, 1024, CHUNK)
    nsteps = s // tq

    def body(u_ref, v_ref, w_ref, b_ref, do_ref, du_ref, dv_ref, dw_ref, db_ref):
        i = pl.program_id(0)

        @pl.when(i == 0)
        def _():
            dw_ref[...] = jnp.zeros_like(dw_ref)
            db_ref[...] = jnp.zeros_like(db_ref)

        for g in range(ng):
            wg = _tril(w_ref[g]).astype(BF16)
            cols = slice(g * HEAD, (g + 1) * HEAD)
            dw_acc = jnp.zeros((CHUNK, CHUNK), F32)
            db_acc = jnp.zeros((CHUNK, 1), F32)
            for n in range(tq // CHUNK):
                rows = slice(n * CHUNK, (n + 1) * CHUNK)
                gu, dgu = _gelu_and_grad(u_ref[rows, cols].astype(F32))
                gv, dgv = _gelu_and_grad(v_ref[rows, cols].astype(F32))
                gvb = gv.astype(BF16)
                mixed = _dot_nn(wg, gvb) + b_ref[g]
                dout = do_ref[rows, cols].astype(F32)
                du_ref[rows, cols] = (dout * mixed * dgu).astype(BF16)
                dmix = dout * gu
                dmb = dmix.astype(BF16)
                dv_ref[rows, cols] = (_dot_tn(wg, dmb) * dgv).astype(BF16)
                dw_acc += _dot_nt(dmb, gvb)
                db_acc += jnp.sum(dmix, axis=1, keepdims=True)
            dw_ref[g] += _tril(dw_acc)
            db_ref[g] += jnp.broadcast_to(db_acc, (CHUNK, CHUNK))

    full = pl.BlockSpec((ng, CHUNK, CHUNK), lambda i: (0, 0, 0))
    out = pl.BlockSpec((tq, gw), lambda i: (i, 0))
    return pl.pallas_call(
        body, name="sgu_bwd", grid=(nsteps,),
        in_specs=[pl.BlockSpec((tq, gw), lambda i: (i, u_blk)), pl.BlockSpec((tq, gw), lambda i: (i, u_blk + 1)), full, full,
                  pl.BlockSpec((tq, gw), lambda i: (i, da_blk))],
        out_specs=[out, out, full, full],
        out_shape=[jax.ShapeDtypeStruct((s, gw), BF16)] * 2 + [jax.ShapeDtypeStruct((ng, CHUNK, CHUNK), F32)] * 2,
        compiler_params=_cparams("arbitrary"),
    )(z, z, w_s, b_col, da)


def _shift_down(y, halo, k):
    rolled = pltpu.roll(y, k, 0)
    row = lax.broadcasted_iota(jnp.int32, y.shape, 0)
    for j in range(k):
        rolled = jnp.where(row == j, halo[8 - k + j:8 - k + j + 1, :], rolled)
    return rolled


def _shift_up(y, halo, k):
    n = y.shape[0]
    rolled = pltpu.roll(y, n - k, 0)
    row = lax.broadcasted_iota(jnp.int32, y.shape, 0)
    for j in range(k):
        rolled = jnp.where(row == n - k + j, halo[j:j + 1, :], rolled)
    return rolled


def _conv_fwd(z, cw):
    s, d3 = z.shape
    d = d3 // 3
    tq = _tile(s, 256, 8)

    def body(z_ref, zh_ref, cw_ref, a_ref):
        i = pl.program_id(0)
        zv = z_ref[...].astype(F32)
        zh = jnp.where(i > 0, zh_ref[...].astype(F32), 0.0)
        y = zv[:, d:2 * d] * zv[:, 2 * d:]
        yh = zh[:, d:2 * d] * zh[:, 2 * d:]
        cwv = cw_ref[...]
        conv = cwv[0:1] * _shift_down(y, yh, 2) + cwv[1:2] * _shift_down(y, yh, 1) + cwv[2:3] * y
        a_ref[...] = (zv[:, :d] * conv).astype(BF16)

    return pl.pallas_call(
        body, name="conv_fwd", grid=(s // tq,),
        in_specs=[pl.BlockSpec((tq, d3), lambda i: (i, 0)),
                  pl.BlockSpec((8, d3), lambda i: (jnp.maximum(i * (tq // 8) - 1, 0), 0)),
                  pl.BlockSpec((3, d), lambda i: (0, 0))],
        out_specs=pl.BlockSpec((tq, d), lambda i: (i, 0)),
        out_shape=jax.ShapeDtypeStruct((s, d), BF16),
        compiler_params=_cparams("parallel"),
    )(z, z, cw)


def _conv_bwd(z, cw, da):
    s, d3 = z.shape
    d = d3 // 3
    tq = _tile(s, 128, 8)
    nsteps = s // tq
    nblk8 = s // 8

    def body(z_ref, zp_ref, zn_ref, da_ref, dan_ref, cw_ref, dz_ref, dcw_ref, acc):
        i = pl.program_id(0)

        @pl.when(i == 0)
        def _():
            acc[...] = jnp.zeros_like(acc)

        zv = z_ref[...].astype(F32)
        zp = jnp.where(i > 0, zp_ref[...].astype(F32), 0.0)
        zn = jnp.where(i < nsteps - 1, zn_ref[...].astype(F32), 0.0)
        dav = da_ref[...].astype(F32)
        dan = jnp.where(i < nsteps - 1, dan_ref[...].astype(F32), 0.0)
        gb, gc, hx = zv[:, :d], zv[:, d:2 * d], zv[:, 2 * d:]
        y = gc * hx
        yp = zp[:, d:2 * d] * zp[:, 2 * d:]
        cwv = cw_ref[...]
        y1, y2 = _shift_down(y, yp, 1), _shift_down(y, yp, 2)
        conv = cwv[0:1] * y2 + cwv[1:2] * y1 + cwv[2:3] * y
        dconv = dav * gb
        dconv_n = dan * zn[:, :d]
        dy = cwv[2:3] * dconv + cwv[1:2] * _shift_up(dconv, dconv_n, 1) + cwv[0:1] * _shift_up(dconv, dconv_n, 2)
        dz_ref[:, :d] = (dav * conv).astype(BF16)
        dz_ref[:, d:2 * d] = (dy * hx).astype(BF16)
        dz_ref[:, 2 * d:] = (dy * gc).astype(BF16)
        acc[0] += _colsum8(dconv * y2)
        acc[1] += _colsum8(dconv * y1)
        acc[2] += _colsum8(dconv * y)

        @pl.when(i == nsteps - 1)
        def _():
            for j in range(3):
                dcw_ref[j:j + 1, :] = jnp.sum(acc[j], axis=0, keepdims=True)

    return pl.pallas_call(
        body, name="conv_bwd", grid=(nsteps,),
        in_specs=[pl.BlockSpec((tq, d3), lambda i: (i, 0)),
                  pl.BlockSpec((8, d3), lambda i: (jnp.maximum(i * (tq // 8) - 1, 0), 0)),
                  pl.BlockSpec((8, d3), lambda i: (jnp.minimum((i + 1) * (tq // 8), nblk8 - 1), 0)),
                  pl.BlockSpec((tq, d), lambda i: (i, 0)),
                  pl.BlockSpec((8, d), lambda i: (jnp.minimum((i + 1) * (tq // 8), nblk8 - 1), 0)),
                  pl.BlockSpec((3, d), lambda i: (0, 0))],
        out_specs=[pl.BlockSpec((tq, d3), lambda i: (i, 0)), pl.BlockSpec((3, d), lambda i: (0, 0))],
        out_shape=[jax.ShapeDtypeStruct((s, d3), BF16), jax.ShapeDtypeStruct((3, d), F32)],
        scratch_shapes=[pltpu.VMEM((3, 8, d), F32)],
        compiler_params=_cparams("arbitrary"),
    )(z, z, z, da, da, cw)


def _swiglu_epilogue(gate, up):
    return gate, up, gate / (1.0 + jnp.exp(-gate)) * up


def _swiglu_bwd_epilogue(dact, gt, up):
    g, u = gt.astype(F32), up.astype(F32)
    sg = 1.0 / (1.0 + jnp.exp(-g))
    return dact * u * (sg * (1.0 + g * (1.0 - sg))), dact * (g * sg)


def _ada_fwd(c_all, ada_w):
    nl, d, n8 = ada_w.shape
    tn = _tile(n8, 768, 128)

    def body(c_ref, w_ref, o_ref):
        cv = c_ref[...]
        act = (cv / (1.0 + jnp.exp(-cv))).astype(BF16)
        o_ref[0] = _dot_nn(act, w_ref[0].astype(BF16))

    return pl.pallas_call(
        body, name="ada_fwd", grid=(nl, n8 // tn),
        in_specs=[pl.BlockSpec((N_DEV, d), lambda l, j: (0, 0)), pl.BlockSpec((1, d, tn), lambda l, j: (l, 0, j))],
        out_specs=pl.BlockSpec((1, N_DEV, tn), lambda l, j: (l, 0, j)),
        out_shape=jax.ShapeDtypeStruct((nl, N_DEV, n8), F32),
        compiler_params=_cparams("parallel", "parallel"),
    )(c_all, ada_w)


def _ada_wgrad(c_all, dmod_cols):
    nl, _, n8 = dmod_cols.shape
    d = c_all.shape[1]
    tn = _tile(n8, 768, 128)

    def body(c_ref, g_ref, o_ref):
        cv = c_ref[...]
        act = (cv / (1.0 + jnp.exp(-cv))).astype(BF16)
        o_ref[0] = _dot_tn(act, g_ref[0].astype(BF16))

    return pl.pallas_call(
        body, name="ada_wgrad", grid=(nl, n8 // tn),
        in_specs=[pl.BlockSpec((N_DEV, d), lambda l, j: (0, 0)), pl.BlockSpec((1, N_DEV, tn), lambda l, j: (l, 0, j))],
        out_specs=pl.BlockSpec((1, d, tn), lambda l, j: (l, 0, j)),
        out_shape=jax.ShapeDtypeStruct((nl, d, n8), F32),
        compiler_params=_cparams("parallel", "parallel"),
    )(c_all, dmod_cols)


def _adamw(name, pieces, w, m, v):
    npc, r, c = pieces.shape
    tr = _tile(r, max(8, (1 << 19) // c // 8 * 8), 8)
    bc1, bc2 = 1.0 - ADAM_B1 ** ADAM_STEP, 1.0 - ADAM_B2 ** ADAM_STEP

    def body(p_ref, w_ref, m_ref, v_ref, g_ref, d_ref, nm_ref, nv_ref):
        g = p_ref[0].astype(F32)
        for i in range(1, npc):
            g = g + p_ref[i].astype(F32)
        nm = ADAM_B1 * m_ref[...] + (1.0 - ADAM_B1) * g
        nv = ADAM_B2 * v_ref[...] + (1.0 - ADAM_B2) * (g * g)
        g_ref[...] = g
        nm_ref[...] = nm
        nv_ref[...] = nv
        d_ref[...] = -ADAM_LR * ((nm / bc1) / (jnp.sqrt(nv / bc2) + ADAM_EPS) + ADAM_WD * w_ref[...])

    blk = pl.BlockSpec((tr, c), lambda i: (i, 0))
    return pl.pallas_call(
        body, name=name, grid=(r // tr,),
        in_specs=[pl.BlockSpec((npc, tr, c), lambda i: (0, i, 0)), blk, blk, blk],
        out_specs=[blk] * 4, out_shape=[jax.ShapeDtypeStruct((r, c), F32)] * 4,
        compiler_params=_cparams("parallel"),
    )(pieces, w, m, v)


def _place():
    x, y, c = lax.axis_index("x"), lax.axis_index("y"), lax.axis_index("c")
    return x, y, c


def _all_gather_small(name, x_shard):
    m_per, n = x_shard.shape

    def body(x_ref, out_ref, token_ref, send_sems, recv_sems, local_sem):
        token_ref[...] = jnp.zeros_like(token_ref)
        x, y, c = _place()
        me, sibling = (x, y, c), (x, y, 1 - c)
        chips = [(1 - x, y), (x, 1 - y), (1 - x, 1 - y)]

        def rows(px, py, pc):
            return out_ref.at[pl.ds((4 * px + 2 * py + pc) * m_per, m_per), :]

        def copy(k, block, to, src=None):
            return pltpu.make_async_remote_copy(
                src_ref=rows(*block) if src is None else src, dst_ref=rows(*block),
                send_sem=send_sems.at[k], recv_sem=recv_sems.at[k], device_id=to, device_id_type=MESH)

        mine = pltpu.make_async_copy(x_ref, rows(*me), local_sem)
        mine.start()
        first = [copy(0, me, sibling, src=x_ref)]
        first += [copy(1 + j, me, (*chip, c), src=x_ref) for j, chip in enumerate(chips)]
        for cp in first:
            cp.start()
        passed = [copy(4 + j, (*chip, c), sibling) for j, chip in enumerate(chips)]
        for j, chip in enumerate(chips):
            copy(1 + j, (*chip, c), me).wait_recv()
            passed[j].start()
        copy(0, sibling, me).wait_recv()
        for j, chip in enumerate(chips):
            copy(4 + j, (*chip, 1 - c), me).wait_recv()
        for cp in first + passed:
            cp.wait_send()
        mine.wait()

    vmem = pl.BlockSpec(memory_space=pltpu.VMEM)
    return pl.pallas_call(
        body, name=name,
        out_shape=[jax.ShapeDtypeStruct((N_DEV * m_per, n), x_shard.dtype), jax.ShapeDtypeStruct((8, HEAD), F32)],
        in_specs=[vmem], out_specs=[vmem, vmem],
        scratch_shapes=[pltpu.SemaphoreType.DMA((7,)), pltpu.SemaphoreType.DMA((7,)), pltpu.SemaphoreType.DMA],
        compiler_params=pltpu.CompilerParams(vmem_limit_bytes=VMEM_LIMIT),
    )(x_shard)


_HBM = pl.BlockSpec(memory_space=pltpu.HBM)
_SEM = pl.BlockSpec(memory_space=pltpu.SEMAPHORE)
_EFFECT = pltpu.SideEffectType.DATAFLOW_SIDE_EFFECTING


def _peers():
    x, y, c = _place()
    peers = []
    for k in range(1, N_DEV):
        px = 1 - x if k & 4 else x
        py = 1 - y if k & 2 else y
        pc = 1 - c if k & 1 else c
        peers.append(((px, py, pc), 4 * px + 2 * py + pc))
    return 4 * x + 2 * y + c, peers


def _push_start(name, srcs, lands, src_view, dst_view):
    na = len(srcs)
    n = na * (N_DEV - 1)

    def body(*refs):
        s_refs, l_refs = refs[:na], refs[na:2 * na]
        send_sems, recv_sems, token = refs[2 * na], refs[2 * na + 1], refs[-1]
        me, peers = _peers()
        for a in range(na):
            for k, (dev, idx) in enumerate(peers):
                pltpu.make_async_remote_copy(
                    src_ref=src_view(s_refs[a], idx), dst_ref=dst_view(l_refs[a], me),
                    send_sem=send_sems.at[a * (N_DEV - 1) + k], recv_sem=recv_sems.at[a * (N_DEV - 1) + k],
                    device_id=dev, device_id_type=MESH).start()
        token[...] = jnp.zeros_like(token)

    outs = pl.pallas_call(
        body, name=name,
        out_shape=(pltpu.SemaphoreType.DMA((n,)), pltpu.SemaphoreType.DMA((n,)),
                   *[pltpu.HBM(t.shape, t.dtype) for t in list(srcs) + list(lands)], jax.ShapeDtypeStruct((8, HEAD), F32)),
        in_specs=[_HBM] * (2 * na),
        out_specs=(_SEM, _SEM, *[_HBM] * (2 * na), pl.BlockSpec(memory_space=pltpu.VMEM)),
        input_output_aliases={i: 2 + i for i in range(2 * na)},
        compiler_params=pltpu.CompilerParams(has_side_effects=_EFFECT),
    )(*[pltpu.with_memory_space_constraint(t, pltpu.HBM) for t in list(srcs) + list(lands)])
    return outs[0], outs[1], list(outs[2:2 + na]), list(outs[2 + na:2 + 2 * na]), outs[-1]


def _push_wait(name, send_sems, recv_sems, srcs, lands, after, src_view, dst_view):
    na = len(srcs)

    def body(*refs):
        s_refs, l_refs = refs[:na], refs[na:2 * na]
        send_sems, recv_sems = refs[2 * na], refs[2 * na + 1]
        me, peers = _peers()
        for a in range(na):
            for k, (dev, idx) in enumerate(peers):
                cp = pltpu.make_async_remote_copy(
                    src_ref=src_view(s_refs[a], idx), dst_ref=dst_view(l_refs[a], idx),
                    send_sem=send_sems.at[a * (N_DEV - 1) + k], recv_sem=recv_sems.at[a * (N_DEV - 1) + k],
                    device_id=dev, device_id_type=MESH)
                cp.wait_send()
                cp.wait_recv()

    outs = pl.pallas_call(
        body, name=name,
        out_shape=[pltpu.HBM(t.shape, t.dtype) for t in list(srcs) + list(lands)],
        in_specs=[_HBM] * (2 * na) + [_SEM, _SEM, pl.BlockSpec(memory_space=pl.ANY)],
        out_specs=[_HBM] * (2 * na),
        input_output_aliases={i: i for i in range(2 * na)},
        compiler_params=pltpu.CompilerParams(has_side_effects=_EFFECT),
    )(*srcs, *lands, send_sems, recv_sems, after)
    return list(outs[na:])


def _gather_start(name, shards):
    lands = [lax.empty((N_DEV,) + t.shape, t.dtype) for t in shards]
    return _push_start(name, shards, lands, lambda ref, idx: ref, lambda ref, slot: ref.at[slot])


def _gather_wait(name, started, shards, after, me):
    send_sems, recv_sems, srcs, lands, _ = started
    lands = _push_wait(name, send_sems, recv_sems, srcs, lands, after, lambda ref, idx: ref, lambda ref, slot: ref.at[slot])
    return [lax.dynamic_update_index_in_dim(g, t, me, 0) for g, t in zip(lands, shards)]


def _cols_to_plain(g):
    n, k, n8 = g.shape
    return jnp.transpose(g, (1, 0, 2)).reshape(k, n * n8)


def _plain_to_cols(w):
    k, n = w.shape
    return jnp.transpose(w.reshape(k, N_DEV, n // N_DEV), (1, 0, 2))


def _pack_rows(parts, width):
    rows, offs, r = [], [], 0
    for p in parts:
        flat = p.reshape(-1).astype(F32)
        nr = -(-flat.shape[0] // (8 * width)) * 8
        rows.append(jnp.pad(flat, (0, nr * width - flat.shape[0])).reshape(nr, width))
        offs.append((r, flat.shape[0], p.shape))
        r += nr
    return jnp.concatenate(rows, axis=0), offs, r


def _unpack_rows(slab, offs, width):
    lead = slab.shape[:-2]
    out = []
    for r0, n, shape in offs:
        nr = -(-n // width)
        out.append(slab[..., r0:r0 + nr, :].reshape(lead + (nr * width,))[..., :n].reshape(lead + tuple(shape)))
    return out


def kernel(x, c, positions, ada_w, ada_b, norm_mix, norm_ffn, ab_w_in, sgu_w, sgu_b, ab_w_out, conv_w_in, conv_w, conv_w_out, ffn_w_gate, ffn_w_up, ffn_w_down, final_norm, loss_target, m_ada_w, m_ada_b, m_norm_mix, m_norm_ffn, m_ab_w_in, m_sgu_w, m_sgu_b, m_ab_w_out, m_conv_w_in, m_conv_w, m_conv_w_out, m_ffn_w_gate, m_ffn_w_up, m_ffn_w_down, m_final_norm, v_ada_w, v_ada_b, v_norm_mix, v_norm_ffn, v_ab_w_in, v_sgu_w, v_sgu_b, v_ab_w_out, v_conv_w_in, v_conv_w, v_conv_w_out, v_ffn_w_gate, v_ffn_w_up, v_ffn_w_down, v_final_norm):
    xi, yi, ci = _place()
    me = 4 * xi + 2 * yi + ci
    s, d = x.shape[1], x.shape[2]
    depth = ada_w.shape[0]
    n_even = ab_w_in.shape[0]
    nh_mix = d // HEAD
    nh = 3 * nh_mix // 4
    ng = nh_mix - nh
    aw, gw = nh * HEAD, ng * HEAD
    x0 = x[0]
    target = loss_target[0]
    n_odd, cwid, d8 = conv_w.shape

    width = 512
    slab, offs, _ = _pack_rows([c, conv_w], width)
    gathered, _ = _all_gather_small("gather_cond", slab)
    c_parts, cw_parts = _unpack_rows(gathered.reshape(N_DEV, -1, width), offs, width)
    c_all = c_parts.reshape(N_DEV, d)
    conv_w_full = jnp.transpose(cw_parts, (1, 2, 0, 3)).reshape(n_odd, cwid, d)

    mod_cols = _ada_fwd(c_all, ada_w)
    n8 = mod_cols.shape[2]
    mod_all, token = _all_gather_small("gather_mod", mod_cols.reshape(depth * N_DEV, n8))
    mod_mine = lax.dynamic_index_in_dim(mod_all.reshape(N_DEV, depth, N_DEV, n8), me, axis=2, keepdims=False)

    def mixer_weights(l):
        return (ab_w_in[l // 2], ab_w_out[l // 2]) if l % 2 == 0 else (conv_w_in[l // 2], conv_w_out[l // 2])

    groups = [[mixer_weights(0)[0]], [mixer_weights(0)[1]], [ffn_w_gate[0], ffn_w_up[0], ffn_w_down[0]]]
    for l in range(1, depth):
        groups += [list(mixer_weights(l)), [ffn_w_gate[l], ffn_w_up[l], ffn_w_down[l]]]
    gathers, tok = [], token[0, 0]
    for n, ws in enumerate(groups):
        shards = [(w + tok).astype(BF16) for w in ws]
        started = _gather_start(f"gather_start_{n}", shards)
        gathers.append((started, shards))
        tok = started[4][0, 0]

    def weights_of_group(n, after):
        started, shards = gathers[n]
        return _gather_wait(f"gather_wait_{n}", started, shards, after, me)

    def plain_rows(g):
        return g.reshape(g.shape[0] * g.shape[1], g.shape[2])

    mod = jnp.transpose(mod_mine, (1, 0, 2)).reshape(depth, N_DEV * n8) + ada_b + tok
    mods = mod.reshape(depth, 6, 1, d)

    ct, st = _rope_tables(positions.reshape(s, 1))
    b_col = jnp.broadcast_to(sgu_b[..., None], sgu_b.shape + (CHUNK,))
    u_blk = 3 * aw // gw

    stream = [x0]
    saved = []
    w_in, w_out, w_gate, w_up, w_down = [[None] * depth for _ in range(5)]
    xcur = x0
    for l in range(depth):
        sh_m, sc_m, g_m, sh_f, sc_f, g_f = [mods[l, j] for j in range(6)]
        i = l // 2
        if l == 0:
            (g_in,) = weights_of_group(0, mod)
        else:
            g_in, g_out = weights_of_group(1 + 2 * l, xcur)
        w_in[l] = _cols_to_plain(g_in)
        h = _norm_mod_fwd(xcur, norm_mix[l][None], sc_m, sh_m)
        if l % 2 == 0:
            z = _matmul("mix_in", "nn", h, w_in[l], [BF16])
            attn, lse = _attn_fwd(z, ct, st, nh)
            sgu = _sgu_fwd(z, sgu_w[i], b_col[i], ng, u_blk)
            a = jnp.concatenate([attn, sgu], axis=1)
            mixer_saved = (z, a, lse)
        else:
            z = _matmul("conv_in", "nn", h, w_in[l], [BF16])
            a = _conv_fwd(z, conv_w_full[i])
            mixer_saved = (z, a, None)
        if l == 0:
            (g_out,) = weights_of_group(1, a)
        w_out[l] = plain_rows(g_out)
        x1, mix = _matmul("mix_out", "nn", a, w_out[l], [F32, BF16], extras=[(xcur, "mn"), (g_m, "n")],
                          epilogue=lambda acc, r, gv: (r + gv * acc, acc))
        g_gate, g_up, g_down = weights_of_group(2 + 2 * l, x1)
        w_gate[l], w_up[l], w_down[l] = _cols_to_plain(g_gate), _cols_to_plain(g_up), plain_rows(g_down)
        h2 = _norm_mod_fwd(x1, norm_ffn[l][None], sc_f, sh_f)
        gt, up, act = _matmul("ffn_in", "nn", h2, [w_gate[l], w_up[l]], [BF16, BF16, BF16], epilogue=_swiglu_epilogue)
        x2, f = _matmul("ffn_down", "nn", act, w_down[l], [F32, BF16], extras=[(x1, "mn"), (g_f, "n")],
                        epilogue=lambda acc, r, gv: (r + gv * acc, acc))
        saved.append((h, mixer_saved, mix, x1, h2, gt, up, act, f))
        stream.append(x2)
        xcur = x2

    f_last = saved[-1][8]
    loss_part, dx, dbr, d_final, dg = _final_loss(xcur, target, final_norm[None], f_last, mods[depth - 1, 5])
    loss = lax.psum(loss_part[0, 0], ("x", "y", "c"))

    dmod = [[None] * 6 for _ in range(depth)]
    d_norm_mix, d_norm_ffn = [None] * depth, [None] * depth
    d_sgu_w, d_sgu_b, d_conv_w = [None] * n_even, [None] * n_even, [None] * n_odd

    big = {"in": ab_w_in, "out": ab_w_out, "cin": conv_w_in, "cout": conv_w_out,
           "gate": ffn_w_gate, "up": ffn_w_up, "down": ffn_w_down}
    lands = {k: lax.empty((N_DEV,) + w.shape, BF16) for k, w in big.items()}
    own = {k: [None] * w.shape[0] for k, w in big.items()}
    pending = {"ffn": None, "mix": None}

    def exchange_finish(tag, after):
        (send_sems, recv_sems, srcs, lds, _), keys, li, layer = pending[tag]
        lds = _push_wait(f"exchange_wait_{tag}_{layer}", send_sems, recv_sems, srcs, lds, after,
                         lambda ref, idx: ref.at[idx], lambda ref, slot: ref.at[slot, li])
        for k, ld in zip(keys, lds):
            lands[k] = ld
        pending[tag] = None

    def exchange_start(tag, layer, keys, li, grads):
        if pending[tag] is not None:
            exchange_finish(tag, grads[0])
        for k, g in zip(keys, grads):
            own[k][li] = lax.dynamic_index_in_dim(g, me, 0, keepdims=False)
        started = _push_start(f"exchange_start_{tag}_{layer}", grads, [lands[k] for k in keys],
                              lambda ref, idx: ref.at[idx], lambda ref, slot: ref.at[slot, li])
        pending[tag] = (started, keys, li, layer)
        return started[4][0, 0]

    def row_shards(g):
        return g.reshape(N_DEV, g.shape[0] // N_DEV, g.shape[1])

    for l in reversed(range(depth)):
        sh_m, sc_m, g_m, sh_f, sc_f, g_f = [mods[l, j] for j in range(6)]
        h, (z, a, lse), mix, x1, h2, gt, up, act, f = saved[l]
        i = l // 2
        dmod[l][5] = dg
        gw_down = _matmul("ffn_down_wgrad", "tn", act, dbr, [BF16], tn=1024)
        dgt, dup = _matmul("ffn_down_dgrad", "nt", dbr, w_down[l], [BF16, BF16], extras=[(gt, "mn"), (up, "mn")],
                           epilogue=_swiglu_bwd_epilogue)
        gw_gate = _matmul("ffn_in_wgrad", "tn", h2, dgt, [BF16])
        gw_up = _matmul("ffn_in_wgrad", "tn", h2, dup, [BF16])
        tok = exchange_start("ffn", l, ("gate", "up", "down"), l, [_plain_to_cols(gw_gate), _plain_to_cols(gw_up), row_shards(gw_down)])
        dh2 = _matmul("ffn_in_dgrad", "nt", [dgt, dup], [w_gate[l], w_up[l]], [F32], after=tok)
        dx, dbr, dmod[l][3], dmod[l][4], d_norm_ffn[l], dg = _norm_mod_bwd(x1, dh2, norm_ffn[l][None], sc_f, dx, mix, g_m)
        dmod[l][2] = dg
        gw_out = _matmul("mix_out_wgrad", "tn", a, dbr, [BF16])
        da = _matmul("mix_out_dgrad", "nt", dbr, w_out[l], [BF16])
        if l % 2 == 0:
            dq, dk, dv = _attn_bwd(z, ct, st, da, a, lse, nh)
            du, dvg, d_sgu_w[i], dbb = _sgu_bwd(z, sgu_w[i], b_col[i], da, ng, u_blk, aw // gw)
            d_sgu_b[i] = dbb[:, :, 0]
            dz = jnp.concatenate([dq, dk, dv, du, dvg], axis=1)
            gw_in = _matmul("mix_in_wgrad", "tn", h, dz, [BF16])
            dh = _matmul("mix_in_dgrad", "nt", dz, w_in[l], [F32])
        else:
            dz, d_conv_w[i] = _conv_bwd(z, conv_w_full[i], da)
            gw_in = _matmul("conv_in_wgrad", "tn", h, dz, [BF16])
            dh = _matmul("conv_in_dgrad", "nt", dz, w_in[l], [F32])
        mix_group = ("mix", l, ("in", "out") if l % 2 == 0 else ("cin", "cout"), i, [_plain_to_cols(gw_in), row_shards(gw_out)])
        w_norm = norm_mix[l][None]
        if l > 0:
            w_norm = w_norm + exchange_start(*mix_group)
            f_prev, g_prev = saved[l - 1][8], mods[l - 1, 5]
            dx, dbr, dmod[l][0], dmod[l][1], d_norm_mix[l], dg = _norm_mod_bwd(stream[l], dh, w_norm, sc_m, dx, f_prev, g_prev)
        else:
            dx, dmod[l][0], dmod[l][1], d_norm_mix[l] = _norm_mod_bwd(stream[l], dh, w_norm, sc_m, dx)
    grad_x = dx[None]

    dmod_mine = jnp.stack([jnp.concatenate([v.reshape(d) for v in dmod[l]]) for l in range(depth)])
    small = [dmod_mine, jnp.concatenate(d_norm_mix), jnp.concatenate(d_norm_ffn), jnp.stack(d_sgu_w), jnp.stack(d_sgu_b),
             d_final, jnp.stack(d_conv_w)]
    slab, offs, _ = _pack_rows(small, width)
    gathered, token = _all_gather_small("gather_small_grads", slab)
    p_dmod, p_nmix, p_nffn, p_sguw, p_sgub, p_final, p_convw = _unpack_rows(gathered.reshape(N_DEV, -1, width), offs, width)
    mix_group[4][0] = mix_group[4][0] + token[0, 0].astype(BF16)
    p_dmod = p_dmod + exchange_start(*mix_group)

    outs = {}

    def update(name, pieces, w, m, v):
        shape = w.shape
        cdim = shape[-1]
        res = _adamw("adamw_" + name, pieces.reshape(pieces.shape[0], -1, cdim), w.reshape(-1, cdim),
                     m.reshape(-1, cdim), v.reshape(-1, cdim))
        outs[name] = [r.reshape(shape) for r in res]

    update("ada_b", p_dmod.reshape(N_DEV, depth, 6 * d), ada_b, m_ada_b, v_ada_b)
    update("norm_mix", p_nmix.reshape(N_DEV, depth, d), norm_mix, m_norm_mix, v_norm_mix)
    update("norm_ffn", p_nffn.reshape(N_DEV, depth, d), norm_ffn, m_norm_ffn, v_norm_ffn)
    update("sgu_w", p_sguw, sgu_w, m_sgu_w, v_sgu_w)
    update("sgu_b", p_sgub.reshape(N_DEV, 1, -1), sgu_b.reshape(1, -1), m_sgu_b.reshape(1, -1), v_sgu_b.reshape(1, -1))
    outs["sgu_b"] = [r.reshape(sgu_b.shape) for r in outs["sgu_b"]]
    update("final_norm", p_final.reshape(N_DEV, 1, d), final_norm[None], m_final_norm[None], v_final_norm[None])
    outs["final_norm"] = [r.reshape(final_norm.shape) for r in outs["final_norm"]]
    cw_mine = lax.dynamic_slice_in_dim(p_convw.reshape(N_DEV, n_odd, cwid, d), me * d8, d8, axis=3)
    update("conv_w", cw_mine, conv_w, m_conv_w, v_conv_w)

    dmod_cols = lax.dynamic_slice_in_dim(p_dmod.reshape(N_DEV, depth, 6 * d), me * n8, n8, axis=2)
    g_ada = _ada_wgrad(c_all, jnp.transpose(dmod_cols, (1, 0, 2)))
    update("ada_w", g_ada[None], ada_w, m_ada_w, v_ada_w)

    names = {"in": "ab_w_in", "out": "ab_w_out", "cin": "conv_w_in", "cout": "conv_w_out",
             "gate": "ffn_w_gate", "up": "ffn_w_up", "down": "ffn_w_down"}
    moments = {"in": (m_ab_w_in, v_ab_w_in), "out": (m_ab_w_out, v_ab_w_out), "cin": (m_conv_w_in, v_conv_w_in),
               "cout": (m_conv_w_out, v_conv_w_out), "gate": (m_ffn_w_gate, v_ffn_w_gate), "up": (m_ffn_w_up, v_ffn_w_up),
               "down": (m_ffn_w_down, v_ffn_w_down)}

    def update_big(k):
        pieces = lax.dynamic_update_slice(lands[k], jnp.stack(own[k])[None], (me, 0, 0, 0))
        update(names[k], pieces, big[k], *moments[k])

    exchange_finish("ffn", g_ada)
    for k in ("gate", "up", "down", "cin", "cout"):
        update_big(k)
    done = sum(outs[n][0][(0,) * outs[n][0].ndim] for n in outs)
    exchange_finish("mix", done.reshape(1, 1))
    for k in ("in", "out"):
        update_big(k)

    order = ["ada_w", "ada_b", "norm_mix", "norm_ffn", "ab_w_in", "sgu_w", "sgu_b", "ab_w_out", "conv_w_in", "conv_w",
             "conv_w_out", "ffn_w_gate", "ffn_w_up", "ffn_w_down", "final_norm"]
    return (loss, grad_x, *[outs[n][0] for n in order], *[outs[n][1] for n in order],
            *[outs[n][2] for n in order], *[outs[n][3] for n in order])
```

```python
import functools
import math

import numpy as np
import jax
import jax.numpy as jnp
from jax import lax
from jax.experimental import pallas as pl
from jax.experimental.pallas import tpu as pltpu

F32, BF16 = jnp.float32, jnp.bfloat16
MESH = pl.DeviceIdType.MESH
N_DEV = 8
EPS = 1e-6
HEAD = 128
CHUNK = 128
DILATIONS = (1, 4, 16)
ATT_CHUNK = CHUNK * DILATIONS[-1]
ATT_BATCH = 8
ROPE_THETA = 500000.0
ROPE_DIM = HEAD // 4
NEG = -1e30
ADAM_LR, ADAM_B1, ADAM_B2, ADAM_EPS, ADAM_WD, ADAM_STEP = 0.001, 0.9, 0.999, 1e-08, 0.01, 10
VMEM_LIMIT = 56 * 1024 * 1024


def _cparams(*sem):
    return pltpu.CompilerParams(dimension_semantics=sem or None, vmem_limit_bytes=VMEM_LIMIT)


def _tile(n, pref, unit):
    t = (min(pref, n) // unit) * unit
    while t >= unit:
        if n % t == 0:
            return t
        t -= unit
    return n


def _dot(a, b, dims):
    return lax.dot_general(a, b, (dims, ((), ())), preferred_element_type=F32)


def _dot_nn(a, b):
    return _dot(a, b, ((1,), (0,)))


def _dot_nt(a, b):
    return _dot(a, b, ((1,), (1,)))


def _dot_tn(a, b):
    return _dot(a, b, ((0,), (0,)))


_TILE_DEFAULT = dict(tm=1024, tn=512, tk=2816)
_TILES = {
    "ffn_down": dict(tn=1024),
    "ffn_down_wgrad": dict(tm=1408, tn=1024),
    "ffn_in_wgrad": dict(tn=1408), "mix_in_wgrad": dict(tn=1408), "conv_in_wgrad": dict(tn=1536),
    "mix_in_dgrad": dict(tn=1024, tk=1408), "conv_in_dgrad": dict(tn=1024, tk=1536),
}


def _matmul(name, mode, a, b, outs, *, extras=(), epilogue=None, after=None):
    a_list = list(a) if isinstance(a, (list, tuple)) else [a]
    b_list = list(b) if isinstance(b, (list, tuple)) else [b]
    na, nb = len(a_list), len(b_list)
    paired = na > 1
    assert na == nb if paired else na == 1
    nacc = 1 if paired else nb
    a0, b0 = a_list[0], b_list[0]
    if mode == "nn":
        (m, kk), (_, n) = a0.shape, b0.shape
    elif mode == "nt":
        (m, kk), (n, _) = a0.shape, b0.shape
    else:
        (kk, m), (_, n) = a0.shape, b0.shape
    if after is not None:
        inner = epilogue or (lambda *accs: accs)
        extras = list(extras) + [(jnp.zeros((1, n), F32) + after, "n")]
        epilogue = lambda *tiles: inner(*tiles[:-1])
    pref = {**_TILE_DEFAULT, **_TILES.get(name, {})}
    tm, tn, tk = _tile(m, pref["tm"], 128), _tile(n, pref["tn"], 128), _tile(kk, pref["tk"], 128)
    nk = kk // tk
    dotf = {"nn": _dot_nn, "nt": _dot_nt, "tn": _dot_tn}[mode]
    a_spec = pl.BlockSpec((tk, tm), lambda i, j, k: (k, i)) if mode == "tn" else pl.BlockSpec((tm, tk), lambda i, j, k: (i, k))
    b_spec = pl.BlockSpec((tn, tk), lambda i, j, k: (j, k)) if mode == "nt" else pl.BlockSpec((tk, tn), lambda i, j, k: (k, j))
    e_specs = [pl.BlockSpec((tm, tn), lambda i, j, k: (i, j)) if kind == "mn" else pl.BlockSpec((1, tn), lambda i, j, k: (0, j))
               for _, kind in extras]
    ne, no = len(extras), len(outs)
    epi = epilogue or (lambda *accs: accs)

    def body(*refs):
        a_refs, b_refs, rest = refs[:na], refs[na:na + nb], refs[na + nb:]
        e_refs, o_refs, acc_refs = rest[:ne], rest[ne:ne + no], rest[ne + no:]

        def products():
            if paired:
                p = dotf(a_refs[0][...], b_refs[0][...])
                for a_ref, b_ref in zip(a_refs[1:], b_refs[1:]):
                    p = p + dotf(a_ref[...], b_ref[...])
                return [p]
            av = a_refs[0][...]
            return [dotf(av, b_ref[...]) for b_ref in b_refs]

        def finish(accs):
            for o_ref, o in zip(o_refs, epi(*accs, *[r[...] for r in e_refs])):
                o_ref[...] = o.astype(o_ref.dtype)

        if nk == 1:
            finish(products())
            return
        k = pl.program_id(2)

        @pl.when(k == 0)
        def _():
            for acc_ref in acc_refs:
                acc_ref[...] = jnp.zeros_like(acc_ref)

        for acc_ref, p in zip(acc_refs, products()):
            acc_ref[...] += p

        @pl.when(k == nk - 1)
        def _():
            finish([acc_ref[...] for acc_ref in acc_refs])

    res = pl.pallas_call(
        body, name=name, grid=(m // tm, n // tn, nk),
        in_specs=[a_spec] * na + [b_spec] * nb + e_specs,
        out_specs=[pl.BlockSpec((tm, tn), lambda i, j, k: (i, j)) for _ in outs],
        out_shape=[jax.ShapeDtypeStruct((m, n), dt) for dt in outs],
        scratch_shapes=[pltpu.VMEM((tm, tn), F32)] * nacc if nk > 1 else [],
        compiler_params=_cparams("parallel", "parallel", "arbitrary"),
    )(*a_list, *b_list, *[e for e, _ in extras])
    return res[0] if no == 1 else res


def _norm_mod_fwd(x, w, sc, sh):
    s, d = x.shape
    tm = _tile(s, 512, 8)

    def body(x_ref, w_ref, sc_ref, sh_ref, h_ref):
        xv = x_ref[...]
        r = lax.rsqrt(jnp.mean(xv * xv, axis=-1, keepdims=True) + EPS)
        h_ref[...] = ((xv * r) * w_ref[...] * (1.0 + sc_ref[...]) + sh_ref[...]).astype(BF16)

    row = pl.BlockSpec((1, d), lambda i: (0, 0))
    return pl.pallas_call(
        body, name="norm_mod_fwd", grid=(s // tm,),
        in_specs=[pl.BlockSpec((tm, d), lambda i: (i, 0)), row, row, row],
        out_specs=pl.BlockSpec((tm, d), lambda i: (i, 0)),
        out_shape=jax.ShapeDtypeStruct((s, d), BF16),
        compiler_params=_cparams("parallel"),
    )(x, w, sc, sh)


def _colsum8(t):
    tm, d = t.shape
    return jnp.sum(t.reshape(tm // 8, 8, d), axis=0)


def _norm_mod_bwd(x, dh, w, sc, dres, branch=None, g=None):
    s, d = x.shape
    tm = _tile(s, 256, 8)
    nsteps = s // tm
    gated = branch is not None

    def body(*refs):
        if gated:
            x_ref, dh_ref, w_ref, sc_ref, dres_ref, br_ref, g_ref, dx_ref, dbr_ref, dsh_ref, dsc_ref, dw_ref, dg_ref, acc = refs
        else:
            x_ref, dh_ref, w_ref, sc_ref, dres_ref, dx_ref, dsh_ref, dsc_ref, dw_ref, acc = refs
        i = pl.program_id(0)

        @pl.when(i == 0)
        def _():
            acc[...] = jnp.zeros_like(acc)

        xv, dhv, wv, scv = x_ref[...], dh_ref[...].astype(F32), w_ref[...], sc_ref[...]
        r = lax.rsqrt(jnp.mean(xv * xv, axis=-1, keepdims=True) + EPS)
        xn = xv * r
        dxn = dhv * (wv * (1.0 + scv))
        dx = dres_ref[...] + r * (dxn - xn * jnp.mean(dxn * xn, axis=-1, keepdims=True))
        dx_ref[...] = dx
        acc[0] += _colsum8(dhv)
        acc[1] += _colsum8(dhv * xn)
        if gated:
            dbr_ref[...] = (dx * g_ref[...]).astype(BF16)
            acc[2] += _colsum8(dx * br_ref[...].astype(F32))

        @pl.when(i == nsteps - 1)
        def _():
            a0 = jnp.sum(acc[0], axis=0, keepdims=True)
            a1 = jnp.sum(acc[1], axis=0, keepdims=True)
            dsh_ref[...] = a0
            dsc_ref[...] = a1 * wv
            dw_ref[...] = a1 * (1.0 + scv)
            if gated:
                dg_ref[...] = jnp.sum(acc[2], axis=0, keepdims=True)

    big = pl.BlockSpec((tm, d), lambda i: (i, 0))
    row = pl.BlockSpec((1, d), lambda i: (0, 0))
    rowo = jax.ShapeDtypeStruct((1, d), F32)
    in_specs = [big, big, row, row, big] + ([big, row] if gated else [])
    out_specs = [big] + ([big] if gated else []) + [row, row, row] + ([row] if gated else [])
    out_shape = ([jax.ShapeDtypeStruct((s, d), F32)] + ([jax.ShapeDtypeStruct((s, d), BF16)] if gated else [])
                 + [rowo, rowo, rowo] + ([rowo] if gated else []))
    args = [x, dh, w, sc, dres] + ([branch, g] if gated else [])
    return pl.pallas_call(
        body, name="norm_mod_bwd_gated" if gated else "norm_mod_bwd", grid=(nsteps,),
        in_specs=in_specs, out_specs=out_specs, out_shape=out_shape,
        scratch_shapes=[pltpu.VMEM((3, 8, d), F32)],
        compiler_params=_cparams("arbitrary"),
    )(*args)


def _final_loss(x, target, w, branch, g):
    s, d = x.shape
    tm = _tile(s, 256, 8)
    nsteps = s // tm

    def body(x_ref, t_ref, w_ref, br_ref, g_ref, loss_ref, dx_ref, dbr_ref, dw_ref, dg_ref, acc):
        i = pl.program_id(0)

        @pl.when(i == 0)
        def _():
            acc[...] = jnp.zeros_like(acc)

        xv, wv = x_ref[...], w_ref[...]
        r = lax.rsqrt(jnp.mean(xv * xv, axis=-1, keepdims=True) + EPS)
        xn = xv * r
        err = xn * wv - t_ref[...]
        dy = err * (1.0 / d)
        dxn = dy * wv
        dx = r * (dxn - xn * jnp.mean(dxn * xn, axis=-1, keepdims=True))
        dx_ref[...] = dx
        dbr_ref[...] = (dx * g_ref[...]).astype(BF16)
        acc[0] += _colsum8(err * err)
        acc[1] += _colsum8(dy * xn)
        acc[2] += _colsum8(dx * br_ref[...].astype(F32))

        @pl.when(i == nsteps - 1)
        def _():
            loss_ref[...] = jnp.sum(jnp.sum(acc[0], axis=0, keepdims=True), axis=1, keepdims=True) * (0.5 / d)
            dw_ref[...] = jnp.sum(acc[1], axis=0, keepdims=True)
            dg_ref[...] = jnp.sum(acc[2], axis=0, keepdims=True)

    big = pl.BlockSpec((tm, d), lambda i: (i, 0))
    row = pl.BlockSpec((1, d), lambda i: (0, 0))
    rowo = jax.ShapeDtypeStruct((1, d), F32)
    return pl.pallas_call(
        body, name="final_loss", grid=(nsteps,),
        in_specs=[big, big, row, big, row],
        out_specs=[pl.BlockSpec((1, 1), lambda i: (0, 0)), big, big, row, row],
        out_shape=[jax.ShapeDtypeStruct((1, 1), F32), jax.ShapeDtypeStruct((s, d), F32),
                   jax.ShapeDtypeStruct((s, d), BF16), rowo, rowo],
        scratch_shapes=[pltpu.VMEM((3, 8, d), F32)],
        compiler_params=_cparams("arbitrary"),
    )(x, target, w, branch, g)


def _rope_tables(pos_col):
    s = pos_col.shape[0]
    tq = _tile(s, 1024, 8)
    half = ROPE_DIM // 2
    inv = np.float32(ROPE_THETA) ** (-np.arange(0, ROPE_DIM, 2, dtype=np.float32) / np.float32(ROPE_DIM))
    inv_row = jnp.asarray(np.tile(inv.astype(np.float32), HEAD // half)[None, :])

    def body(p_ref, inv_ref, ct_ref, st_ref):
        lane = lax.broadcasted_iota(jnp.int32, (tq, HEAD), 1)
        ang = p_ref[...].astype(F32) * inv_ref[...]
        cs, sn = jnp.cos(ang), jnp.sin(ang)
        ct_ref[...] = jnp.where(lane < ROPE_DIM, cs, 1.0)
        st_ref[...] = jnp.where(lane < half, -sn, jnp.where(lane < ROPE_DIM, sn, 0.0))

    blk = pl.BlockSpec((tq, HEAD), lambda i: (i, 0))
    return pl.pallas_call(
        body, name="rope_tables", grid=(s // tq,),
        in_specs=[pl.BlockSpec((tq, 1), lambda i: (i, 0)), pl.BlockSpec((1, HEAD), lambda i: (0, 0))],
        out_specs=[blk, blk],
        out_shape=[jax.ShapeDtypeStruct((s, HEAD), F32)] * 2,
        compiler_params=_cparams("parallel"),
    )(pos_col, inv_row)


def _swap_halves(x):
    lane = lax.broadcasted_iota(jnp.int32, x.shape, 1)
    half = ROPE_DIM // 2
    return jnp.where(lane < half, pltpu.roll(x, HEAD - half, 1), pltpu.roll(x, half, 1))


def _rope(x, ct, st):
    return x * ct + _swap_halves(x) * st


def _rope_t(dy, ct, st):
    return dy * ct - _swap_halves(dy) * st


def _band_bias(bias_ref):
    qi = lax.broadcasted_iota(jnp.int32, (CHUNK, 2 * CHUNK), 0)
    kj = lax.broadcasted_iota(jnp.int32, (CHUNK, 2 * CHUNK), 1)
    band = (kj >= qi) & (kj <= qi + CHUNK)
    bias_ref[0] = jnp.where(band, 0.0, NEG)
    bias_ref[1] = jnp.where(band & (kj >= CHUNK), 0.0, NEG)


def _rows(start, size, d):
    return pl.ds(start, size) if d == 1 else pl.ds(start, size, stride=d)


def _batch_units(ub, d, c):
    if d >= ATT_BATCH:
        per_sb = d // ATT_BATCH
        units = [(ub // per_sb, (ub % per_sb) * ATT_BATCH + j) for j in range(ATT_BATCH)]
    else:
        n_sb = ATT_BATCH // d
        units = [(ub * n_sb + t, r) for t in range(n_sb) for r in range(d)]
    out = []
    for sb, r in units:
        base = sb * (CHUNK * d)
        if not isinstance(base, int):
            base = pl.multiple_of(base, CHUNK)
        first = jnp.where((c == 0) & (sb == 0), 1, 0)
        out.append((_rows(base + r, CHUNK, d), _rows(base + (ATT_CHUNK - CHUNK * d + r), 2 * CHUNK, d), first))
    return out


def _for_batches(d, fn):
    n = ATT_CHUNK // CHUNK // ATT_BATCH
    if d >= ATT_BATCH:
        for ub in range(n):
            fn(ub)
    else:
        def step(ub, carry):
            fn(ub)
            return carry
        lax.fori_loop(0, n, step, 0)


def _bdot(a, b, ca, cb):
    return lax.dot_general(a, b, (((ca,), (cb,)), ((0,), (0,))), preferred_element_type=F32)


def _attn_specs(nh, nc):
    cur = lambda off: pl.BlockSpec((ATT_CHUNK, HEAD), lambda h, c: (jnp.minimum(c, nc - 1), off + h))
    prev = lambda off: pl.BlockSpec((ATT_CHUNK, HEAD), lambda h, c: (jnp.maximum(c - 1, 0), off + h))
    tcur = pl.BlockSpec((ATT_CHUNK, HEAD), lambda h, c: (jnp.minimum(c, nc - 1), 0))
    tprev = pl.BlockSpec((ATT_CHUNK, HEAD), lambda h, c: (jnp.maximum(c - 1, 0), 0))
    return [cur(0), prev(nh), cur(nh), prev(2 * nh), cur(2 * nh), tcur, tcur, tprev, tprev]


def _attn_fwd(z, ct, st, nh):
    s = z.shape[0]
    nc = s // ATT_CHUNK
    scale = HEAD ** -0.5

    def body(q_ref, kp_ref, kc_ref, vp_ref, vc_ref, ctc, stc, ctp, stp, o_ref, lse_ref, qf, kf, vf, ob, lb, bias):
        c = pl.program_id(1)
        qf[...] = _rope(q_ref[...].astype(F32), ctc[...], stc[...])
        kf[0:ATT_CHUNK] = _rope(kp_ref[...].astype(F32), ctp[...], stp[...])
        kf[ATT_CHUNK:] = _rope(kc_ref[...].astype(F32), ctc[...], stc[...])
        vf[0:ATT_CHUNK] = vp_ref[...].astype(F32)
        vf[ATT_CHUNK:] = vc_ref[...].astype(F32)
        _band_bias(bias)
        for b, d in enumerate(DILATIONS):
            def batch(ub, b=b, d=d):
                units = _batch_units(ub, d, c)
                q_b = jnp.stack([qf[qr, :] for qr, _, _ in units]).astype(BF16)
                k_b = jnp.stack([kf[kr, :] for _, kr, _ in units]).astype(BF16)
                v_b = jnp.stack([vf[kr, :] for _, kr, _ in units]).astype(BF16)
                bias_b = jnp.stack([bias[first] for _, _, first in units])
                sc = _bdot(q_b, k_b, 2, 2) * scale + bias_b
                m = jnp.max(sc, axis=2, keepdims=True)
                p = jnp.exp(sc - m)
                l = jnp.sum(p, axis=2, keepdims=True)
                o = _bdot(p.astype(BF16), v_b, 2, 1) / l
                lse = m + jnp.log(l)
                for j, (qr, _, _) in enumerate(units):
                    ob[b, qr, :] = o[j]
                    lb[b, qr, :] = jnp.broadcast_to(lse[j], (CHUNK, HEAD))
            _for_batches(d, batch)
        mx = jnp.maximum(jnp.maximum(lb[0], lb[1]), lb[2])
        e0, e1, e2 = jnp.exp(lb[0] - mx), jnp.exp(lb[1] - mx), jnp.exp(lb[2] - mx)
        den = e0 + e1 + e2
        o_ref[...] = ((e0 * ob[0] + e1 * ob[1] + e2 * ob[2]) / den).astype(BF16)
        lse_ref[...] = mx + jnp.log(den)

    blk = pl.BlockSpec((ATT_CHUNK, HEAD), lambda h, c: (c, h))
    return pl.pallas_call(
        body, name="attn_fwd", grid=(nh, nc),
        in_specs=_attn_specs(nh, nc), out_specs=[blk, blk],
        out_shape=[jax.ShapeDtypeStruct((s, nh * HEAD), BF16), jax.ShapeDtypeStruct((s, nh * HEAD), F32)],
        scratch_shapes=[pltpu.VMEM((ATT_CHUNK, HEAD), F32), pltpu.VMEM((2 * ATT_CHUNK, HEAD), F32),
                        pltpu.VMEM((2 * ATT_CHUNK, HEAD), F32), pltpu.VMEM((3, ATT_CHUNK, HEAD), F32),
                        pltpu.VMEM((3, ATT_CHUNK, HEAD), F32), pltpu.VMEM((2, CHUNK, 2 * CHUNK), F32)],
        compiler_params=_cparams("parallel", "arbitrary"),
    )(z, z, z, z, z, ct, st, ct, st)


def _attn_bwd(z, ct, st, da, o, lse, nh):
    s = z.shape[0]
    nc = s // ATT_CHUNK
    scale = HEAD ** -0.5

    def body(q_ref, kp_ref, kc_ref, vp_ref, vc_ref, ctc, stc, ctp, stp, do_ref, o_ref, lse_ref,
             dq_ref, dk_ref, dv_ref, qf, kf, vf, dof, dbar, dqa, dkf, dvf, bias):
        c = pl.program_id(1)

        @pl.when(c == 0)
        def _():
            dkf[...] = jnp.zeros_like(dkf)
            dvf[...] = jnp.zeros_like(dvf)

        @pl.when(c > 0)
        def _():
            dkf[0:ATT_CHUNK] = dkf[ATT_CHUNK:]
            dvf[0:ATT_CHUNK] = dvf[ATT_CHUNK:]
            dkf[ATT_CHUNK:] = jnp.zeros((ATT_CHUNK, HEAD), F32)
            dvf[ATT_CHUNK:] = jnp.zeros((ATT_CHUNK, HEAD), F32)

        @pl.when(c < nc)
        def _():
            qf[...] = _rope(q_ref[...].astype(F32), ctc[...], stc[...])
            kf[0:ATT_CHUNK] = _rope(kp_ref[...].astype(F32), ctp[...], stp[...])
            kf[ATT_CHUNK:] = _rope(kc_ref[...].astype(F32), ctc[...], stc[...])
            vf[0:ATT_CHUNK] = vp_ref[...].astype(F32)
            vf[ATT_CHUNK:] = vc_ref[...].astype(F32)
            dov = do_ref[...].astype(F32)
            dof[...] = dov
            dbar[...] = jnp.broadcast_to(jnp.sum(dov * o_ref[...].astype(F32), axis=1, keepdims=True), (ATT_CHUNK, HEAD))
            dqa[...] = jnp.zeros_like(dqa)
            _band_bias(bias)
            for d in DILATIONS:
                def batch(ub, d=d):
                    units = _batch_units(ub, d, c)
                    q_b = jnp.stack([qf[qr, :] for qr, _, _ in units]).astype(BF16)
                    k_b = jnp.stack([kf[kr, :] for _, kr, _ in units]).astype(BF16)
                    v_b = jnp.stack([vf[kr, :] for _, kr, _ in units]).astype(BF16)
                    do_b = jnp.stack([dof[qr, :] for qr, _, _ in units]).astype(BF16)
                    lse_b = jnp.stack([lse_ref[qr, :] for qr, _, _ in units])
                    dbar_b = jnp.stack([dbar[qr, :] for qr, _, _ in units])
                    bias_b = jnp.stack([bias[first] for _, _, first in units])
                    sc = _bdot(q_b, k_b, 2, 2) * scale + bias_b
                    p = jnp.exp(sc - jnp.concatenate([lse_b, lse_b], axis=2))
                    dp = _bdot(do_b, v_b, 2, 2)
                    ds = (p * (dp - jnp.concatenate([dbar_b, dbar_b], axis=2)) * scale).astype(BF16)
                    dq = _bdot(ds, k_b, 2, 1)
                    dk = _bdot(ds, q_b, 1, 1)
                    dv = _bdot(p.astype(BF16), do_b, 1, 1)
                    for j, (qr, kr, _) in enumerate(units):
                        dqa[qr, :] += dq[j]
                        dkf[kr, :] += dk[j]
                        dvf[kr, :] += dv[j]
                _for_batches(d, batch)
            dq_ref[...] = _rope_t(dqa[...], ctc[...], stc[...]).astype(BF16)

        @pl.when(c > 0)
        def _():
            dk_ref[...] = _rope_t(dkf[0:ATT_CHUNK], ctp[...], stp[...]).astype(BF16)
            dv_ref[...] = dvf[0:ATT_CHUNK].astype(BF16)

    cur = pl.BlockSpec((ATT_CHUNK, HEAD), lambda h, c: (jnp.minimum(c, nc - 1), h))
    late = pl.BlockSpec((ATT_CHUNK, HEAD), lambda h, c: (jnp.maximum(c - 1, 0), h))
    shp = jax.ShapeDtypeStruct((s, nh * HEAD), BF16)
    big = pltpu.VMEM((2 * ATT_CHUNK, HEAD), F32)
    one = pltpu.VMEM((ATT_CHUNK, HEAD), F32)
    return pl.pallas_call(
        body, name="attn_bwd", grid=(nh, nc + 1),
        in_specs=_attn_specs(nh, nc) + [cur, cur, cur], out_specs=[cur, late, late],
        out_shape=[shp, shp, shp],
        scratch_shapes=[one, big, big, one, one, one, big, big, pltpu.VMEM((2, CHUNK, 2 * CHUNK), F32)],
        compiler_params=_cparams("parallel", "arbitrary"),
    )(z, z, z, z, z, ct, st, ct, st, da, o, lse)


_GELU_K = math.sqrt(2.0 / math.pi)


def _gelu(x):
    return 0.5 * x * (1.0 + jnp.tanh(_GELU_K * (x + 0.044715 * x * x * x)))


def _gelu_and_grad(x):
    t = jnp.tanh(_GELU_K * (x + 0.044715 * x * x * x))
    g = 0.5 * x * (1.0 + t)
    dg = 0.5 * (1.0 + t) + 0.5 * x * (1.0 - t * t) * (_GELU_K * (1.0 + 3 * 0.044715 * x * x))
    return g, dg


def _tril(w):
    ti = lax.broadcasted_iota(jnp.int32, (CHUNK, CHUNK), 0)
    si = lax.broadcasted_iota(jnp.int32, (CHUNK, CHUNK), 1)
    return jnp.where(si <= ti, w, 0.0)


def _sgu_fwd(z, w_s, b_col, ng, u_blk):
    s = z.shape[0]
    gw = ng * HEAD
    tq = _tile(s, 1024, CHUNK)

    def body(u_ref, v_ref, w_ref, b_ref, o_ref):
        for g in range(ng):
            wg = _tril(w_ref[g]).astype(BF16)
            cols = slice(g * HEAD, (g + 1) * HEAD)
            for n in range(tq // CHUNK):
                rows = slice(n * CHUNK, (n + 1) * CHUNK)
                gv = _gelu(v_ref[rows, cols].astype(F32)).astype(BF16)
                mixed = _dot_nn(wg, gv) + b_ref[g]
                o_ref[rows, cols] = (_gelu(u_ref[rows, cols].astype(F32)) * mixed).astype(BF16)

    full = pl.BlockSpec((ng, CHUNK, CHUNK), lambda i: (0, 0, 0))
    return pl.pallas_call(
        body, name="sgu_fwd", grid=(s // tq,),
        in_specs=[pl.BlockSpec((tq, gw), lambda i: (i, u_blk)), pl.BlockSpec((tq, gw), lambda i: (i, u_blk + 1)), full, full],
        out_specs=pl.BlockSpec((tq, gw), lambda i: (i, 0)),
        out_shape=jax.ShapeDtypeStruct((s, gw), BF16),
        compiler_params=_cparams("parallel"),
    )(z, z, w_s, b_col)


def _sgu_bwd(z, w_s, b_col, da, ng, u_blk, da_blk):
    s = z.shape[0]
    gw = ng * HEAD
    tq = _tile(s, 1024, CHUNK)
    nsteps = s // tq

    def body(u_ref, v_ref, w_ref, b_ref, do_ref, du_ref, dv_ref, dw_ref, db_ref):
        i = pl.program_id(0)

        @pl.when(i == 0)
        def _():
            dw_ref[...] = jnp.zeros_like(dw_ref)
            db_ref[...] = jnp.zeros_like(db_ref)

        for g in range(ng):
            wg = _tril(w_ref[g]).astype(BF16)
            cols = slice(g * HEAD, (g + 1) * HEAD)
            dw_acc = jnp.zeros((CHUNK, CHUNK), F32)
            db_acc = jnp.zeros((CHUNK, 1), F32)
            for n in range(tq // CHUNK):
                rows = slice(n * CHUNK, (n + 1) * CHUNK)
                gu, dgu = _gelu_and_grad(u_ref[rows, cols].astype(F32))
                gv, dgv = _gelu_and_grad(v_ref[rows, cols].astype(F32))
                gvb = gv.astype(BF16)
                mixed = _dot_nn(wg, gvb) + b_ref[g]
                dout = do_ref[rows, cols].astype(F32)
                du_ref[rows, cols] = (dout * mixed * dgu).astype(BF16)
                dmix = dout * gu
                dmb = dmix.astype(BF16)
                dv_ref[rows, cols] = (_dot_tn(wg, dmb) * dgv).astype(BF16)
                dw_acc += _dot_nt(dmb, gvb)
                db_acc += jnp.sum(dmix, axis=1, keepdims=True)
            dw_ref[g] += _tril(dw_acc)
            db_ref[g] += jnp.broadcast_to(db_acc, (CHUNK, CHUNK))

    full = pl.BlockSpec((ng, CHUNK, CHUNK), lambda i: (0, 0, 0))
    out = pl.BlockSpec((tq, gw), lambda i: (i, 0))
    return pl.pallas_call(
        body, name="sgu_bwd", grid=(nsteps,),
        in_specs=[pl.BlockSpec((tq, gw), lambda i: (i, u_blk)), pl.BlockSpec((tq, gw), lambda i: (i, u_blk + 1)), full, full,
                  pl.BlockSpec((tq, gw), lambda i: (i, da_blk))],
        out_specs=[out, out, full, full],
        out_shape=[jax.ShapeDtypeStruct((s, gw), BF16)] * 2 + [jax.ShapeDtypeStruct((ng, CHUNK, CHUNK), F32)] * 2,
        compiler_params=_cparams("arbitrary"),
    )(z, z, w_s, b_col, da)


def _shift_down(y, halo, k):
    rolled = pltpu.roll(y, k, 0)
    row = lax.broadcasted_iota(jnp.int32, y.shape, 0)
    for j in range(k):
        rolled = jnp.where(row == j, halo[8 - k + j:8 - k + j + 1, :], rolled)
    return rolled


def _shift_up(y, halo, k):
    n = y.shape[0]
    rolled = pltpu.roll(y, n - k, 0)
    row = lax.broadcasted_iota(jnp.int32, y.shape, 0)
    for j in range(k):
        rolled = jnp.where(row == n - k + j, halo[j:j + 1, :], rolled)
    return rolled


def _conv_fwd(z, cw):
    s, d3 = z.shape
    d = d3 // 3
    tq = _tile(s, 256, 8)

    def body(z_ref, zh_ref, cw_ref, a_ref):
        i = pl.program_id(0)
        zv = z_ref[...].astype(F32)
        zh = jnp.where(i > 0, zh_ref[...].astype(F32), 0.0)
        y = zv[:, d:2 * d] * zv[:, 2 * d:]
        yh = zh[:, d:2 * d] * zh[:, 2 * d:]
        cwv = cw_ref[...]
        conv = cwv[0:1] * _shift_down(y, yh, 2) + cwv[1:2] * _shift_down(y, yh, 1) + cwv[2:3] * y
        a_ref[...] = (zv[:, :d] * conv).astype(BF16)

    return pl.pallas_call(
        body, name="conv_fwd", grid=(s // tq,),
        in_specs=[pl.BlockSpec((tq, d3), lambda i: (i, 0)),
                  pl.BlockSpec((8, d3), lambda i: (jnp.maximum(i * (tq // 8) - 1, 0), 0)),
                  pl.BlockSpec((3, d), lambda i: (0, 0))],
        out_specs=pl.BlockSpec((tq, d), lambda i: (i, 0)),
        out_shape=jax.ShapeDtypeStruct((s, d), BF16),
        compiler_params=_cparams("parallel"),
    )(z, z, cw)


def _conv_bwd(z, cw, da):
    s, d3 = z.shape
    d = d3 // 3
    tq = _tile(s, 128, 8)
    nsteps = s // tq
    nblk8 = s // 8

    def body(z_ref, zp_ref, zn_ref, da_ref, dan_ref, cw_ref, dz_ref, dcw_ref, acc):
        i = pl.program_id(0)

        @pl.when(i == 0)
        def _():
            acc[...] = jnp.zeros_like(acc)

        zv = z_ref[...].astype(F32)
        zp = jnp.where(i > 0, zp_ref[...].astype(F32), 0.0)
        zn = jnp.where(i < nsteps - 1, zn_ref[...].astype(F32), 0.0)
        dav = da_ref[...].astype(F32)
        dan = jnp.where(i < nsteps - 1, dan_ref[...].astype(F32), 0.0)
        gb, gc, hx = zv[:, :d], zv[:, d:2 * d], zv[:, 2 * d:]
        y = gc * hx
        yp = zp[:, d:2 * d] * zp[:, 2 * d:]
        cwv = cw_ref[...]
        y1, y2 = _shift_down(y, yp, 1), _shift_down(y, yp, 2)
        conv = cwv[0:1] * y2 + cwv[1:2] * y1 + cwv[2:3] * y
        dconv = dav * gb
        dconv_n = dan * zn[:, :d]
        dy = cwv[2:3] * dconv + cwv[1:2] * _shift_up(dconv, dconv_n, 1) + cwv[0:1] * _shift_up(dconv, dconv_n, 2)
        dz_ref[:, :d] = (dav * conv).astype(BF16)
        dz_ref[:, d:2 * d] = (dy * hx).astype(BF16)
        dz_ref[:, 2 * d:] = (dy * gc).astype(BF16)
        acc[0] += _colsum8(dconv * y2)
        acc[1] += _colsum8(dconv * y1)
        acc[2] += _colsum8(dconv * y)

        @pl.when(i == nsteps - 1)
        def _():
            for j in range(3):
                dcw_ref[j:j + 1, :] = jnp.sum(acc[j], axis=0, keepdims=True)

    return pl.pallas_call(
        body, name="conv_bwd", grid=(nsteps,),
        in_specs=[pl.BlockSpec((tq, d3), lambda i: (i, 0)),
                  pl.BlockSpec((8, d3), lambda i: (jnp.maximum(i * (tq // 8) - 1, 0), 0)),
                  pl.BlockSpec((8, d3), lambda i: (jnp.minimum((i + 1) * (tq // 8), nblk8 - 1), 0)),
                  pl.BlockSpec((tq, d), lambda i: (i, 0)),
                  pl.BlockSpec((8, d), lambda i: (jnp.minimum((i + 1) * (tq // 8), nblk8 - 1), 0)),
                  pl.BlockSpec((3, d), lambda i: (0, 0))],
        out_specs=[pl.BlockSpec((tq, d3), lambda i: (i, 0)), pl.BlockSpec((3, d), lambda i: (0, 0))],
        out_shape=[jax.ShapeDtypeStruct((s, d3), BF16), jax.ShapeDtypeStruct((3, d), F32)],
        scratch_shapes=[pltpu.VMEM((3, 8, d), F32)],
        compiler_params=_cparams("arbitrary"),
    )(z, z, z, da, da, cw)


def _swiglu_epilogue(gate, up):
    return gate, up, gate / (1.0 + jnp.exp(-gate)) * up


def _swiglu_bwd_epilogue(dact, gt, up):
    g, u = gt.astype(F32), up.astype(F32)
    sg = 1.0 / (1.0 + jnp.exp(-g))
    return dact * u * (sg * (1.0 + g * (1.0 - sg))), dact * (g * sg)


def _ada_fwd(c_all, ada_w):
    nl, d, n8 = ada_w.shape
    tn = _tile(n8, 768, 128)

    def body(c_ref, w_ref, o_ref):
        cv = c_ref[...]
        act = (cv / (1.0 + jnp.exp(-cv))).astype(BF16)
        o_ref[0] = _dot_nn(act, w_ref[0].astype(BF16))

    return pl.pallas_call(
        body, name="ada_fwd", grid=(nl, n8 // tn),
        in_specs=[pl.BlockSpec((N_DEV, d), lambda l, j: (0, 0)), pl.BlockSpec((1, d, tn), lambda l, j: (l, 0, j))],
        out_specs=pl.BlockSpec((1, N_DEV, tn), lambda l, j: (l, 0, j)),
        out_shape=jax.ShapeDtypeStruct((nl, N_DEV, n8), F32),
        compiler_params=_cparams("parallel", "parallel"),
    )(c_all, ada_w)


def _ada_wgrad(c_all, dmod_cols):
    nl, _, n8 = dmod_cols.shape
    d = c_all.shape[1]
    tn = _tile(n8, 768, 128)

    def body(c_ref, g_ref, o_ref):
        cv = c_ref[...]
        act = (cv / (1.0 + jnp.exp(-cv))).astype(BF16)
        o_ref[0] = _dot_tn(act, g_ref[0].astype(BF16))

    return pl.pallas_call(
        body, name="ada_wgrad", grid=(nl, n8 // tn),
        in_specs=[pl.BlockSpec((N_DEV, d), lambda l, j: (0, 0)), pl.BlockSpec((1, N_DEV, tn), lambda l, j: (l, 0, j))],
        out_specs=pl.BlockSpec((1, d, tn), lambda l, j: (l, 0, j)),
        out_shape=jax.ShapeDtypeStruct((nl, d, n8), F32),
        compiler_params=_cparams("parallel", "parallel"),
    )(c_all, dmod_cols)


def _adamw(name, pieces, w, m, v):
    npc, r, c = pieces.shape
    tr = _tile(r, max(8, (1 << 19) // c // 8 * 8), 8)
    bc1, bc2 = 1.0 - ADAM_B1 ** ADAM_STEP, 1.0 - ADAM_B2 ** ADAM_STEP

    def body(p_ref, w_ref, m_ref, v_ref, g_ref, d_ref, nm_ref, nv_ref):
        g = p_ref[0].astype(F32)
        for i in range(1, npc):
            g = g + p_ref[i].astype(F32)
        nm = ADAM_B1 * m_ref[...] + (1.0 - ADAM_B1) * g
        nv = ADAM_B2 * v_ref[...] + (1.0 - ADAM_B2) * (g * g)
        g_ref[...] = g
        nm_ref[...] = nm
        nv_ref[...] = nv
        d_ref[...] = -ADAM_LR * ((nm / bc1) / (jnp.sqrt(nv / bc2) + ADAM_EPS) + ADAM_WD * w_ref[...])

    blk = pl.BlockSpec((tr, c), lambda i: (i, 0))
    return pl.pallas_call(
        body, name=name, grid=(r // tr,),
        in_specs=[pl.BlockSpec((npc, tr, c), lambda i: (0, i, 0)), blk, blk, blk],
        out_specs=[blk] * 4, out_shape=[jax.ShapeDtypeStruct((r, c), F32)] * 4,
        compiler_params=_cparams("parallel"),
    )(pieces, w, m, v)


def _place():
    x, y, c = lax.axis_index("x"), lax.axis_index("y"), lax.axis_index("c")
    return x, y, c


def _all_gather_small(name, x_shard):
    m_per, n = x_shard.shape

    def body(x_ref, out_ref, token_ref, send_sems, recv_sems, local_sem):
        token_ref[...] = jnp.zeros_like(token_ref)
        x, y, c = _place()
        me, sibling = (x, y, c), (x, y, 1 - c)
        chips = [(1 - x, y), (x, 1 - y), (1 - x, 1 - y)]

        def rows(px, py, pc):
            return out_ref.at[pl.ds((4 * px + 2 * py + pc) * m_per, m_per), :]

        def copy(k, block, to, src=None):
            return pltpu.make_async_remote_copy(
                src_ref=rows(*block) if src is None else src, dst_ref=rows(*block),
                send_sem=send_sems.at[k], recv_sem=recv_sems.at[k], device_id=to, device_id_type=MESH)

        mine = pltpu.make_async_copy(x_ref, rows(*me), local_sem)
        mine.start()
        first = [copy(0, me, sibling, src=x_ref)]
        first += [copy(1 + j, me, (*chip, c), src=x_ref) for j, chip in enumerate(chips)]
        for cp in first:
            cp.start()
        passed = [copy(4 + j, (*chip, c), sibling) for j, chip in enumerate(chips)]
        for j, chip in enumerate(chips):
            copy(1 + j, (*chip, c), me).wait_recv()
            passed[j].start()
        copy(0, sibling, me).wait_recv()
        for j, chip in enumerate(chips):
            copy(4 + j, (*chip, 1 - c), me).wait_recv()
        for cp in first + passed:
            cp.wait_send()
        mine.wait()

    vmem = pl.BlockSpec(memory_space=pltpu.VMEM)
    return pl.pallas_call(
        body, name=name,
        out_shape=[jax.ShapeDtypeStruct((N_DEV * m_per, n), x_shard.dtype), jax.ShapeDtypeStruct((8, HEAD), F32)],
        in_specs=[vmem], out_specs=[vmem, vmem],
        scratch_shapes=[pltpu.SemaphoreType.DMA((7,)), pltpu.SemaphoreType.DMA((7,)), pltpu.SemaphoreType.DMA],
        compiler_params=pltpu.CompilerParams(vmem_limit_bytes=VMEM_LIMIT),
    )(x_shard)


_HBM = pl.BlockSpec(memory_space=pltpu.HBM)
_SEM = pl.BlockSpec(memory_space=pltpu.SEMAPHORE)
_EFFECT = pltpu.SideEffectType.DATAFLOW_SIDE_EFFECTING


def _peers():
    x, y, c = _place()
    peers = []
    for k in range(1, N_DEV):
        px = 1 - x if k & 4 else x
        py = 1 - y if k & 2 else y
        pc = 1 - c if k & 1 else c
        peers.append(((px, py, pc), 4 * px + 2 * py + pc))
    return 4 * x + 2 * y + c, peers


def _push_start(name, srcs, lands, src_view, dst_view):
    na = len(srcs)
    n = na * (N_DEV - 1)

    def body(*refs):
        s_refs, l_refs = refs[:na], refs[na:2 * na]
        send_sems, recv_sems, token = refs[2 * na], refs[2 * na + 1], refs[-1]
        me, peers = _peers()
        for a in range(na):
            for k, (dev, idx) in enumerate(peers):
                pltpu.make_async_remote_copy(
                    src_ref=src_view(s_refs[a], idx), dst_ref=dst_view(l_refs[a], me),
                    send_sem=send_sems.at[a * (N_DEV - 1) + k], recv_sem=recv_sems.at[a * (N_DEV - 1) + k],
                    device_id=dev, device_id_type=MESH).start()
        token[...] = jnp.zeros_like(token)

    outs = pl.pallas_call(
        body, name=name,
        out_shape=(pltpu.SemaphoreType.DMA((n,)), pltpu.SemaphoreType.DMA((n,)),
                   *[pltpu.HBM(t.shape, t.dtype) for t in list(srcs) + list(lands)], jax.ShapeDtypeStruct((8, HEAD), F32)),
        in_specs=[_HBM] * (2 * na),
        out_specs=(_SEM, _SEM, *[_HBM] * (2 * na), pl.BlockSpec(memory_space=pltpu.VMEM)),
        input_output_aliases={i: 2 + i for i in range(2 * na)},
        compiler_params=pltpu.CompilerParams(has_side_effects=_EFFECT),
    )(*[pltpu.with_memory_space_constraint(t, pltpu.HBM) for t in list(srcs) + list(lands)])
    return outs[0], outs[1], list(outs[2:2 + na]), list(outs[2 + na:2 + 2 * na]), outs[-1]


def _push_wait(name, send_sems, recv_sems, srcs, lands, after, src_view, dst_view):
    na = len(srcs)

    def body(*refs):
        s_refs, l_refs = refs[:na], refs[na:2 * na]
        send_sems, recv_sems = refs[2 * na], refs[2 * na + 1]
        me, peers = _peers()
        for a in range(na):
            for k, (dev, idx) in enumerate(peers):
                cp = pltpu.make_async_remote_copy(
                    src_ref=src_view(s_refs[a], idx), dst_ref=dst_view(l_refs[a], idx),
                    send_sem=send_sems.at[a * (N_DEV - 1) + k], recv_sem=recv_sems.at[a * (N_DEV - 1) + k],
                    device_id=dev, device_id_type=MESH)
                cp.wait_send()
                cp.wait_recv()

    outs = pl.pallas_call(
        body, name=name,
        out_shape=[pltpu.HBM(t.shape, t.dtype) for t in list(srcs) + list(lands)],
        in_specs=[_HBM] * (2 * na) + [_SEM, _SEM, pl.BlockSpec(memory_space=pl.ANY)],
        out_specs=[_HBM] * (2 * na),
        input_output_aliases={i: i for i in range(2 * na)},
        compiler_params=pltpu.CompilerParams(has_side_effects=_EFFECT),
    )(*srcs, *lands, send_sems, recv_sems, after)
    return list(outs[na:])


def _gather_start(name, shards):
    lands = [lax.empty((N_DEV,) + t.shape, t.dtype) for t in shards]
    return _push_start(name, shards, lands, lambda ref, idx: ref, lambda ref, slot: ref.at[slot])


def _gather_wait(name, started, shards, after, me):
    send_sems, recv_sems, srcs, lands, _ = started
    lands = _push_wait(name, send_sems, recv_sems, srcs, lands, after, lambda ref, idx: ref, lambda ref, slot: ref.at[slot])
    return [lax.dynamic_update_index_in_dim(g, t, me, 0) for g, t in zip(lands, shards)]


def _cols_to_plain(g):
    n, k, n8 = g.shape
    return jnp.transpose(g, (1, 0, 2)).reshape(k, n * n8)


def _plain_to_cols(w):
    k, n = w.shape
    return jnp.transpose(w.reshape(k, N_DEV, n // N_DEV), (1, 0, 2))


def _pack_rows(parts, width):
    rows, offs, r = [], [], 0
    for p in parts:
        flat = p.reshape(-1).astype(F32)
        nr = -(-flat.shape[0] // (8 * width)) * 8
        rows.append(jnp.pad(flat, (0, nr * width - flat.shape[0])).reshape(nr, width))
        offs.append((r, flat.shape[0], p.shape))
        r += nr
    return jnp.concatenate(rows, axis=0), offs, r


def _unpack_rows(slab, offs, width):
    lead = slab.shape[:-2]
    out = []
    for r0, n, shape in offs:
        nr = -(-n // width)
        out.append(slab[..., r0:r0 + nr, :].reshape(lead + (nr * width,))[..., :n].reshape(lead + tuple(shape)))
    return out


def kernel(x, c, positions, ada_w, ada_b, norm_mix, norm_ffn, ab_w_in, sgu_w, sgu_b, ab_w_out, conv_w_in, conv_w, conv_w_out, ffn_w_gate, ffn_w_up, ffn_w_down, final_norm, loss_target, m_ada_w, m_ada_b, m_norm_mix, m_norm_ffn, m_ab_w_in, m_sgu_w, m_sgu_b, m_ab_w_out, m_conv_w_in, m_conv_w, m_conv_w_out, m_ffn_w_gate, m_ffn_w_up, m_ffn_w_down, m_final_norm, v_ada_w, v_ada_b, v_norm_mix, v_norm_ffn, v_ab_w_in, v_sgu_w, v_sgu_b, v_ab_w_out, v_conv_w_in, v_conv_w, v_conv_w_out, v_ffn_w_gate, v_ffn_w_up, v_ffn_w_down, v_final_norm):
    xi, yi, ci = _place()
    me = 4 * xi + 2 * yi + ci
    s, d = x.shape[1], x.shape[2]
    depth = ada_w.shape[0]
    n_even = ab_w_in.shape[0]
    nh_mix = d // HEAD
    nh = 3 * nh_mix // 4
    ng = nh_mix - nh
    aw, gw = nh * HEAD, ng * HEAD
    x0 = x[0]
    target = loss_target[0]
    n_odd, cwid, d8 = conv_w.shape

    width = 512
    slab, offs, _ = _pack_rows([c, conv_w], width)
    gathered, _ = _all_gather_small("gather_cond", slab)
    c_parts, cw_parts = _unpack_rows(gathered.reshape(N_DEV, -1, width), offs, width)
    c_all = c_parts.reshape(N_DEV, d)
    conv_w_full = jnp.transpose(cw_parts, (1, 2, 0, 3)).reshape(n_odd, cwid, d)

    mod_cols = _ada_fwd(c_all, ada_w)
    n8 = mod_cols.shape[2]
    mod_all, token = _all_gather_small("gather_mod", mod_cols.reshape(depth * N_DEV, n8))
    mod_mine = lax.dynamic_index_in_dim(mod_all.reshape(N_DEV, depth, N_DEV, n8), me, axis=2, keepdims=False)

    def mixer_weights(l):
        return (ab_w_in[l // 2], ab_w_out[l // 2]) if l % 2 == 0 else (conv_w_in[l // 2], conv_w_out[l // 2])

    groups = [[mixer_weights(0)[0]], [mixer_weights(0)[1]], [ffn_w_gate[0], ffn_w_up[0], ffn_w_down[0]]]
    for l in range(1, depth):
        groups += [list(mixer_weights(l)), [ffn_w_gate[l], ffn_w_up[l], ffn_w_down[l]]]
    gathers, tok = [], token[0, 0]
    for n, ws in enumerate(groups):
        shards = [(w + tok).astype(BF16) for w in ws]
        started = _gather_start(f"gather_start_{n}", shards)
        gathers.append((started, shards))
        tok = started[4][0, 0]

    def weights_of_group(n, after):
        started, shards = gathers[n]
        return _gather_wait(f"gather_wait_{n}", started, shards, after, me)

    def plain_rows(g):
        return g.reshape(g.shape[0] * g.shape[1], g.shape[2])

    mod = jnp.transpose(mod_mine, (1, 0, 2)).reshape(depth, N_DEV * n8) + ada_b + tok
    mods = mod.reshape(depth, 6, 1, d)

    ct, st = _rope_tables(positions.reshape(s, 1))
    b_col = jnp.broadcast_to(sgu_b[..., None], sgu_b.shape + (CHUNK,))
    u_blk = 3 * aw // gw

    stream = [x0]
    saved = []
    w_in, w_out, w_gate, w_up, w_down = [[None] * depth for _ in range(5)]
    xcur = x0
    for l in range(depth):
        sh_m, sc_m, g_m, sh_f, sc_f, g_f = [mods[l, j] for j in range(6)]
        i = l // 2
        if l == 0:
            (g_in,) = weights_of_group(0, mod)
        else:
            g_in, g_out = weights_of_group(1 + 2 * l, xcur)
        w_in[l] = _cols_to_plain(g_in)
        h = _norm_mod_fwd(xcur, norm_mix[l][None], sc_m, sh_m)
        if l % 2 == 0:
            z = _matmul("mix_in", "nn", h, w_in[l], [BF16])
            attn, lse = _attn_fwd(z, ct, st, nh)
            sgu = _sgu_fwd(z, sgu_w[i], b_col[i], ng, u_blk)
            a = jnp.concatenate([attn, sgu], axis=1)
            mixer_saved = (z, a, lse)
        else:
            z = _matmul("conv_in", "nn", h, w_in[l], [BF16])
            a = _conv_fwd(z, conv_w_full[i])
            mixer_saved = (z, a, None)
        if l == 0:
            (g_out,) = weights_of_group(1, a)
        w_out[l] = plain_rows(g_out)
        x1, mix = _matmul("mix_out", "nn", a, w_out[l], [F32, BF16], extras=[(xcur, "mn"), (g_m, "n")],
                          epilogue=lambda acc, r, gv: (r + gv * acc, acc))
        g_gate, g_up, g_down = weights_of_group(2 + 2 * l, x1)
        w_gate[l], w_up[l], w_down[l] = _cols_to_plain(g_gate), _cols_to_plain(g_up), plain_rows(g_down)
        h2 = _norm_mod_fwd(x1, norm_ffn[l][None], sc_f, sh_f)
        gt, up, act = _matmul("ffn_in", "nn", h2, [w_gate[l], w_up[l]], [BF16, BF16, BF16], epilogue=_swiglu_epilogue)
        x2, f = _matmul("ffn_down", "nn", act, w_down[l], [F32, BF16], extras=[(x1, "mn"), (g_f, "n")],
                        epilogue=lambda acc, r, gv: (r + gv * acc, acc))
        saved.append((h, mixer_saved, mix, x1, h2, gt, up, act, f))
        stream.append(x2)
        xcur = x2

    f_last = saved[-1][8]
    loss_part, dx, dbr, d_final, dg = _final_loss(xcur, target, final_norm[None], f_last, mods[depth - 1, 5])
    loss = lax.psum(loss_part[0, 0], ("x", "y", "c"))

    dmod = [[None] * 6 for _ in range(depth)]
    d_norm_mix, d_norm_ffn = [None] * depth, [None] * depth
    d_sgu_w, d_sgu_b, d_conv_w = [None] * n_even, [None] * n_even, [None] * n_odd

    big = {"in": ab_w_in, "out": ab_w_out, "cin": conv_w_in, "cout": conv_w_out,
           "gate": ffn_w_gate, "up": ffn_w_up, "down": ffn_w_down}
    lands = {k: lax.empty((N_DEV,) + w.shape, BF16) for k, w in big.items()}
    own = {k: [None] * w.shape[0] for k, w in big.items()}
    pending = {"ffn": None, "mix": None}

    def exchange_finish(tag, after):
        (send_sems, recv_sems, srcs, lds, _), keys, li, layer = pending[tag]
        lds = _push_wait(f"exchange_wait_{tag}_{layer}", send_sems, recv_sems, srcs, lds, after,
                         lambda ref, idx: ref.at[idx], lambda ref, slot: ref.at[slot, li])
        for k, ld in zip(keys, lds):
            lands[k] = ld
        pending[tag] = None

    def exchange_start(tag, layer, keys, li, grads):
        if pending[tag] is not None:
            exchange_finish(tag, grads[0])
        for k, g in zip(keys, grads):
            own[k][li] = lax.dynamic_index_in_dim(g, me, 0, keepdims=False)
        started = _push_start(f"exchange_start_{tag}_{layer}", grads, [lands[k] for k in keys],
                              lambda ref, idx: ref.at[idx], lambda ref, slot: ref.at[slot, li])
        pending[tag] = (started, keys, li, layer)
        return started[4][0, 0]

    def row_shards(g):
        return g.reshape(N_DEV, g.shape[0] // N_DEV, g.shape[1])

    for l in reversed(range(depth)):
        sh_m, sc_m, g_m, sh_f, sc_f, g_f = [mods[l, j] for j in range(6)]
        h, (z, a, lse), mix, x1, h2, gt, up, act, f = saved[l]
        i = l // 2
        dmod[l][5] = dg
        gw_down = _matmul("ffn_down_wgrad", "tn", act, dbr, [BF16])
        dgt, dup = _matmul("ffn_down_dgrad", "nt", dbr, w_down[l], [BF16, BF16], extras=[(gt, "mn"), (up, "mn")],
                           epilogue=_swiglu_bwd_epilogue)
        gw_gate = _matmul("ffn_in_wgrad", "tn", h2, dgt, [BF16])
        gw_up = _matmul("ffn_in_wgrad", "tn", h2, dup, [BF16])
        tok = exchange_start("ffn", l, ("gate", "up", "down"), l, [_plain_to_cols(gw_gate), _plain_to_cols(gw_up), row_shards(gw_down)])
        dh2 = _matmul("ffn_in_dgrad", "nt", [dgt, dup], [w_gate[l], w_up[l]], [F32], after=tok)
        dx, dbr, dmod[l][3], dmod[l][4], d_norm_ffn[l], dg = _norm_mod_bwd(x1, dh2, norm_ffn[l][None], sc_f, dx, mix, g_m)
        dmod[l][2] = dg
        gw_out = _matmul("mix_out_wgrad", "tn", a, dbr, [BF16])
        da = _matmul("mix_out_dgrad", "nt", dbr, w_out[l], [BF16])
        if l % 2 == 0:
            dq, dk, dv = _attn_bwd(z, ct, st, da, a, lse, nh)
            du, dvg, d_sgu_w[i], dbb = _sgu_bwd(z, sgu_w[i], b_col[i], da, ng, u_blk, aw // gw)
            d_sgu_b[i] = dbb[:, :, 0]
            dz = jnp.concatenate([dq, dk, dv, du, dvg], axis=1)
            gw_in = _matmul("mix_in_wgrad", "tn", h, dz, [BF16])
            dh = _matmul("mix_in_dgrad", "nt", dz, w_in[l], [F32])
        else:
            dz, d_conv_w[i] = _conv_bwd(z, conv_w_full[i], da)
            gw_in = _matmul("conv_in_wgrad", "tn", h, dz, [BF16])
            dh = _matmul("conv_in_dgrad", "nt", dz, w_in[l], [F32])
        mix_group = ("mix", l, ("in", "out") if l % 2 == 0 else ("cin", "cout"), i, [_plain_to_cols(gw_in), row_shards(gw_out)])
        w_norm = norm_mix[l][None]
        if l > 0:
            w_norm = w_norm + exchange_start(*mix_group)
            f_prev, g_prev = saved[l - 1][8], mods[l - 1, 5]
            dx, dbr, dmod[l][0], dmod[l][1], d_norm_mix[l], dg = _norm_mod_bwd(stream[l], dh, w_norm, sc_m, dx, f_prev, g_prev)
        else:
            dx, dmod[l][0], dmod[l][1], d_norm_mix[l] = _norm_mod_bwd(stream[l], dh, w_norm, sc_m, dx)
    grad_x = dx[None]

    dmod_mine = jnp.stack([jnp.concatenate([v.reshape(d) for v in dmod[l]]) for l in range(depth)])
    small = [dmod_mine, jnp.concatenate(d_norm_mix), jnp.concatenate(d_norm_ffn), jnp.stack(d_sgu_w), jnp.stack(d_sgu_b),
             d_final, jnp.stack(d_conv_w)]
    slab, offs, _ = _pack_rows(small, width)
    gathered, token = _all_gather_small("gather_small_grads", slab)
    p_dmod, p_nmix, p_nffn, p_sguw, p_sgub, p_final, p_convw = _unpack_rows(gathered.reshape(N_DEV, -1, width), offs, width)
    mix_group[4][0] = mix_group[4][0] + token[0, 0].astype(BF16)
    p_dmod = p_dmod + exchange_start(*mix_group)

    outs = {}

    def update(name, pieces, w, m, v):
        shape = w.shape
        cdim = shape[-1]
        res = _adamw("adamw_" + name, pieces.reshape(pieces.shape[0], -1, cdim), w.reshape(-1, cdim),
                     m.reshape(-1, cdim), v.reshape(-1, cdim))
        outs[name] = [r.reshape(shape) for r in res]

    update("ada_b", p_dmod.reshape(N_DEV, depth, 6 * d), ada_b, m_ada_b, v_ada_b)
    update("norm_mix", p_nmix.reshape(N_DEV, depth, d), norm_mix, m_norm_mix, v_norm_mix)
    update("norm_ffn", p_nffn.reshape(N_DEV, depth, d), norm_ffn, m_norm_ffn, v_norm_ffn)
    update("sgu_w", p_sguw, sgu_w, m_sgu_w, v_sgu_w)
    update("sgu_b", p_sgub.reshape(N_DEV, 1, -1), sgu_b.reshape(1, -1), m_sgu_b.reshape(1, -1), v_sgu_b.reshape(1, -1))
    outs["sgu_b"] = [r.reshape(sgu_b.shape) for r in outs["sgu_b"]]
    update("final_norm", p_final.reshape(N_DEV, 1, d), final_norm[None], m_final_norm[None], v_final_norm[None])
    outs["final_norm"] = [r.reshape(final_norm.shape) for r in outs["final_norm"]]
    cw_mine = lax.dynamic_slice_in_dim(p_convw.reshape(N_DEV, n_odd, cwid, d), me * d8, d8, axis=3)
    update("conv_w", cw_mine, conv_w, m_conv_w, v_conv_w)

    dmod_cols = lax.dynamic_slice_in_dim(p_dmod.reshape(N_DEV, depth, 6 * d), me * n8, n8, axis=2)
    g_ada = _ada_wgrad(c_all, jnp.transpose(dmod_cols, (1, 0, 2)))
    update("ada_w", g_ada[None], ada_w, m_ada_w, v_ada_w)

    names = {"in": "ab_w_in", "out": "ab_w_out", "cin": "conv_w_in", "cout": "conv_w_out",
             "gate": "ffn_w_gate", "up": "ffn_w_up", "down": "ffn_w_down"}
    moments = {"in": (m_ab_w_in, v_ab_w_in), "out": (m_ab_w_out, v_ab_w_out), "cin": (m_conv_w_in, v_conv_w_in),
               "cout": (m_conv_w_out, v_conv_w_out), "gate": (m_ffn_w_gate, v_ffn_w_gate), "up": (m_ffn_w_up, v_ffn_w_up),
               "down": (m_ffn_w_down, v_ffn_w_down)}

    def update_big(k):
        pieces = lax.dynamic_update_slice(lands[k], jnp.stack(own[k])[None], (me, 0, 0, 0))
        update(names[k], pieces, big[k], *moments[k])

    exchange_finish("ffn", g_ada)
    for k in ("gate", "up", "down", "cin", "cout"):
        update_big(k)
    done = sum(outs[n][0][(0,) * outs[n][0].ndim] for n in outs)
    exchange_finish("mix", done.reshape(1, 1))
    for k in ("in", "out"):
        update_big(k)

    order = ["ada_w", "ada_b", "norm_mix", "norm_ffn", "ab_w_in", "sgu_w", "sgu_b", "ab_w_out", "conv_w_in", "conv_w",
             "conv_w_out", "ffn_w_gate", "ffn_w_up", "ffn_w_down", "final_norm"]
    return (loss, grad_x, *[outs[n][0] for n in order], *[outs[n][1] for n in order],
            *[outs[n][2] for n in order], *[outs[n][3] for n in order])
```

```python
import functools
import math

import numpy as np
import jax
import jax.numpy as jnp
from jax import lax
from jax.experimental import pallas as pl
from jax.experimental.pallas import tpu as pltpu

F32, BF16 = jnp.float32, jnp.bfloat16
MESH = pl.DeviceIdType.MESH
N_DEV = 8
EPS = 1e-6
HEAD = 128
CHUNK = 128
DILATIONS = (1, 4, 16)
ATT_CHUNK = CHUNK * DILATIONS[-1]
ATT_BATCH = 8
ROPE_THETA = 500000.0
ROPE_DIM = HEAD // 4
NEG = -1e30
ADAM_LR, ADAM_B1, ADAM_B2, ADAM_EPS, ADAM_WD, ADAM_STEP = 0.001, 0.9, 0.999, 1e-08, 0.01, 10
VMEM_LIMIT = 56 * 1024 * 1024


def _cparams(*sem):
    return pltpu.CompilerParams(dimension_semantics=sem or None, vmem_limit_bytes=VMEM_LIMIT)


def _tile(n, pref, unit):
    t = (min(pref, n) // unit) * unit
    while t >= unit:
        if n % t == 0:
            return t
        t -= unit
    return n


def _dot(a, b, dims):
    return lax.dot_general(a, b, (dims, ((), ())), preferred_element_type=F32)


def _dot_nn(a, b):
    return _dot(a, b, ((1,), (0,)))


def _dot_nt(a, b):
    return _dot(a, b, ((1,), (1,)))


def _dot_tn(a, b):
    return _dot(a, b, ((0,), (0,)))


FULL_ROW = 2048
_TILE_DEFAULT = dict(tm=1024, tn=512, tk=2816)
_TILES = {
    "ffn_down": dict(tn=1024),
    "ffn_down_wgrad": dict(tm=1408, tn=1024),
    "ffn_in_wgrad": dict(tn=1408), "mix_in_wgrad": dict(tn=1408), "conv_in_wgrad": dict(tn=1536),
    "mix_in_dgrad": dict(tn=1024, tk=1408), "conv_in_dgrad": dict(tn=1024, tk=1536),
    "mix_out": dict(tm=512, tn=FULL_ROW),
    "mix_out_wgrad": dict(tn=1024), "mix_out_dgrad": dict(tn=1024),
}


def _matmul(name, mode, a, b, outs, *, extras=(), epilogue=None, after=None):
    a_list = list(a) if isinstance(a, (list, tuple)) else [a]
    b_list = list(b) if isinstance(b, (list, tuple)) else [b]
    na, nb = len(a_list), len(b_list)
    paired = na > 1
    assert na == nb if paired else na == 1
    nacc = 1 if paired else nb
    a0, b0 = a_list[0], b_list[0]
    if mode == "nn":
        (m, kk), (_, n) = a0.shape, b0.shape
    elif mode == "nt":
        (m, kk), (n, _) = a0.shape, b0.shape
    else:
        (kk, m), (_, n) = a0.shape, b0.shape
    if after is not None:
        inner = epilogue or (lambda *accs: accs)
        extras = list(extras) + [(jnp.zeros((1, n), F32) + after, "n")]
        epilogue = lambda *tiles: inner(*tiles[:-1])
    pref = {**_TILE_DEFAULT, **_TILES.get(name, {})}
    tm, tn, tk = _tile(m, pref["tm"], 128), _tile(n, pref["tn"], 128), _tile(kk, pref["tk"], 128)
    nk = kk // tk
    dotf = {"nn": _dot_nn, "nt": _dot_nt, "tn": _dot_tn}[mode]
    a_spec = pl.BlockSpec((tk, tm), lambda i, j, k: (k, i)) if mode == "tn" else pl.BlockSpec((tm, tk), lambda i, j, k: (i, k))
    b_spec = pl.BlockSpec((tn, tk), lambda i, j, k: (j, k)) if mode == "nt" else pl.BlockSpec((tk, tn), lambda i, j, k: (k, j))
    e_specs = [pl.BlockSpec((tm, tn), lambda i, j, k: (i, j)) if kind == "mn" else pl.BlockSpec((1, tn), lambda i, j, k: (0, j))
               for _, kind in extras]
    ne, no = len(extras), len(outs)
    epi = epilogue or (lambda *accs: accs)

    def body(*refs):
        a_refs, b_refs, rest = refs[:na], refs[na:na + nb], refs[na + nb:]
        e_refs, o_refs, acc_refs = rest[:ne], rest[ne:ne + no], rest[ne + no:]

        def products():
            if paired:
                p = dotf(a_refs[0][...], b_refs[0][...])
                for a_ref, b_ref in zip(a_refs[1:], b_refs[1:]):
                    p = p + dotf(a_ref[...], b_ref[...])
                return [p]
            av = a_refs[0][...]
            return [dotf(av, b_ref[...]) for b_ref in b_refs]

        def finish(accs):
            for o_ref, o in zip(o_refs, epi(*accs, *[r[...] for r in e_refs])):
                o_ref[...] = o.astype(o_ref.dtype)

        if nk == 1:
            finish(products())
            return
        k = pl.program_id(2)

        @pl.when(k == 0)
        def _():
            for acc_ref in acc_refs:
                acc_ref[...] = jnp.zeros_like(acc_ref)

        for acc_ref, p in zip(acc_refs, products()):
            acc_ref[...] += p

        @pl.when(k == nk - 1)
        def _():
            finish([acc_ref[...] for acc_ref in acc_refs])

    res = pl.pallas_call(
        body, name=name, grid=(m // tm, n // tn, nk),
        in_specs=[a_spec] * na + [b_spec] * nb + e_specs,
        out_specs=[pl.BlockSpec((tm, tn), lambda i, j, k: (i, j)) for _ in outs],
        out_shape=[jax.ShapeDtypeStruct((m, n), dt) for dt in outs],
        scratch_shapes=[pltpu.VMEM((tm, tn), F32)] * nacc if nk > 1 else [],
        compiler_params=_cparams("parallel", "parallel", "arbitrary"),
    )(*a_list, *b_list, *[e for e, _ in extras])
    return res[0] if no == 1 else res


def _norm_mod_fwd(x, w, sc, sh):
    s, d = x.shape
    tm = _tile(s, 512, 8)

    def body(x_ref, w_ref, sc_ref, sh_ref, h_ref):
        xv = x_ref[...]
        r = lax.rsqrt(jnp.mean(xv * xv, axis=-1, keepdims=True) + EPS)
        h_ref[...] = ((xv * r) * w_ref[...] * (1.0 + sc_ref[...]) + sh_ref[...]).astype(BF16)

    row = pl.BlockSpec((1, d), lambda i: (0, 0))
    return pl.pallas_call(
        body, name="norm_mod_fwd", grid=(s // tm,),
        in_specs=[pl.BlockSpec((tm, d), lambda i: (i, 0)), row, row, row],
        out_specs=pl.BlockSpec((tm, d), lambda i: (i, 0)),
        out_shape=jax.ShapeDtypeStruct((s, d), BF16),
        compiler_params=_cparams("parallel"),
    )(x, w, sc, sh)


def _gated_add_norm_epilogue(acc, res, g, w, sc, sh):
    xv = res + g * acc
    r = lax.rsqrt(jnp.mean(xv * xv, axis=-1, keepdims=True) + EPS)
    return xv, acc, (xv * r) * w * (1.0 + sc) + sh


def _colsum8(t):
    tm, d = t.shape
    return jnp.sum(t.reshape(tm // 8, 8, d), axis=0)


def _norm_mod_bwd(x, dh, w, sc, dres, branch=None, g=None):
    s, d = x.shape
    tm = _tile(s, 256, 8)
    nsteps = s // tm
    gated = branch is not None

    def body(*refs):
        if gated:
            x_ref, dh_ref, w_ref, sc_ref, dres_ref, br_ref, g_ref, dx_ref, dbr_ref, dsh_ref, dsc_ref, dw_ref, dg_ref, acc = refs
        else:
            x_ref, dh_ref, w_ref, sc_ref, dres_ref, dx_ref, dsh_ref, dsc_ref, dw_ref, acc = refs
        i = pl.program_id(0)

        @pl.when(i == 0)
        def _():
            acc[...] = jnp.zeros_like(acc)

        xv, dhv, wv, scv = x_ref[...], dh_ref[...].astype(F32), w_ref[...], sc_ref[...]
        r = lax.rsqrt(jnp.mean(xv * xv, axis=-1, keepdims=True) + EPS)
        xn = xv * r
        dxn = dhv * (wv * (1.0 + scv))
        dx = dres_ref[...] + r * (dxn - xn * jnp.mean(dxn * xn, axis=-1, keepdims=True))
        dx_ref[...] = dx
        acc[0] += _colsum8(dhv)
        acc[1] += _colsum8(dhv * xn)
        if gated:
            dbr_ref[...] = (dx * g_ref[...]).astype(BF16)
            acc[2] += _colsum8(dx * br_ref[...].astype(F32))

        @pl.when(i == nsteps - 1)
        def _():
            a0 = jnp.sum(acc[0], axis=0, keepdims=True)
            a1 = jnp.sum(acc[1], axis=0, keepdims=True)
            dsh_ref[...] = a0
            dsc_ref[...] = a1 * wv
            dw_ref[...] = a1 * (1.0 + scv)
            if gated:
                dg_ref[...] = jnp.sum(acc[2], axis=0, keepdims=True)

    big = pl.BlockSpec((tm, d), lambda i: (i, 0))
    row = pl.BlockSpec((1, d), lambda i: (0, 0))
    rowo = jax.ShapeDtypeStruct((1, d), F32)
    in_specs = [big, big, row, row, big] + ([big, row] if gated else [])
    out_specs = [big] + ([big] if gated else []) + [row, row, row] + ([row] if gated else [])
    out_shape = ([jax.ShapeDtypeStruct((s, d), F32)] + ([jax.ShapeDtypeStruct((s, d), BF16)] if gated else [])
                 + [rowo, rowo, rowo] + ([rowo] if gated else []))
    args = [x, dh, w, sc, dres] + ([branch, g] if gated else [])
    return pl.pallas_call(
        body, name="norm_mod_bwd_gated" if gated else "norm_mod_bwd", grid=(nsteps,),
        in_specs=in_specs, out_specs=out_specs, out_shape=out_shape,
        scratch_shapes=[pltpu.VMEM((3, 8, d), F32)],
        compiler_params=_cparams("arbitrary"),
    )(*args)


def _final_loss(x, target, w, branch, g):
    s, d = x.shape
    tm = _tile(s, 256, 8)
    nsteps = s // tm

    def body(x_ref, t_ref, w_ref, br_ref, g_ref, loss_ref, dx_ref, dbr_ref, dw_ref, dg_ref, acc):
        i = pl.program_id(0)

        @pl.when(i == 0)
        def _():
            acc[...] = jnp.zeros_like(acc)

        xv, wv = x_ref[...], w_ref[...]
        r = lax.rsqrt(jnp.mean(xv * xv, axis=-1, keepdims=True) + EPS)
        xn = xv * r
        err = xn * wv - t_ref[...]
        dy = err * (1.0 / d)
        dxn = dy * wv
        dx = r * (dxn - xn * jnp.mean(dxn * xn, axis=-1, keepdims=True))
        dx_ref[...] = dx
        dbr_ref[...] = (dx * g_ref[...]).astype(BF16)
        acc[0] += _colsum8(err * err)
        acc[1] += _colsum8(dy * xn)
        acc[2] += _colsum8(dx * br_ref[...].astype(F32))

        @pl.when(i == nsteps - 1)
        def _():
            loss_ref[...] = jnp.sum(jnp.sum(acc[0], axis=0, keepdims=True), axis=1, keepdims=True) * (0.5 / d)
            dw_ref[...] = jnp.sum(acc[1], axis=0, keepdims=True)
            dg_ref[...] = jnp.sum(acc[2], axis=0, keepdims=True)

    big = pl.BlockSpec((tm, d), lambda i: (i, 0))
    row = pl.BlockSpec((1, d), lambda i: (0, 0))
    rowo = jax.ShapeDtypeStruct((1, d), F32)
    return pl.pallas_call(
        body, name="final_loss", grid=(nsteps,),
        in_specs=[big, big, row, big, row],
        out_specs=[pl.BlockSpec((1, 1), lambda i: (0, 0)), big, big, row, row],
        out_shape=[jax.ShapeDtypeStruct((1, 1), F32), jax.ShapeDtypeStruct((s, d), F32),
                   jax.ShapeDtypeStruct((s, d), BF16), rowo, rowo],
        scratch_shapes=[pltpu.VMEM((3, 8, d), F32)],
        compiler_params=_cparams("arbitrary"),
    )(x, target, w, branch, g)


def _rope_tables(pos_col):
    s = pos_col.shape[0]
    tq = _tile(s, 1024, 8)
    half = ROPE_DIM // 2
    inv = np.float32(ROPE_THETA) ** (-np.arange(0, ROPE_DIM, 2, dtype=np.float32) / np.float32(ROPE_DIM))
    inv_row = jnp.asarray(np.tile(inv.astype(np.float32), HEAD // half)[None, :])

    def body(p_ref, inv_ref, ct_ref, st_ref):
        lane = lax.broadcasted_iota(jnp.int32, (tq, HEAD), 1)
        ang = p_ref[...].astype(F32) * inv_ref[...]
        cs, sn = jnp.cos(ang), jnp.sin(ang)
        ct_ref[...] = jnp.where(lane < ROPE_DIM, cs, 1.0)
        st_ref[...] = jnp.where(lane < half, -sn, jnp.where(lane < ROPE_DIM, sn, 0.0))

    blk = pl.BlockSpec((tq, HEAD), lambda i: (i, 0))
    return pl.pallas_call(
        body, name="rope_tables", grid=(s // tq,),
        in_specs=[pl.BlockSpec((tq, 1), lambda i: (i, 0)), pl.BlockSpec((1, HEAD), lambda i: (0, 0))],
        out_specs=[blk, blk],
        out_shape=[jax.ShapeDtypeStruct((s, HEAD), F32)] * 2,
        compiler_params=_cparams("parallel"),
    )(pos_col, inv_row)


def _swap_halves(x):
    lane = lax.broadcasted_iota(jnp.int32, x.shape, 1)
    half = ROPE_DIM // 2
    return jnp.where(lane < half, pltpu.roll(x, HEAD - half, 1), pltpu.roll(x, half, 1))


def _rope(x, ct, st):
    return x * ct + _swap_halves(x) * st


def _rope_t(dy, ct, st):
    return dy * ct - _swap_halves(dy) * st


def _band_bias(bias_ref):
    qi = lax.broadcasted_iota(jnp.int32, (CHUNK, 2 * CHUNK), 0)
    kj = lax.broadcasted_iota(jnp.int32, (CHUNK, 2 * CHUNK), 1)
    band = (kj >= qi) & (kj <= qi + CHUNK)
    bias_ref[0] = jnp.where(band, 0.0, NEG)
    bias_ref[1] = jnp.where(band & (kj >= CHUNK), 0.0, NEG)


def _rows(start, size, d):
    return pl.ds(start, size) if d == 1 else pl.ds(start, size, stride=d)


def _batch_units(ub, d, c):
    if d >= ATT_BATCH:
        per_sb = d // ATT_BATCH
        units = [(ub // per_sb, (ub % per_sb) * ATT_BATCH + j) for j in range(ATT_BATCH)]
    else:
        n_sb = ATT_BATCH // d
        units = [(ub * n_sb + t, r) for t in range(n_sb) for r in range(d)]
    out = []
    for sb, r in units:
        base = sb * (CHUNK * d)
        if not isinstance(base, int):
            base = pl.multiple_of(base, CHUNK)
        first = jnp.where((c == 0) & (sb == 0), 1, 0)
        out.append((_rows(base + r, CHUNK, d), _rows(base + (ATT_CHUNK - CHUNK * d + r), 2 * CHUNK, d), first))
    return out


def _for_batches(d, fn):
    n = ATT_CHUNK // CHUNK // ATT_BATCH
    if d >= ATT_BATCH:
        for ub in range(n):
            fn(ub)
    else:
        def step(ub, carry):
            fn(ub)
            return carry
        lax.fori_loop(0, n, step, 0)


def _bdot(a, b, ca, cb):
    return lax.dot_general(a, b, (((ca,), (cb,)), ((0,), (0,))), preferred_element_type=F32)


def _attn_specs(nh, nc):
    cur = lambda off: pl.BlockSpec((ATT_CHUNK, HEAD), lambda h, c: (jnp.minimum(c, nc - 1), off + h))
    prev = lambda off: pl.BlockSpec((ATT_CHUNK, HEAD), lambda h, c: (jnp.maximum(c - 1, 0), off + h))
    tcur = pl.BlockSpec((ATT_CHUNK, HEAD), lambda h, c: (jnp.minimum(c, nc - 1), 0))
    tprev = pl.BlockSpec((ATT_CHUNK, HEAD), lambda h, c: (jnp.maximum(c - 1, 0), 0))
    return [cur(0), prev(nh), cur(nh), prev(2 * nh), cur(2 * nh), tcur, tcur, tprev, tprev]


def _attn_fwd(z, ct, st, nh):
    s = z.shape[0]
    nc = s // ATT_CHUNK
    scale = HEAD ** -0.5

    def body(q_ref, kp_ref, kc_ref, vp_ref, vc_ref, ctc, stc, ctp, stp, o_ref, lse_ref, qf, kf, vf, ob, lb, bias):
        c = pl.program_id(1)
        qf[...] = _rope(q_ref[...].astype(F32), ctc[...], stc[...])
        kf[0:ATT_CHUNK] = _rope(kp_ref[...].astype(F32), ctp[...], stp[...])
        kf[ATT_CHUNK:] = _rope(kc_ref[...].astype(F32), ctc[...], stc[...])
        vf[0:ATT_CHUNK] = vp_ref[...].astype(F32)
        vf[ATT_CHUNK:] = vc_ref[...].astype(F32)
        _band_bias(bias)
        for b, d in enumerate(DILATIONS):
            def batch(ub, b=b, d=d):
                units = _batch_units(ub, d, c)
                q_b = jnp.stack([qf[qr, :] for qr, _, _ in units]).astype(BF16)
                k_b = jnp.stack([kf[kr, :] for _, kr, _ in units]).astype(BF16)
                v_b = jnp.stack([vf[kr, :] for _, kr, _ in units]).astype(BF16)
                bias_b = jnp.stack([bias[first] for _, _, first in units])
                sc = _bdot(q_b, k_b, 2, 2) * scale + bias_b
                m = jnp.max(sc, axis=2, keepdims=True)
                p = jnp.exp(sc - m)
                l = jnp.sum(p, axis=2, keepdims=True)
                o = _bdot(p.astype(BF16), v_b, 2, 1) / l
                lse = m + jnp.log(l)
                for j, (qr, _, _) in enumerate(units):
                    ob[b, qr, :] = o[j]
                    lb[b, qr, :] = jnp.broadcast_to(lse[j], (CHUNK, HEAD))
            _for_batches(d, batch)
        mx = jnp.maximum(jnp.maximum(lb[0], lb[1]), lb[2])
        e0, e1, e2 = jnp.exp(lb[0] - mx), jnp.exp(lb[1] - mx), jnp.exp(lb[2] - mx)
        den = e0 + e1 + e2
        o_ref[...] = ((e0 * ob[0] + e1 * ob[1] + e2 * ob[2]) / den).astype(BF16)
        lse_ref[...] = mx + jnp.log(den)

    blk = pl.BlockSpec((ATT_CHUNK, HEAD), lambda h, c: (c, h))
    return pl.pallas_call(
        body, name="attn_fwd", grid=(nh, nc),
        in_specs=_attn_specs(nh, nc), out_specs=[blk, blk],
        out_shape=[jax.ShapeDtypeStruct((s, nh * HEAD), BF16), jax.ShapeDtypeStruct((s, nh * HEAD), F32)],
        scratch_shapes=[pltpu.VMEM((ATT_CHUNK, HEAD), F32), pltpu.VMEM((2 * ATT_CHUNK, HEAD), F32),
                        pltpu.VMEM((2 * ATT_CHUNK, HEAD), F32), pltpu.VMEM((3, ATT_CHUNK, HEAD), F32),
                        pltpu.VMEM((3, ATT_CHUNK, HEAD), F32), pltpu.VMEM((2, CHUNK, 2 * CHUNK), F32)],
        compiler_params=_cparams("parallel", "arbitrary"),
    )(z, z, z, z, z, ct, st, ct, st)


def _attn_bwd(z, ct, st, da, o, lse, nh):
    s = z.shape[0]
    nc = s // ATT_CHUNK
    scale = HEAD ** -0.5

    def body(q_ref, kp_ref, kc_ref, vp_ref, vc_ref, ctc, stc, ctp, stp, do_ref, o_ref, lse_ref,
             dq_ref, dk_ref, dv_ref, qf, kf, vf, dof, dbar, dqa, dkf, dvf, bias):
        c = pl.program_id(1)

        @pl.when(c == 0)
        def _():
            dkf[...] = jnp.zeros_like(dkf)
            dvf[...] = jnp.zeros_like(dvf)

        @pl.when(c > 0)
        def _():
            dkf[0:ATT_CHUNK] = dkf[ATT_CHUNK:]
            dvf[0:ATT_CHUNK] = dvf[ATT_CHUNK:]
            dkf[ATT_CHUNK:] = jnp.zeros((ATT_CHUNK, HEAD), F32)
            dvf[ATT_CHUNK:] = jnp.zeros((ATT_CHUNK, HEAD), F32)

        @pl.when(c < nc)
        def _():
            qf[...] = _rope(q_ref[...].astype(F32), ctc[...], stc[...])
            kf[0:ATT_CHUNK] = _rope(kp_ref[...].astype(F32), ctp[...], stp[...])
            kf[ATT_CHUNK:] = _rope(kc_ref[...].astype(F32), ctc[...], stc[...])
            vf[0:ATT_CHUNK] = vp_ref[...].astype(F32)
            vf[ATT_CHUNK:] = vc_ref[...].astype(F32)
            dov = do_ref[...].astype(F32)
            dof[...] = dov
            dbar[...] = jnp.broadcast_to(jnp.sum(dov * o_ref[...].astype(F32), axis=1, keepdims=True), (ATT_CHUNK, HEAD))
            dqa[...] = jnp.zeros_like(dqa)
            _band_bias(bias)
            for d in DILATIONS:
                def batch(ub, d=d):
                    units = _batch_units(ub, d, c)
                    q_b = jnp.stack([qf[qr, :] for qr, _, _ in units]).astype(BF16)
                    k_b = jnp.stack([kf[kr, :] for _, kr, _ in units]).astype(BF16)
                    v_b = jnp.stack([vf[kr, :] for _, kr, _ in units]).astype(BF16)
                    do_b = jnp.stack([dof[qr, :] for qr, _, _ in units]).astype(BF16)
                    lse_b = jnp.stack([lse_ref[qr, :] for qr, _, _ in units])
                    dbar_b = jnp.stack([dbar[qr, :] for qr, _, _ in units])
                    bias_b = jnp.stack([bias[first] for _, _, first in units])
                    sc = _bdot(q_b, k_b, 2, 2) * scale + bias_b
                    p = jnp.exp(sc - jnp.concatenate([lse_b, lse_b], axis=2))
                    dp = _bdot(do_b, v_b, 2, 2)
                    ds = (p * (dp - jnp.concatenate([dbar_b, dbar_b], axis=2)) * scale).astype(BF16)
                    dq = _bdot(ds, k_b, 2, 1)
                    dk = _bdot(ds, q_b, 1, 1)
                    dv = _bdot(p.astype(BF16), do_b, 1, 1)
                    for j, (qr, kr, _) in enumerate(units):
                        dqa[qr, :] += dq[j]
                        dkf[kr, :] += dk[j]
                        dvf[kr, :] += dv[j]
                _for_batches(d, batch)
            dq_ref[...] = _rope_t(dqa[...], ctc[...], stc[...]).astype(BF16)

        @pl.when(c > 0)
        def _():
            dk_ref[...] = _rope_t(dkf[0:ATT_CHUNK], ctp[...], stp[...]).astype(BF16)
            dv_ref[...] = dvf[0:ATT_CHUNK].astype(BF16)

    cur = pl.BlockSpec((ATT_CHUNK, HEAD), lambda h, c: (jnp.minimum(c, nc - 1), h))
    late = pl.BlockSpec((ATT_CHUNK, HEAD), lambda h, c: (jnp.maximum(c - 1, 0), h))
    shp = jax.ShapeDtypeStruct((s, nh * HEAD), BF16)
    big = pltpu.VMEM((2 * ATT_CHUNK, HEAD), F32)
    one = pltpu.VMEM((ATT_CHUNK, HEAD), F32)
    return pl.pallas_call(
        body, name="attn_bwd", grid=(nh, nc + 1),
        in_specs=_attn_specs(nh, nc) + [cur, cur, cur], out_specs=[cur, late, late],
        out_shape=[shp, shp, shp],
        scratch_shapes=[one, big, big, one, one, one, big, big, pltpu.VMEM((2, CHUNK, 2 * CHUNK), F32)],
        compiler_params=_cparams("parallel", "arbitrary"),
    )(z, z, z, z, z, ct, st, ct, st, da, o, lse)


_GELU_K = math.sqrt(2.0 / math.pi)


def _gelu(x):
    return 0.5 * x * (1.0 + jnp.tanh(_GELU_K * (x + 0.044715 * x * x * x)))


def _gelu_and_grad(x):
    t = jnp.tanh(_GELU_K * (x + 0.044715 * x * x * x))
    g = 0.5 * x * (1.0 + t)
    dg = 0.5 * (1.0 + t) + 0.5 * x * (1.0 - t * t) * (_GELU_K * (1.0 + 3 * 0.044715 * x * x))
    return g, dg


def _tril(w):
    ti = lax.broadcasted_iota(jnp.int32, (CHUNK, CHUNK), 0)
    si = lax.broadcasted_iota(jnp.int32, (CHUNK, CHUNK), 1)
    return jnp.where(si <= ti, w, 0.0)


def _sgu_fwd(z, w_s, b_col, ng, u_blk):
    s = z.shape[0]
    gw = ng * HEAD
    tq = _tile(s, 1024, CHUNK)

    def body(u_ref, v_ref, w_ref, b_ref, o_ref):
        for g in range(ng):
            wg = _tril(w_ref[g]).astype(BF16)
            cols = slice(g * HEAD, (g + 1) * HEAD)
            for n in range(tq // CHUNK):
                rows = slice(n * CHUNK, (n + 1) * CHUNK)
                gv = _gelu(v_ref[rows, cols].astype(F32)).astype(BF16)
                mixed = _dot_nn(wg, gv) + b_ref[g]
                o_ref[rows, cols] = (_gelu(u_ref[rows, cols].astype(F32)) * mixed).astype(BF16)

    full = pl.BlockSpec((ng, CHUNK, CHUNK), lambda i: (0, 0, 0))
    return pl.pallas_call(
        body, name="sgu_fwd", grid=(s // tq,),
        in_specs=[pl.BlockSpec((tq, gw), lambda i: (i, u_blk)), pl.BlockSpec((tq, gw), lambda i: (i, u_blk + 1)), full, full],
        out_specs=pl.BlockSpec((tq, gw), lambda i: (i, 0)),
        out_shape=jax.ShapeDtypeStruct((s, gw), BF16),
        compiler_params=_cparams("parallel"),
    )(z, z, w_s, b_col)


def _sgu_bwd(z, w_s, b_col, da, ng, u_blk, da_blk):
    s = z.shape[0]
    gw = ng * HEAD
    tq = _tile(s, 1024, CHUNK)
    nsteps = s // tq

    def body(u_ref, v_ref, w_ref, b_ref, do_ref, du_ref, dv_ref, dw_ref, db_ref):
        i = pl.program_id(0)

        @pl.when(i == 0)
        def _():
            dw_ref[...] = jnp.zeros_like(dw_ref)
            db_ref[...] = jnp.zeros_like(db_ref)

        for g in range(ng):
            wg = _tril(w_ref[g]).astype(BF16)
            cols = slice(g * HEAD, (g + 1) * HEAD)
            dw_acc = jnp.zeros((CHUNK, CHUNK), F32)
            db_acc = jnp.zeros((CHUNK, 1), F32)
            for n in range(tq // CHUNK):
                rows = slice(n * CHUNK, (n + 1) * CHUNK)
                gu, dgu = _gelu_and_grad(u_ref[rows, cols].astype(F32))
                gv, dgv = _gelu_and_grad(v_ref[rows, cols].astype(F32))
                gvb = gv.astype(BF16)
                mixed = _dot_nn(wg, gvb) + b_ref[g]
                dout = do_ref[rows, cols].astype(F32)
                du_ref[rows, cols] = (dout * mixed * dgu).astype(BF16)
                dmix = dout * gu
                dmb = dmix.astype(BF16)
                dv_ref[rows, cols] = (_dot_tn(wg, dmb) * dgv).astype(BF16)
                dw_acc += _dot_nt(dmb, gvb)
                db_acc += jnp.sum(dmix, axis=1, keepdims=True)
            dw_ref[g] += _tril(dw_acc)
            db_ref[g] += jnp.broadcast_to(db_acc, (CHUNK, CHUNK))

    full = pl.BlockSpec((ng, CHUNK, CHUNK), lambda i: (0, 0, 0))
    out = pl.BlockSpec((tq, gw), lambda i: (i, 0))
    return pl.pallas_call(
        body, name="sgu_bwd", grid=(nsteps,),
        in_specs=[pl.BlockSpec((tq, gw), lambda i: (i, u_blk)), pl.BlockSpec((tq, gw), lambda i: (i, u_blk + 1)), full, full,
                  pl.BlockSpec((tq, gw), lambda i: (i, da_blk))],
        out_specs=[out, out, full, full],
        out_shape=[jax.ShapeDtypeStruct((s, gw), BF16)] * 2 + [jax.ShapeDtypeStruct((ng, CHUNK, CHUNK), F32)] * 2,
        compiler_params=_cparams("arbitrary"),
    )(z, z, w_s, b_col, da)


def _shift_down(y, halo, k):
    rolled = pltpu.roll(y, k, 0)
    row = lax.broadcasted_iota(jnp.int32, y.shape, 0)
    for j in range(k):
        rolled = jnp.where(row == j, halo[8 - k + j:8 - k + j + 1, :], rolled)
    return rolled


def _shift_up(y, halo, k):
    n = y.shape[0]
    rolled = pltpu.roll(y, n - k, 0)
    row = lax.broadcasted_iota(jnp.int32, y.shape, 0)
    for j in range(k):
        rolled = jnp.where(row == n - k + j, halo[j:j + 1, :], rolled)
    return rolled


def _conv_fwd(z, cw):
    s, d3 = z.shape
    d = d3 // 3
    tq = _tile(s, 256, 8)

    def body(z_ref, zh_ref, cw_ref, a_ref):
        i = pl.program_id(0)
        zv = z_ref[...].astype(F32)
        zh = jnp.where(i > 0, zh_ref[...].astype(F32), 0.0)
        y = zv[:, d:2 * d] * zv[:, 2 * d:]
        yh = zh[:, d:2 * d] * zh[:, 2 * d:]
        cwv = cw_ref[...]
        conv = cwv[0:1] * _shift_down(y, yh, 2) + cwv[1:2] * _shift_down(y, yh, 1) + cwv[2:3] * y
        a_ref[...] = (zv[:, :d] * conv).astype(BF16)

    return pl.pallas_call(
        body, name="conv_fwd", grid=(s // tq,),
        in_specs=[pl.BlockSpec((tq, d3), lambda i: (i, 0)),
                  pl.BlockSpec((8, d3), lambda i: (jnp.maximum(i * (tq // 8) - 1, 0), 0)),
                  pl.BlockSpec((3, d), lambda i: (0, 0))],
        out_specs=pl.BlockSpec((tq, d), lambda i: (i, 0)),
        out_shape=jax.ShapeDtypeStruct((s, d), BF16),
        compiler_params=_cparams("parallel"),
    )(z, z, cw)


def _conv_bwd(z, cw, da):
    s, d3 = z.shape
    d = d3 // 3
    tq = _tile(s, 128, 8)
    nsteps = s // tq
    nblk8 = s // 8

    def body(z_ref, zp_ref, zn_ref, da_ref, dan_ref, cw_ref, dz_ref, dcw_ref, acc):
        i = pl.program_id(0)

        @pl.when(i == 0)
        def _():
            acc[...] = jnp.zeros_like(acc)

        zv = z_ref[...].astype(F32)
        zp = jnp.where(i > 0, zp_ref[...].astype(F32), 0.0)
        zn = jnp.where(i < nsteps - 1, zn_ref[...].astype(F32), 0.0)
        dav = da_ref[...].astype(F32)
        dan = jnp.where(i < nsteps - 1, dan_ref[...].astype(F32), 0.0)
        gb, gc, hx = zv[:, :d], zv[:, d:2 * d], zv[:, 2 * d:]
        y = gc * hx
        yp = zp[:, d:2 * d] * zp[:, 2 * d:]
        cwv = cw_ref[...]
        y1, y2 = _shift_down(y, yp, 1), _shift_down(y, yp, 2)
        conv = cwv[0:1] * y2 + cwv[1:2] * y1 + cwv[2:3] * y
        dconv = dav * gb
        dconv_n = dan * zn[:, :d]
        dy = cwv[2:3] * dconv + cwv[1:2] * _shift_up(dconv, dconv_n, 1) + cwv[0:1] * _shift_up(dconv, dconv_n, 2)
        dz_ref[:, :d] = (dav * conv).astype(BF16)
        dz_ref[:, d:2 * d] = (dy * hx).astype(BF16)
        dz_ref[:, 2 * d:] = (dy * gc).astype(BF16)
        acc[0] += _colsum8(dconv * y2)
        acc[1] += _colsum8(dconv * y1)
        acc[2] += _colsum8(dconv * y)

        @pl.when(i == nsteps - 1)
        def _():
            for j in range(3):
                dcw_ref[j:j + 1, :] = jnp.sum(acc[j], axis=0, keepdims=True)

    return pl.pallas_call(
        body, name="conv_bwd", grid=(nsteps,),
        in_specs=[pl.BlockSpec((tq, d3), lambda i: (i, 0)),
                  pl.BlockSpec((8, d3), lambda i: (jnp.maximum(i * (tq // 8) - 1, 0), 0)),
                  pl.BlockSpec((8, d3), lambda i: (jnp.minimum((i + 1) * (tq // 8), nblk8 - 1), 0)),
                  pl.BlockSpec((tq, d), lambda i: (i, 0)),
                  pl.BlockSpec((8, d), lambda i: (jnp.minimum((i + 1) * (tq // 8), nblk8 - 1), 0)),
                  pl.BlockSpec((3, d), lambda i: (0, 0))],
        out_specs=[pl.BlockSpec((tq, d3), lambda i: (i, 0)), pl.BlockSpec((3, d), lambda i: (0, 0))],
        out_shape=[jax.ShapeDtypeStruct((s, d3), BF16), jax.ShapeDtypeStruct((3, d), F32)],
        scratch_shapes=[pltpu.VMEM((3, 8, d), F32)],
        compiler_params=_cparams("arbitrary"),
    )(z, z, z, da, da, cw)


def _swiglu_epilogue(gate, up):
    return gate, up, gate / (1.0 + jnp.exp(-gate)) * up


def _swiglu_bwd_epilogue(dact, gt, up):
    g, u = gt.astype(F32), up.astype(F32)
    sg = 1.0 / (1.0 + jnp.exp(-g))
    return dact * u * (sg * (1.0 + g * (1.0 - sg))), dact * (g * sg)


def _ada_fwd(c_all, ada_w):
    nl, d, n8 = ada_w.shape
    tn = _tile(n8, 768, 128)

    def body(c_ref, w_ref, o_ref):
        cv = c_ref[...]
        act = (cv / (1.0 + jnp.exp(-cv))).astype(BF16)
        o_ref[0] = _dot_nn(act, w_ref[0].astype(BF16))

    return pl.pallas_call(
        body, name="ada_fwd", grid=(nl, n8 // tn),
        in_specs=[pl.BlockSpec((N_DEV, d), lambda l, j: (0, 0)), pl.BlockSpec((1, d, tn), lambda l, j: (l, 0, j))],
        out_specs=pl.BlockSpec((1, N_DEV, tn), lambda l, j: (l, 0, j)),
        out_shape=jax.ShapeDtypeStruct((nl, N_DEV, n8), F32),
        compiler_params=_cparams("parallel", "parallel"),
    )(c_all, ada_w)


def _ada_wgrad(c_all, dmod_cols):
    nl, _, n8 = dmod_cols.shape
    d = c_all.shape[1]
    tn = _tile(n8, 768, 128)

    def body(c_ref, g_ref, o_ref):
        cv = c_ref[...]
        act = (cv / (1.0 + jnp.exp(-cv))).astype(BF16)
        o_ref[0] = _dot_tn(act, g_ref[0].astype(BF16))

    return pl.pallas_call(
        body, name="ada_wgrad", grid=(nl, n8 // tn),
        in_specs=[pl.BlockSpec((N_DEV, d), lambda l, j: (0, 0)), pl.BlockSpec((1, N_DEV, tn), lambda l, j: (l, 0, j))],
        out_specs=pl.BlockSpec((1, d, tn), lambda l, j: (l, 0, j)),
        out_shape=jax.ShapeDtypeStruct((nl, d, n8), F32),
        compiler_params=_cparams("parallel", "parallel"),
    )(c_all, dmod_cols)


def _adamw(name, pieces, w, m, v):
    npc, r, c = pieces.shape
    tr = _tile(r, max(8, (1 << 19) // c // 8 * 8), 8)
    bc1, bc2 = 1.0 - ADAM_B1 ** ADAM_STEP, 1.0 - ADAM_B2 ** ADAM_STEP

    def body(p_ref, w_ref, m_ref, v_ref, g_ref, d_ref, nm_ref, nv_ref):
        g = p_ref[0].astype(F32)
        for i in range(1, npc):
            g = g + p_ref[i].astype(F32)
        nm = ADAM_B1 * m_ref[...] + (1.0 - ADAM_B1) * g
        nv = ADAM_B2 * v_ref[...] + (1.0 - ADAM_B2) * (g * g)
        g_ref[...] = g
        nm_ref[...] = nm
        nv_ref[...] = nv
        d_ref[...] = -ADAM_LR * ((nm / bc1) / (jnp.sqrt(nv / bc2) + ADAM_EPS) + ADAM_WD * w_ref[...])

    blk = pl.BlockSpec((tr, c), lambda i: (i, 0))
    return pl.pallas_call(
        body, name=name, grid=(r // tr,),
        in_specs=[pl.BlockSpec((npc, tr, c), lambda i: (0, i, 0)), blk, blk, blk],
        out_specs=[blk] * 4, out_shape=[jax.ShapeDtypeStruct((r, c), F32)] * 4,
        compiler_params=_cparams("parallel"),
    )(pieces, w, m, v)


def _place():
    x, y, c = lax.axis_index("x"), lax.axis_index("y"), lax.axis_index("c")
    return x, y, c


def _all_gather_small(name, x_shard):
    m_per, n = x_shard.shape

    def body(x_ref, out_ref, token_ref, send_sems, recv_sems, local_sem):
        token_ref[...] = jnp.zeros_like(token_ref)
        x, y, c = _place()
        me, sibling = (x, y, c), (x, y, 1 - c)
        chips = [(1 - x, y), (x, 1 - y), (1 - x, 1 - y)]

        def rows(px, py, pc):
            return out_ref.at[pl.ds((4 * px + 2 * py + pc) * m_per, m_per), :]

        def copy(k, block, to, src=None):
            return pltpu.make_async_remote_copy(
                src_ref=rows(*block) if src is None else src, dst_ref=rows(*block),
                send_sem=send_sems.at[k], recv_sem=recv_sems.at[k], device_id=to, device_id_type=MESH)

        mine = pltpu.make_async_copy(x_ref, rows(*me), local_sem)
        mine.start()
        first = [copy(0, me, sibling, src=x_ref)]
        first += [copy(1 + j, me, (*chip, c), src=x_ref) for j, chip in enumerate(chips)]
        for cp in first:
            cp.start()
        passed = [copy(4 + j, (*chip, c), sibling) for j, chip in enumerate(chips)]
        for j, chip in enumerate(chips):
            copy(1 + j, (*chip, c), me).wait_recv()
            passed[j].start()
        copy(0, sibling, me).wait_recv()
        for j, chip in enumerate(chips):
            copy(4 + j, (*chip, 1 - c), me).wait_recv()
        for cp in first + passed:
            cp.wait_send()
        mine.wait()

    vmem = pl.BlockSpec(memory_space=pltpu.VMEM)
    return pl.pallas_call(
        body, name=name,
        out_shape=[jax.ShapeDtypeStruct((N_DEV * m_per, n), x_shard.dtype), jax.ShapeDtypeStruct((8, HEAD), F32)],
        in_specs=[vmem], out_specs=[vmem, vmem],
        scratch_shapes=[pltpu.SemaphoreType.DMA((7,)), pltpu.SemaphoreType.DMA((7,)), pltpu.SemaphoreType.DMA],
        compiler_params=pltpu.CompilerParams(vmem_limit_bytes=VMEM_LIMIT),
    )(x_shard)


_HBM = pl.BlockSpec(memory_space=pltpu.HBM)
_SEM = pl.BlockSpec(memory_space=pltpu.SEMAPHORE)
_EFFECT = pltpu.SideEffectType.DATAFLOW_SIDE_EFFECTING


def _peers():
    x, y, c = _place()
    peers = []
    for k in range(1, N_DEV):
        px = 1 - x if k & 4 else x
        py = 1 - y if k & 2 else y
        pc = 1 - c if k & 1 else c
        peers.append(((px, py, pc), 4 * px + 2 * py + pc))
    return 4 * x + 2 * y + c, peers


def _push_start(name, srcs, lands, src_view, dst_view):
    na = len(srcs)
    n = na * (N_DEV - 1)

    def body(*refs):
        s_refs, l_refs = refs[:na], refs[na:2 * na]
        send_sems, recv_sems, token = refs[2 * na], refs[2 * na + 1], refs[-1]
        me, peers = _peers()
        for a in range(na):
            for k, (dev, idx) in enumerate(peers):
                pltpu.make_async_remote_copy(
                    src_ref=src_view(s_refs[a], idx), dst_ref=dst_view(l_refs[a], me),
                    send_sem=send_sems.at[a * (N_DEV - 1) + k], recv_sem=recv_sems.at[a * (N_DEV - 1) + k],
                    device_id=dev, device_id_type=MESH).start()
        token[...] = jnp.zeros_like(token)

    outs = pl.pallas_call(
        body, name=name,
        out_shape=(pltpu.SemaphoreType.DMA((n,)), pltpu.SemaphoreType.DMA((n,)),
                   *[pltpu.HBM(t.shape, t.dtype) for t in list(srcs) + list(lands)], jax.ShapeDtypeStruct((8, HEAD), F32)),
        in_specs=[_HBM] * (2 * na),
        out_specs=(_SEM, _SEM, *[_HBM] * (2 * na), pl.BlockSpec(memory_space=pltpu.VMEM)),
        input_output_aliases={i: 2 + i for i in range(2 * na)},
        compiler_params=pltpu.CompilerParams(has_side_effects=_EFFECT),
    )(*[pltpu.with_memory_space_constraint(t, pltpu.HBM) for t in list(srcs) + list(lands)])
    return outs[0], outs[1], list(outs[2:2 + na]), list(outs[2 + na:2 + 2 * na]), outs[-1]


def _push_wait(name, send_sems, recv_sems, srcs, lands, after, src_view, dst_view):
    na = len(srcs)

    def body(*refs):
        s_refs, l_refs = refs[:na], refs[na:2 * na]
        send_sems, recv_sems = refs[2 * na], refs[2 * na + 1]
        me, peers = _peers()
        for a in range(na):
            for k, (dev, idx) in enumerate(peers):
                cp = pltpu.make_async_remote_copy(
                    src_ref=src_view(s_refs[a], idx), dst_ref=dst_view(l_refs[a], idx),
                    send_sem=send_sems.at[a * (N_DEV - 1) + k], recv_sem=recv_sems.at[a * (N_DEV - 1) + k],
                    device_id=dev, device_id_type=MESH)
                cp.wait_send()
                cp.wait_recv()

    outs = pl.pallas_call(
        body, name=name,
        out_shape=[pltpu.HBM(t.shape, t.dtype) for t in list(srcs) + list(lands)],
        in_specs=[_HBM] * (2 * na) + [_SEM, _SEM, pl.BlockSpec(memory_space=pl.ANY)],
        out_specs=[_HBM] * (2 * na),
        input_output_aliases={i: i for i in range(2 * na)},
        compiler_params=pltpu.CompilerParams(has_side_effects=_EFFECT),
    )(*srcs, *lands, send_sems, recv_sems, after)
    return list(outs[na:])


def _gather_start(name, shards):
    lands = [lax.empty((N_DEV,) + t.shape, t.dtype) for t in shards]
    return _push_start(name, shards, lands, lambda ref, idx: ref, lambda ref, slot: ref.at[slot])


def _gather_wait(name, started, shards, after, me):
    send_sems, recv_sems, srcs, lands, _ = started
    lands = _push_wait(name, send_sems, recv_sems, srcs, lands, after, lambda ref, idx: ref, lambda ref, slot: ref.at[slot])
    return [lax.dynamic_update_index_in_dim(g, t, me, 0) for g, t in zip(lands, shards)]


def _cols_to_plain(g):
    n, k, n8 = g.shape
    return jnp.transpose(g, (1, 0, 2)).reshape(k, n * n8)


def _plain_to_cols(w):
    k, n = w.shape
    return jnp.transpose(w.reshape(k, N_DEV, n // N_DEV), (1, 0, 2))


def _pack_rows(parts, width):
    rows, offs, r = [], [], 0
    for p in parts:
        flat = p.reshape(-1).astype(F32)
        nr = -(-flat.shape[0] // (8 * width)) * 8
        rows.append(jnp.pad(flat, (0, nr * width - flat.shape[0])).reshape(nr, width))
        offs.append((r, flat.shape[0], p.shape))
        r += nr
    return jnp.concatenate(rows, axis=0), offs, r


def _unpack_rows(slab, offs, width):
    lead = slab.shape[:-2]
    out = []
    for r0, n, shape in offs:
        nr = -(-n // width)
        out.append(slab[..., r0:r0 + nr, :].reshape(lead + (nr * width,))[..., :n].reshape(lead + tuple(shape)))
    return out


def kernel(x, c, positions, ada_w, ada_b, norm_mix, norm_ffn, ab_w_in, sgu_w, sgu_b, ab_w_out, conv_w_in, conv_w, conv_w_out, ffn_w_gate, ffn_w_up, ffn_w_down, final_norm, loss_target, m_ada_w, m_ada_b, m_norm_mix, m_norm_ffn, m_ab_w_in, m_sgu_w, m_sgu_b, m_ab_w_out, m_conv_w_in, m_conv_w, m_conv_w_out, m_ffn_w_gate, m_ffn_w_up, m_ffn_w_down, m_final_norm, v_ada_w, v_ada_b, v_norm_mix, v_norm_ffn, v_ab_w_in, v_sgu_w, v_sgu_b, v_ab_w_out, v_conv_w_in, v_conv_w, v_conv_w_out, v_ffn_w_gate, v_ffn_w_up, v_ffn_w_down, v_final_norm):
    xi, yi, ci = _place()
    me = 4 * xi + 2 * yi + ci
    s, d = x.shape[1], x.shape[2]
    depth = ada_w.shape[0]
    n_even = ab_w_in.shape[0]
    nh_mix = d // HEAD
    nh = 3 * nh_mix // 4
    ng = nh_mix - nh
    aw, gw = nh * HEAD, ng * HEAD
    assert d <= FULL_ROW and s % ATT_CHUNK == 0
    x0 = x[0]
    target = loss_target[0]
    n_odd, cwid, d8 = conv_w.shape

    width = 512
    slab, offs, _ = _pack_rows([c, conv_w], width)
    gathered, _ = _all_gather_small("gather_cond", slab)
    c_parts, cw_parts = _unpack_rows(gathered.reshape(N_DEV, -1, width), offs, width)
    c_all = c_parts.reshape(N_DEV, d)
    conv_w_full = jnp.transpose(cw_parts, (1, 2, 0, 3)).reshape(n_odd, cwid, d)

    mod_cols = _ada_fwd(c_all, ada_w)
    n8 = mod_cols.shape[2]
    mod_all, token = _all_gather_small("gather_mod", mod_cols.reshape(depth * N_DEV, n8))
    mod_mine = lax.dynamic_index_in_dim(mod_all.reshape(N_DEV, depth, N_DEV, n8), me, axis=2, keepdims=False)

    def mixer_weights(l):
        return (ab_w_in[l // 2], ab_w_out[l // 2]) if l % 2 == 0 else (conv_w_in[l // 2], conv_w_out[l // 2])

    groups = [[mixer_weights(0)[0]], [mixer_weights(0)[1]], [ffn_w_gate[0], ffn_w_up[0], ffn_w_down[0]]]
    for l in range(1, depth):
        groups += [list(mixer_weights(l)), [ffn_w_gate[l], ffn_w_up[l], ffn_w_down[l]]]
    gathers, tok = [], token[0, 0]
    for n, ws in enumerate(groups):
        shards = [(w + tok).astype(BF16) for w in ws]
        started = _gather_start(f"gather_start_{n}", shards)
        gathers.append((started, shards))
        tok = started[4][0, 0]

    def weights_of_group(n, after):
        started, shards = gathers[n]
        return _gather_wait(f"gather_wait_{n}", started, shards, after, me)

    def plain_rows(g):
        return g.reshape(g.shape[0] * g.shape[1], g.shape[2])

    mod = jnp.transpose(mod_mine, (1, 0, 2)).reshape(depth, N_DEV * n8) + ada_b + tok
    mods = mod.reshape(depth, 6, 1, d)

    ct, st = _rope_tables(positions.reshape(s, 1))
    b_col = jnp.broadcast_to(sgu_b[..., None], sgu_b.shape + (CHUNK,))
    u_blk = 3 * aw // gw

    stream = [x0]
    saved = []
    w_in, w_out, w_gate, w_up, w_down = [[None] * depth for _ in range(5)]
    xcur = x0
    for l in range(depth):
        sh_m, sc_m, g_m, sh_f, sc_f, g_f = [mods[l, j] for j in range(6)]
        i = l // 2
        if l == 0:
            (g_in,) = weights_of_group(0, mod)
        else:
            g_in, g_out = weights_of_group(1 + 2 * l, xcur)
        w_in[l] = _cols_to_plain(g_in)
        h = _norm_mod_fwd(xcur, norm_mix[l][None], sc_m, sh_m)
        if l % 2 == 0:
            z = _matmul("mix_in", "nn", h, w_in[l], [BF16])
            attn, lse = _attn_fwd(z, ct, st, nh)
            sgu = _sgu_fwd(z, sgu_w[i], b_col[i], ng, u_blk)
            a = jnp.concatenate([attn, sgu], axis=1)
            mixer_saved = (z, a, lse)
        else:
            z = _matmul("conv_in", "nn", h, w_in[l], [BF16])
            a = _conv_fwd(z, conv_w_full[i])
            mixer_saved = (z, a, None)
        if l == 0:
            (g_out,) = weights_of_group(1, a)
        w_out[l] = plain_rows(g_out)
        x1, mix, h2 = _matmul("mix_out", "nn", a, w_out[l], [F32, BF16, BF16],
                              extras=[(xcur, "mn"), (g_m, "n"), (norm_ffn[l][None], "n"), (sc_f, "n"), (sh_f, "n")],
                              epilogue=_gated_add_norm_epilogue)
        g_gate, g_up, g_down = weights_of_group(2 + 2 * l, x1)
        w_gate[l], w_up[l], w_down[l] = _cols_to_plain(g_gate), _cols_to_plain(g_up), plain_rows(g_down)
        gt, up, act = _matmul("ffn_in", "nn", h2, [w_gate[l], w_up[l]], [BF16, BF16, BF16], epilogue=_swiglu_epilogue)
        x2, f = _matmul("ffn_down", "nn", act, w_down[l], [F32, BF16], extras=[(x1, "mn"), (g_f, "n")],
                        epilogue=lambda acc, r, gv: (r + gv * acc, acc))
        saved.append((h, mixer_saved, mix, x1, h2, gt, up, act, f))
        stream.append(x2)
        xcur = x2

    f_last = saved[-1][8]
    loss_part, dx, dbr, d_final, dg = _final_loss(xcur, target, final_norm[None], f_last, mods[depth - 1, 5])
    loss = lax.psum(loss_part[0, 0], ("x", "y", "c"))

    dmod = [[None] * 6 for _ in range(depth)]
    d_norm_mix, d_norm_ffn = [None] * depth, [None] * depth
    d_sgu_w, d_sgu_b, d_conv_w = [None] * n_even, [None] * n_even, [None] * n_odd

    big = {"in": ab_w_in, "out": ab_w_out, "cin": conv_w_in, "cout": conv_w_out,
           "gate": ffn_w_gate, "up": ffn_w_up, "down": ffn_w_down}
    lands = {k: lax.empty((N_DEV,) + w.shape, BF16) for k, w in big.items()}
    own = {k: [None] * w.shape[0] for k, w in big.items()}
    pending = {"ffn": None, "mix": None}

    def exchange_finish(tag, after):
        (send_sems, recv_sems, srcs, lds, _), keys, li, layer = pending[tag]
        lds = _push_wait(f"exchange_wait_{tag}_{layer}", send_sems, recv_sems, srcs, lds, after,
                         lambda ref, idx: ref.at[idx], lambda ref, slot: ref.at[slot, li])
        for k, ld in zip(keys, lds):
            lands[k] = ld
        pending[tag] = None

    def exchange_start(tag, layer, keys, li, grads):
        if pending[tag] is not None:
            exchange_finish(tag, grads[0])
        for k, g in zip(keys, grads):
            own[k][li] = lax.dynamic_index_in_dim(g, me, 0, keepdims=False)
        started = _push_start(f"exchange_start_{tag}_{layer}", grads, [lands[k] for k in keys],
                              lambda ref, idx: ref.at[idx], lambda ref, slot: ref.at[slot, li])
        pending[tag] = (started, keys, li, layer)
        return started[4][0, 0]

    def row_shards(g):
        return g.reshape(N_DEV, g.shape[0] // N_DEV, g.shape[1])

    for l in reversed(range(depth)):
        sh_m, sc_m, g_m, sh_f, sc_f, g_f = [mods[l, j] for j in range(6)]
        h, (z, a, lse), mix, x1, h2, gt, up, act, f = saved[l]
        i = l // 2
        dmod[l][5] = dg
        gw_down = _matmul("ffn_down_wgrad", "tn", act, dbr, [BF16])
        dgt, dup = _matmul("ffn_down_dgrad", "nt", dbr, w_down[l], [BF16, BF16], extras=[(gt, "mn"), (up, "mn")],
                           epilogue=_swiglu_bwd_epilogue)
        gw_gate = _matmul("ffn_in_wgrad", "tn", h2, dgt, [BF16])
        gw_up = _matmul("ffn_in_wgrad", "tn", h2, dup, [BF16])
        tok = exchange_start("ffn", l, ("gate", "up", "down"), l, [_plain_to_cols(gw_gate), _plain_to_cols(gw_up), row_shards(gw_down)])
        dh2 = _matmul("ffn_in_dgrad", "nt", [dgt, dup], [w_gate[l], w_up[l]], [BF16], after=tok)
        dx, dbr, dmod[l][3], dmod[l][4], d_norm_ffn[l], dg = _norm_mod_bwd(x1, dh2, norm_ffn[l][None], sc_f, dx, mix, g_m)
        dmod[l][2] = dg
        gw_out = _matmul("mix_out_wgrad", "tn", a, dbr, [BF16])
        da = _matmul("mix_out_dgrad", "nt", dbr, w_out[l], [BF16])
        if l % 2 == 0:
            dq, dk, dv = _attn_bwd(z, ct, st, da, a, lse, nh)
            du, dvg, d_sgu_w[i], dbb = _sgu_bwd(z, sgu_w[i], b_col[i], da, ng, u_blk, aw // gw)
            d_sgu_b[i] = dbb[:, :, 0]
            dz = jnp.concatenate([dq, dk, dv, du, dvg], axis=1)
            gw_in = _matmul("mix_in_wgrad", "tn", h, dz, [BF16])
            dh = _matmul("mix_in_dgrad", "nt", dz, w_in[l], [BF16])
        else:
            dz, d_conv_w[i] = _conv_bwd(z, conv_w_full[i], da)
            gw_in = _matmul("conv_in_wgrad", "tn", h, dz, [BF16])
            dh = _matmul("conv_in_dgrad", "nt", dz, w_in[l], [BF16])
        mix_group = ("mix", l, ("in", "out") if l % 2 == 0 else ("cin", "cout"), i, [_plain_to_cols(gw_in), row_shards(gw_out)])
        w_norm = norm_mix[l][None]
        if l > 0:
            w_norm = w_norm + exchange_start(*mix_group)
            f_prev, g_prev = saved[l - 1][8], mods[l - 1, 5]
            dx, dbr, dmod[l][0], dmod[l][1], d_norm_mix[l], dg = _norm_mod_bwd(stream[l], dh, w_norm, sc_m, dx, f_prev, g_prev)
        else:
            dx, dmod[l][0], dmod[l][1], d_norm_mix[l] = _norm_mod_bwd(stream[l], dh, w_norm, sc_m, dx)
    grad_x = dx[None]

    dmod_mine = jnp.stack([jnp.concatenate([v.reshape(d) for v in dmod[l]]) for l in range(depth)])
    small = [dmod_mine, jnp.concatenate(d_norm_mix), jnp.concatenate(d_norm_ffn), jnp.stack(d_sgu_w), jnp.stack(d_sgu_b),
             d_final, jnp.stack(d_conv_w)]
    slab, offs, _ = _pack_rows(small, width)
    gathered, token = _all_gather_small("gather_small_grads", slab)
    p_dmod, p_nmix, p_nffn, p_sguw, p_sgub, p_final, p_convw = _unpack_rows(gathered.reshape(N_DEV, -1, width), offs, width)
    mix_group[4][0] = mix_group[4][0] + token[0, 0].astype(BF16)
    p_dmod = p_dmod + exchange_start(*mix_group)

    outs = {}

    def update(name, pieces, w, m, v):
        shape = w.shape
        cdim = shape[-1]
        res = _adamw("adamw_" + name, pieces.reshape(pieces.shape[0], -1, cdim), w.reshape(-1, cdim),
                     m.reshape(-1, cdim), v.reshape(-1, cdim))
        outs[name] = [r.reshape(shape) for r in res]

    update("ada_b", p_dmod.reshape(N_DEV, depth, 6 * d), ada_b, m_ada_b, v_ada_b)
    update("norm_mix", p_nmix.reshape(N_DEV, depth, d), norm_mix, m_norm_mix, v_norm_mix)
    update("norm_ffn", p_nffn.reshape(N_DEV, depth, d), norm_ffn, m_norm_ffn, v_norm_ffn)
    update("sgu_w", p_sguw, sgu_w, m_sgu_w, v_sgu_w)
    update("sgu_b", p_sgub.reshape(N_DEV, 1, -1), sgu_b.reshape(1, -1), m_sgu_b.reshape(1, -1), v_sgu_b.reshape(1, -1))
    outs["sgu_b"] = [r.reshape(sgu_b.shape) for r in outs["sgu_b"]]
    update("final_norm", p_final.reshape(N_DEV, 1, d), final_norm[None], m_final_norm[None], v_final_norm[None])
    outs["final_norm"] = [r.reshape(final_norm.shape) for r in outs["final_norm"]]
    cw_mine = lax.dynamic_slice_in_dim(p_convw.reshape(N_DEV, n_odd, cwid, d), me * d8, d8, axis=3)
    update("conv_w", cw_mine, conv_w, m_conv_w, v_conv_w)

    dmod_cols = lax.dynamic_slice_in_dim(p_dmod.reshape(N_DEV, depth, 6 * d), me * n8, n8, axis=2)
    g_ada = _ada_wgrad(c_all, jnp.transpose(dmod_cols, (1, 0, 2)))
    update("ada_w", g_ada[None], ada_w, m_ada_w, v_ada_w)

    names = {"in": "ab_w_in", "out": "ab_w_out", "cin": "conv_w_in", "cout": "conv_w_out",
             "gate": "ffn_w_gate", "up": "ffn_w_up", "down": "ffn_w_down"}
    moments = {"in": (m_ab_w_in, v_ab_w_in), "out": (m_ab_w_out, v_ab_w_out), "cin": (m_conv_w_in, v_conv_w_in),
               "cout": (m_conv_w_out, v_conv_w_out), "gate": (m_ffn_w_gate, v_ffn_w_gate), "up": (m_ffn_w_up, v_ffn_w_up),
               "down": (m_ffn_w_down, v_ffn_w_down)}

    def update_big(k):
        pieces = lax.dynamic_update_slice(lands[k], jnp.stack(own[k])[None], (me, 0, 0, 0))
        update(names[k], pieces, big[k], *moments[k])

    exchange_finish("ffn", g_ada)
    for k in ("gate", "up", "down", "cin", "cout"):
        update_big(k)
    done = sum(outs[n][0][(0,) * outs[n][0].ndim] for n in outs)
    exchange_finish("mix", done.reshape(1, 1))
    for k in ("in", "out"):
        update_big(k)

    order = ["ada_w", "ada_b", "norm_mix", "norm_ffn", "ab_w_in", "sgu_w", "sgu_b", "ab_w_out", "conv_w_in", "conv_w",
             "conv_w_out", "ffn_w_gate", "ffn_w_up", "ffn_w_down", "final_norm"]
    return (loss, grad_x, *[outs[n][0] for n in order], *[outs[n][1] for n in order],
            *[outs[n][2] for n in order], *[outs[n][3] for n in order])
```

```python
import functools
import math

import numpy as np
import jax
import jax.numpy as jnp
from jax import lax
from jax.experimental import pallas as pl
from jax.experimental.pallas import tpu as pltpu

F32, BF16 = jnp.float32, jnp.bfloat16
MESH = pl.DeviceIdType.MESH
N_DEV = 8
EPS = 1e-6
HEAD = 128
CHUNK = 128
DILATIONS = (1, 4, 16)
ATT_CHUNK = CHUNK * DILATIONS[-1]
ATT_BATCH = 8
ROPE_THETA = 500000.0
ROPE_DIM = HEAD // 4
NEG = -1e30
ADAM_LR, ADAM_B1, ADAM_B2, ADAM_EPS, ADAM_WD, ADAM_STEP = 0.001, 0.9, 0.999, 1e-08, 0.01, 10
VMEM_LIMIT = 56 * 1024 * 1024


def _cparams(*sem):
    return pltpu.CompilerParams(dimension_semantics=sem or None, vmem_limit_bytes=VMEM_LIMIT)


def _tile(n, pref, unit):
    t = (min(pref, n) // unit) * unit
    while t >= unit:
        if n % t == 0:
            return t
        t -= unit
    return n


def _dot(a, b, dims):
    return lax.dot_general(a, b, (dims, ((), ())), preferred_element_type=F32)


def _dot_nn(a, b):
    return _dot(a, b, ((1,), (0,)))


def _dot_nt(a, b):
    return _dot(a, b, ((1,), (1,)))


def _dot_tn(a, b):
    return _dot(a, b, ((0,), (0,)))


FULL_ROW = 2048
_TILE_DEFAULT = dict(tm=1024, tn=512, tk=2816)
_TILES = {
    "ffn_down": dict(tn=1024),
    "ffn_down_wgrad": dict(tm=1408, tn=1024),
    "ffn_in_wgrad": dict(tn=1408), "mix_in_wgrad": dict(tn=1408), "conv_in_wgrad": dict(tn=1536),
    "mix_in_dgrad": dict(tn=1024, tk=1408), "conv_in_dgrad": dict(tn=1024, tk=1536),
    "mix_out": dict(tm=512, tn=FULL_ROW),
    "mix_out_wgrad": dict(tn=1024), "mix_out_dgrad": dict(tn=1024),
}


def _matmul(name, mode, a, b, outs, *, extras=(), epilogue=None, after=None):
    a_list = list(a) if isinstance(a, (list, tuple)) else [a]
    b_list = list(b) if isinstance(b, (list, tuple)) else [b]
    na, nb = len(a_list), len(b_list)
    paired = na > 1
    assert na == nb if paired else na == 1
    nacc = 1 if paired else nb
    a0, b0 = a_list[0], b_list[0]
    if mode == "nn":
        (m, kk), (_, n) = a0.shape, b0.shape
    elif mode == "nt":
        (m, kk), (n, _) = a0.shape, b0.shape
    else:
        (kk, m), (_, n) = a0.shape, b0.shape
    if after is not None:
        inner = epilogue or (lambda *accs: accs)
        extras = list(extras) + [(jnp.zeros((1, n), F32) + after, "n")]
        epilogue = lambda *tiles: inner(*tiles[:-1])
    pref = {**_TILE_DEFAULT, **_TILES.get(name, {})}
    tm, tn, tk = _tile(m, pref["tm"], 128), _tile(n, pref["tn"], 128), _tile(kk, pref["tk"], 128)
    nk = kk // tk
    dotf = {"nn": _dot_nn, "nt": _dot_nt, "tn": _dot_tn}[mode]
    a_spec = pl.BlockSpec((tk, tm), lambda i, j, k: (k, i)) if mode == "tn" else pl.BlockSpec((tm, tk), lambda i, j, k: (i, k))
    b_spec = pl.BlockSpec((tn, tk), lambda i, j, k: (j, k)) if mode == "nt" else pl.BlockSpec((tk, tn), lambda i, j, k: (k, j))
    e_specs = [pl.BlockSpec((tm, tn), lambda i, j, k: (i, j)) if kind == "mn" else pl.BlockSpec((1, tn), lambda i, j, k: (0, j))
               for _, kind in extras]
    ne, no = len(extras), len(outs)
    epi = epilogue or (lambda *accs: accs)

    def body(*refs):
        a_refs, b_refs, rest = refs[:na], refs[na:na + nb], refs[na + nb:]
        e_refs, o_refs, acc_refs = rest[:ne], rest[ne:ne + no], rest[ne + no:]

        def products():
            if paired:
                p = dotf(a_refs[0][...], b_refs[0][...])
                for a_ref, b_ref in zip(a_refs[1:], b_refs[1:]):
                    p = p + dotf(a_ref[...], b_ref[...])
                return [p]
            av = a_refs[0][...]
            return [dotf(av, b_ref[...]) for b_ref in b_refs]

        def finish(accs):
            for o_ref, o in zip(o_refs, epi(*accs, *[r[...] for r in e_refs])):
                o_ref[...] = o.astype(o_ref.dtype)

        if nk == 1:
            finish(products())
            return
        k = pl.program_id(2)

        @pl.when(k == 0)
        def _():
            for acc_ref in acc_refs:
                acc_ref[...] = jnp.zeros_like(acc_ref)

        for acc_ref, p in zip(acc_refs, products()):
            acc_ref[...] += p

        @pl.when(k == nk - 1)
        def _():
            finish([acc_ref[...] for acc_ref in acc_refs])

    res = pl.pallas_call(
        body, name=name, grid=(m // tm, n // tn, nk),
        in_specs=[a_spec] * na + [b_spec] * nb + e_specs,
        out_specs=[pl.BlockSpec((tm, tn), lambda i, j, k: (i, j)) for _ in outs],
        out_shape=[jax.ShapeDtypeStruct((m, n), dt) for dt in outs],
        scratch_shapes=[pltpu.VMEM((tm, tn), F32)] * nacc if nk > 1 else [],
        compiler_params=_cparams("parallel", "parallel", "arbitrary"),
    )(*a_list, *b_list, *[e for e, _ in extras])
    return res[0] if no == 1 else res


def _norm_mod_fwd(x, w, sc, sh):
    s, d = x.shape
    tm = _tile(s, 512, 8)

    def body(x_ref, w_ref, sc_ref, sh_ref, h_ref):
        xv = x_ref[...]
        r = lax.rsqrt(jnp.mean(xv * xv, axis=-1, keepdims=True) + EPS)
        h_ref[...] = ((xv * r) * w_ref[...] * (1.0 + sc_ref[...]) + sh_ref[...]).astype(BF16)

    row = pl.BlockSpec((1, d), lambda i: (0, 0))
    return pl.pallas_call(
        body, name="norm_mod_fwd", grid=(s // tm,),
        in_specs=[pl.BlockSpec((tm, d), lambda i: (i, 0)), row, row, row],
        out_specs=pl.BlockSpec((tm, d), lambda i: (i, 0)),
        out_shape=jax.ShapeDtypeStruct((s, d), BF16),
        compiler_params=_cparams("parallel"),
    )(x, w, sc, sh)


def _gated_add_norm_epilogue(acc, res, g, w, sc, sh):
    xv = res + g * acc
    r = lax.rsqrt(jnp.mean(xv * xv, axis=-1, keepdims=True) + EPS)
    return xv, acc, (xv * r) * w * (1.0 + sc) + sh


def _colsum8(t):
    tm, d = t.shape
    return jnp.sum(t.reshape(tm // 8, 8, d), axis=0)


def _norm_mod_bwd(x, dh, w, sc, dres, branch=None, g=None):
    s, d = x.shape
    tm = _tile(s, 256, 8)
    nsteps = s // tm
    gated = branch is not None

    def body(*refs):
        if gated:
            x_ref, dh_ref, w_ref, sc_ref, dres_ref, br_ref, g_ref, dx_ref, dbr_ref, dsh_ref, dsc_ref, dw_ref, dg_ref, acc = refs
        else:
            x_ref, dh_ref, w_ref, sc_ref, dres_ref, dx_ref, dsh_ref, dsc_ref, dw_ref, acc = refs
        i = pl.program_id(0)

        @pl.when(i == 0)
        def _():
            acc[...] = jnp.zeros_like(acc)

        xv, dhv, wv, scv = x_ref[...], dh_ref[...].astype(F32), w_ref[...], sc_ref[...]
        r = lax.rsqrt(jnp.mean(xv * xv, axis=-1, keepdims=True) + EPS)
        xn = xv * r
        dxn = dhv * (wv * (1.0 + scv))
        dx = dres_ref[...] + r * (dxn - xn * jnp.mean(dxn * xn, axis=-1, keepdims=True))
        dx_ref[...] = dx
        acc[0] += _colsum8(dhv)
        acc[1] += _colsum8(dhv * xn)
        if gated:
            dbr_ref[...] = (dx * g_ref[...]).astype(BF16)
            acc[2] += _colsum8(dx * br_ref[...].astype(F32))

        @pl.when(i == nsteps - 1)
        def _():
            a0 = jnp.sum(acc[0], axis=0, keepdims=True)
            a1 = jnp.sum(acc[1], axis=0, keepdims=True)
            dsh_ref[...] = a0
            dsc_ref[...] = a1 * wv
            dw_ref[...] = a1 * (1.0 + scv)
            if gated:
                dg_ref[...] = jnp.sum(acc[2], axis=0, keepdims=True)

    big = pl.BlockSpec((tm, d), lambda i: (i, 0))
    row = pl.BlockSpec((1, d), lambda i: (0, 0))
    rowo = jax.ShapeDtypeStruct((1, d), F32)
    in_specs = [big, big, row, row, big] + ([big, row] if gated else [])
    out_specs = [big] + ([big] if gated else []) + [row, row, row] + ([row] if gated else [])
    out_shape = ([jax.ShapeDtypeStruct((s, d), F32)] + ([jax.ShapeDtypeStruct((s, d), BF16)] if gated else [])
                 + [rowo, rowo, rowo] + ([rowo] if gated else []))
    args = [x, dh, w, sc, dres] + ([branch, g] if gated else [])
    return pl.pallas_call(
        body, name="norm_mod_bwd_gated" if gated else "norm_mod_bwd", grid=(nsteps,),
        in_specs=in_specs, out_specs=out_specs, out_shape=out_shape,
        scratch_shapes=[pltpu.VMEM((3, 8, d), F32)],
        compiler_params=_cparams("arbitrary"),
    )(*args)


def _final_loss(x, target, w, branch, g):
    s, d = x.shape
    tm = _tile(s, 256, 8)
    nsteps = s // tm

    def body(x_ref, t_ref, w_ref, br_ref, g_ref, loss_ref, dx_ref, dbr_ref, dw_ref, dg_ref, acc):
        i = pl.program_id(0)

        @pl.when(i == 0)
        def _():
            acc[...] = jnp.zeros_like(acc)

        xv, wv = x_ref[...], w_ref[...]
        r = lax.rsqrt(jnp.mean(xv * xv, axis=-1, keepdims=True) + EPS)
        xn = xv * r
        err = xn * wv - t_ref[...]
        dy = err * (1.0 / d)
        dxn = dy * wv
        dx = r * (dxn - xn * jnp.mean(dxn * xn, axis=-1, keepdims=True))
        dx_ref[...] = dx
        dbr_ref[...] = (dx * g_ref[...]).astype(BF16)
        acc[0] += _colsum8(err * err)
        acc[1] += _colsum8(dy * xn)
        acc[2] += _colsum8(dx * br_ref[...].astype(F32))

        @pl.when(i == nsteps - 1)
        def _():
            loss_ref[...] = jnp.sum(jnp.sum(acc[0], axis=0, keepdims=True), axis=1, keepdims=True) * (0.5 / d)
            dw_ref[...] = jnp.sum(acc[1], axis=0, keepdims=True)
            dg_ref[...] = jnp.sum(acc[2], axis=0, keepdims=True)

    big = pl.BlockSpec((tm, d), lambda i: (i, 0))
    row = pl.BlockSpec((1, d), lambda i: (0, 0))
    rowo = jax.ShapeDtypeStruct((1, d), F32)
    return pl.pallas_call(
        body, name="final_loss", grid=(nsteps,),
        in_specs=[big, big, row, big, row],
        out_specs=[pl.BlockSpec((1, 1), lambda i: (0, 0)), big, big, row, row],
        out_shape=[jax.ShapeDtypeStruct((1, 1), F32), jax.ShapeDtypeStruct((s, d), F32),
                   jax.ShapeDtypeStruct((s, d), BF16), rowo, rowo],
        scratch_shapes=[pltpu.VMEM((3, 8, d), F32)],
        compiler_params=_cparams("arbitrary"),
    )(x, target, w, branch, g)


def _rope_tables(pos_col):
    s = pos_col.shape[0]
    tq = _tile(s, 1024, 8)
    half = ROPE_DIM // 2
    inv = np.float32(ROPE_THETA) ** (-np.arange(0, ROPE_DIM, 2, dtype=np.float32) / np.float32(ROPE_DIM))
    inv_row = jnp.asarray(np.tile(inv.astype(np.float32), HEAD // half)[None, :])

    def body(p_ref, inv_ref, ct_ref, st_ref):
        lane = lax.broadcasted_iota(jnp.int32, (tq, HEAD), 1)
        ang = p_ref[...].astype(F32) * inv_ref[...]
        cs, sn = jnp.cos(ang), jnp.sin(ang)
        ct_ref[...] = jnp.where(lane < ROPE_DIM, cs, 1.0)
        st_ref[...] = jnp.where(lane < half, -sn, jnp.where(lane < ROPE_DIM, sn, 0.0))

    blk = pl.BlockSpec((tq, HEAD), lambda i: (i, 0))
    return pl.pallas_call(
        body, name="rope_tables", grid=(s // tq,),
        in_specs=[pl.BlockSpec((tq, 1), lambda i: (i, 0)), pl.BlockSpec((1, HEAD), lambda i: (0, 0))],
        out_specs=[blk, blk],
        out_shape=[jax.ShapeDtypeStruct((s, HEAD), F32)] * 2,
        compiler_params=_cparams("parallel"),
    )(pos_col, inv_row)


def _swap_halves(x):
    lane = lax.broadcasted_iota(jnp.int32, x.shape, 1)
    half = ROPE_DIM // 2
    return jnp.where(lane < half, pltpu.roll(x, HEAD - half, 1), pltpu.roll(x, half, 1))


def _rope(x, ct, st):
    return x * ct + _swap_halves(x) * st


def _rope_t(dy, ct, st):
    return dy * ct - _swap_halves(dy) * st


def _band_bias(bias_ref):
    qi = lax.broadcasted_iota(jnp.int32, (CHUNK, 2 * CHUNK), 0)
    kj = lax.broadcasted_iota(jnp.int32, (CHUNK, 2 * CHUNK), 1)
    band = (kj >= qi) & (kj <= qi + CHUNK)
    bias_ref[0] = jnp.where(band, 0.0, NEG)
    bias_ref[1] = jnp.where(band & (kj >= CHUNK), 0.0, NEG)


def _rows(start, size, d):
    return pl.ds(start, size) if d == 1 else pl.ds(start, size, stride=d)


def _batch_units(ub, d, c):
    n_sb = ATT_BATCH // d
    out = []
    for sb, r in [(ub * n_sb + t, r) for t in range(n_sb) for r in range(d)]:
        base = pl.multiple_of(sb * (CHUNK * d), CHUNK)
        first = jnp.where((c == 0) & (sb == 0), 1, 0)
        out.append((_rows(base + r, CHUNK, d), _rows(base + (ATT_CHUNK - CHUNK * d + r), 2 * CHUNK, d), first))
    return out


def _for_batches(d, fn):
    def step(ub, carry):
        fn(ub)
        return carry
    lax.fori_loop(0, ATT_CHUNK // CHUNK // ATT_BATCH, step, 0)


def _bdot(a, b, ca, cb):
    return lax.dot_general(a, b, (((ca,), (cb,)), ((0,), (0,))), preferred_element_type=F32)


WIDE = DILATIONS[-1]
assert ATT_CHUNK == CHUNK * WIDE


def _deint(x):
    return jnp.swapaxes(x.reshape(CHUNK, WIDE, HEAD), 0, 1).reshape(ATT_CHUNK, HEAD)


def _reint(x):
    return jnp.swapaxes(x.reshape(WIDE, CHUNK, HEAD), 0, 1).reshape(ATT_CHUNK, HEAD)


def _blk(r):
    return pl.ds(r * CHUNK, CHUNK)


def _key_rows(ref, r):
    return jnp.concatenate([ref[_blk(r), :], ref[pl.ds(ATT_CHUNK + r * CHUNK, CHUNK), :]], axis=0)


def _attn_specs(nh, nc):
    cur = lambda off: pl.BlockSpec((ATT_CHUNK, HEAD), lambda h, c: (jnp.minimum(c, nc - 1), off + h))
    prev = lambda off: pl.BlockSpec((ATT_CHUNK, HEAD), lambda h, c: (jnp.maximum(c - 1, 0), off + h))
    tcur = pl.BlockSpec((ATT_CHUNK, HEAD), lambda h, c: (jnp.minimum(c, nc - 1), 0))
    tprev = pl.BlockSpec((ATT_CHUNK, HEAD), lambda h, c: (jnp.maximum(c - 1, 0), 0))
    return [cur(0), prev(nh), cur(nh), prev(2 * nh), cur(2 * nh), tcur, tcur, tprev, tprev]


def _attn_fwd(z, ct, st, nh):
    s = z.shape[0]
    nc = s // ATT_CHUNK
    scale = HEAD ** -0.5

    def body(q_ref, kp_ref, kc_ref, vp_ref, vc_ref, ctc, stc, ctp, stp, o_ref, lse_ref, qf, kf, vf, ob, lb, bias, q16, k16, v16):
        c = pl.program_id(1)
        qf[...] = _rope(q_ref[...].astype(F32), ctc[...], stc[...])
        kf[0:ATT_CHUNK] = _rope(kp_ref[...].astype(F32), ctp[...], stp[...])
        kf[ATT_CHUNK:] = _rope(kc_ref[...].astype(F32), ctc[...], stc[...])
        vf[0:ATT_CHUNK] = vp_ref[...].astype(F32)
        vf[ATT_CHUNK:] = vc_ref[...].astype(F32)
        _band_bias(bias)

        def softmax_units(q_b, k_b, v_b, bias_b):
            sc = _bdot(q_b, k_b, 2, 2) * scale + bias_b
            m = jnp.max(sc, axis=2, keepdims=True)
            p = jnp.exp(sc - m)
            l = jnp.sum(p, axis=2, keepdims=True)
            return _bdot(p.astype(BF16), v_b, 2, 1) / l, m + jnp.log(l)

        for b, d in enumerate(DILATIONS[:-1]):
            def batch(ub, b=b, d=d):
                units = _batch_units(ub, d, c)
                q_b = jnp.stack([qf[qr, :] for qr, _, _ in units]).astype(BF16)
                k_b = jnp.stack([kf[kr, :] for _, kr, _ in units]).astype(BF16)
                v_b = jnp.stack([vf[kr, :] for _, kr, _ in units]).astype(BF16)
                o, lse = softmax_units(q_b, k_b, v_b, jnp.stack([bias[first] for _, _, first in units]))
                for j, (qr, _, _) in enumerate(units):
                    ob[b, qr, :] = o[j]
                    lb[b, qr, :] = jnp.broadcast_to(lse[j], (CHUNK, HEAD))
            _for_batches(d, batch)

        b = len(DILATIONS) - 1
        q16[...] = _deint(qf[...]).astype(BF16)
        for half in range(2):
            rows = pl.ds(half * ATT_CHUNK, ATT_CHUNK)
            k16[rows, :] = _deint(kf[rows, :]).astype(BF16)
            v16[rows, :] = _deint(vf[rows, :]).astype(BF16)
        bias_w = bias[jnp.where(c == 0, 1, 0)][None]
        for ub in range(WIDE // ATT_BATCH):
            rs = [ub * ATT_BATCH + j for j in range(ATT_BATCH)]
            o, lse = softmax_units(jnp.stack([q16[_blk(r), :] for r in rs]), jnp.stack([_key_rows(k16, r) for r in rs]),
                                   jnp.stack([_key_rows(v16, r) for r in rs]), bias_w)
            for j, r in enumerate(rs):
                qf[_blk(r), :] = o[j]
                kf[_blk(r), :] = jnp.broadcast_to(lse[j], (CHUNK, HEAD))
        ob[b] = _reint(qf[...])
        lb[b] = _reint(kf[0:ATT_CHUNK])

        mx = jnp.maximum(jnp.maximum(lb[0], lb[1]), lb[2])
        e0, e1, e2 = jnp.exp(lb[0] - mx), jnp.exp(lb[1] - mx), jnp.exp(lb[2] - mx)
        den = e0 + e1 + e2
        o_ref[...] = ((e0 * ob[0] + e1 * ob[1] + e2 * ob[2]) / den).astype(BF16)
        lse_ref[...] = mx + jnp.log(den)

    blk = pl.BlockSpec((ATT_CHUNK, HEAD), lambda h, c: (c, h))
    return pl.pallas_call(
        body, name="attn_fwd", grid=(nh, nc),
        in_specs=_attn_specs(nh, nc), out_specs=[blk, blk],
        out_shape=[jax.ShapeDtypeStruct((s, nh * HEAD), BF16), jax.ShapeDtypeStruct((s, nh * HEAD), F32)],
        scratch_shapes=[pltpu.VMEM((ATT_CHUNK, HEAD), F32), pltpu.VMEM((2 * ATT_CHUNK, HEAD), F32),
                        pltpu.VMEM((2 * ATT_CHUNK, HEAD), F32), pltpu.VMEM((3, ATT_CHUNK, HEAD), F32),
                        pltpu.VMEM((3, ATT_CHUNK, HEAD), F32), pltpu.VMEM((2, CHUNK, 2 * CHUNK), F32),
                        pltpu.VMEM((ATT_CHUNK, HEAD), BF16), pltpu.VMEM((2 * ATT_CHUNK, HEAD), BF16),
                        pltpu.VMEM((2 * ATT_CHUNK, HEAD), BF16)],
        compiler_params=_cparams("parallel", "arbitrary"),
    )(z, z, z, z, z, ct, st, ct, st)


def _attn_bwd(z, ct, st, da, o, lse, nh):
    s = z.shape[0]
    nc = s // ATT_CHUNK
    scale = HEAD ** -0.5

    def body(q_ref, kp_ref, kc_ref, vp_ref, vc_ref, ctc, stc, ctp, stp, do_ref, o_ref, lse_ref,
             dq_ref, dk_ref, dv_ref, qf, kf, vf, dof, dbar, dqa, dkf, dvf, bias, q16, k16, v16, do16):
        c = pl.program_id(1)

        @pl.when(c == 0)
        def _():
            dkf[...] = jnp.zeros_like(dkf)
            dvf[...] = jnp.zeros_like(dvf)

        @pl.when(c > 0)
        def _():
            dkf[0:ATT_CHUNK] = dkf[ATT_CHUNK:]
            dvf[0:ATT_CHUNK] = dvf[ATT_CHUNK:]
            dkf[ATT_CHUNK:] = jnp.zeros((ATT_CHUNK, HEAD), F32)
            dvf[ATT_CHUNK:] = jnp.zeros((ATT_CHUNK, HEAD), F32)

        @pl.when(c < nc)
        def _():
            qf[...] = _rope(q_ref[...].astype(F32), ctc[...], stc[...])
            kf[0:ATT_CHUNK] = _rope(kp_ref[...].astype(F32), ctp[...], stp[...])
            kf[ATT_CHUNK:] = _rope(kc_ref[...].astype(F32), ctc[...], stc[...])
            vf[0:ATT_CHUNK] = vp_ref[...].astype(F32)
            vf[ATT_CHUNK:] = vc_ref[...].astype(F32)
            dov = do_ref[...].astype(F32)
            dof[...] = dov
            dbar[...] = jnp.broadcast_to(jnp.sum(dov * o_ref[...].astype(F32), axis=1, keepdims=True), (ATT_CHUNK, HEAD))
            dqa[...] = jnp.zeros_like(dqa)
            _band_bias(bias)

            def grads(q_b, k_b, v_b, do_b, lse_b, dbar_b, bias_b):
                sc = _bdot(q_b, k_b, 2, 2) * scale + bias_b
                p = jnp.exp(sc - jnp.concatenate([lse_b, lse_b], axis=2))
                dp = _bdot(do_b, v_b, 2, 2)
                ds = (p * (dp - jnp.concatenate([dbar_b, dbar_b], axis=2)) * scale).astype(BF16)
                return _bdot(ds, k_b, 2, 1), _bdot(ds, q_b, 1, 1), _bdot(p.astype(BF16), do_b, 1, 1)

            for d in DILATIONS[:-1]:
                def batch(ub, d=d):
                    units = _batch_units(ub, d, c)
                    dq, dk, dv = grads(jnp.stack([qf[qr, :] for qr, _, _ in units]).astype(BF16),
                                       jnp.stack([kf[kr, :] for _, kr, _ in units]).astype(BF16),
                                       jnp.stack([vf[kr, :] for _, kr, _ in units]).astype(BF16),
                                       jnp.stack([dof[qr, :] for qr, _, _ in units]).astype(BF16),
                                       jnp.stack([lse_ref[qr, :] for qr, _, _ in units]),
                                       jnp.stack([dbar[qr, :] for qr, _, _ in units]),
                                       jnp.stack([bias[first] for _, _, first in units]))
                    for j, (qr, kr, _) in enumerate(units):
                        dqa[qr, :] += dq[j]
                        dkf[kr, :] += dk[j]
                        dvf[kr, :] += dv[j]
                _for_batches(d, batch)

            q16[...] = _deint(qf[...]).astype(BF16)
            do16[...] = _deint(dof[...]).astype(BF16)
            for half in range(2):
                rows = pl.ds(half * ATT_CHUNK, ATT_CHUNK)
                k16[rows, :] = _deint(kf[rows, :]).astype(BF16)
                v16[rows, :] = _deint(vf[rows, :]).astype(BF16)
            dof[...] = _deint(lse_ref[...])
            dbar[...] = _deint(dbar[...])
            bias_w = bias[jnp.where(c == 0, 1, 0)][None]
            for ub in range(WIDE // ATT_BATCH):
                rs = [ub * ATT_BATCH + j for j in range(ATT_BATCH)]
                dq, dk, dv = grads(jnp.stack([q16[_blk(r), :] for r in rs]), jnp.stack([_key_rows(k16, r) for r in rs]),
                                   jnp.stack([_key_rows(v16, r) for r in rs]), jnp.stack([do16[_blk(r), :] for r in rs]),
                                   jnp.stack([dof[_blk(r), :] for r in rs]), jnp.stack([dbar[_blk(r), :] for r in rs]), bias_w)
                for j, r in enumerate(rs):
                    qf[_blk(r), :] = dq[j]
                    kf[_blk(r), :] = dk[j][0:CHUNK]
                    kf[pl.ds(ATT_CHUNK + r * CHUNK, CHUNK), :] = dk[j][CHUNK:]
                    vf[_blk(r), :] = dv[j][0:CHUNK]
                    vf[pl.ds(ATT_CHUNK + r * CHUNK, CHUNK), :] = dv[j][CHUNK:]
            dqa[...] += _reint(qf[...])
            for half in range(2):
                rows = pl.ds(half * ATT_CHUNK, ATT_CHUNK)
                dkf[rows, :] += _reint(kf[rows, :])
                dvf[rows, :] += _reint(vf[rows, :])
            dq_ref[...] = _rope_t(dqa[...], ctc[...], stc[...]).astype(BF16)

        @pl.when(c > 0)
        def _():
            dk_ref[...] = _rope_t(dkf[0:ATT_CHUNK], ctp[...], stp[...]).astype(BF16)
            dv_ref[...] = dvf[0:ATT_CHUNK].astype(BF16)

    cur = pl.BlockSpec((ATT_CHUNK, HEAD), lambda h, c: (jnp.minimum(c, nc - 1), h))
    late = pl.BlockSpec((ATT_CHUNK, HEAD), lambda h, c: (jnp.maximum(c - 1, 0), h))
    shp = jax.ShapeDtypeStruct((s, nh * HEAD), BF16)
    big = pltpu.VMEM((2 * ATT_CHUNK, HEAD), F32)
    one = pltpu.VMEM((ATT_CHUNK, HEAD), F32)
    one16, big16 = pltpu.VMEM((ATT_CHUNK, HEAD), BF16), pltpu.VMEM((2 * ATT_CHUNK, HEAD), BF16)
    return pl.pallas_call(
        body, name="attn_bwd", grid=(nh, nc + 1),
        in_specs=_attn_specs(nh, nc) + [cur, cur, cur], out_specs=[cur, late, late],
        out_shape=[shp, shp, shp],
        scratch_shapes=[one, big, big, one, one, one, big, big, pltpu.VMEM((2, CHUNK, 2 * CHUNK), F32),
                        one16, big16, big16, one16],
        compiler_params=_cparams("parallel", "arbitrary"),
    )(z, z, z, z, z, ct, st, ct, st, da, o, lse)


_GELU_K = math.sqrt(2.0 / math.pi)


def _gelu(x):
    return 0.5 * x * (1.0 + jnp.tanh(_GELU_K * (x + 0.044715 * x * x * x)))


def _gelu_and_grad(x):
    t = jnp.tanh(_GELU_K * (x + 0.044715 * x * x * x))
    g = 0.5 * x * (1.0 + t)
    dg = 0.5 * (1.0 + t) + 0.5 * x * (1.0 - t * t) * (_GELU_K * (1.0 + 3 * 0.044715 * x * x))
    return g, dg


def _tril(w):
    ti = lax.broadcasted_iota(jnp.int32, (CHUNK, CHUNK), 0)
    si = lax.broadcasted_iota(jnp.int32, (CHUNK, CHUNK), 1)
    return jnp.where(si <= ti, w, 0.0)


def _sgu_fwd(z, w_s, b_col, ng, u_blk):
    s = z.shape[0]
    gw = ng * HEAD
    tq = _tile(s, 1024, CHUNK)

    def body(u_ref, v_ref, w_ref, b_ref, o_ref):
        for g in range(ng):
            wg = _tril(w_ref[g]).astype(BF16)
            cols = slice(g * HEAD, (g + 1) * HEAD)
            for n in range(tq // CHUNK):
                rows = slice(n * CHUNK, (n + 1) * CHUNK)
                gv = _gelu(v_ref[rows, cols].astype(F32)).astype(BF16)
                mixed = _dot_nn(wg, gv) + b_ref[g]
                o_ref[rows, cols] = (_gelu(u_ref[rows, cols].astype(F32)) * mixed).astype(BF16)

    full = pl.BlockSpec((ng, CHUNK, CHUNK), lambda i: (0, 0, 0))
    return pl.pallas_call(
        body, name="sgu_fwd", grid=(s // tq,),
        in_specs=[pl.BlockSpec((tq, gw), lambda i: (i, u_blk)), pl.BlockSpec((tq, gw), lambda i: (i, u_blk + 1)), full, full],
        out_specs=pl.BlockSpec((tq, gw), lambda i: (i, 0)),
        out_shape=jax.ShapeDtypeStruct((s, gw), BF16),
        compiler_params=_cparams("parallel"),
    )(z, z, w_s, b_col)


def _sgu_bwd(z, w_s, b_col, da, ng, u_blk, da_blk):
    s = z.shape[0]
    gw = ng * HEAD
    tq = _tile(s, 1024, CHUNK)
    nsteps = s // tq

    def body(u_ref, v_ref, w_ref, b_ref, do_ref, du_ref, dv_ref, dw_ref, db_ref):
        i = pl.program_id(0)

        @pl.when(i == 0)
        def _():
            dw_ref[...] = jnp.zeros_like(dw_ref)
            db_ref[...] = jnp.zeros_like(db_ref)

        for g in range(ng):
            wg = _tril(w_ref[g]).astype(BF16)
            cols = slice(g * HEAD, (g + 1) * HEAD)
            dw_acc = jnp.zeros((CHUNK, CHUNK), F32)
            db_acc = jnp.zeros((CHUNK, 1), F32)
            for n in range(tq // CHUNK):
                rows = slice(n * CHUNK, (n + 1) * CHUNK)
                gu, dgu = _gelu_and_grad(u_ref[rows, cols].astype(F32))
                gv, dgv = _gelu_and_grad(v_ref[rows, cols].astype(F32))
                gvb = gv.astype(BF16)
                mixed = _dot_nn(wg, gvb) + b_ref[g]
                dout = do_ref[rows, cols].astype(F32)
                du_ref[rows, cols] = (dout * mixed * dgu).astype(BF16)
                dmix = dout * gu
                dmb = dmix.astype(BF16)
                dv_ref[rows, cols] = (_dot_tn(wg, dmb) * dgv).astype(BF16)
                dw_acc += _dot_nt(dmb, gvb)
                db_acc += jnp.sum(dmix, axis=1, keepdims=True)
            dw_ref[g] += _tril(dw_acc)
            db_ref[g] += jnp.broadcast_to(db_acc, (CHUNK, CHUNK))

    full = pl.BlockSpec((ng, CHUNK, CHUNK), lambda i: (0, 0, 0))
    out = pl.BlockSpec((tq, gw), lambda i: (i, 0))
    return pl.pallas_call(
        body, name="sgu_bwd", grid=(nsteps,),
        in_specs=[pl.BlockSpec((tq, gw), lambda i: (i, u_blk)), pl.BlockSpec((tq, gw), lambda i: (i, u_blk + 1)), full, full,
                  pl.BlockSpec((tq, gw), lambda i: (i, da_blk))],
        out_specs=[out, out, full, full],
        out_shape=[jax.ShapeDtypeStruct((s, gw), BF16)] * 2 + [jax.ShapeDtypeStruct((ng, CHUNK, CHUNK), F32)] * 2,
        compiler_params=_cparams("arbitrary"),
    )(z, z, w_s, b_col, da)


def _shift_down(y, halo, k):
    return pltpu.roll(jnp.concatenate([halo, y], axis=0), k, 0)[8:]


def _shift_up(y, halo, k):
    n = y.shape[0]
    return pltpu.roll(jnp.concatenate([y, halo], axis=0), n + 8 - k, 0)[:n]


def _conv_fwd(z, cw):
    s, d3 = z.shape
    d = d3 // 3
    tq = _tile(s, 256, 8)

    def body(z_ref, zh_ref, cw_ref, a_ref):
        i = pl.program_id(0)
        zv = z_ref[...].astype(F32)
        zh = jnp.where(i > 0, zh_ref[...].astype(F32), 0.0)
        y = zv[:, d:2 * d] * zv[:, 2 * d:]
        yh = zh[:, d:2 * d] * zh[:, 2 * d:]
        cwv = cw_ref[...]
        conv = cwv[0:1] * _shift_down(y, yh, 2) + cwv[1:2] * _shift_down(y, yh, 1) + cwv[2:3] * y
        a_ref[...] = (zv[:, :d] * conv).astype(BF16)

    return pl.pallas_call(
        body, name="conv_fwd", grid=(s // tq,),
        in_specs=[pl.BlockSpec((tq, d3), lambda i: (i, 0)),
                  pl.BlockSpec((8, d3), lambda i: (jnp.maximum(i * (tq // 8) - 1, 0), 0)),
                  pl.BlockSpec((3, d), lambda i: (0, 0))],
        out_specs=pl.BlockSpec((tq, d), lambda i: (i, 0)),
        out_shape=jax.ShapeDtypeStruct((s, d), BF16),
        compiler_params=_cparams("parallel"),
    )(z, z, cw)


def _conv_bwd(z, cw, da):
    s, d3 = z.shape
    d = d3 // 3
    tq = _tile(s, 128, 8)
    nsteps = s // tq
    nblk8 = s // 8

    def body(z_ref, zp_ref, zn_ref, da_ref, dan_ref, cw_ref, dz_ref, dcw_ref, acc):
        i = pl.program_id(0)

        @pl.when(i == 0)
        def _():
            acc[...] = jnp.zeros_like(acc)

        zv = z_ref[...].astype(F32)
        zp = jnp.where(i > 0, zp_ref[...].astype(F32), 0.0)
        zn = jnp.where(i < nsteps - 1, zn_ref[...].astype(F32), 0.0)
        dav = da_ref[...].astype(F32)
        dan = jnp.where(i < nsteps - 1, dan_ref[...].astype(F32), 0.0)
        gb, gc, hx = zv[:, :d], zv[:, d:2 * d], zv[:, 2 * d:]
        y = gc * hx
        yp = zp[:, d:2 * d] * zp[:, 2 * d:]
        cwv = cw_ref[...]
        y1, y2 = _shift_down(y, yp, 1), _shift_down(y, yp, 2)
        conv = cwv[0:1] * y2 + cwv[1:2] * y1 + cwv[2:3] * y
        dconv = dav * gb
        dconv_n = dan * zn[:, :d]
        dy = cwv[2:3] * dconv + cwv[1:2] * _shift_up(dconv, dconv_n, 1) + cwv[0:1] * _shift_up(dconv, dconv_n, 2)
        dz_ref[:, :d] = (dav * conv).astype(BF16)
        dz_ref[:, d:2 * d] = (dy * hx).astype(BF16)
        dz_ref[:, 2 * d:] = (dy * gc).astype(BF16)
        acc[0] += _colsum8(dconv * y2)
        acc[1] += _colsum8(dconv * y1)
        acc[2] += _colsum8(dconv * y)

        @pl.when(i == nsteps - 1)
        def _():
            for j in range(3):
                dcw_ref[j:j + 1, :] = jnp.sum(acc[j], axis=0, keepdims=True)

    return pl.pallas_call(
        body, name="conv_bwd", grid=(nsteps,),
        in_specs=[pl.BlockSpec((tq, d3), lambda i: (i, 0)),
                  pl.BlockSpec((8, d3), lambda i: (jnp.maximum(i * (tq // 8) - 1, 0), 0)),
                  pl.BlockSpec((8, d3), lambda i: (jnp.minimum((i + 1) * (tq // 8), nblk8 - 1), 0)),
                  pl.BlockSpec((tq, d), lambda i: (i, 0)),
                  pl.BlockSpec((8, d), lambda i: (jnp.minimum((i + 1) * (tq // 8), nblk8 - 1), 0)),
                  pl.BlockSpec((3, d), lambda i: (0, 0))],
        out_specs=[pl.BlockSpec((tq, d3), lambda i: (i, 0)), pl.BlockSpec((3, d), lambda i: (0, 0))],
        out_shape=[jax.ShapeDtypeStruct((s, d3), BF16), jax.ShapeDtypeStruct((3, d), F32)],
        scratch_shapes=[pltpu.VMEM((3, 8, d), F32)],
        compiler_params=_cparams("arbitrary"),
    )(z, z, z, da, da, cw)


def _swiglu_epilogue(gate, up):
    return gate, up, gate / (1.0 + jnp.exp(-gate)) * up


def _swiglu_bwd_epilogue(dact, gt, up):
    g, u = gt.astype(F32), up.astype(F32)
    sg = 1.0 / (1.0 + jnp.exp(-g))
    return dact * u * (sg * (1.0 + g * (1.0 - sg))), dact * (g * sg)


def _ada_fwd(c_all, ada_w):
    nl, d, n8 = ada_w.shape
    tn = _tile(n8, 768, 128)

    def body(c_ref, w_ref, o_ref):
        cv = c_ref[...]
        act = (cv / (1.0 + jnp.exp(-cv))).astype(BF16)
        o_ref[0] = _dot_nn(act, w_ref[0].astype(BF16))

    return pl.pallas_call(
        body, name="ada_fwd", grid=(nl, n8 // tn),
        in_specs=[pl.BlockSpec((N_DEV, d), lambda l, j: (0, 0)), pl.BlockSpec((1, d, tn), lambda l, j: (l, 0, j))],
        out_specs=pl.BlockSpec((1, N_DEV, tn), lambda l, j: (l, 0, j)),
        out_shape=jax.ShapeDtypeStruct((nl, N_DEV, n8), F32),
        compiler_params=_cparams("parallel", "parallel"),
    )(c_all, ada_w)


def _ada_wgrad(c_all, dmod_cols):
    nl, _, n8 = dmod_cols.shape
    d = c_all.shape[1]
    tn = _tile(n8, 768, 128)

    def body(c_ref, g_ref, o_ref):
        cv = c_ref[...]
        act = (cv / (1.0 + jnp.exp(-cv))).astype(BF16)
        o_ref[0] = _dot_tn(act, g_ref[0].astype(BF16))

    return pl.pallas_call(
        body, name="ada_wgrad", grid=(nl, n8 // tn),
        in_specs=[pl.BlockSpec((N_DEV, d), lambda l, j: (0, 0)), pl.BlockSpec((1, N_DEV, tn), lambda l, j: (l, 0, j))],
        out_specs=pl.BlockSpec((1, d, tn), lambda l, j: (l, 0, j)),
        out_shape=jax.ShapeDtypeStruct((nl, d, n8), F32),
        compiler_params=_cparams("parallel", "parallel"),
    )(c_all, dmod_cols)


def _adamw(name, pieces, w, m, v):
    npc, r, c = pieces.shape
    tr = _tile(r, max(8, (1 << 19) // c // 8 * 8), 8)
    bc1, bc2 = 1.0 - ADAM_B1 ** ADAM_STEP, 1.0 - ADAM_B2 ** ADAM_STEP

    def body(p_ref, w_ref, m_ref, v_ref, g_ref, d_ref, nm_ref, nv_ref):
        g = p_ref[0].astype(F32)
        for i in range(1, npc):
            g = g + p_ref[i].astype(F32)
        nm = ADAM_B1 * m_ref[...] + (1.0 - ADAM_B1) * g
        nv = ADAM_B2 * v_ref[...] + (1.0 - ADAM_B2) * (g * g)
        g_ref[...] = g
        nm_ref[...] = nm
        nv_ref[...] = nv
        d_ref[...] = -ADAM_LR * ((nm / bc1) / (jnp.sqrt(nv / bc2) + ADAM_EPS) + ADAM_WD * w_ref[...])

    blk = pl.BlockSpec((tr, c), lambda i: (i, 0))
    return pl.pallas_call(
        body, name=name, grid=(r // tr,),
        in_specs=[pl.BlockSpec((npc, tr, c), lambda i: (0, i, 0)), blk, blk, blk],
        out_specs=[blk] * 4, out_shape=[jax.ShapeDtypeStruct((r, c), F32)] * 4,
        compiler_params=_cparams("parallel"),
    )(pieces, w, m, v)


def _place():
    x, y, c = lax.axis_index("x"), lax.axis_index("y"), lax.axis_index("c")
    return x, y, c


def _all_gather_small(name, x_shard):
    m_per, n = x_shard.shape

    def body(x_ref, out_ref, token_ref, send_sems, recv_sems, local_sem):
        token_ref[...] = jnp.zeros_like(token_ref)
        x, y, c = _place()
        me, sibling = (x, y, c), (x, y, 1 - c)
        chips = [(1 - x, y), (x, 1 - y), (1 - x, 1 - y)]

        def rows(px, py, pc):
            return out_ref.at[pl.ds((4 * px + 2 * py + pc) * m_per, m_per), :]

        def copy(k, block, to, src=None):
            return pltpu.make_async_remote_copy(
                src_ref=rows(*block) if src is None else src, dst_ref=rows(*block),
                send_sem=send_sems.at[k], recv_sem=recv_sems.at[k], device_id=to, device_id_type=MESH)

        mine = pltpu.make_async_copy(x_ref, rows(*me), local_sem)
        mine.start()
        first = [copy(0, me, sibling, src=x_ref)]
        first += [copy(1 + j, me, (*chip, c), src=x_ref) for j, chip in enumerate(chips)]
        for cp in first:
            cp.start()
        passed = [copy(4 + j, (*chip, c), sibling) for j, chip in enumerate(chips)]
        for j, chip in enumerate(chips):
            copy(1 + j, (*chip, c), me).wait_recv()
            passed[j].start()
        copy(0, sibling, me).wait_recv()
        for j, chip in enumerate(chips):
            copy(4 + j, (*chip, 1 - c), me).wait_recv()
        for cp in first + passed:
            cp.wait_send()
        mine.wait()

    vmem = pl.BlockSpec(memory_space=pltpu.VMEM)
    return pl.pallas_call(
        body, name=name,
        out_shape=[jax.ShapeDtypeStruct((N_DEV * m_per, n), x_shard.dtype), jax.ShapeDtypeStruct((8, HEAD), F32)],
        in_specs=[vmem], out_specs=[vmem, vmem],
        scratch_shapes=[pltpu.SemaphoreType.DMA((7,)), pltpu.SemaphoreType.DMA((7,)), pltpu.SemaphoreType.DMA],
        compiler_params=pltpu.CompilerParams(vmem_limit_bytes=VMEM_LIMIT),
    )(x_shard)


_HBM = pl.BlockSpec(memory_space=pltpu.HBM)
_SEM = pl.BlockSpec(memory_space=pltpu.SEMAPHORE)
_EFFECT = pltpu.SideEffectType.DATAFLOW_SIDE_EFFECTING


def _peers():
    x, y, c = _place()
    peers = []
    for k in range(1, N_DEV):
        px = 1 - x if k & 4 else x
        py = 1 - y if k & 2 else y
        pc = 1 - c if k & 1 else c
        peers.append(((px, py, pc), 4 * px + 2 * py + pc))
    return 4 * x + 2 * y + c, peers


def _push_start(name, srcs, lands, src_view, dst_view):
    na = len(srcs)
    n = na * (N_DEV - 1)

    def body(*refs):
        s_refs, l_refs = refs[:na], refs[na:2 * na]
        send_sems, recv_sems, token = refs[2 * na], refs[2 * na + 1], refs[-1]
        me, peers = _peers()
        for a in range(na):
            for k, (dev, idx) in enumerate(peers):
                pltpu.make_async_remote_copy(
                    src_ref=src_view(s_refs[a], idx), dst_ref=dst_view(l_refs[a], me),
                    send_sem=send_sems.at[a * (N_DEV - 1) + k], recv_sem=recv_sems.at[a * (N_DEV - 1) + k],
                    device_id=dev, device_id_type=MESH).start()
        token[...] = jnp.zeros_like(token)

    outs = pl.pallas_call(
        body, name=name,
        out_shape=(pltpu.SemaphoreType.DMA((n,)), pltpu.SemaphoreType.DMA((n,)),
                   *[pltpu.HBM(t.shape, t.dtype) for t in list(srcs) + list(lands)], jax.ShapeDtypeStruct((8, HEAD), F32)),
        in_specs=[_HBM] * (2 * na),
        out_specs=(_SEM, _SEM, *[_HBM] * (2 * na), pl.BlockSpec(memory_space=pltpu.VMEM)),
        input_output_aliases={i: 2 + i for i in range(2 * na)},
        compiler_params=pltpu.CompilerParams(has_side_effects=_EFFECT),
    )(*[pltpu.with_memory_space_constraint(t, pltpu.HBM) for t in list(srcs) + list(lands)])
    return outs[0], outs[1], list(outs[2:2 + na]), list(outs[2 + na:2 + 2 * na]), outs[-1]


def _push_wait(name, send_sems, recv_sems, srcs, lands, after, src_view, dst_view):
    na = len(srcs)

    def body(*refs):
        s_refs, l_refs = refs[:na], refs[na:2 * na]
        send_sems, recv_sems = refs[2 * na], refs[2 * na + 1]
        me, peers = _peers()
        for a in range(na):
            for k, (dev, idx) in enumerate(peers):
                cp = pltpu.make_async_remote_copy(
                    src_ref=src_view(s_refs[a], idx), dst_ref=dst_view(l_refs[a], idx),
                    send_sem=send_sems.at[a * (N_DEV - 1) + k], recv_sem=recv_sems.at[a * (N_DEV - 1) + k],
                    device_id=dev, device_id_type=MESH)
                cp.wait_send()
                cp.wait_recv()

    outs = pl.pallas_call(
        body, name=name,
        out_shape=[pltpu.HBM(t.shape, t.dtype) for t in list(srcs) + list(lands)],
        in_specs=[_HBM] * (2 * na) + [_SEM, _SEM, pl.BlockSpec(memory_space=pl.ANY)],
        out_specs=[_HBM] * (2 * na),
        input_output_aliases={i: i for i in range(2 * na)},
        compiler_params=pltpu.CompilerParams(has_side_effects=_EFFECT),
    )(*srcs, *lands, send_sems, recv_sems, after)
    return list(outs[na:])


def _gather_start(name, shards):
    lands = [lax.empty((N_DEV,) + t.shape, t.dtype) for t in shards]
    return _push_start(name, shards, lands, lambda ref, idx: ref, lambda ref, slot: ref.at[slot])


def _gather_wait(name, started, shards, after, me):
    send_sems, recv_sems, srcs, lands, _ = started
    lands = _push_wait(name, send_sems, recv_sems, srcs, lands, after, lambda ref, idx: ref, lambda ref, slot: ref.at[slot])
    return [lax.dynamic_update_index_in_dim(g, t, me, 0) for g, t in zip(lands, shards)]


def _cols_to_plain(g):
    n, k, n8 = g.shape
    return jnp.transpose(g, (1, 0, 2)).reshape(k, n * n8)


def _plain_to_cols(w):
    k, n = w.shape
    return jnp.transpose(w.reshape(k, N_DEV, n // N_DEV), (1, 0, 2))


def _pack_rows(parts, width):
    rows, offs, r = [], [], 0
    for p in parts:
        flat = p.reshape(-1).astype(F32)
        nr = -(-flat.shape[0] // (8 * width)) * 8
        rows.append(jnp.pad(flat, (0, nr * width - flat.shape[0])).reshape(nr, width))
        offs.append((r, flat.shape[0], p.shape))
        r += nr
    return jnp.concatenate(rows, axis=0), offs, r


def _unpack_rows(slab, offs, width):
    lead = slab.shape[:-2]
    out = []
    for r0, n, shape in offs:
        nr = -(-n // width)
        out.append(slab[..., r0:r0 + nr, :].reshape(lead + (nr * width,))[..., :n].reshape(lead + tuple(shape)))
    return out


def kernel(x, c, positions, ada_w, ada_b, norm_mix, norm_ffn, ab_w_in, sgu_w, sgu_b, ab_w_out, conv_w_in, conv_w, conv_w_out, ffn_w_gate, ffn_w_up, ffn_w_down, final_norm, loss_target, m_ada_w, m_ada_b, m_norm_mix, m_norm_ffn, m_ab_w_in, m_sgu_w, m_sgu_b, m_ab_w_out, m_conv_w_in, m_conv_w, m_conv_w_out, m_ffn_w_gate, m_ffn_w_up, m_ffn_w_down, m_final_norm, v_ada_w, v_ada_b, v_norm_mix, v_norm_ffn, v_ab_w_in, v_sgu_w, v_sgu_b, v_ab_w_out, v_conv_w_in, v_conv_w, v_conv_w_out, v_ffn_w_gate, v_ffn_w_up, v_ffn_w_down, v_final_norm):
    xi, yi, ci = _place()
    me = 4 * xi + 2 * yi + ci
    s, d = x.shape[1], x.shape[2]
    depth = ada_w.shape[0]
    n_even = ab_w_in.shape[0]
    nh_mix = d // HEAD
    nh = 3 * nh_mix // 4
    ng = nh_mix - nh
    aw, gw = nh * HEAD, ng * HEAD
    assert d <= FULL_ROW and s % ATT_CHUNK == 0
    x0 = x[0]
    target = loss_target[0]
    n_odd, cwid, d8 = conv_w.shape

    width = 512
    slab, offs, _ = _pack_rows([c, conv_w], width)
    gathered, _ = _all_gather_small("gather_cond", slab)
    c_parts, cw_parts = _unpack_rows(gathered.reshape(N_DEV, -1, width), offs, width)
    c_all = c_parts.reshape(N_DEV, d)
    conv_w_full = jnp.transpose(cw_parts, (1, 2, 0, 3)).reshape(n_odd, cwid, d)

    mod_cols = _ada_fwd(c_all, ada_w)
    n8 = mod_cols.shape[2]
    mod_all, token = _all_gather_small("gather_mod", mod_cols.reshape(depth * N_DEV, n8))
    mod_mine = lax.dynamic_index_in_dim(mod_all.reshape(N_DEV, depth, N_DEV, n8), me, axis=2, keepdims=False)

    def mixer_weights(l):
        return (ab_w_in[l // 2], ab_w_out[l // 2]) if l % 2 == 0 else (conv_w_in[l // 2], conv_w_out[l // 2])

    groups = [[mixer_weights(0)[0]], [mixer_weights(0)[1]], [ffn_w_gate[0], ffn_w_up[0], ffn_w_down[0]]]
    for l in range(1, depth):
        groups += [list(mixer_weights(l)), [ffn_w_gate[l], ffn_w_up[l], ffn_w_down[l]]]
    gathers, tok = [], token[0, 0]
    for n, ws in enumerate(groups):
        shards = [(w + tok).astype(BF16) for w in ws]
        started = _gather_start(f"gather_start_{n}", shards)
        gathers.append((started, shards))
        tok = started[4][0, 0]

    def weights_of_group(n, after):
        started, shards = gathers[n]
        return _gather_wait(f"gather_wait_{n}", started, shards, after, me)

    def plain_rows(g):
        return g.reshape(g.shape[0] * g.shape[1], g.shape[2])

    mod = jnp.transpose(mod_mine, (1, 0, 2)).reshape(depth, N_DEV * n8) + ada_b + tok
    mods = mod.reshape(depth, 6, 1, d)

    ct, st = _rope_tables(positions.reshape(s, 1))
    b_col = jnp.broadcast_to(sgu_b[..., None], sgu_b.shape + (CHUNK,))
    u_blk = 3 * aw // gw

    stream = [x0]
    saved = []
    w_in, w_out, w_gate, w_up, w_down = [[None] * depth for _ in range(5)]
    xcur = x0
    for l in range(depth):
        sh_m, sc_m, g_m, sh_f, sc_f, g_f = [mods[l, j] for j in range(6)]
        i = l // 2
        if l == 0:
            (g_in,) = weights_of_group(0, mod)
        else:
            g_in, g_out = weights_of_group(1 + 2 * l, xcur)
        w_in[l] = _cols_to_plain(g_in)
        h = _norm_mod_fwd(xcur, norm_mix[l][None], sc_m, sh_m)
        if l % 2 == 0:
            z = _matmul("mix_in", "nn", h, w_in[l], [BF16])
            attn, lse = _attn_fwd(z, ct, st, nh)
            sgu = _sgu_fwd(z, sgu_w[i], b_col[i], ng, u_blk)
            a = jnp.concatenate([attn, sgu], axis=1)
            mixer_saved = (z, a, lse)
        else:
            z = _matmul("conv_in", "nn", h, w_in[l], [BF16])
            a = _conv_fwd(z, conv_w_full[i])
            mixer_saved = (z, a, None)
        if l == 0:
            (g_out,) = weights_of_group(1, a)
        w_out[l] = plain_rows(g_out)
        x1, mix, h2 = _matmul("mix_out", "nn", a, w_out[l], [F32, BF16, BF16],
                              extras=[(xcur, "mn"), (g_m, "n"), (norm_ffn[l][None], "n"), (sc_f, "n"), (sh_f, "n")],
                              epilogue=_gated_add_norm_epilogue)
        g_gate, g_up, g_down = weights_of_group(2 + 2 * l, x1)
        w_gate[l], w_up[l], w_down[l] = _cols_to_plain(g_gate), _cols_to_plain(g_up), plain_rows(g_down)
        gt, up, act = _matmul("ffn_in", "nn", h2, [w_gate[l], w_up[l]], [BF16, BF16, BF16], epilogue=_swiglu_epilogue)
        x2, f = _matmul("ffn_down", "nn", act, w_down[l], [F32, BF16], extras=[(x1, "mn"), (g_f, "n")],
                        epilogue=lambda acc, r, gv: (r + gv * acc, acc))
        saved.append((h, mixer_saved, mix, x1, h2, gt, up, act, f))
        stream.append(x2)
        xcur = x2

    f_last = saved[-1][8]
    loss_part, dx, dbr, d_final, dg = _final_loss(xcur, target, final_norm[None], f_last, mods[depth - 1, 5])
    loss = lax.psum(loss_part[0, 0], ("x", "y", "c"))

    dmod = [[None] * 6 for _ in range(depth)]
    d_norm_mix, d_norm_ffn = [None] * depth, [None] * depth
    d_sgu_w, d_sgu_b, d_conv_w = [None] * n_even, [None] * n_even, [None] * n_odd

    big = {"in": ab_w_in, "out": ab_w_out, "cin": conv_w_in, "cout": conv_w_out,
           "gate": ffn_w_gate, "up": ffn_w_up, "down": ffn_w_down}
    lands = {k: lax.empty((N_DEV,) + w.shape, BF16) for k, w in big.items()}
    own = {k: [None] * w.shape[0] for k, w in big.items()}
    pending = {"ffn": None, "mix": None}

    def exchange_finish(tag, after):
        (send_sems, recv_sems, srcs, lds, _), keys, li, layer = pending[tag]
        lds = _push_wait(f"exchange_wait_{tag}_{layer}", send_sems, recv_sems, srcs, lds, after,
                         lambda ref, idx: ref.at[idx], lambda ref, slot: ref.at[slot, li])
        for k, ld in zip(keys, lds):
            lands[k] = ld
        pending[tag] = None

    def exchange_start(tag, layer, keys, li, grads):
        if pending[tag] is not None:
            exchange_finish(tag, grads[0])
        for k, g in zip(keys, grads):
            own[k][li] = lax.dynamic_index_in_dim(g, me, 0, keepdims=False)
        started = _push_start(f"exchange_start_{tag}_{layer}", grads, [lands[k] for k in keys],
                              lambda ref, idx: ref.at[idx], lambda ref, slot: ref.at[slot, li])
        pending[tag] = (started, keys, li, layer)
        return started[4][0, 0]

    def row_shards(g):
        return g.reshape(N_DEV, g.shape[0] // N_DEV, g.shape[1])

    for l in reversed(range(depth)):
        sh_m, sc_m, g_m, sh_f, sc_f, g_f = [mods[l, j] for j in range(6)]
        h, (z, a, lse), mix, x1, h2, gt, up, act, f = saved[l]
        i = l // 2
        dmod[l][5] = dg
        gw_down = _matmul("ffn_down_wgrad", "tn", act, dbr, [BF16])
        dgt, dup = _matmul("ffn_down_dgrad", "nt", dbr, w_down[l], [BF16, BF16], extras=[(gt, "mn"), (up, "mn")],
                           epilogue=_swiglu_bwd_epilogue)
        gw_gate = _matmul("ffn_in_wgrad", "tn", h2, dgt, [BF16])
        gw_up = _matmul("ffn_in_wgrad", "tn", h2, dup, [BF16])
        tok = exchange_start("ffn", l, ("gate", "up", "down"), l, [_plain_to_cols(gw_gate), _plain_to_cols(gw_up), row_shards(gw_down)])
        dh2 = _matmul("ffn_in_dgrad", "nt", [dgt, dup], [w_gate[l], w_up[l]], [BF16], after=tok)
        dx, dbr, dmod[l][3], dmod[l][4], d_norm_ffn[l], dg = _norm_mod_bwd(x1, dh2, norm_ffn[l][None], sc_f, dx, mix, g_m)
        dmod[l][2] = dg
        gw_out = _matmul("mix_out_wgrad", "tn", a, dbr, [BF16])
        da = _matmul("mix_out_dgrad", "nt", dbr, w_out[l], [BF16])
        if l % 2 == 0:
            dq, dk, dv = _attn_bwd(z, ct, st, da, a, lse, nh)
            du, dvg, d_sgu_w[i], dbb = _sgu_bwd(z, sgu_w[i], b_col[i], da, ng, u_blk, aw // gw)
            d_sgu_b[i] = dbb[:, :, 0]
            dz = jnp.concatenate([dq, dk, dv, du, dvg], axis=1)
            gw_in = _matmul("mix_in_wgrad", "tn", h, dz, [BF16])
            dh = _matmul("mix_in_dgrad", "nt", dz, w_in[l], [BF16])
        else:
            dz, d_conv_w[i] = _conv_bwd(z, conv_w_full[i], da)
            gw_in = _matmul("conv_in_wgrad", "tn", h, dz, [BF16])
            dh = _matmul("conv_in_dgrad", "nt", dz, w_in[l], [BF16])
        mix_group = ("mix", l, ("in", "out") if l % 2 == 0 else ("cin", "cout"), i, [_plain_to_cols(gw_in), row_shards(gw_out)])
        w_norm = norm_mix[l][None]
        if l > 0:
            w_norm = w_norm + exchange_start(*mix_group)
            f_prev, g_prev = saved[l - 1][8], mods[l - 1, 5]
            dx, dbr, dmod[l][0], dmod[l][1], d_norm_mix[l], dg = _norm_mod_bwd(stream[l], dh, w_norm, sc_m, dx, f_prev, g_prev)
        else:
            dx, dmod[l][0], dmod[l][1], d_norm_mix[l] = _norm_mod_bwd(stream[l], dh, w_norm, sc_m, dx)
    grad_x = dx[None]

    dmod_mine = jnp.stack([jnp.concatenate([v.reshape(d) for v in dmod[l]]) for l in range(depth)])
    small = [dmod_mine, jnp.concatenate(d_norm_mix), jnp.concatenate(d_norm_ffn), jnp.stack(d_sgu_w), jnp.stack(d_sgu_b),
             d_final, jnp.stack(d_conv_w)]
    slab, offs, _ = _pack_rows(small, width)
    gathered, token = _all_gather_small("gather_small_grads", slab)
    p_dmod, p_nmix, p_nffn, p_sguw, p_sgub, p_final, p_convw = _unpack_rows(gathered.reshape(N_DEV, -1, width), offs, width)
    mix_group[4][0] = mix_group[4][0] + token[0, 0].astype(BF16)
    p_dmod = p_dmod + exchange_start(*mix_group)

    outs = {}

    def update(name, pieces, w, m, v):
        shape = w.shape
        cdim = shape[-1]
        res = _adamw("adamw_" + name, pieces.reshape(pieces.shape[0], -1, cdim), w.reshape(-1, cdim),
                     m.reshape(-1, cdim), v.reshape(-1, cdim))
        outs[name] = [r.reshape(shape) for r in res]

    update("ada_b", p_dmod.reshape(N_DEV, depth, 6 * d), ada_b, m_ada_b, v_ada_b)
    update("norm_mix", p_nmix.reshape(N_DEV, depth, d), norm_mix, m_norm_mix, v_norm_mix)
    update("norm_ffn", p_nffn.reshape(N_DEV, depth, d), norm_ffn, m_norm_ffn, v_norm_ffn)
    update("sgu_w", p_sguw, sgu_w, m_sgu_w, v_sgu_w)
    update("sgu_b", p_sgub.reshape(N_DEV, 1, -1), sgu_b.reshape(1, -1), m_sgu_b.reshape(1, -1), v_sgu_b.reshape(1, -1))
    outs["sgu_b"] = [r.reshape(sgu_b.shape) for r in outs["sgu_b"]]
    update("final_norm", p_final.reshape(N_DEV, 1, d), final_norm[None], m_final_norm[None], v_final_norm[None])
    outs["final_norm"] = [r.reshape(final_norm.shape) for r in outs["final_norm"]]
    cw_mine = lax.dynamic_slice_in_dim(p_convw.reshape(N_DEV, n_odd, cwid, d), me * d8, d8, axis=3)
    update("conv_w", cw_mine, conv_w, m_conv_w, v_conv_w)

    dmod_cols = lax.dynamic_slice_in_dim(p_dmod.reshape(N_DEV, depth, 6 * d), me * n8, n8, axis=2)
    g_ada = _ada_wgrad(c_all, jnp.transpose(dmod_cols, (1, 0, 2)))
    update("ada_w", g_ada[None], ada_w, m_ada_w, v_ada_w)

    names = {"in": "ab_w_in", "out": "ab_w_out", "cin": "conv_w_in", "cout": "conv_w_out",
             "gate": "ffn_w_gate", "up": "ffn_w_up", "down": "ffn_w_down"}
    moments = {"in": (m_ab_w_in, v_ab_w_in), "out": (m_ab_w_out, v_ab_w_out), "cin": (m_conv_w_in, v_conv_w_in),
               "cout": (m_conv_w_out, v_conv_w_out), "gate": (m_ffn_w_gate, v_ffn_w_gate), "up": (m_ffn_w_up, v_ffn_w_up),
               "down": (m_ffn_w_down, v_ffn_w_down)}

    def update_big(k):
        pieces = lax.dynamic_update_slice(lands[k], jnp.stack(own[k])[None], (me, 0, 0, 0))
        update(names[k], pieces, big[k], *moments[k])

    exchange_finish("ffn", g_ada)
    for k in ("gate", "up", "down", "cin", "cout"):
        update_big(k)
    done = sum(outs[n][0][(0,) * outs[n][0].ndim] for n in outs)
    exchange_finish("mix", done.reshape(1, 1))
    for k in ("in", "out"):
        update_big(k)

    order = ["ada_w", "ada_b", "norm_mix", "norm_ffn", "ab_w_in", "sgu_w", "sgu_b", "ab_w_out", "conv_w_in", "conv_w",
             "conv_w_out", "ffn_w_gate", "ffn_w_up", "ffn_w_down", "final_norm"]
    return (loss, grad_x, *[outs[n][0] for n in order], *[outs[n][1] for n in order],
            *[outs[n][2] for n in order], *[outs[n][3] for n in order])
```

```python
import functools
import math

import numpy as np
import jax
import jax.numpy as jnp
from jax import lax
from jax.experimental import pallas as pl
from jax.experimental.pallas import tpu as pltpu

F32, BF16 = jnp.float32, jnp.bfloat16
MESH = pl.DeviceIdType.MESH
N_DEV = 8
EPS = 1e-6
HEAD = 128
CHUNK = 128
DILATIONS = (1, 4, 16)
ATT_CHUNK = CHUNK * DILATIONS[-1]
ATT_BATCH = 8
ROPE_THETA = 500000.0
ROPE_DIM = HEAD // 4
NEG = -1e30
ADAM_LR, ADAM_B1, ADAM_B2, ADAM_EPS, ADAM_WD, ADAM_STEP = 0.001, 0.9, 0.999, 1e-08, 0.01, 10
VMEM_LIMIT = 56 * 1024 * 1024


def _cparams(*sem):
    return pltpu.CompilerParams(dimension_semantics=sem or None, vmem_limit_bytes=VMEM_LIMIT)


def _tile(n, pref, unit):
    t = (min(pref, n) // unit) * unit
    while t >= unit:
        if n % t == 0:
            return t
        t -= unit
    return n


def _dot(a, b, dims):
    return lax.dot_general(a, b, (dims, ((), ())), preferred_element_type=F32)


def _dot_nn(a, b):
    return _dot(a, b, ((1,), (0,)))


def _dot_nt(a, b):
    return _dot(a, b, ((1,), (1,)))


def _dot_tn(a, b):
    return _dot(a, b, ((0,), (0,)))


FULL_ROW = 2048
_TILE_DEFAULT = dict(tm=1024, tn=512, tk=2816)
_TILES = {
    "ffn_down": dict(tn=1024),
    "ffn_down_wgrad": dict(tm=1408, tn=1024),
    "ffn_in_wgrad": dict(tn=1408), "mix_in_wgrad": dict(tn=1408), "conv_in_wgrad": dict(tn=1536),
    "mix_in_dgrad": dict(tn=1024, tk=1408), "conv_in_dgrad": dict(tn=1024, tk=1536),
    "mix_out": dict(tm=512, tn=FULL_ROW),
    "mix_out_wgrad": dict(tn=1024), "mix_out_dgrad": dict(tn=1024),
}


def _matmul(name, mode, a, b, outs, *, extras=(), epilogue=None, after=None):
    a_list = list(a) if isinstance(a, (list, tuple)) else [a]
    b_list = list(b) if isinstance(b, (list, tuple)) else [b]
    na, nb = len(a_list), len(b_list)
    paired = na > 1
    assert na == nb if paired else na == 1
    nacc = 1 if paired else nb
    a0, b0 = a_list[0], b_list[0]
    if mode == "nn":
        (m, kk), (_, n) = a0.shape, b0.shape
    elif mode == "nt":
        (m, kk), (n, _) = a0.shape, b0.shape
    else:
        (kk, m), (_, n) = a0.shape, b0.shape
    if after is not None:
        inner = epilogue or (lambda *accs: accs)
        extras = list(extras) + [(jnp.zeros((1, n), F32) + after, "n")]
        epilogue = lambda *tiles: inner(*tiles[:-1])
    pref = {**_TILE_DEFAULT, **_TILES.get(name, {})}
    tm, tn, tk = _tile(m, pref["tm"], 128), _tile(n, pref["tn"], 128), _tile(kk, pref["tk"], 128)
    nk = kk // tk
    dotf = {"nn": _dot_nn, "nt": _dot_nt, "tn": _dot_tn}[mode]
    a_spec = pl.BlockSpec((tk, tm), lambda i, j, k: (k, i)) if mode == "tn" else pl.BlockSpec((tm, tk), lambda i, j, k: (i, k))
    b_spec = pl.BlockSpec((tn, tk), lambda i, j, k: (j, k)) if mode == "nt" else pl.BlockSpec((tk, tn), lambda i, j, k: (k, j))
    e_specs = [pl.BlockSpec((tm, tn), lambda i, j, k: (i, j)) if kind == "mn" else pl.BlockSpec((1, tn), lambda i, j, k: (0, j))
               for _, kind in extras]
    ne, no = len(extras), len(outs)
    epi = epilogue or (lambda *accs: accs)

    def body(*refs):
        a_refs, b_refs, rest = refs[:na], refs[na:na + nb], refs[na + nb:]
        e_refs, o_refs, acc_refs = rest[:ne], rest[ne:ne + no], rest[ne + no:]

        def products():
            if paired:
                p = dotf(a_refs[0][...], b_refs[0][...])
                for a_ref, b_ref in zip(a_refs[1:], b_refs[1:]):
                    p = p + dotf(a_ref[...], b_ref[...])
                return [p]
            av = a_refs[0][...]
            return [dotf(av, b_ref[...]) for b_ref in b_refs]

        def finish(accs):
            for o_ref, o in zip(o_refs, epi(*accs, *[r[...] for r in e_refs])):
                o_ref[...] = o.astype(o_ref.dtype)

        if nk == 1:
            finish(products())
            return
        k = pl.program_id(2)

        @pl.when(k == 0)
        def _():
            for acc_ref in acc_refs:
                acc_ref[...] = jnp.zeros_like(acc_ref)

        for acc_ref, p in zip(acc_refs, products()):
            acc_ref[...] += p

        @pl.when(k == nk - 1)
        def _():
            finish([acc_ref[...] for acc_ref in acc_refs])

    res = pl.pallas_call(
        body, name=name, grid=(m // tm, n // tn, nk),
        in_specs=[a_spec] * na + [b_spec] * nb + e_specs,
        out_specs=[pl.BlockSpec((tm, tn), lambda i, j, k: (i, j)) for _ in outs],
        out_shape=[jax.ShapeDtypeStruct((m, n), dt) for dt in outs],
        scratch_shapes=[pltpu.VMEM((tm, tn), F32)] * nacc if nk > 1 else [],
        compiler_params=_cparams("parallel", "parallel", "arbitrary"),
    )(*a_list, *b_list, *[e for e, _ in extras])
    return res[0] if no == 1 else res


def _norm_mod_fwd(x, w, sc, sh):
    s, d = x.shape
    tm = _tile(s, 512, 8)

    def body(x_ref, w_ref, sc_ref, sh_ref, h_ref):
        xv = x_ref[...]
        r = lax.rsqrt(jnp.mean(xv * xv, axis=-1, keepdims=True) + EPS)
        h_ref[...] = ((xv * r) * w_ref[...] * (1.0 + sc_ref[...]) + sh_ref[...]).astype(BF16)

    row = pl.BlockSpec((1, d), lambda i: (0, 0))
    return pl.pallas_call(
        body, name="norm_mod_fwd", grid=(s // tm,),
        in_specs=[pl.BlockSpec((tm, d), lambda i: (i, 0)), row, row, row],
        out_specs=pl.BlockSpec((tm, d), lambda i: (i, 0)),
        out_shape=jax.ShapeDtypeStruct((s, d), BF16),
        compiler_params=_cparams("parallel"),
    )(x, w, sc, sh)


def _gated_add_norm_epilogue(acc, res, g, w, sc, sh):
    xv = res + g * acc
    r = lax.rsqrt(jnp.mean(xv * xv, axis=-1, keepdims=True) + EPS)
    return xv, acc, (xv * r) * w * (1.0 + sc) + sh


def _colsum8(t):
    tm, d = t.shape
    return jnp.sum(t.reshape(tm // 8, 8, d), axis=0)


def _norm_mod_bwd(x, dh, w, sc, dres, branch=None, g=None):
    s, d = x.shape
    tm = _tile(s, 256, 8)
    nsteps = s // tm
    gated = branch is not None

    def body(*refs):
        if gated:
            x_ref, dh_ref, w_ref, sc_ref, dres_ref, br_ref, g_ref, dx_ref, dbr_ref, dsh_ref, dsc_ref, dw_ref, dg_ref, acc = refs
        else:
            x_ref, dh_ref, w_ref, sc_ref, dres_ref, dx_ref, dsh_ref, dsc_ref, dw_ref, acc = refs
        i = pl.program_id(0)

        @pl.when(i == 0)
        def _():
            acc[...] = jnp.zeros_like(acc)

        xv, dhv, wv, scv = x_ref[...], dh_ref[...].astype(F32), w_ref[...], sc_ref[...]
        r = lax.rsqrt(jnp.mean(xv * xv, axis=-1, keepdims=True) + EPS)
        xn = xv * r
        dxn = dhv * (wv * (1.0 + scv))
        dx = dres_ref[...] + r * (dxn - xn * jnp.mean(dxn * xn, axis=-1, keepdims=True))
        dx_ref[...] = dx
        acc[0] += _colsum8(dhv)
        acc[1] += _colsum8(dhv * xn)
        if gated:
            dbr_ref[...] = (dx * g_ref[...]).astype(BF16)
            acc[2] += _colsum8(dx * br_ref[...].astype(F32))

        @pl.when(i == nsteps - 1)
        def _():
            a0 = jnp.sum(acc[0], axis=0, keepdims=True)
            a1 = jnp.sum(acc[1], axis=0, keepdims=True)
            dsh_ref[...] = a0
            dsc_ref[...] = a1 * wv
            dw_ref[...] = a1 * (1.0 + scv)
            if gated:
                dg_ref[...] = jnp.sum(acc[2], axis=0, keepdims=True)

    big = pl.BlockSpec((tm, d), lambda i: (i, 0))
    row = pl.BlockSpec((1, d), lambda i: (0, 0))
    rowo = jax.ShapeDtypeStruct((1, d), F32)
    in_specs = [big, big, row, row, big] + ([big, row] if gated else [])
    out_specs = [big] + ([big] if gated else []) + [row, row, row] + ([row] if gated else [])
    out_shape = ([jax.ShapeDtypeStruct((s, d), F32)] + ([jax.ShapeDtypeStruct((s, d), BF16)] if gated else [])
                 + [rowo, rowo, rowo] + ([rowo] if gated else []))
    args = [x, dh, w, sc, dres] + ([branch, g] if gated else [])
    return pl.pallas_call(
        body, name="norm_mod_bwd_gated" if gated else "norm_mod_bwd", grid=(nsteps,),
        in_specs=in_specs, out_specs=out_specs, out_shape=out_shape,
        scratch_shapes=[pltpu.VMEM((3, 8, d), F32)],
        compiler_params=_cparams("arbitrary"),
    )(*args)


def _final_loss(x, target, w, branch, g):
    s, d = x.shape
    tm = _tile(s, 256, 8)
    nsteps = s // tm

    def body(x_ref, t_ref, w_ref, br_ref, g_ref, loss_ref, dx_ref, dbr_ref, dw_ref, dg_ref, acc):
        i = pl.program_id(0)

        @pl.when(i == 0)
        def _():
            acc[...] = jnp.zeros_like(acc)

        xv, wv = x_ref[...], w_ref[...]
        r = lax.rsqrt(jnp.mean(xv * xv, axis=-1, keepdims=True) + EPS)
        xn = xv * r
        err = xn * wv - t_ref[...]
        dy = err * (1.0 / d)
        dxn = dy * wv
        dx = r * (dxn - xn * jnp.mean(dxn * xn, axis=-1, keepdims=True))
        dx_ref[...] = dx
        dbr_ref[...] = (dx * g_ref[...]).astype(BF16)
        acc[0] += _colsum8(err * err)
        acc[1] += _colsum8(dy * xn)
        acc[2] += _colsum8(dx * br_ref[...].astype(F32))

        @pl.when(i == nsteps - 1)
        def _():
            loss_ref[...] = jnp.sum(jnp.sum(acc[0], axis=0, keepdims=True), axis=1, keepdims=True) * (0.5 / d)
            dw_ref[...] = jnp.sum(acc[1], axis=0, keepdims=True)
            dg_ref[...] = jnp.sum(acc[2], axis=0, keepdims=True)

    big = pl.BlockSpec((tm, d), lambda i: (i, 0))
    row = pl.BlockSpec((1, d), lambda i: (0, 0))
    rowo = jax.ShapeDtypeStruct((1, d), F32)
    return pl.pallas_call(
        body, name="final_loss", grid=(nsteps,),
        in_specs=[big, big, row, big, row],
        out_specs=[pl.BlockSpec((1, 1), lambda i: (0, 0)), big, big, row, row],
        out_shape=[jax.ShapeDtypeStruct((1, 1), F32), jax.ShapeDtypeStruct((s, d), F32),
                   jax.ShapeDtypeStruct((s, d), BF16), rowo, rowo],
        scratch_shapes=[pltpu.VMEM((3, 8, d), F32)],
        compiler_params=_cparams("arbitrary"),
    )(x, target, w, branch, g)


def _rope_tables(pos_col):
    s = pos_col.shape[0]
    tq = _tile(s, 1024, 8)
    half = ROPE_DIM // 2
    inv = np.float32(ROPE_THETA) ** (-np.arange(0, ROPE_DIM, 2, dtype=np.float32) / np.float32(ROPE_DIM))
    inv_row = jnp.asarray(np.tile(inv.astype(np.float32), HEAD // half)[None, :])

    def body(p_ref, inv_ref, ct_ref, st_ref):
        lane = lax.broadcasted_iota(jnp.int32, (tq, HEAD), 1)
        ang = p_ref[...].astype(F32) * inv_ref[...]
        cs, sn = jnp.cos(ang), jnp.sin(ang)
        ct_ref[...] = jnp.where(lane < ROPE_DIM, cs, 1.0)
        st_ref[...] = jnp.where(lane < half, -sn, jnp.where(lane < ROPE_DIM, sn, 0.0))

    blk = pl.BlockSpec((tq, HEAD), lambda i: (i, 0))
    return pl.pallas_call(
        body, name="rope_tables", grid=(s // tq,),
        in_specs=[pl.BlockSpec((tq, 1), lambda i: (i, 0)), pl.BlockSpec((1, HEAD), lambda i: (0, 0))],
        out_specs=[blk, blk],
        out_shape=[jax.ShapeDtypeStruct((s, HEAD), F32)] * 2,
        compiler_params=_cparams("parallel"),
    )(pos_col, inv_row)


def _swap_halves(x):
    lane = lax.broadcasted_iota(jnp.int32, x.shape, 1)
    half = ROPE_DIM // 2
    return jnp.where(lane < half, pltpu.roll(x, HEAD - half, 1), pltpu.roll(x, half, 1))


def _rope(x, ct, st):
    return x * ct + _swap_halves(x) * st


def _rope_t(dy, ct, st):
    return dy * ct - _swap_halves(dy) * st


def _band_bias(bias_ref):
    qi = lax.broadcasted_iota(jnp.int32, (CHUNK, 2 * CHUNK), 0)
    kj = lax.broadcasted_iota(jnp.int32, (CHUNK, 2 * CHUNK), 1)
    band = (kj >= qi) & (kj <= qi + CHUNK)
    bias_ref[0] = jnp.where(band, 0.0, NEG)
    bias_ref[1] = jnp.where(band & (kj >= CHUNK), 0.0, NEG)


def _rows(start, size, d):
    return pl.ds(start, size) if d == 1 else pl.ds(start, size, stride=d)


def _batch_units(ub, d, c):
    n_sb = ATT_BATCH // d
    out = []
    for sb, r in [(ub * n_sb + t, r) for t in range(n_sb) for r in range(d)]:
        base = pl.multiple_of(sb * (CHUNK * d), CHUNK)
        first = jnp.where((c == 0) & (sb == 0), 1, 0)
        out.append((_rows(base + r, CHUNK, d), _rows(base + (ATT_CHUNK - CHUNK * d + r), 2 * CHUNK, d), first))
    return out


def _for_batches(d, fn):
    def step(ub, carry):
        fn(ub)
        return carry
    lax.fori_loop(0, ATT_CHUNK // CHUNK // ATT_BATCH, step, 0)


def _bdot(a, b, ca, cb):
    return lax.dot_general(a, b, (((ca,), (cb,)), ((0,), (0,))), preferred_element_type=F32)


WIDE = DILATIONS[-1]
assert ATT_CHUNK == CHUNK * WIDE


def _deint(x):
    return jnp.swapaxes(x.reshape(CHUNK, WIDE, HEAD), 0, 1).reshape(ATT_CHUNK, HEAD)


def _reint(x):
    return jnp.swapaxes(x.reshape(WIDE, CHUNK, HEAD), 0, 1).reshape(ATT_CHUNK, HEAD)


def _blk(r):
    return pl.ds(r * CHUNK, CHUNK)


def _key_rows(ref, r):
    return jnp.concatenate([ref[_blk(r), :], ref[pl.ds(ATT_CHUNK + r * CHUNK, CHUNK), :]], axis=0)


def _attn_specs(nh, nc):
    cur = lambda off: pl.BlockSpec((ATT_CHUNK, HEAD), lambda h, c: (jnp.minimum(c, nc - 1), off + h))
    prev = lambda off: pl.BlockSpec((ATT_CHUNK, HEAD), lambda h, c: (jnp.maximum(c - 1, 0), off + h))
    tcur = pl.BlockSpec((ATT_CHUNK, HEAD), lambda h, c: (jnp.minimum(c, nc - 1), 0))
    tprev = pl.BlockSpec((ATT_CHUNK, HEAD), lambda h, c: (jnp.maximum(c - 1, 0), 0))
    return [cur(0), prev(nh), cur(nh), prev(2 * nh), cur(2 * nh), tcur, tcur, tprev, tprev]


def _attn_fwd(z, ct, st, nh):
    s = z.shape[0]
    nc = s // ATT_CHUNK
    scale = HEAD ** -0.5

    def body(q_ref, kp_ref, kc_ref, vp_ref, vc_ref, ctc, stc, ctp, stp, o_ref, lse_ref, qf, kf, vf, ob, lb, bias, q16, k16, v16):
        c = pl.program_id(1)
        qf[...] = _rope(q_ref[...].astype(F32), ctc[...], stc[...])
        kf[0:ATT_CHUNK] = _rope(kp_ref[...].astype(F32), ctp[...], stp[...])
        kf[ATT_CHUNK:] = _rope(kc_ref[...].astype(F32), ctc[...], stc[...])
        vf[0:ATT_CHUNK] = vp_ref[...].astype(F32)
        vf[ATT_CHUNK:] = vc_ref[...].astype(F32)
        _band_bias(bias)

        def softmax_units(q_b, k_b, v_b, bias_b):
            sc = _bdot(q_b, k_b, 2, 2) * scale + bias_b
            m = jnp.max(sc, axis=2, keepdims=True)
            p = jnp.exp(sc - m)
            l = jnp.sum(p, axis=2, keepdims=True)
            return _bdot(p.astype(BF16), v_b, 2, 1) / l, m + jnp.log(l)

        for b, d in enumerate(DILATIONS[:-1]):
            def batch(ub, b=b, d=d):
                units = _batch_units(ub, d, c)
                q_b = jnp.stack([qf[qr, :] for qr, _, _ in units]).astype(BF16)
                k_b = jnp.stack([kf[kr, :] for _, kr, _ in units]).astype(BF16)
                v_b = jnp.stack([vf[kr, :] for _, kr, _ in units]).astype(BF16)
                o, lse = softmax_units(q_b, k_b, v_b, jnp.stack([bias[first] for _, _, first in units]))
                for j, (qr, _, _) in enumerate(units):
                    ob[b, qr, :] = o[j]
                    lb[b, qr, :] = jnp.broadcast_to(lse[j], (CHUNK, HEAD))
            _for_batches(d, batch)

        b = len(DILATIONS) - 1
        q16[...] = _deint(qf[...]).astype(BF16)
        for half in range(2):
            rows = pl.ds(half * ATT_CHUNK, ATT_CHUNK)
            k16[rows, :] = _deint(kf[rows, :]).astype(BF16)
            v16[rows, :] = _deint(vf[rows, :]).astype(BF16)
        bias_w = bias[jnp.where(c == 0, 1, 0)][None]
        for ub in range(WIDE // ATT_BATCH):
            rs = [ub * ATT_BATCH + j for j in range(ATT_BATCH)]
            o, lse = softmax_units(jnp.stack([q16[_blk(r), :] for r in rs]), jnp.stack([_key_rows(k16, r) for r in rs]),
                                   jnp.stack([_key_rows(v16, r) for r in rs]), bias_w)
            for j, r in enumerate(rs):
                qf[_blk(r), :] = o[j]
                kf[_blk(r), :] = jnp.broadcast_to(lse[j], (CHUNK, HEAD))
        ob[b] = _reint(qf[...])
        lb[b] = _reint(kf[0:ATT_CHUNK])

        mx = jnp.maximum(jnp.maximum(lb[0], lb[1]), lb[2])
        e0, e1, e2 = jnp.exp(lb[0] - mx), jnp.exp(lb[1] - mx), jnp.exp(lb[2] - mx)
        den = e0 + e1 + e2
        o_ref[...] = ((e0 * ob[0] + e1 * ob[1] + e2 * ob[2]) / den).astype(BF16)
        lse_ref[...] = mx + jnp.log(den)

    blk = pl.BlockSpec((ATT_CHUNK, HEAD), lambda h, c: (c, h))
    return pl.pallas_call(
        body, name="attn_fwd", grid=(nh, nc),
        in_specs=_attn_specs(nh, nc), out_specs=[blk, blk],
        out_shape=[jax.ShapeDtypeStruct((s, nh * HEAD), BF16), jax.ShapeDtypeStruct((s, nh * HEAD), F32)],
        scratch_shapes=[pltpu.VMEM((ATT_CHUNK, HEAD), F32), pltpu.VMEM((2 * ATT_CHUNK, HEAD), F32),
                        pltpu.VMEM((2 * ATT_CHUNK, HEAD), F32), pltpu.VMEM((3, ATT_CHUNK, HEAD), F32),
                        pltpu.VMEM((3, ATT_CHUNK, HEAD), F32), pltpu.VMEM((2, CHUNK, 2 * CHUNK), F32),
                        pltpu.VMEM((ATT_CHUNK, HEAD), BF16), pltpu.VMEM((2 * ATT_CHUNK, HEAD), BF16),
                        pltpu.VMEM((2 * ATT_CHUNK, HEAD), BF16)],
        compiler_params=_cparams("parallel", "arbitrary"),
    )(z, z, z, z, z, ct, st, ct, st)


def _attn_bwd(z, ct, st, da, o, lse, nh):
    s = z.shape[0]
    nc = s // ATT_CHUNK
    scale = HEAD ** -0.5

    def body(q_ref, kp_ref, kc_ref, vp_ref, vc_ref, ctc, stc, ctp, stp, do_ref, o_ref, lse_ref,
             dq_ref, dk_ref, dv_ref, qf, kf, vf, dof, dbar, dqa, dkf, dvf, bias, q16, k16, v16, do16):
        c = pl.program_id(1)

        @pl.when(c == 0)
        def _():
            dkf[...] = jnp.zeros_like(dkf)
            dvf[...] = jnp.zeros_like(dvf)

        @pl.when(c > 0)
        def _():
            dkf[0:ATT_CHUNK] = dkf[ATT_CHUNK:]
            dvf[0:ATT_CHUNK] = dvf[ATT_CHUNK:]
            dkf[ATT_CHUNK:] = jnp.zeros((ATT_CHUNK, HEAD), F32)
            dvf[ATT_CHUNK:] = jnp.zeros((ATT_CHUNK, HEAD), F32)

        @pl.when(c < nc)
        def _():
            qf[...] = _rope(q_ref[...].astype(F32), ctc[...], stc[...])
            kf[0:ATT_CHUNK] = _rope(kp_ref[...].astype(F32), ctp[...], stp[...])
            kf[ATT_CHUNK:] = _rope(kc_ref[...].astype(F32), ctc[...], stc[...])
            vf[0:ATT_CHUNK] = vp_ref[...].astype(F32)
            vf[ATT_CHUNK:] = vc_ref[...].astype(F32)
            dov = do_ref[...].astype(F32)
            dof[...] = dov
            dbar[...] = jnp.broadcast_to(jnp.sum(dov * o_ref[...].astype(F32), axis=1, keepdims=True), (ATT_CHUNK, HEAD))
            dqa[...] = jnp.zeros_like(dqa)
            _band_bias(bias)

            def grads(q_b, k_b, v_b, do_b, lse_b, dbar_b, bias_b):
                sc = _bdot(q_b, k_b, 2, 2) * scale + bias_b
                p = jnp.exp(sc - jnp.concatenate([lse_b, lse_b], axis=2))
                dp = _bdot(do_b, v_b, 2, 2)
                ds = (p * (dp - jnp.concatenate([dbar_b, dbar_b], axis=2)) * scale).astype(BF16)
                return _bdot(ds, k_b, 2, 1), _bdot(ds, q_b, 1, 1), _bdot(p.astype(BF16), do_b, 1, 1)

            for d in DILATIONS[:-1]:
                def batch(ub, d=d):
                    units = _batch_units(ub, d, c)
                    dq, dk, dv = grads(jnp.stack([qf[qr, :] for qr, _, _ in units]).astype(BF16),
                                       jnp.stack([kf[kr, :] for _, kr, _ in units]).astype(BF16),
                                       jnp.stack([vf[kr, :] for _, kr, _ in units]).astype(BF16),
                                       jnp.stack([dof[qr, :] for qr, _, _ in units]).astype(BF16),
                                       jnp.stack([lse_ref[qr, :] for qr, _, _ in units]),
                                       jnp.stack([dbar[qr, :] for qr, _, _ in units]),
                                       jnp.stack([bias[first] for _, _, first in units]))
                    for j, (qr, kr, _) in enumerate(units):
                        dqa[qr, :] += dq[j]
                        dkf[kr, :] += dk[j]
                        dvf[kr, :] += dv[j]
                _for_batches(d, batch)

            q16[...] = _deint(qf[...]).astype(BF16)
            do16[...] = _deint(dof[...]).astype(BF16)
            for half in range(2):
                rows = pl.ds(half * ATT_CHUNK, ATT_CHUNK)
                k16[rows, :] = _deint(kf[rows, :]).astype(BF16)
                v16[rows, :] = _deint(vf[rows, :]).astype(BF16)
            dof[...] = _deint(lse_ref[...])
            dbar[...] = _deint(dbar[...])
            bias_w = bias[jnp.where(c == 0, 1, 0)][None]
            for ub in range(WIDE // ATT_BATCH):
                rs = [ub * ATT_BATCH + j for j in range(ATT_BATCH)]
                dq, dk, dv = grads(jnp.stack([q16[_blk(r), :] for r in rs]), jnp.stack([_key_rows(k16, r) for r in rs]),
                                   jnp.stack([_key_rows(v16, r) for r in rs]), jnp.stack([do16[_blk(r), :] for r in rs]),
                                   jnp.stack([dof[_blk(r), :] for r in rs]), jnp.stack([dbar[_blk(r), :] for r in rs]), bias_w)
                for j, r in enumerate(rs):
                    qf[_blk(r), :] = dq[j]
                    kf[_blk(r), :] = dk[j][0:CHUNK]
                    kf[pl.ds(ATT_CHUNK + r * CHUNK, CHUNK), :] = dk[j][CHUNK:]
                    vf[_blk(r), :] = dv[j][0:CHUNK]
                    vf[pl.ds(ATT_CHUNK + r * CHUNK, CHUNK), :] = dv[j][CHUNK:]
            dqa[...] += _reint(qf[...])
            for half in range(2):
                rows = pl.ds(half * ATT_CHUNK, ATT_CHUNK)
                dkf[rows, :] += _reint(kf[rows, :])
                dvf[rows, :] += _reint(vf[rows, :])
            dq_ref[...] = _rope_t(dqa[...], ctc[...], stc[...]).astype(BF16)

        @pl.when(c > 0)
        def _():
            dk_ref[...] = _rope_t(dkf[0:ATT_CHUNK], ctp[...], stp[...]).astype(BF16)
            dv_ref[...] = dvf[0:ATT_CHUNK].astype(BF16)

    cur = pl.BlockSpec((ATT_CHUNK, HEAD), lambda h, c: (jnp.minimum(c, nc - 1), h))
    late = pl.BlockSpec((ATT_CHUNK, HEAD), lambda h, c: (jnp.maximum(c - 1, 0), h))
    shp = jax.ShapeDtypeStruct((s, nh * HEAD), BF16)
    big = pltpu.VMEM((2 * ATT_CHUNK, HEAD), F32)
    one = pltpu.VMEM((ATT_CHUNK, HEAD), F32)
    one16, big16 = pltpu.VMEM((ATT_CHUNK, HEAD), BF16), pltpu.VMEM((2 * ATT_CHUNK, HEAD), BF16)
    return pl.pallas_call(
        body, name="attn_bwd", grid=(nh, nc + 1),
        in_specs=_attn_specs(nh, nc) + [cur, cur, cur], out_specs=[cur, late, late],
        out_shape=[shp, shp, shp],
        scratch_shapes=[one, big, big, one, one, one, big, big, pltpu.VMEM((2, CHUNK, 2 * CHUNK), F32),
                        one16, big16, big16, one16],
        compiler_params=_cparams("parallel", "arbitrary"),
    )(z, z, z, z, z, ct, st, ct, st, da, o, lse)


_GELU_K = math.sqrt(2.0 / math.pi)


def _gelu(x):
    return 0.5 * x * (1.0 + jnp.tanh(_GELU_K * (x + 0.044715 * x * x * x)))


def _gelu_and_grad(x):
    t = jnp.tanh(_GELU_K * (x + 0.044715 * x * x * x))
    g = 0.5 * x * (1.0 + t)
    dg = 0.5 * (1.0 + t) + 0.5 * x * (1.0 - t * t) * (_GELU_K * (1.0 + 3 * 0.044715 * x * x))
    return g, dg


def _tril(w):
    ti = lax.broadcasted_iota(jnp.int32, (CHUNK, CHUNK), 0)
    si = lax.broadcasted_iota(jnp.int32, (CHUNK, CHUNK), 1)
    return jnp.where(si <= ti, w, 0.0)


def _sgu_fwd(z, w_s, b_col, ng, u_blk):
    s = z.shape[0]
    gw = ng * HEAD
    tq = _tile(s, 1024, CHUNK)

    def body(u_ref, v_ref, w_ref, b_ref, o_ref):
        for g in range(ng):
            wg = _tril(w_ref[g]).astype(BF16)
            cols = slice(g * HEAD, (g + 1) * HEAD)
            for n in range(tq // CHUNK):
                rows = slice(n * CHUNK, (n + 1) * CHUNK)
                gv = _gelu(v_ref[rows, cols].astype(F32)).astype(BF16)
                mixed = _dot_nn(wg, gv) + b_ref[g]
                o_ref[rows, cols] = (_gelu(u_ref[rows, cols].astype(F32)) * mixed).astype(BF16)

    full = pl.BlockSpec((ng, CHUNK, CHUNK), lambda i: (0, 0, 0))
    return pl.pallas_call(
        body, name="sgu_fwd", grid=(s // tq,),
        in_specs=[pl.BlockSpec((tq, gw), lambda i: (i, u_blk)), pl.BlockSpec((tq, gw), lambda i: (i, u_blk + 1)), full, full],
        out_specs=pl.BlockSpec((tq, gw), lambda i: (i, 0)),
        out_shape=jax.ShapeDtypeStruct((s, gw), BF16),
        compiler_params=_cparams("parallel"),
    )(z, z, w_s, b_col)


def _sgu_bwd(z, w_s, b_col, da, ng, u_blk, da_blk):
    s = z.shape[0]
    gw = ng * HEAD
    tq = _tile(s, 1024, CHUNK)
    nsteps = s // tq

    def body(u_ref, v_ref, w_ref, b_ref, do_ref, du_ref, dv_ref, dw_ref, db_ref):
        i = pl.program_id(0)

        @pl.when(i == 0)
        def _():
            dw_ref[...] = jnp.zeros_like(dw_ref)
            db_ref[...] = jnp.zeros_like(db_ref)

        for g in range(ng):
            wg = _tril(w_ref[g]).astype(BF16)
            cols = slice(g * HEAD, (g + 1) * HEAD)
            dw_acc = jnp.zeros((CHUNK, CHUNK), F32)
            db_acc = jnp.zeros((CHUNK, 1), F32)
            for n in range(tq // CHUNK):
                rows = slice(n * CHUNK, (n + 1) * CHUNK)
                gu, dgu = _gelu_and_grad(u_ref[rows, cols].astype(F32))
                gv, dgv = _gelu_and_grad(v_ref[rows, cols].astype(F32))
                gvb = gv.astype(BF16)
                mixed = _dot_nn(wg, gvb) + b_ref[g]
                dout = do_ref[rows, cols].astype(F32)
                du_ref[rows, cols] = (dout * mixed * dgu).astype(BF16)
                dmix = dout * gu
                dmb = dmix.astype(BF16)
                dv_ref[rows, cols] = (_dot_tn(wg, dmb) * dgv).astype(BF16)
                dw_acc += _dot_nt(dmb, gvb)
                db_acc += jnp.sum(dmix, axis=1, keepdims=True)
            dw_ref[g] += _tril(dw_acc)
            db_ref[g] += jnp.broadcast_to(db_acc, (CHUNK, CHUNK))

    full = pl.BlockSpec((ng, CHUNK, CHUNK), lambda i: (0, 0, 0))
    out = pl.BlockSpec((tq, gw), lambda i: (i, 0))
    return pl.pallas_call(
        body, name="sgu_bwd", grid=(nsteps,),
        in_specs=[pl.BlockSpec((tq, gw), lambda i: (i, u_blk)), pl.BlockSpec((tq, gw), lambda i: (i, u_blk + 1)), full, full,
                  pl.BlockSpec((tq, gw), lambda i: (i, da_blk))],
        out_specs=[out, out, full, full],
        out_shape=[jax.ShapeDtypeStruct((s, gw), BF16)] * 2 + [jax.ShapeDtypeStruct((ng, CHUNK, CHUNK), F32)] * 2,
        compiler_params=_cparams("arbitrary"),
    )(z, z, w_s, b_col, da)


def _shift_down(y, halo, k):
    return pltpu.roll(jnp.concatenate([halo, y], axis=0), k, 0)[8:]


def _shift_up(y, halo, k):
    n = y.shape[0]
    return pltpu.roll(jnp.concatenate([y, halo], axis=0), n + 8 - k, 0)[:n]


def _conv_fwd(z, cw):
    s, d3 = z.shape
    d = d3 // 3
    tq = _tile(s, 256, 8)

    def body(z_ref, zh_ref, cw_ref, a_ref):
        i = pl.program_id(0)
        zv = z_ref[...].astype(F32)
        zh = jnp.where(i > 0, zh_ref[...].astype(F32), 0.0)
        y = zv[:, d:2 * d] * zv[:, 2 * d:]
        yh = zh[:, d:2 * d] * zh[:, 2 * d:]
        cwv = cw_ref[...]
        conv = cwv[0:1] * _shift_down(y, yh, 2) + cwv[1:2] * _shift_down(y, yh, 1) + cwv[2:3] * y
        a_ref[...] = (zv[:, :d] * conv).astype(BF16)

    return pl.pallas_call(
        body, name="conv_fwd", grid=(s // tq,),
        in_specs=[pl.BlockSpec((tq, d3), lambda i: (i, 0)),
                  pl.BlockSpec((8, d3), lambda i: (jnp.maximum(i * (tq // 8) - 1, 0), 0)),
                  pl.BlockSpec((3, d), lambda i: (0, 0))],
        out_specs=pl.BlockSpec((tq, d), lambda i: (i, 0)),
        out_shape=jax.ShapeDtypeStruct((s, d), BF16),
        compiler_params=_cparams("parallel"),
    )(z, z, cw)


def _conv_bwd(z, cw, da):
    s, d3 = z.shape
    d = d3 // 3
    tq = _tile(s, 128, 8)
    nsteps = s // tq
    nblk8 = s // 8

    def body(z_ref, zp_ref, zn_ref, da_ref, dan_ref, cw_ref, dz_ref, dcw_ref, acc):
        i = pl.program_id(0)

        @pl.when(i == 0)
        def _():
            acc[...] = jnp.zeros_like(acc)

        zv = z_ref[...].astype(F32)
        zp = jnp.where(i > 0, zp_ref[...].astype(F32), 0.0)
        zn = jnp.where(i < nsteps - 1, zn_ref[...].astype(F32), 0.0)
        dav = da_ref[...].astype(F32)
        dan = jnp.where(i < nsteps - 1, dan_ref[...].astype(F32), 0.0)
        gb, gc, hx = zv[:, :d], zv[:, d:2 * d], zv[:, 2 * d:]
        y = gc * hx
        yp = zp[:, d:2 * d] * zp[:, 2 * d:]
        cwv = cw_ref[...]
        y1, y2 = _shift_down(y, yp, 1), _shift_down(y, yp, 2)
        conv = cwv[0:1] * y2 + cwv[1:2] * y1 + cwv[2:3] * y
        dconv = dav * gb
        dconv_n = dan * zn[:, :d]
        dy = cwv[2:3] * dconv + cwv[1:2] * _shift_up(dconv, dconv_n, 1) + cwv[0:1] * _shift_up(dconv, dconv_n, 2)
        dz_ref[:, :d] = (dav * conv).astype(BF16)
        dz_ref[:, d:2 * d] = (dy * hx).astype(BF16)
        dz_ref[:, 2 * d:] = (dy * gc).astype(BF16)
        acc[0] += _colsum8(dconv * y2)
        acc[1] += _colsum8(dconv * y1)
        acc[2] += _colsum8(dconv * y)

        @pl.when(i == nsteps - 1)
        def _():
            for j in range(3):
                dcw_ref[j:j + 1, :] = jnp.sum(acc[j], axis=0, keepdims=True)

    return pl.pallas_call(
        body, name="conv_bwd", grid=(nsteps,),
        in_specs=[pl.BlockSpec((tq, d3), lambda i: (i, 0)),
                  pl.BlockSpec((8, d3), lambda i: (jnp.maximum(i * (tq // 8) - 1, 0), 0)),
                  pl.BlockSpec((8, d3), lambda i: (jnp.minimum((i + 1) * (tq // 8), nblk8 - 1), 0)),
                  pl.BlockSpec((tq, d), lambda i: (i, 0)),
                  pl.BlockSpec((8, d), lambda i: (jnp.minimum((i + 1) * (tq // 8), nblk8 - 1), 0)),
                  pl.BlockSpec((3, d), lambda i: (0, 0))],
        out_specs=[pl.BlockSpec((tq, d3), lambda i: (i, 0)), pl.BlockSpec((3, d), lambda i: (0, 0))],
        out_shape=[jax.ShapeDtypeStruct((s, d3), BF16), jax.ShapeDtypeStruct((3, d), F32)],
        scratch_shapes=[pltpu.VMEM((3, 8, d), F32)],
        compiler_params=_cparams("arbitrary"),
    )(z, z, z, da, da, cw)


def _swiglu_epilogue(gate, up):
    return gate, up, gate / (1.0 + jnp.exp(-gate)) * up


def _swiglu_bwd_epilogue(dact, gt, up):
    g, u = gt.astype(F32), up.astype(F32)
    sg = 1.0 / (1.0 + jnp.exp(-g))
    return dact * u * (sg * (1.0 + g * (1.0 - sg))), dact * (g * sg)


def _ada_fwd(c_all, ada_w):
    nl, d, n8 = ada_w.shape
    tn = _tile(n8, 768, 128)

    def body(c_ref, w_ref, o_ref):
        cv = c_ref[...]
        act = (cv / (1.0 + jnp.exp(-cv))).astype(BF16)
        o_ref[0] = _dot_nn(act, w_ref[0].astype(BF16))

    return pl.pallas_call(
        body, name="ada_fwd", grid=(nl, n8 // tn),
        in_specs=[pl.BlockSpec((N_DEV, d), lambda l, j: (0, 0)), pl.BlockSpec((1, d, tn), lambda l, j: (l, 0, j))],
        out_specs=pl.BlockSpec((1, N_DEV, tn), lambda l, j: (l, 0, j)),
        out_shape=jax.ShapeDtypeStruct((nl, N_DEV, n8), F32),
        compiler_params=_cparams("parallel", "parallel"),
    )(c_all, ada_w)


def _ada_wgrad(c_all, dmod_cols):
    nl, _, n8 = dmod_cols.shape
    d = c_all.shape[1]
    tn = _tile(n8, 768, 128)

    def body(c_ref, g_ref, o_ref):
        cv = c_ref[...]
        act = (cv / (1.0 + jnp.exp(-cv))).astype(BF16)
        o_ref[0] = _dot_tn(act, g_ref[0].astype(BF16))

    return pl.pallas_call(
        body, name="ada_wgrad", grid=(nl, n8 // tn),
        in_specs=[pl.BlockSpec((N_DEV, d), lambda l, j: (0, 0)), pl.BlockSpec((1, N_DEV, tn), lambda l, j: (l, 0, j))],
        out_specs=pl.BlockSpec((1, d, tn), lambda l, j: (l, 0, j)),
        out_shape=jax.ShapeDtypeStruct((nl, d, n8), F32),
        compiler_params=_cparams("parallel", "parallel"),
    )(c_all, dmod_cols)


def _adamw(name, pieces, w, m, v):
    npc, r, c = pieces.shape
    tr = _tile(r, max(8, (1 << 19) // c // 8 * 8), 8)
    bc1, bc2 = 1.0 - ADAM_B1 ** ADAM_STEP, 1.0 - ADAM_B2 ** ADAM_STEP

    def body(p_ref, w_ref, m_ref, v_ref, g_ref, d_ref, nm_ref, nv_ref):
        g = p_ref[0].astype(F32)
        for i in range(1, npc):
            g = g + p_ref[i].astype(F32)
        nm = ADAM_B1 * m_ref[...] + (1.0 - ADAM_B1) * g
        nv = ADAM_B2 * v_ref[...] + (1.0 - ADAM_B2) * (g * g)
        g_ref[...] = g
        nm_ref[...] = nm
        nv_ref[...] = nv
        d_ref[...] = -ADAM_LR * ((nm / bc1) / (jnp.sqrt(nv / bc2) + ADAM_EPS) + ADAM_WD * w_ref[...])

    blk = pl.BlockSpec((tr, c), lambda i: (i, 0))
    return pl.pallas_call(
        body, name=name, grid=(r // tr,),
        in_specs=[pl.BlockSpec((npc, tr, c), lambda i: (0, i, 0)), blk, blk, blk],
        out_specs=[blk] * 4, out_shape=[jax.ShapeDtypeStruct((r, c), F32)] * 4,
        compiler_params=_cparams("parallel"),
    )(pieces, w, m, v)


def _place():
    x, y, c = lax.axis_index("x"), lax.axis_index("y"), lax.axis_index("c")
    return x, y, c


def _all_gather_small(name, x_shard):
    m_per, n = x_shard.shape

    def body(x_ref, out_ref, token_ref, send_sems, recv_sems, local_sem):
        token_ref[...] = jnp.zeros_like(token_ref)
        x, y, c = _place()
        me, sibling = (x, y, c), (x, y, 1 - c)
        chips = [(1 - x, y), (x, 1 - y), (1 - x, 1 - y)]

        def rows(px, py, pc):
            return out_ref.at[pl.ds((4 * px + 2 * py + pc) * m_per, m_per), :]

        def copy(k, block, to, src=None):
            return pltpu.make_async_remote_copy(
                src_ref=rows(*block) if src is None else src, dst_ref=rows(*block),
                send_sem=send_sems.at[k], recv_sem=recv_sems.at[k], device_id=to, device_id_type=MESH)

        mine = pltpu.make_async_copy(x_ref, rows(*me), local_sem)
        mine.start()
        first = [copy(0, me, sibling, src=x_ref)]
        first += [copy(1 + j, me, (*chip, c), src=x_ref) for j, chip in enumerate(chips)]
        for cp in first:
            cp.start()
        passed = [copy(4 + j, (*chip, c), sibling) for j, chip in enumerate(chips)]
        for j, chip in enumerate(chips):
            copy(1 + j, (*chip, c), me).wait_recv()
            passed[j].start()
        copy(0, sibling, me).wait_recv()
        for j, chip in enumerate(chips):
            copy(4 + j, (*chip, 1 - c), me).wait_recv()
        for cp in first + passed:
            cp.wait_send()
        mine.wait()

    vmem = pl.BlockSpec(memory_space=pltpu.VMEM)
    return pl.pallas_call(
        body, name=name,
        out_shape=[jax.ShapeDtypeStruct((N_DEV * m_per, n), x_shard.dtype), jax.ShapeDtypeStruct((8, HEAD), F32)],
        in_specs=[vmem], out_specs=[vmem, vmem],
        scratch_shapes=[pltpu.SemaphoreType.DMA((7,)), pltpu.SemaphoreType.DMA((7,)), pltpu.SemaphoreType.DMA],
        compiler_params=pltpu.CompilerParams(vmem_limit_bytes=VMEM_LIMIT),
    )(x_shard)


_HBM = pl.BlockSpec(memory_space=pltpu.HBM)
_SEM = pl.BlockSpec(memory_space=pltpu.SEMAPHORE)
_EFFECT = pltpu.SideEffectType.DATAFLOW_SIDE_EFFECTING


def _peers():
    x, y, c = _place()
    peers = []
    for k in range(1, N_DEV):
        px = 1 - x if k & 4 else x
        py = 1 - y if k & 2 else y
        pc = 1 - c if k & 1 else c
        peers.append(((px, py, pc), 4 * px + 2 * py + pc))
    return 4 * x + 2 * y + c, peers


def _push_start(name, srcs, lands, src_view, dst_view):
    na = len(srcs)
    n = na * (N_DEV - 1)

    def body(*refs):
        s_refs, l_refs = refs[:na], refs[na:2 * na]
        send_sems, recv_sems, token = refs[2 * na], refs[2 * na + 1], refs[-1]
        me, peers = _peers()
        for a in range(na):
            for k, (dev, idx) in enumerate(peers):
                pltpu.make_async_remote_copy(
                    src_ref=src_view(s_refs[a], idx), dst_ref=dst_view(l_refs[a], me),
                    send_sem=send_sems.at[a * (N_DEV - 1) + k], recv_sem=recv_sems.at[a * (N_DEV - 1) + k],
                    device_id=dev, device_id_type=MESH).start()
        token[...] = jnp.zeros_like(token)

    outs = pl.pallas_call(
        body, name=name,
        out_shape=(pltpu.SemaphoreType.DMA((n,)), pltpu.SemaphoreType.DMA((n,)),
                   *[pltpu.HBM(t.shape, t.dtype) for t in list(srcs) + list(lands)], jax.ShapeDtypeStruct((8, HEAD), F32)),
        in_specs=[_HBM] * (2 * na),
        out_specs=(_SEM, _SEM, *[_HBM] * (2 * na), pl.BlockSpec(memory_space=pltpu.VMEM)),
        input_output_aliases={i: 2 + i for i in range(2 * na)},
        compiler_params=pltpu.CompilerParams(has_side_effects=_EFFECT),
    )(*[pltpu.with_memory_space_constraint(t, pltpu.HBM) for t in list(srcs) + list(lands)])
    return outs[0], outs[1], list(outs[2:2 + na]), list(outs[2 + na:2 + 2 * na]), outs[-1]


def _push_wait(name, send_sems, recv_sems, srcs, lands, after, src_view, dst_view):
    na = len(srcs)

    def body(*refs):
        s_refs, l_refs = refs[:na], refs[na:2 * na]
        send_sems, recv_sems = refs[2 * na], refs[2 * na + 1]
        me, peers = _peers()
        for a in range(na):
            for k, (dev, idx) in enumerate(peers):
                cp = pltpu.make_async_remote_copy(
                    src_ref=src_view(s_refs[a], idx), dst_ref=dst_view(l_refs[a], idx),
                    send_sem=send_sems.at[a * (N_DEV - 1) + k], recv_sem=recv_sems.at[a * (N_DEV - 1) + k],
                    device_id=dev, device_id_type=MESH)
                cp.wait_send()
                cp.wait_recv()

    outs = pl.pallas_call(
        body, name=name,
        out_shape=[pltpu.HBM(t.shape, t.dtype) for t in list(srcs) + list(lands)],
        in_specs=[_HBM] * (2 * na) + [_SEM, _SEM, pl.BlockSpec(memory_space=pl.ANY)],
        out_specs=[_HBM] * (2 * na),
        input_output_aliases={i: i for i in range(2 * na)},
        compiler_params=pltpu.CompilerParams(has_side_effects=_EFFECT),
    )(*srcs, *lands, send_sems, recv_sems, after)
    return list(outs[na:])


def _gather_start(name, shards):
    lands = [lax.empty((N_DEV,) + t.shape, t.dtype) for t in shards]
    return _push_start(name, shards, lands, lambda ref, idx: ref, lambda ref, slot: ref.at[slot])


def _gather_wait(name, started, shards, after, me):
    send_sems, recv_sems, srcs, lands, _ = started
    lands = _push_wait(name, send_sems, recv_sems, srcs, lands, after, lambda ref, idx: ref, lambda ref, slot: ref.at[slot])
    return [lax.dynamic_update_index_in_dim(g, t, me, 0) for g, t in zip(lands, shards)]


def _plain_to_cols(w):
    k, n = w.shape
    return jnp.transpose(w.reshape(k, N_DEV, n // N_DEV), (1, 0, 2))


def _pack_rows(parts, width):
    rows, offs, r = [], [], 0
    for p in parts:
        flat = p.reshape(-1).astype(F32)
        nr = -(-flat.shape[0] // (8 * width)) * 8
        rows.append(jnp.pad(flat, (0, nr * width - flat.shape[0])).reshape(nr, width))
        offs.append((r, flat.shape[0], p.shape))
        r += nr
    return jnp.concatenate(rows, axis=0), offs, r


def _unpack_rows(slab, offs, width):
    lead = slab.shape[:-2]
    out = []
    for r0, n, shape in offs:
        nr = -(-n // width)
        out.append(slab[..., r0:r0 + nr, :].reshape(lead + (nr * width,))[..., :n].reshape(lead + tuple(shape)))
    return out


def kernel(x, c, positions, ada_w, ada_b, norm_mix, norm_ffn, ab_w_in, sgu_w, sgu_b, ab_w_out, conv_w_in, conv_w, conv_w_out, ffn_w_gate, ffn_w_up, ffn_w_down, final_norm, loss_target, m_ada_w, m_ada_b, m_norm_mix, m_norm_ffn, m_ab_w_in, m_sgu_w, m_sgu_b, m_ab_w_out, m_conv_w_in, m_conv_w, m_conv_w_out, m_ffn_w_gate, m_ffn_w_up, m_ffn_w_down, m_final_norm, v_ada_w, v_ada_b, v_norm_mix, v_norm_ffn, v_ab_w_in, v_sgu_w, v_sgu_b, v_ab_w_out, v_conv_w_in, v_conv_w, v_conv_w_out, v_ffn_w_gate, v_ffn_w_up, v_ffn_w_down, v_final_norm):
    xi, yi, ci = _place()
    me = 4 * xi + 2 * yi + ci
    s, d = x.shape[1], x.shape[2]
    depth = ada_w.shape[0]
    n_even = ab_w_in.shape[0]
    nh_mix = d // HEAD
    nh = 3 * nh_mix // 4
    ng = nh_mix - nh
    aw, gw = nh * HEAD, ng * HEAD
    assert d <= FULL_ROW and s % ATT_CHUNK == 0
    x0 = x[0]
    target = loss_target[0]
    n_odd, cwid, d8 = conv_w.shape

    width = 512
    slab, offs, _ = _pack_rows([c, conv_w], width)
    gathered, _ = _all_gather_small("gather_cond", slab)
    c_parts, cw_parts = _unpack_rows(gathered.reshape(N_DEV, -1, width), offs, width)
    c_all = c_parts.reshape(N_DEV, d)
    conv_w_full = jnp.transpose(cw_parts, (1, 2, 0, 3)).reshape(n_odd, cwid, d)

    mod_cols = _ada_fwd(c_all, ada_w)
    n8 = mod_cols.shape[2]
    mod_all, token = _all_gather_small("gather_mod", mod_cols.reshape(depth * N_DEV, n8))
    mod_mine = lax.dynamic_index_in_dim(mod_all.reshape(N_DEV, depth, N_DEV, n8), me, axis=2, keepdims=False)

    def cols(w):
        return jnp.swapaxes(w, 0, 1)

    def mixer_weights(l):
        w_i, w_o = (ab_w_in, ab_w_out) if l % 2 == 0 else (conv_w_in, conv_w_out)
        return cols(w_i[l // 2]), w_o[l // 2]

    def ffn_weights(l):
        return [cols(ffn_w_gate[l]), cols(ffn_w_up[l]), ffn_w_down[l]]

    groups = [[mixer_weights(0)[0]], [mixer_weights(0)[1]], ffn_weights(0)]
    for l in range(1, depth):
        groups += [list(mixer_weights(l)), ffn_weights(l)]
    gathers, tok = [], token[0, 0]
    for n, ws in enumerate(groups):
        shards = [(w + tok).astype(BF16) for w in ws]
        started = _gather_start(f"gather_start_{n}", shards)
        gathers.append((started, shards))
        tok = started[4][0, 0]

    def weights_of_group(n, after):
        started, shards = gathers[n]
        return _gather_wait(f"gather_wait_{n}", started, shards, after, me)

    def plain_rows(g):
        return g.reshape(g.shape[0] * g.shape[1], g.shape[2])

    mod = jnp.transpose(mod_mine, (1, 0, 2)).reshape(depth, N_DEV * n8) + ada_b + tok
    mods = mod.reshape(depth, 6, 1, d)

    ct, st = _rope_tables(positions.reshape(s, 1))
    b_col = jnp.broadcast_to(sgu_b[..., None], sgu_b.shape + (CHUNK,))
    u_blk = 3 * aw // gw

    stream = [x0]
    saved = []
    w_in, w_out, w_gate, w_up, w_down = [[None] * depth for _ in range(5)]
    xcur = x0
    for l in range(depth):
        sh_m, sc_m, g_m, sh_f, sc_f, g_f = [mods[l, j] for j in range(6)]
        i = l // 2
        if l == 0:
            (g_in,) = weights_of_group(0, mod)
        else:
            g_in, g_out = weights_of_group(1 + 2 * l, xcur)
        w_in[l] = plain_rows(g_in)
        h = _norm_mod_fwd(xcur, norm_mix[l][None], sc_m, sh_m)
        if l % 2 == 0:
            z = _matmul("mix_in", "nt", h, w_in[l], [BF16])
            attn, lse = _attn_fwd(z, ct, st, nh)
            sgu = _sgu_fwd(z, sgu_w[i], b_col[i], ng, u_blk)
            a = jnp.concatenate([attn, sgu], axis=1)
            mixer_saved = (z, a, lse)
        else:
            z = _matmul("conv_in", "nt", h, w_in[l], [BF16])
            a = _conv_fwd(z, conv_w_full[i])
            mixer_saved = (z, a, None)
        if l == 0:
            (g_out,) = weights_of_group(1, a)
        w_out[l] = plain_rows(g_out)
        x1, mix, h2 = _matmul("mix_out", "nn", a, w_out[l], [F32, BF16, BF16],
                              extras=[(xcur, "mn"), (g_m, "n"), (norm_ffn[l][None], "n"), (sc_f, "n"), (sh_f, "n")],
                              epilogue=_gated_add_norm_epilogue)
        g_gate, g_up, g_down = weights_of_group(2 + 2 * l, x1)
        w_gate[l], w_up[l], w_down[l] = plain_rows(g_gate), plain_rows(g_up), plain_rows(g_down)
        gt, up, act = _matmul("ffn_in", "nt", h2, [w_gate[l], w_up[l]], [BF16, BF16, BF16], epilogue=_swiglu_epilogue)
        x2, f = _matmul("ffn_down", "nn", act, w_down[l], [F32, BF16], extras=[(x1, "mn"), (g_f, "n")],
                        epilogue=lambda acc, r, gv: (r + gv * acc, acc))
        saved.append((h, mixer_saved, mix, x1, h2, gt, up, act, f))
        stream.append(x2)
        xcur = x2

    f_last = saved[-1][8]
    loss_part, dx, dbr, d_final, dg = _final_loss(xcur, target, final_norm[None], f_last, mods[depth - 1, 5])
    loss = lax.psum(loss_part[0, 0], ("x", "y", "c"))

    dmod = [[None] * 6 for _ in range(depth)]
    d_norm_mix, d_norm_ffn = [None] * depth, [None] * depth
    d_sgu_w, d_sgu_b, d_conv_w = [None] * n_even, [None] * n_even, [None] * n_odd

    big = {"in": ab_w_in, "out": ab_w_out, "cin": conv_w_in, "cout": conv_w_out,
           "gate": ffn_w_gate, "up": ffn_w_up, "down": ffn_w_down}
    lands = {k: lax.empty((N_DEV,) + w.shape, BF16) for k, w in big.items()}
    own = {k: [None] * w.shape[0] for k, w in big.items()}
    pending = {"ffn": None, "mix": None}

    def exchange_finish(tag, after):
        (send_sems, recv_sems, srcs, lds, _), keys, li, layer = pending[tag]
        lds = _push_wait(f"exchange_wait_{tag}_{layer}", send_sems, recv_sems, srcs, lds, after,
                         lambda ref, idx: ref.at[idx], lambda ref, slot: ref.at[slot, li])
        for k, ld in zip(keys, lds):
            lands[k] = ld
        pending[tag] = None

    def exchange_start(tag, layer, keys, li, grads):
        if pending[tag] is not None:
            exchange_finish(tag, grads[0])
        for k, g in zip(keys, grads):
            own[k][li] = lax.dynamic_index_in_dim(g, me, 0, keepdims=False)
        started = _push_start(f"exchange_start_{tag}_{layer}", grads, [lands[k] for k in keys],
                              lambda ref, idx: ref.at[idx], lambda ref, slot: ref.at[slot, li])
        pending[tag] = (started, keys, li, layer)
        return started[4][0, 0]

    def row_shards(g):
        return g.reshape(N_DEV, g.shape[0] // N_DEV, g.shape[1])

    for l in reversed(range(depth)):
        sh_m, sc_m, g_m, sh_f, sc_f, g_f = [mods[l, j] for j in range(6)]
        h, (z, a, lse), mix, x1, h2, gt, up, act, f = saved[l]
        i = l // 2
        dmod[l][5] = dg
        gw_down = _matmul("ffn_down_wgrad", "tn", act, dbr, [BF16])
        dgt, dup = _matmul("ffn_down_dgrad", "nt", dbr, w_down[l], [BF16, BF16], extras=[(gt, "mn"), (up, "mn")],
                           epilogue=_swiglu_bwd_epilogue)
        gw_gate = _matmul("ffn_in_wgrad", "tn", h2, dgt, [BF16])
        gw_up = _matmul("ffn_in_wgrad", "tn", h2, dup, [BF16])
        tok = exchange_start("ffn", l, ("gate", "up", "down"), l, [_plain_to_cols(gw_gate), _plain_to_cols(gw_up), row_shards(gw_down)])
        dh2 = _matmul("ffn_in_dgrad", "nn", [dgt, dup], [w_gate[l], w_up[l]], [BF16], after=tok)
        dx, dbr, dmod[l][3], dmod[l][4], d_norm_ffn[l], dg = _norm_mod_bwd(x1, dh2, norm_ffn[l][None], sc_f, dx, mix, g_m)
        dmod[l][2] = dg
        gw_out = _matmul("mix_out_wgrad", "tn", a, dbr, [BF16])
        da = _matmul("mix_out_dgrad", "nt", dbr, w_out[l], [BF16])
        if l % 2 == 0:
            dq, dk, dv = _attn_bwd(z, ct, st, da, a, lse, nh)
            du, dvg, d_sgu_w[i], dbb = _sgu_bwd(z, sgu_w[i], b_col[i], da, ng, u_blk, aw // gw)
            d_sgu_b[i] = dbb[:, :, 0]
            dz = jnp.concatenate([dq, dk, dv, du, dvg], axis=1)
            gw_in = _matmul("mix_in_wgrad", "tn", h, dz, [BF16])
            dh = _matmul("mix_in_dgrad", "nn", dz, w_in[l], [BF16])
        else:
            dz, d_conv_w[i] = _conv_bwd(z, conv_w_full[i], da)
            gw_in = _matmul("conv_in_wgrad", "tn", h, dz, [BF16])
            dh = _matmul("conv_in_dgrad", "nn", dz, w_in[l], [BF16])
        mix_group = ("mix", l, ("in", "out") if l % 2 == 0 else ("cin", "cout"), i, [_plain_to_cols(gw_in), row_shards(gw_out)])
        w_norm = norm_mix[l][None]
        if l > 0:
            w_norm = w_norm + exchange_start(*mix_group)
            f_prev, g_prev = saved[l - 1][8], mods[l - 1, 5]
            dx, dbr, dmod[l][0], dmod[l][1], d_norm_mix[l], dg = _norm_mod_bwd(stream[l], dh, w_norm, sc_m, dx, f_prev, g_prev)
        else:
            dx, dmod[l][0], dmod[l][1], d_norm_mix[l] = _norm_mod_bwd(stream[l], dh, w_norm, sc_m, dx)
    grad_x = dx[None]

    dmod_mine = jnp.stack([jnp.concatenate([v.reshape(d) for v in dmod[l]]) for l in range(depth)])
    small = [dmod_mine, jnp.concatenate(d_norm_mix), jnp.concatenate(d_norm_ffn), jnp.stack(d_sgu_w), jnp.stack(d_sgu_b),
             d_final, jnp.stack(d_conv_w)]
    slab, offs, _ = _pack_rows(small, width)
    gathered, token = _all_gather_small("gather_small_grads", slab)
    p_dmod, p_nmix, p_nffn, p_sguw, p_sgub, p_final, p_convw = _unpack_rows(gathered.reshape(N_DEV, -1, width), offs, width)
    mix_group[4][0] = mix_group[4][0] + token[0, 0].astype(BF16)
    p_dmod = p_dmod + exchange_start(*mix_group)

    outs = {}

    def update(name, pieces, w, m, v):
        shape = w.shape
        cdim = shape[-1]
        res = _adamw("adamw_" + name, pieces.reshape(pieces.shape[0], -1, cdim), w.reshape(-1, cdim),
                     m.reshape(-1, cdim), v.reshape(-1, cdim))
        outs[name] = [r.reshape(shape) for r in res]

    update("ada_b", p_dmod.reshape(N_DEV, depth, 6 * d), ada_b, m_ada_b, v_ada_b)
    update("norm_mix", p_nmix.reshape(N_DEV, depth, d), norm_mix, m_norm_mix, v_norm_mix)
    update("norm_ffn", p_nffn.reshape(N_DEV, depth, d), norm_ffn, m_norm_ffn, v_norm_ffn)
    update("sgu_w", p_sguw, sgu_w, m_sgu_w, v_sgu_w)
    update("sgu_b", p_sgub.reshape(N_DEV, 1, -1), sgu_b.reshape(1, -1), m_sgu_b.reshape(1, -1), v_sgu_b.reshape(1, -1))
    outs["sgu_b"] = [r.reshape(sgu_b.shape) for r in outs["sgu_b"]]
    update("final_norm", p_final.reshape(N_DEV, 1, d), final_norm[None], m_final_norm[None], v_final_norm[None])
    outs["final_norm"] = [r.reshape(final_norm.shape) for r in outs["final_norm"]]
    cw_mine = lax.dynamic_slice_in_dim(p_convw.reshape(N_DEV, n_odd, cwid, d), me * d8, d8, axis=3)
    update("conv_w", cw_mine, conv_w, m_conv_w, v_conv_w)

    dmod_cols = lax.dynamic_slice_in_dim(p_dmod.reshape(N_DEV, depth, 6 * d), me * n8, n8, axis=2)
    g_ada = _ada_wgrad(c_all, jnp.transpose(dmod_cols, (1, 0, 2)))
    update("ada_w", g_ada[None], ada_w, m_ada_w, v_ada_w)

    names = {"in": "ab_w_in", "out": "ab_w_out", "cin": "conv_w_in", "cout": "conv_w_out",
             "gate": "ffn_w_gate", "up": "ffn_w_up", "down": "ffn_w_down"}
    moments = {"in": (m_ab_w_in, v_ab_w_in), "out": (m_ab_w_out, v_ab_w_out), "cin": (m_conv_w_in, v_conv_w_in),
               "cout": (m_conv_w_out, v_conv_w_out), "gate": (m_ffn_w_gate, v_ffn_w_gate), "up": (m_ffn_w_up, v_ffn_w_up),
               "down": (m_ffn_w_down, v_ffn_w_down)}

    def update_big(k):
        pieces = lax.dynamic_update_slice(lands[k], jnp.stack(own[k])[None], (me, 0, 0, 0))
        update(names[k], pieces, big[k], *moments[k])

    exchange_finish("ffn", g_ada)
    for k in ("gate", "up", "down", "cin", "cout"):
        update_big(k)
    done = sum(outs[n][0][(0,) * outs[n][0].ndim] for n in outs)
    exchange_finish("mix", done.reshape(1, 1))
    for k in ("in", "out"):
        update_big(k)

    order = ["ada_w", "ada_b", "norm_mix", "norm_ffn", "ab_w_in", "sgu_w", "sgu_b", "ab_w_out", "conv_w_in", "conv_w",
             "conv_w_out", "ffn_w_gate", "ffn_w_up", "ffn_w_down", "final_norm"]
    return (loss, grad_x, *[outs[n][0] for n in order], *[outs[n][1] for n in order],
            *[outs[n][2] for n in order], *[outs[n][3] for n in order])
```

```python
import functools
import math

import numpy as np
import jax
import jax.numpy as jnp
from jax import lax
from jax.experimental import pallas as pl
from jax.experimental.pallas import tpu as pltpu

F32, BF16 = jnp.float32, jnp.bfloat16
MESH = pl.DeviceIdType.MESH
N_DEV = 8
EPS = 1e-6
HEAD = 128
CHUNK = 128
DILATIONS = (1, 4, 16)
ATT_CHUNK = CHUNK * DILATIONS[-1]
ATT_BATCH = 8
ROPE_THETA = 500000.0
ROPE_DIM = HEAD // 4
NEG = -1e30
ADAM_LR, ADAM_B1, ADAM_B2, ADAM_EPS, ADAM_WD, ADAM_STEP = 0.001, 0.9, 0.999, 1e-08, 0.01, 10
VMEM_LIMIT = 56 * 1024 * 1024


def _cparams(*sem):
    return pltpu.CompilerParams(dimension_semantics=sem or None, vmem_limit_bytes=VMEM_LIMIT)


def _tile(n, pref, unit):
    t = (min(pref, n) // unit) * unit
    while t >= unit:
        if n % t == 0:
            return t
        t -= unit
    return n


def _dot(a, b, dims):
    return lax.dot_general(a, b, (dims, ((), ())), preferred_element_type=F32)


def _dot_nn(a, b):
    return _dot(a, b, ((1,), (0,)))


def _dot_nt(a, b):
    return _dot(a, b, ((1,), (1,)))


def _dot_tn(a, b):
    return _dot(a, b, ((0,), (0,)))


FULL_ROW = 2048
_TILE_DEFAULT = dict(tm=1024, tn=512, tk=2816)
_TILES = {
    "ffn_down": dict(tn=1024),
    "ffn_down_wgrad": dict(tm=1408, tn=1024),
    "ffn_in_wgrad": dict(tm=1408, tn=1024), "mix_in_wgrad": dict(tm=1408, tn=1024), "conv_in_wgrad": dict(tm=1536, tn=1024),
    "mix_in_dgrad": dict(tn=1024, tk=1408), "conv_in_dgrad": dict(tn=1024, tk=1536),
    "mix_out": dict(tm=512, tn=FULL_ROW),
    "mix_out_wgrad": dict(tn=1024), "mix_out_dgrad": dict(tn=1024),
}


def _matmul(name, mode, a, b, outs, *, extras=(), epilogue=None, after=None):
    a_list = list(a) if isinstance(a, (list, tuple)) else [a]
    b_list = list(b) if isinstance(b, (list, tuple)) else [b]
    na, nb = len(a_list), len(b_list)
    paired = na > 1
    assert na == nb if paired else na == 1
    nacc = 1 if paired else nb
    a0, b0 = a_list[0], b_list[0]
    if mode == "nn":
        (m, kk), (_, n) = a0.shape, b0.shape
    elif mode == "nt":
        (m, kk), (n, _) = a0.shape, b0.shape
    else:
        (kk, m), (_, n) = a0.shape, b0.shape
    if after is not None:
        inner = epilogue or (lambda *accs: accs)
        extras = list(extras) + [(jnp.zeros((1, n), F32) + after, "n")]
        epilogue = lambda *tiles: inner(*tiles[:-1])
    pref = {**_TILE_DEFAULT, **_TILES.get(name, {})}
    tm, tn, tk = _tile(m, pref["tm"], 128), _tile(n, pref["tn"], 128), _tile(kk, pref["tk"], 128)
    nk = kk // tk
    dotf = {"nn": _dot_nn, "nt": _dot_nt, "tn": _dot_tn}[mode]
    a_spec = pl.BlockSpec((tk, tm), lambda i, j, k: (k, i)) if mode == "tn" else pl.BlockSpec((tm, tk), lambda i, j, k: (i, k))
    b_spec = pl.BlockSpec((tn, tk), lambda i, j, k: (j, k)) if mode == "nt" else pl.BlockSpec((tk, tn), lambda i, j, k: (k, j))
    e_specs = [pl.BlockSpec((tm, tn), lambda i, j, k: (i, j)) if kind == "mn" else pl.BlockSpec((1, tn), lambda i, j, k: (0, j))
               for _, kind in extras]
    ne, no = len(extras), len(outs)
    epi = epilogue or (lambda *accs: accs)

    def body(*refs):
        a_refs, b_refs, rest = refs[:na], refs[na:na + nb], refs[na + nb:]
        e_refs, o_refs, acc_refs = rest[:ne], rest[ne:ne + no], rest[ne + no:]

        def products():
            if paired:
                p = dotf(a_refs[0][...], b_refs[0][...])
                for a_ref, b_ref in zip(a_refs[1:], b_refs[1:]):
                    p = p + dotf(a_ref[...], b_ref[...])
                return [p]
            av = a_refs[0][...]
            return [dotf(av, b_ref[...]) for b_ref in b_refs]

        def finish(accs):
            for o_ref, o in zip(o_refs, epi(*accs, *[r[...] for r in e_refs])):
                o_ref[...] = o.astype(o_ref.dtype)

        if nk == 1:
            finish(products())
            return
        k = pl.program_id(2)

        @pl.when(k == 0)
        def _():
            for acc_ref in acc_refs:
                acc_ref[...] = jnp.zeros_like(acc_ref)

        for acc_ref, p in zip(acc_refs, products()):
            acc_ref[...] += p

        @pl.when(k == nk - 1)
        def _():
            finish([acc_ref[...] for acc_ref in acc_refs])

    res = pl.pallas_call(
        body, name=name, grid=(m // tm, n // tn, nk),
        in_specs=[a_spec] * na + [b_spec] * nb + e_specs,
        out_specs=[pl.BlockSpec((tm, tn), lambda i, j, k: (i, j)) for _ in outs],
        out_shape=[jax.ShapeDtypeStruct((m, n), dt) for dt in outs],
        scratch_shapes=[pltpu.VMEM((tm, tn), F32)] * nacc if nk > 1 else [],
        compiler_params=_cparams("parallel", "parallel", "arbitrary"),
    )(*a_list, *b_list, *[e for e, _ in extras])
    return res[0] if no == 1 else res


def _norm_mod_fwd(x, w, sc, sh):
    s, d = x.shape
    tm = _tile(s, 512, 8)

    def body(x_ref, w_ref, sc_ref, sh_ref, h_ref):
        xv = x_ref[...]
        r = lax.rsqrt(jnp.mean(xv * xv, axis=-1, keepdims=True) + EPS)
        h_ref[...] = ((xv * r) * w_ref[...] * (1.0 + sc_ref[...]) + sh_ref[...]).astype(BF16)

    row = pl.BlockSpec((1, d), lambda i: (0, 0))
    return pl.pallas_call(
        body, name="norm_mod_fwd", grid=(s // tm,),
        in_specs=[pl.BlockSpec((tm, d), lambda i: (i, 0)), row, row, row],
        out_specs=pl.BlockSpec((tm, d), lambda i: (i, 0)),
        out_shape=jax.ShapeDtypeStruct((s, d), BF16),
        compiler_params=_cparams("parallel"),
    )(x, w, sc, sh)


def _gated_add_norm_epilogue(acc, res, g, w, sc, sh):
    xv = res + g * acc
    r = lax.rsqrt(jnp.mean(xv * xv, axis=-1, keepdims=True) + EPS)
    return xv, acc, (xv * r) * w * (1.0 + sc) + sh


def _colsum8(t):
    tm, d = t.shape
    return jnp.sum(t.reshape(tm // 8, 8, d), axis=0)


def _norm_mod_bwd(x, dh, w, sc, dres, branch=None, g=None):
    s, d = x.shape
    tm = _tile(s, 256, 8)
    nsteps = s // tm
    gated = branch is not None

    def body(*refs):
        if gated:
            x_ref, dh_ref, w_ref, sc_ref, dres_ref, br_ref, g_ref, dx_ref, dbr_ref, dsh_ref, dsc_ref, dw_ref, dg_ref, acc = refs
        else:
            x_ref, dh_ref, w_ref, sc_ref, dres_ref, dx_ref, dsh_ref, dsc_ref, dw_ref, acc = refs
        i = pl.program_id(0)

        @pl.when(i == 0)
        def _():
            acc[...] = jnp.zeros_like(acc)

        xv, dhv, wv, scv = x_ref[...], dh_ref[...].astype(F32), w_ref[...], sc_ref[...]
        r = lax.rsqrt(jnp.mean(xv * xv, axis=-1, keepdims=True) + EPS)
        xn = xv * r
        dxn = dhv * (wv * (1.0 + scv))
        dx = dres_ref[...] + r * (dxn - xn * jnp.mean(dxn * xn, axis=-1, keepdims=True))
        dx_ref[...] = dx
        acc[0] += _colsum8(dhv)
        acc[1] += _colsum8(dhv * xn)
        if gated:
            dbr_ref[...] = (dx * g_ref[...]).astype(BF16)
            acc[2] += _colsum8(dx * br_ref[...].astype(F32))

        @pl.when(i == nsteps - 1)
        def _():
            a0 = jnp.sum(acc[0], axis=0, keepdims=True)
            a1 = jnp.sum(acc[1], axis=0, keepdims=True)
            dsh_ref[...] = a0
            dsc_ref[...] = a1 * wv
            dw_ref[...] = a1 * (1.0 + scv)
            if gated:
                dg_ref[...] = jnp.sum(acc[2], axis=0, keepdims=True)

    big = pl.BlockSpec((tm, d), lambda i: (i, 0))
    row = pl.BlockSpec((1, d), lambda i: (0, 0))
    rowo = jax.ShapeDtypeStruct((1, d), F32)
    in_specs = [big, big, row, row, big] + ([big, row] if gated else [])
    out_specs = [big] + ([big] if gated else []) + [row, row, row] + ([row] if gated else [])
    out_shape = ([jax.ShapeDtypeStruct((s, d), F32)] + ([jax.ShapeDtypeStruct((s, d), BF16)] if gated else [])
                 + [rowo, rowo, rowo] + ([rowo] if gated else []))
    args = [x, dh, w, sc, dres] + ([branch, g] if gated else [])
    return pl.pallas_call(
        body, name="norm_mod_bwd_gated" if gated else "norm_mod_bwd", grid=(nsteps,),
        in_specs=in_specs, out_specs=out_specs, out_shape=out_shape,
        scratch_shapes=[pltpu.VMEM((3, 8, d), F32)],
        compiler_params=_cparams("arbitrary"),
    )(*args)


def _final_loss(x, target, w, branch, g):
    s, d = x.shape
    tm = _tile(s, 256, 8)
    nsteps = s // tm

    def body(x_ref, t_ref, w_ref, br_ref, g_ref, loss_ref, dx_ref, dbr_ref, dw_ref, dg_ref, acc):
        i = pl.program_id(0)

        @pl.when(i == 0)
        def _():
            acc[...] = jnp.zeros_like(acc)

        xv, wv = x_ref[...], w_ref[...]
        r = lax.rsqrt(jnp.mean(xv * xv, axis=-1, keepdims=True) + EPS)
        xn = xv * r
        err = xn * wv - t_ref[...]
        dy = err * (1.0 / d)
        dxn = dy * wv
        dx = r * (dxn - xn * jnp.mean(dxn * xn, axis=-1, keepdims=True))
        dx_ref[...] = dx
        dbr_ref[...] = (dx * g_ref[...]).astype(BF16)
        acc[0] += _colsum8(err * err)
        acc[1] += _colsum8(dy * xn)
        acc[2] += _colsum8(dx * br_ref[...].astype(F32))

        @pl.when(i == nsteps - 1)
        def _():
            loss_ref[...] = jnp.sum(jnp.sum(acc[0], axis=0, keepdims=True), axis=1, keepdims=True) * (0.5 / d)
            dw_ref[...] = jnp.sum(acc[1], axis=0, keepdims=True)
            dg_ref[...] = jnp.sum(acc[2], axis=0, keepdims=True)

    big = pl.BlockSpec((tm, d), lambda i: (i, 0))
    row = pl.BlockSpec((1, d), lambda i: (0, 0))
    rowo = jax.ShapeDtypeStruct((1, d), F32)
    return pl.pallas_call(
        body, name="final_loss", grid=(nsteps,),
        in_specs=[big, big, row, big, row],
        out_specs=[pl.BlockSpec((1, 1), lambda i: (0, 0)), big, big, row, row],
        out_shape=[jax.ShapeDtypeStruct((1, 1), F32), jax.ShapeDtypeStruct((s, d), F32),
                   jax.ShapeDtypeStruct((s, d), BF16), rowo, rowo],
        scratch_shapes=[pltpu.VMEM((3, 8, d), F32)],
        compiler_params=_cparams("arbitrary"),
    )(x, target, w, branch, g)


def _rope_tables(pos_col):
    s = pos_col.shape[0]
    tq = _tile(s, 1024, 8)
    half = ROPE_DIM // 2
    inv = np.float32(ROPE_THETA) ** (-np.arange(0, ROPE_DIM, 2, dtype=np.float32) / np.float32(ROPE_DIM))
    inv_row = jnp.asarray(np.tile(inv.astype(np.float32), HEAD // half)[None, :])

    def body(p_ref, inv_ref, ct_ref, st_ref):
        lane = lax.broadcasted_iota(jnp.int32, (tq, HEAD), 1)
        ang = p_ref[...].astype(F32) * inv_ref[...]
        cs, sn = jnp.cos(ang), jnp.sin(ang)
        ct_ref[...] = jnp.where(lane < ROPE_DIM, cs, 1.0)
        st_ref[...] = jnp.where(lane < half, -sn, jnp.where(lane < ROPE_DIM, sn, 0.0))

    blk = pl.BlockSpec((tq, HEAD), lambda i: (i, 0))
    return pl.pallas_call(
        body, name="rope_tables", grid=(s // tq,),
        in_specs=[pl.BlockSpec((tq, 1), lambda i: (i, 0)), pl.BlockSpec((1, HEAD), lambda i: (0, 0))],
        out_specs=[blk, blk],
        out_shape=[jax.ShapeDtypeStruct((s, HEAD), F32)] * 2,
        compiler_params=_cparams("parallel"),
    )(pos_col, inv_row)


def _swap_halves(x):
    lane = lax.broadcasted_iota(jnp.int32, x.shape, 1)
    half = ROPE_DIM // 2
    return jnp.where(lane < half, pltpu.roll(x, HEAD - half, 1), pltpu.roll(x, half, 1))


def _rope(x, ct, st):
    return x * ct + _swap_halves(x) * st


def _rope_t(dy, ct, st):
    return dy * ct - _swap_halves(dy) * st


def _band_bias(bias_ref):
    qi = lax.broadcasted_iota(jnp.int32, (CHUNK, 2 * CHUNK), 0)
    kj = lax.broadcasted_iota(jnp.int32, (CHUNK, 2 * CHUNK), 1)
    band = (kj >= qi) & (kj <= qi + CHUNK)
    bias_ref[0] = jnp.where(band, 0.0, NEG)
    bias_ref[1] = jnp.where(band & (kj >= CHUNK), 0.0, NEG)


def _rows(start, size, d):
    return pl.ds(start, size) if d == 1 else pl.ds(start, size, stride=d)


def _batch_units(ub, d, c):
    n_sb = ATT_BATCH // d
    out = []
    for sb, r in [(ub * n_sb + t, r) for t in range(n_sb) for r in range(d)]:
        base = pl.multiple_of(sb * (CHUNK * d), CHUNK)
        first = jnp.where((c == 0) & (sb == 0), 1, 0)
        out.append((_rows(base + r, CHUNK, d), _rows(base + (ATT_CHUNK - CHUNK * d + r), 2 * CHUNK, d), first))
    return out


def _for_batches(d, fn):
    def step(ub, carry):
        fn(ub)
        return carry
    lax.fori_loop(0, ATT_CHUNK // CHUNK // ATT_BATCH, step, 0)


def _bdot(a, b, ca, cb):
    return lax.dot_general(a, b, (((ca,), (cb,)), ((0,), (0,))), preferred_element_type=F32)


WIDE = DILATIONS[-1]
assert ATT_CHUNK == CHUNK * WIDE


def _deint(x):
    return jnp.swapaxes(x.reshape(CHUNK, WIDE, HEAD), 0, 1).reshape(ATT_CHUNK, HEAD)


def _reint(x):
    return jnp.swapaxes(x.reshape(WIDE, CHUNK, HEAD), 0, 1).reshape(ATT_CHUNK, HEAD)


def _blk(r):
    return pl.ds(r * CHUNK, CHUNK)


def _key_rows(ref, r):
    return jnp.concatenate([ref[_blk(r), :], ref[pl.ds(ATT_CHUNK + r * CHUNK, CHUNK), :]], axis=0)


def _attn_specs(nh, nc):
    cur = lambda off: pl.BlockSpec((ATT_CHUNK, HEAD), lambda h, c: (jnp.minimum(c, nc - 1), off + h))
    prev = lambda off: pl.BlockSpec((ATT_CHUNK, HEAD), lambda h, c: (jnp.maximum(c - 1, 0), off + h))
    tcur = pl.BlockSpec((ATT_CHUNK, HEAD), lambda h, c: (jnp.minimum(c, nc - 1), 0))
    tprev = pl.BlockSpec((ATT_CHUNK, HEAD), lambda h, c: (jnp.maximum(c - 1, 0), 0))
    return [cur(0), prev(nh), cur(nh), prev(2 * nh), cur(2 * nh), tcur, tcur, tprev, tprev]


def _attn_fwd(z, ct, st, nh):
    s = z.shape[0]
    nc = s // ATT_CHUNK
    scale = HEAD ** -0.5

    def body(q_ref, kp_ref, kc_ref, vp_ref, vc_ref, ctc, stc, ctp, stp, o_ref, lse_ref, qf, kf, vf, ob, lb, bias, q16, k16, v16):
        c = pl.program_id(1)
        qf[...] = _rope(q_ref[...].astype(F32), ctc[...], stc[...])
        kf[0:ATT_CHUNK] = _rope(kp_ref[...].astype(F32), ctp[...], stp[...])
        kf[ATT_CHUNK:] = _rope(kc_ref[...].astype(F32), ctc[...], stc[...])
        vf[0:ATT_CHUNK] = vp_ref[...].astype(F32)
        vf[ATT_CHUNK:] = vc_ref[...].astype(F32)
        _band_bias(bias)

        def softmax_units(q_b, k_b, v_b, bias_b):
            sc = _bdot(q_b, k_b, 2, 2) * scale + bias_b
            m = jnp.max(sc, axis=2, keepdims=True)
            p = jnp.exp(sc - m)
            l = jnp.sum(p, axis=2, keepdims=True)
            return _bdot(p.astype(BF16), v_b, 2, 1) / l, m + jnp.log(l)

        for b, d in enumerate(DILATIONS[:-1]):
            def batch(ub, b=b, d=d):
                units = _batch_units(ub, d, c)
                q_b = jnp.stack([qf[qr, :] for qr, _, _ in units]).astype(BF16)
                k_b = jnp.stack([kf[kr, :] for _, kr, _ in units]).astype(BF16)
                v_b = jnp.stack([vf[kr, :] for _, kr, _ in units]).astype(BF16)
                o, lse = softmax_units(q_b, k_b, v_b, jnp.stack([bias[first] for _, _, first in units]))
                for j, (qr, _, _) in enumerate(units):
                    ob[b, qr, :] = o[j]
                    lb[b, qr, :] = jnp.broadcast_to(lse[j], (CHUNK, HEAD))
            _for_batches(d, batch)

        b = len(DILATIONS) - 1
        q16[...] = _deint(qf[...]).astype(BF16)
        for half in range(2):
            rows = pl.ds(half * ATT_CHUNK, ATT_CHUNK)
            k16[rows, :] = _deint(kf[rows, :]).astype(BF16)
            v16[rows, :] = _deint(vf[rows, :]).astype(BF16)
        bias_w = bias[jnp.where(c == 0, 1, 0)][None]
        for ub in range(WIDE // ATT_BATCH):
            rs = [ub * ATT_BATCH + j for j in range(ATT_BATCH)]
            o, lse = softmax_units(jnp.stack([q16[_blk(r), :] for r in rs]), jnp.stack([_key_rows(k16, r) for r in rs]),
                                   jnp.stack([_key_rows(v16, r) for r in rs]), bias_w)
            for j, r in enumerate(rs):
                qf[_blk(r), :] = o[j]
                kf[_blk(r), :] = jnp.broadcast_to(lse[j], (CHUNK, HEAD))
        ob[b] = _reint(qf[...])
        lb[b] = _reint(kf[0:ATT_CHUNK])

        mx = jnp.maximum(jnp.maximum(lb[0], lb[1]), lb[2])
        e0, e1, e2 = jnp.exp(lb[0] - mx), jnp.exp(lb[1] - mx), jnp.exp(lb[2] - mx)
        den = e0 + e1 + e2
        o_ref[...] = ((e0 * ob[0] + e1 * ob[1] + e2 * ob[2]) / den).astype(BF16)
        lse_ref[...] = mx + jnp.log(den)

    blk = pl.BlockSpec((ATT_CHUNK, HEAD), lambda h, c: (c, h))
    return pl.pallas_call(
        body, name="attn_fwd", grid=(nh, nc),
        in_specs=_attn_specs(nh, nc), out_specs=[blk, blk],
        out_shape=[jax.ShapeDtypeStruct((s, nh * HEAD), BF16), jax.ShapeDtypeStruct((s, nh * HEAD), F32)],
        scratch_shapes=[pltpu.VMEM((ATT_CHUNK, HEAD), F32), pltpu.VMEM((2 * ATT_CHUNK, HEAD), F32),
                        pltpu.VMEM((2 * ATT_CHUNK, HEAD), F32), pltpu.VMEM((3, ATT_CHUNK, HEAD), F32),
                        pltpu.VMEM((3, ATT_CHUNK, HEAD), F32), pltpu.VMEM((2, CHUNK, 2 * CHUNK), F32),
                        pltpu.VMEM((ATT_CHUNK, HEAD), BF16), pltpu.VMEM((2 * ATT_CHUNK, HEAD), BF16),
                        pltpu.VMEM((2 * ATT_CHUNK, HEAD), BF16)],
        compiler_params=_cparams("parallel", "arbitrary"),
    )(z, z, z, z, z, ct, st, ct, st)


def _attn_bwd(z, ct, st, da, o, lse, nh):
    s = z.shape[0]
    nc = s // ATT_CHUNK
    scale = HEAD ** -0.5

    def body(q_ref, kp_ref, kc_ref, vp_ref, vc_ref, ctc, stc, ctp, stp, do_ref, o_ref, lse_ref,
             dq_ref, dk_ref, dv_ref, qf, kf, vf, dof, dbar, dqa, dkf, dvf, bias, q16, k16, v16, do16):
        c = pl.program_id(1)

        @pl.when(c == 0)
        def _():
            dkf[...] = jnp.zeros_like(dkf)
            dvf[...] = jnp.zeros_like(dvf)

        @pl.when(c > 0)
        def _():
            dkf[0:ATT_CHUNK] = dkf[ATT_CHUNK:]
            dvf[0:ATT_CHUNK] = dvf[ATT_CHUNK:]
            dkf[ATT_CHUNK:] = jnp.zeros((ATT_CHUNK, HEAD), F32)
            dvf[ATT_CHUNK:] = jnp.zeros((ATT_CHUNK, HEAD), F32)

        @pl.when(c < nc)
        def _():
            qf[...] = _rope(q_ref[...].astype(F32), ctc[...], stc[...])
            kf[0:ATT_CHUNK] = _rope(kp_ref[...].astype(F32), ctp[...], stp[...])
            kf[ATT_CHUNK:] = _rope(kc_ref[...].astype(F32), ctc[...], stc[...])
            vf[0:ATT_CHUNK] = vp_ref[...].astype(F32)
            vf[ATT_CHUNK:] = vc_ref[...].astype(F32)
            dov = do_ref[...].astype(F32)
            dof[...] = dov
            dbar[...] = jnp.broadcast_to(jnp.sum(dov * o_ref[...].astype(F32), axis=1, keepdims=True), (ATT_CHUNK, HEAD))
            dqa[...] = jnp.zeros_like(dqa)
            _band_bias(bias)

            def grads(q_b, k_b, v_b, do_b, lse_b, dbar_b, bias_b):
                sc = _bdot(q_b, k_b, 2, 2) * scale + bias_b
                p = jnp.exp(sc - jnp.concatenate([lse_b, lse_b], axis=2))
                dp = _bdot(do_b, v_b, 2, 2)
                ds = (p * (dp - jnp.concatenate([dbar_b, dbar_b], axis=2)) * scale).astype(BF16)
                return _bdot(ds, k_b, 2, 1), _bdot(ds, q_b, 1, 1), _bdot(p.astype(BF16), do_b, 1, 1)

            for d in DILATIONS[:-1]:
                def batch(ub, d=d):
                    units = _batch_units(ub, d, c)
                    dq, dk, dv = grads(jnp.stack([qf[qr, :] for qr, _, _ in units]).astype(BF16),
                                       jnp.stack([kf[kr, :] for _, kr, _ in units]).astype(BF16),
                                       jnp.stack([vf[kr, :] for _, kr, _ in units]).astype(BF16),
                                       jnp.stack([dof[qr, :] for qr, _, _ in units]).astype(BF16),
                                       jnp.stack([lse_ref[qr, :] for qr, _, _ in units]),
                                       jnp.stack([dbar[qr, :] for qr, _, _ in units]),
                                       jnp.stack([bias[first] for _, _, first in units]))
                    for j, (qr, kr, _) in enumerate(units):
                        dqa[qr, :] += dq[j]
                        dkf[kr, :] += dk[j]
                        dvf[kr, :] += dv[j]
                _for_batches(d, batch)

            q16[...] = _deint(qf[...]).astype(BF16)
            do16[...] = _deint(dof[...]).astype(BF16)
            for half in range(2):
                rows = pl.ds(half * ATT_CHUNK, ATT_CHUNK)
                k16[rows, :] = _deint(kf[rows, :]).astype(BF16)
                v16[rows, :] = _deint(vf[rows, :]).astype(BF16)
            dof[...] = _deint(lse_ref[...])
            dbar[...] = _deint(dbar[...])
            bias_w = bias[jnp.where(c == 0, 1, 0)][None]
            for ub in range(WIDE // ATT_BATCH):
                rs = [ub * ATT_BATCH + j for j in range(ATT_BATCH)]
                dq, dk, dv = grads(jnp.stack([q16[_blk(r), :] for r in rs]), jnp.stack([_key_rows(k16, r) for r in rs]),
                                   jnp.stack([_key_rows(v16, r) for r in rs]), jnp.stack([do16[_blk(r), :] for r in rs]),
                                   jnp.stack([dof[_blk(r), :] for r in rs]), jnp.stack([dbar[_blk(r), :] for r in rs]), bias_w)
                for j, r in enumerate(rs):
                    qf[_blk(r), :] = dq[j]
                    kf[_blk(r), :] = dk[j][0:CHUNK]
                    kf[pl.ds(ATT_CHUNK + r * CHUNK, CHUNK), :] = dk[j][CHUNK:]
                    vf[_blk(r), :] = dv[j][0:CHUNK]
                    vf[pl.ds(ATT_CHUNK + r * CHUNK, CHUNK), :] = dv[j][CHUNK:]
            dqa[...] += _reint(qf[...])
            for half in range(2):
                rows = pl.ds(half * ATT_CHUNK, ATT_CHUNK)
                dkf[rows, :] += _reint(kf[rows, :])
                dvf[rows, :] += _reint(vf[rows, :])
            dq_ref[...] = _rope_t(dqa[...], ctc[...], stc[...]).astype(BF16)

        @pl.when(c > 0)
        def _():
            dk_ref[...] = _rope_t(dkf[0:ATT_CHUNK], ctp[...], stp[...]).astype(BF16)
            dv_ref[...] = dvf[0:ATT_CHUNK].astype(BF16)

    cur = pl.BlockSpec((ATT_CHUNK, HEAD), lambda h, c: (jnp.minimum(c, nc - 1), h))
    late = pl.BlockSpec((ATT_CHUNK, HEAD), lambda h, c: (jnp.maximum(c - 1, 0), h))
    shp = jax.ShapeDtypeStruct((s, nh * HEAD), BF16)
    big = pltpu.VMEM((2 * ATT_CHUNK, HEAD), F32)
    one = pltpu.VMEM((ATT_CHUNK, HEAD), F32)
    one16, big16 = pltpu.VMEM((ATT_CHUNK, HEAD), BF16), pltpu.VMEM((2 * ATT_CHUNK, HEAD), BF16)
    return pl.pallas_call(
        body, name="attn_bwd", grid=(nh, nc + 1),
        in_specs=_attn_specs(nh, nc) + [cur, cur, cur], out_specs=[cur, late, late],
        out_shape=[shp, shp, shp],
        scratch_shapes=[one, big, big, one, one, one, big, big, pltpu.VMEM((2, CHUNK, 2 * CHUNK), F32),
                        one16, big16, big16, one16],
        compiler_params=_cparams("parallel", "arbitrary"),
    )(z, z, z, z, z, ct, st, ct, st, da, o, lse)


_GELU_K = math.sqrt(2.0 / math.pi)


def _gelu(x):
    return 0.5 * x * (1.0 + jnp.tanh(_GELU_K * (x + 0.044715 * x * x * x)))


def _gelu_and_grad(x):
    t = jnp.tanh(_GELU_K * (x + 0.044715 * x * x * x))
    g = 0.5 * x * (1.0 + t)
    dg = 0.5 * (1.0 + t) + 0.5 * x * (1.0 - t * t) * (_GELU_K * (1.0 + 3 * 0.044715 * x * x))
    return g, dg


def _tril(w):
    ti = lax.broadcasted_iota(jnp.int32, (CHUNK, CHUNK), 0)
    si = lax.broadcasted_iota(jnp.int32, (CHUNK, CHUNK), 1)
    return jnp.where(si <= ti, w, 0.0)


def _sgu_fwd(z, w_s, b_col, ng, u_blk):
    s = z.shape[0]
    gw = ng * HEAD
    tq = _tile(s, 1024, CHUNK)

    def body(u_ref, v_ref, w_ref, b_ref, o_ref):
        for g in range(ng):
            wg = _tril(w_ref[g]).astype(BF16)
            cols = slice(g * HEAD, (g + 1) * HEAD)
            for n in range(tq // CHUNK):
                rows = slice(n * CHUNK, (n + 1) * CHUNK)
                gv = _gelu(v_ref[rows, cols].astype(F32)).astype(BF16)
                mixed = _dot_nn(wg, gv) + b_ref[g]
                o_ref[rows, cols] = (_gelu(u_ref[rows, cols].astype(F32)) * mixed).astype(BF16)

    full = pl.BlockSpec((ng, CHUNK, CHUNK), lambda i: (0, 0, 0))
    return pl.pallas_call(
        body, name="sgu_fwd", grid=(s // tq,),
        in_specs=[pl.BlockSpec((tq, gw), lambda i: (i, u_blk)), pl.BlockSpec((tq, gw), lambda i: (i, u_blk + 1)), full, full],
        out_specs=pl.BlockSpec((tq, gw), lambda i: (i, 0)),
        out_shape=jax.ShapeDtypeStruct((s, gw), BF16),
        compiler_params=_cparams("parallel"),
    )(z, z, w_s, b_col)


def _sgu_bwd(z, w_s, b_col, da, ng, u_blk, da_blk):
    s = z.shape[0]
    gw = ng * HEAD
    tq = _tile(s, 1024, CHUNK)
    nsteps = s // tq

    def body(u_ref, v_ref, w_ref, b_ref, do_ref, du_ref, dv_ref, dw_ref, db_ref):
        i = pl.program_id(0)

        @pl.when(i == 0)
        def _():
            dw_ref[...] = jnp.zeros_like(dw_ref)
            db_ref[...] = jnp.zeros_like(db_ref)

        for g in range(ng):
            wg = _tril(w_ref[g]).astype(BF16)
            cols = slice(g * HEAD, (g + 1) * HEAD)
            dw_acc = jnp.zeros((CHUNK, CHUNK), F32)
            db_acc = jnp.zeros((CHUNK, 1), F32)
            for n in range(tq // CHUNK):
                rows = slice(n * CHUNK, (n + 1) * CHUNK)
                gu, dgu = _gelu_and_grad(u_ref[rows, cols].astype(F32))
                gv, dgv = _gelu_and_grad(v_ref[rows, cols].astype(F32))
                gvb = gv.astype(BF16)
                mixed = _dot_nn(wg, gvb) + b_ref[g]
                dout = do_ref[rows, cols].astype(F32)
                du_ref[rows, cols] = (dout * mixed * dgu).astype(BF16)
                dmix = dout * gu
                dmb = dmix.astype(BF16)
                dv_ref[rows, cols] = (_dot_tn(wg, dmb) * dgv).astype(BF16)
                dw_acc += _dot_nt(dmb, gvb)
                db_acc += jnp.sum(dmix, axis=1, keepdims=True)
            dw_ref[g] += _tril(dw_acc)
            db_ref[g] += jnp.broadcast_to(db_acc, (CHUNK, CHUNK))

    full = pl.BlockSpec((ng, CHUNK, CHUNK), lambda i: (0, 0, 0))
    out = pl.BlockSpec((tq, gw), lambda i: (i, 0))
    return pl.pallas_call(
        body, name="sgu_bwd", grid=(nsteps,),
        in_specs=[pl.BlockSpec((tq, gw), lambda i: (i, u_blk)), pl.BlockSpec((tq, gw), lambda i: (i, u_blk + 1)), full, full,
                  pl.BlockSpec((tq, gw), lambda i: (i, da_blk))],
        out_specs=[out, out, full, full],
        out_shape=[jax.ShapeDtypeStruct((s, gw), BF16)] * 2 + [jax.ShapeDtypeStruct((ng, CHUNK, CHUNK), F32)] * 2,
        compiler_params=_cparams("arbitrary"),
    )(z, z, w_s, b_col, da)


def _shift_down(y, halo, k):
    return pltpu.roll(jnp.concatenate([halo, y], axis=0), k, 0)[8:]


def _shift_up(y, halo, k):
    n = y.shape[0]
    return pltpu.roll(jnp.concatenate([y, halo], axis=0), n + 8 - k, 0)[:n]


def _conv_fwd(z, cw):
    s, d3 = z.shape
    d = d3 // 3
    tq = _tile(s, 256, 8)

    def body(z_ref, zh_ref, cw_ref, a_ref):
        i = pl.program_id(0)
        zv = z_ref[...].astype(F32)
        zh = jnp.where(i > 0, zh_ref[...].astype(F32), 0.0)
        y = zv[:, d:2 * d] * zv[:, 2 * d:]
        yh = zh[:, d:2 * d] * zh[:, 2 * d:]
        cwv = cw_ref[...]
        conv = cwv[0:1] * _shift_down(y, yh, 2) + cwv[1:2] * _shift_down(y, yh, 1) + cwv[2:3] * y
        a_ref[...] = (zv[:, :d] * conv).astype(BF16)

    return pl.pallas_call(
        body, name="conv_fwd", grid=(s // tq,),
        in_specs=[pl.BlockSpec((tq, d3), lambda i: (i, 0)),
                  pl.BlockSpec((8, d3), lambda i: (jnp.maximum(i * (tq // 8) - 1, 0), 0)),
                  pl.BlockSpec((3, d), lambda i: (0, 0))],
        out_specs=pl.BlockSpec((tq, d), lambda i: (i, 0)),
        out_shape=jax.ShapeDtypeStruct((s, d), BF16),
        compiler_params=_cparams("parallel"),
    )(z, z, cw)


def _conv_bwd(z, cw, da):
    s, d3 = z.shape
    d = d3 // 3
    tq = _tile(s, 128, 8)
    nsteps = s // tq
    nblk8 = s // 8

    def body(z_ref, zp_ref, zn_ref, da_ref, dan_ref, cw_ref, dz_ref, dcw_ref, acc):
        i = pl.program_id(0)

        @pl.when(i == 0)
        def _():
            acc[...] = jnp.zeros_like(acc)

        zv = z_ref[...].astype(F32)
        zp = jnp.where(i > 0, zp_ref[...].astype(F32), 0.0)
        zn = jnp.where(i < nsteps - 1, zn_ref[...].astype(F32), 0.0)
        dav = da_ref[...].astype(F32)
        dan = jnp.where(i < nsteps - 1, dan_ref[...].astype(F32), 0.0)
        gb, gc, hx = zv[:, :d], zv[:, d:2 * d], zv[:, 2 * d:]
        y = gc * hx
        yp = zp[:, d:2 * d] * zp[:, 2 * d:]
        cwv = cw_ref[...]
        y1, y2 = _shift_down(y, yp, 1), _shift_down(y, yp, 2)
        conv = cwv[0:1] * y2 + cwv[1:2] * y1 + cwv[2:3] * y
        dconv = dav * gb
        dconv_n = dan * zn[:, :d]
        dy = cwv[2:3] * dconv + cwv[1:2] * _shift_up(dconv, dconv_n, 1) + cwv[0:1] * _shift_up(dconv, dconv_n, 2)
        dz_ref[:, :d] = (dav * conv).astype(BF16)
        dz_ref[:, d:2 * d] = (dy * hx).astype(BF16)
        dz_ref[:, 2 * d:] = (dy * gc).astype(BF16)
        acc[0] += _colsum8(dconv * y2)
        acc[1] += _colsum8(dconv * y1)
        acc[2] += _colsum8(dconv * y)

        @pl.when(i == nsteps - 1)
        def _():
            for j in range(3):
                dcw_ref[j:j + 1, :] = jnp.sum(acc[j], axis=0, keepdims=True)

    return pl.pallas_call(
        body, name="conv_bwd", grid=(nsteps,),
        in_specs=[pl.BlockSpec((tq, d3), lambda i: (i, 0)),
                  pl.BlockSpec((8, d3), lambda i: (jnp.maximum(i * (tq // 8) - 1, 0), 0)),
                  pl.BlockSpec((8, d3), lambda i: (jnp.minimum((i + 1) * (tq // 8), nblk8 - 1), 0)),
                  pl.BlockSpec((tq, d), lambda i: (i, 0)),
                  pl.BlockSpec((8, d), lambda i: (jnp.minimum((i + 1) * (tq // 8), nblk8 - 1), 0)),
                  pl.BlockSpec((3, d), lambda i: (0, 0))],
        out_specs=[pl.BlockSpec((tq, d3), lambda i: (i, 0)), pl.BlockSpec((3, d), lambda i: (0, 0))],
        out_shape=[jax.ShapeDtypeStruct((s, d3), BF16), jax.ShapeDtypeStruct((3, d), F32)],
        scratch_shapes=[pltpu.VMEM((3, 8, d), F32)],
        compiler_params=_cparams("arbitrary"),
    )(z, z, z, da, da, cw)


def _swiglu_epilogue(gate, up):
    return gate, up, gate / (1.0 + jnp.exp(-gate)) * up


def _swiglu_bwd_epilogue(dact, gt, up):
    g, u = gt.astype(F32), up.astype(F32)
    sg = 1.0 / (1.0 + jnp.exp(-g))
    return dact * u * (sg * (1.0 + g * (1.0 - sg))), dact * (g * sg)


def _ada_fwd(c_all, ada_w):
    nl, d, n8 = ada_w.shape
    tn = _tile(n8, 768, 128)

    def body(c_ref, w_ref, o_ref):
        cv = c_ref[...]
        act = (cv / (1.0 + jnp.exp(-cv))).astype(BF16)
        o_ref[0] = _dot_nn(act, w_ref[0].astype(BF16))

    return pl.pallas_call(
        body, name="ada_fwd", grid=(nl, n8 // tn),
        in_specs=[pl.BlockSpec((N_DEV, d), lambda l, j: (0, 0)), pl.BlockSpec((1, d, tn), lambda l, j: (l, 0, j))],
        out_specs=pl.BlockSpec((1, N_DEV, tn), lambda l, j: (l, 0, j)),
        out_shape=jax.ShapeDtypeStruct((nl, N_DEV, n8), F32),
        compiler_params=_cparams("parallel", "parallel"),
    )(c_all, ada_w)


def _ada_wgrad(c_all, dmod_cols):
    nl, _, n8 = dmod_cols.shape
    d = c_all.shape[1]
    tn = _tile(n8, 768, 128)

    def body(c_ref, g_ref, o_ref):
        cv = c_ref[...]
        act = (cv / (1.0 + jnp.exp(-cv))).astype(BF16)
        o_ref[0] = _dot_tn(act, g_ref[0].astype(BF16))

    return pl.pallas_call(
        body, name="ada_wgrad", grid=(nl, n8 // tn),
        in_specs=[pl.BlockSpec((N_DEV, d), lambda l, j: (0, 0)), pl.BlockSpec((1, N_DEV, tn), lambda l, j: (l, 0, j))],
        out_specs=pl.BlockSpec((1, d, tn), lambda l, j: (l, 0, j)),
        out_shape=jax.ShapeDtypeStruct((nl, d, n8), F32),
        compiler_params=_cparams("parallel", "parallel"),
    )(c_all, dmod_cols)


def _adamw(name, pieces, w, m, v):
    npc, r, c = pieces.shape
    tr = _tile(r, max(8, (1 << 19) // c // 8 * 8), 8)
    bc1, bc2 = 1.0 - ADAM_B1 ** ADAM_STEP, 1.0 - ADAM_B2 ** ADAM_STEP

    def body(p_ref, w_ref, m_ref, v_ref, g_ref, d_ref, nm_ref, nv_ref):
        g = p_ref[0].astype(F32)
        for i in range(1, npc):
            g = g + p_ref[i].astype(F32)
        nm = ADAM_B1 * m_ref[...] + (1.0 - ADAM_B1) * g
        nv = ADAM_B2 * v_ref[...] + (1.0 - ADAM_B2) * (g * g)
        g_ref[...] = g
        nm_ref[...] = nm
        nv_ref[...] = nv
        d_ref[...] = -ADAM_LR * ((nm / bc1) / (jnp.sqrt(nv / bc2) + ADAM_EPS) + ADAM_WD * w_ref[...])

    blk = pl.BlockSpec((tr, c), lambda i: (i, 0))
    return pl.pallas_call(
        body, name=name, grid=(r // tr,),
        in_specs=[pl.BlockSpec((npc, tr, c), lambda i: (0, i, 0)), blk, blk, blk],
        out_specs=[blk] * 4, out_shape=[jax.ShapeDtypeStruct((r, c), F32)] * 4,
        compiler_params=_cparams("parallel"),
    )(pieces, w, m, v)


def _adamw_transposed(name, pieces, w, m, v):
    npc, nl, c, k = pieces.shape
    tk = _tile(k, 256, 128)
    bc1, bc2 = 1.0 - ADAM_B1 ** ADAM_STEP, 1.0 - ADAM_B2 ** ADAM_STEP

    def body(p_ref, w_ref, m_ref, v_ref, g_ref, d_ref, nm_ref, nv_ref):
        gt = p_ref[0].astype(F32)
        for i in range(1, npc):
            gt = gt + p_ref[i].astype(F32)
        g = gt.T
        nm = ADAM_B1 * m_ref[...] + (1.0 - ADAM_B1) * g
        nv = ADAM_B2 * v_ref[...] + (1.0 - ADAM_B2) * (g * g)
        g_ref[...] = g
        nm_ref[...] = nm
        nv_ref[...] = nv
        d_ref[...] = -ADAM_LR * ((nm / bc1) / (jnp.sqrt(nv / bc2) + ADAM_EPS) + ADAM_WD * w_ref[...])

    blk = pl.BlockSpec((None, tk, c), lambda l, i: (l, i, 0))
    return pl.pallas_call(
        body, name=name, grid=(nl, k // tk),
        in_specs=[pl.BlockSpec((npc, None, c, tk), lambda l, i: (0, l, 0, i)), blk, blk, blk],
        out_specs=[blk] * 4, out_shape=[jax.ShapeDtypeStruct((nl, k, c), F32)] * 4,
        compiler_params=_cparams("parallel", "parallel"),
    )(pieces, w, m, v)


def _place():
    x, y, c = lax.axis_index("x"), lax.axis_index("y"), lax.axis_index("c")
    return x, y, c


def _all_gather_small(name, x_shard):
    m_per, n = x_shard.shape

    def body(x_ref, out_ref, token_ref, send_sems, recv_sems, local_sem):
        token_ref[...] = jnp.zeros_like(token_ref)
        x, y, c = _place()
        me, sibling = (x, y, c), (x, y, 1 - c)
        chips = [(1 - x, y), (x, 1 - y), (1 - x, 1 - y)]

        def rows(px, py, pc):
            return out_ref.at[pl.ds((4 * px + 2 * py + pc) * m_per, m_per), :]

        def copy(k, block, to, src=None):
            return pltpu.make_async_remote_copy(
                src_ref=rows(*block) if src is None else src, dst_ref=rows(*block),
                send_sem=send_sems.at[k], recv_sem=recv_sems.at[k], device_id=to, device_id_type=MESH)

        mine = pltpu.make_async_copy(x_ref, rows(*me), local_sem)
        mine.start()
        first = [copy(0, me, sibling, src=x_ref)]
        first += [copy(1 + j, me, (*chip, c), src=x_ref) for j, chip in enumerate(chips)]
        for cp in first:
            cp.start()
        passed = [copy(4 + j, (*chip, c), sibling) for j, chip in enumerate(chips)]
        for j, chip in enumerate(chips):
            copy(1 + j, (*chip, c), me).wait_recv()
            passed[j].start()
        copy(0, sibling, me).wait_recv()
        for j, chip in enumerate(chips):
            copy(4 + j, (*chip, 1 - c), me).wait_recv()
        for cp in first + passed:
            cp.wait_send()
        mine.wait()

    vmem = pl.BlockSpec(memory_space=pltpu.VMEM)
    return pl.pallas_call(
        body, name=name,
        out_shape=[jax.ShapeDtypeStruct((N_DEV * m_per, n), x_shard.dtype), jax.ShapeDtypeStruct((8, HEAD), F32)],
        in_specs=[vmem], out_specs=[vmem, vmem],
        scratch_shapes=[pltpu.SemaphoreType.DMA((7,)), pltpu.SemaphoreType.DMA((7,)), pltpu.SemaphoreType.DMA],
        compiler_params=pltpu.CompilerParams(vmem_limit_bytes=VMEM_LIMIT),
    )(x_shard)


_HBM = pl.BlockSpec(memory_space=pltpu.HBM)
_SEM = pl.BlockSpec(memory_space=pltpu.SEMAPHORE)
_EFFECT = pltpu.SideEffectType.DATAFLOW_SIDE_EFFECTING


def _peers():
    x, y, c = _place()
    peers = []
    for k in range(1, N_DEV):
        px = 1 - x if k & 4 else x
        py = 1 - y if k & 2 else y
        pc = 1 - c if k & 1 else c
        peers.append(((px, py, pc), 4 * px + 2 * py + pc))
    return 4 * x + 2 * y + c, peers


def _push_start(name, srcs, lands, src_view, dst_view):
    na = len(srcs)
    n = na * (N_DEV - 1)

    def body(*refs):
        s_refs, l_refs = refs[:na], refs[na:2 * na]
        send_sems, recv_sems, token = refs[2 * na], refs[2 * na + 1], refs[-1]
        me, peers = _peers()
        for a in range(na):
            for k, (dev, idx) in enumerate(peers):
                pltpu.make_async_remote_copy(
                    src_ref=src_view(s_refs[a], idx), dst_ref=dst_view(l_refs[a], me),
                    send_sem=send_sems.at[a * (N_DEV - 1) + k], recv_sem=recv_sems.at[a * (N_DEV - 1) + k],
                    device_id=dev, device_id_type=MESH).start()
        token[...] = jnp.zeros_like(token)

    outs = pl.pallas_call(
        body, name=name,
        out_shape=(pltpu.SemaphoreType.DMA((n,)), pltpu.SemaphoreType.DMA((n,)),
                   *[pltpu.HBM(t.shape, t.dtype) for t in list(srcs) + list(lands)], jax.ShapeDtypeStruct((8, HEAD), F32)),
        in_specs=[_HBM] * (2 * na),
        out_specs=(_SEM, _SEM, *[_HBM] * (2 * na), pl.BlockSpec(memory_space=pltpu.VMEM)),
        input_output_aliases={i: 2 + i for i in range(2 * na)},
        compiler_params=pltpu.CompilerParams(has_side_effects=_EFFECT),
    )(*[pltpu.with_memory_space_constraint(t, pltpu.HBM) for t in list(srcs) + list(lands)])
    return outs[0], outs[1], list(outs[2:2 + na]), list(outs[2 + na:2 + 2 * na]), outs[-1]


def _push_wait(name, send_sems, recv_sems, srcs, lands, after, src_view, dst_view):
    na = len(srcs)

    def body(*refs):
        s_refs, l_refs = refs[:na], refs[na:2 * na]
        send_sems, recv_sems = refs[2 * na], refs[2 * na + 1]
        me, peers = _peers()
        for a in range(na):
            for k, (dev, idx) in enumerate(peers):
                cp = pltpu.make_async_remote_copy(
                    src_ref=src_view(s_refs[a], idx), dst_ref=dst_view(l_refs[a], idx),
                    send_sem=send_sems.at[a * (N_DEV - 1) + k], recv_sem=recv_sems.at[a * (N_DEV - 1) + k],
                    device_id=dev, device_id_type=MESH)
                cp.wait_send()
                cp.wait_recv()

    outs = pl.pallas_call(
        body, name=name,
        out_shape=[pltpu.HBM(t.shape, t.dtype) for t in list(srcs) + list(lands)],
        in_specs=[_HBM] * (2 * na) + [_SEM, _SEM, pl.BlockSpec(memory_space=pl.ANY)],
        out_specs=[_HBM] * (2 * na),
        input_output_aliases={i: i for i in range(2 * na)},
        compiler_params=pltpu.CompilerParams(has_side_effects=_EFFECT),
    )(*srcs, *lands, send_sems, recv_sems, after)
    return list(outs[na:])


def _gather_start(name, shards):
    lands = [lax.empty((N_DEV,) + t.shape, t.dtype) for t in shards]
    return _push_start(name, shards, lands, lambda ref, idx: ref, lambda ref, slot: ref.at[slot])


def _gather_wait(name, started, shards, after, me):
    send_sems, recv_sems, srcs, lands, _ = started
    lands = _push_wait(name, send_sems, recv_sems, srcs, lands, after, lambda ref, idx: ref, lambda ref, slot: ref.at[slot])
    return [lax.dynamic_update_index_in_dim(g, t, me, 0) for g, t in zip(lands, shards)]


def _pack_rows(parts, width):
    rows, offs, r = [], [], 0
    for p in parts:
        flat = p.reshape(-1).astype(F32)
        nr = -(-flat.shape[0] // (8 * width)) * 8
        rows.append(jnp.pad(flat, (0, nr * width - flat.shape[0])).reshape(nr, width))
        offs.append((r, flat.shape[0], p.shape))
        r += nr
    return jnp.concatenate(rows, axis=0), offs, r


def _unpack_rows(slab, offs, width):
    lead = slab.shape[:-2]
    out = []
    for r0, n, shape in offs:
        nr = -(-n // width)
        out.append(slab[..., r0:r0 + nr, :].reshape(lead + (nr * width,))[..., :n].reshape(lead + tuple(shape)))
    return out


def kernel(x, c, positions, ada_w, ada_b, norm_mix, norm_ffn, ab_w_in, sgu_w, sgu_b, ab_w_out, conv_w_in, conv_w, conv_w_out, ffn_w_gate, ffn_w_up, ffn_w_down, final_norm, loss_target, m_ada_w, m_ada_b, m_norm_mix, m_norm_ffn, m_ab_w_in, m_sgu_w, m_sgu_b, m_ab_w_out, m_conv_w_in, m_conv_w, m_conv_w_out, m_ffn_w_gate, m_ffn_w_up, m_ffn_w_down, m_final_norm, v_ada_w, v_ada_b, v_norm_mix, v_norm_ffn, v_ab_w_in, v_sgu_w, v_sgu_b, v_ab_w_out, v_conv_w_in, v_conv_w, v_conv_w_out, v_ffn_w_gate, v_ffn_w_up, v_ffn_w_down, v_final_norm):
    xi, yi, ci = _place()
    me = 4 * xi + 2 * yi + ci
    s, d = x.shape[1], x.shape[2]
    depth = ada_w.shape[0]
    n_even = ab_w_in.shape[0]
    nh_mix = d // HEAD
    nh = 3 * nh_mix // 4
    ng = nh_mix - nh
    aw, gw = nh * HEAD, ng * HEAD
    assert d <= FULL_ROW and s % ATT_CHUNK == 0
    x0 = x[0]
    target = loss_target[0]
    n_odd, cwid, d8 = conv_w.shape

    width = 512
    slab, offs, _ = _pack_rows([c, conv_w], width)
    gathered, _ = _all_gather_small("gather_cond", slab)
    c_parts, cw_parts = _unpack_rows(gathered.reshape(N_DEV, -1, width), offs, width)
    c_all = c_parts.reshape(N_DEV, d)
    conv_w_full = jnp.transpose(cw_parts, (1, 2, 0, 3)).reshape(n_odd, cwid, d)

    mod_cols = _ada_fwd(c_all, ada_w)
    n8 = mod_cols.shape[2]
    mod_all, token = _all_gather_small("gather_mod", mod_cols.reshape(depth * N_DEV, n8))
    mod_mine = lax.dynamic_index_in_dim(mod_all.reshape(N_DEV, depth, N_DEV, n8), me, axis=2, keepdims=False)

    def cols(w):
        return jnp.swapaxes(w, 0, 1)

    def mixer_weights(l):
        w_i, w_o = (ab_w_in, ab_w_out) if l % 2 == 0 else (conv_w_in, conv_w_out)
        return cols(w_i[l // 2]), w_o[l // 2]

    def ffn_weights(l):
        return [cols(ffn_w_gate[l]), cols(ffn_w_up[l]), ffn_w_down[l]]

    groups = [[mixer_weights(0)[0]], [mixer_weights(0)[1]], ffn_weights(0)]
    for l in range(1, depth):
        groups += [list(mixer_weights(l)), ffn_weights(l)]
    gathers, tok = [], token[0, 0]
    for n, ws in enumerate(groups):
        shards = [(w + tok).astype(BF16) for w in ws]
        started = _gather_start(f"gather_start_{n}", shards)
        gathers.append((started, shards))
        tok = started[4][0, 0]

    def weights_of_group(n, after):
        started, shards = gathers[n]
        return _gather_wait(f"gather_wait_{n}", started, shards, after, me)

    def plain_rows(g):
        return g.reshape(g.shape[0] * g.shape[1], g.shape[2])

    mod = jnp.transpose(mod_mine, (1, 0, 2)).reshape(depth, N_DEV * n8) + ada_b + tok
    mods = mod.reshape(depth, 6, 1, d)

    ct, st = _rope_tables(positions.reshape(s, 1))
    b_col = jnp.broadcast_to(sgu_b[..., None], sgu_b.shape + (CHUNK,))
    u_blk = 3 * aw // gw

    stream = [x0]
    saved = []
    w_in, w_out, w_gate, w_up, w_down = [[None] * depth for _ in range(5)]
    xcur = x0
    for l in range(depth):
        sh_m, sc_m, g_m, sh_f, sc_f, g_f = [mods[l, j] for j in range(6)]
        i = l // 2
        if l == 0:
            (g_in,) = weights_of_group(0, mod)
        else:
            g_in, g_out = weights_of_group(1 + 2 * l, xcur)
        w_in[l] = plain_rows(g_in)
        h = _norm_mod_fwd(xcur, norm_mix[l][None], sc_m, sh_m)
        if l % 2 == 0:
            z = _matmul("mix_in", "nt", h, w_in[l], [BF16])
            attn, lse = _attn_fwd(z, ct, st, nh)
            sgu = _sgu_fwd(z, sgu_w[i], b_col[i], ng, u_blk)
            a = jnp.concatenate([attn, sgu], axis=1)
            mixer_saved = (z, a, lse)
        else:
            z = _matmul("conv_in", "nt", h, w_in[l], [BF16])
            a = _conv_fwd(z, conv_w_full[i])
            mixer_saved = (z, a, None)
        if l == 0:
            (g_out,) = weights_of_group(1, a)
        w_out[l] = plain_rows(g_out)
        x1, mix, h2 = _matmul("mix_out", "nn", a, w_out[l], [F32, BF16, BF16],
                              extras=[(xcur, "mn"), (g_m, "n"), (norm_ffn[l][None], "n"), (sc_f, "n"), (sh_f, "n")],
                              epilogue=_gated_add_norm_epilogue)
        g_gate, g_up, g_down = weights_of_group(2 + 2 * l, x1)
        w_gate[l], w_up[l], w_down[l] = plain_rows(g_gate), plain_rows(g_up), plain_rows(g_down)
        gt, up, act = _matmul("ffn_in", "nt", h2, [w_gate[l], w_up[l]], [BF16, BF16, BF16], epilogue=_swiglu_epilogue)
        x2, f = _matmul("ffn_down", "nn", act, w_down[l], [F32, BF16], extras=[(x1, "mn"), (g_f, "n")],
                        epilogue=lambda acc, r, gv: (r + gv * acc, acc))
        saved.append((h, mixer_saved, mix, x1, h2, gt, up, act, f))
        stream.append(x2)
        xcur = x2

    f_last = saved[-1][8]
    loss_part, dx, dbr, d_final, dg = _final_loss(xcur, target, final_norm[None], f_last, mods[depth - 1, 5])
    loss = lax.psum(loss_part[0, 0], ("x", "y", "c"))

    dmod = [[None] * 6 for _ in range(depth)]
    d_norm_mix, d_norm_ffn = [None] * depth, [None] * depth
    d_sgu_w, d_sgu_b, d_conv_w = [None] * n_even, [None] * n_even, [None] * n_odd

    big = {"in": ab_w_in, "out": ab_w_out, "cin": conv_w_in, "cout": conv_w_out,
           "gate": ffn_w_gate, "up": ffn_w_up, "down": ffn_w_down}
    col_sharded = ("in", "cin", "gate", "up")
    lands = {k: lax.empty((N_DEV, w.shape[0]) + (w.shape[1:][::-1] if k in col_sharded else w.shape[1:]), BF16)
             for k, w in big.items()}
    own = {k: [None] * w.shape[0] for k, w in big.items()}
    pending = {"ffn": None, "mix": None}

    def exchange_finish(tag, after):
        (send_sems, recv_sems, srcs, lds, _), keys, li, layer = pending[tag]
        lds = _push_wait(f"exchange_wait_{tag}_{layer}", send_sems, recv_sems, srcs, lds, after,
                         lambda ref, idx: ref.at[idx], lambda ref, slot: ref.at[slot, li])
        for k, ld in zip(keys, lds):
            lands[k] = ld
        pending[tag] = None

    def exchange_start(tag, layer, keys, li, grads):
        if pending[tag] is not None:
            exchange_finish(tag, grads[0])
        for k, g in zip(keys, grads):
            own[k][li] = lax.dynamic_index_in_dim(g, me, 0, keepdims=False)
        started = _push_start(f"exchange_start_{tag}_{layer}", grads, [lands[k] for k in keys],
                              lambda ref, idx: ref.at[idx], lambda ref, slot: ref.at[slot, li])
        pending[tag] = (started, keys, li, layer)
        return started[4][0, 0]

    def row_shards(g):
        return g.reshape(N_DEV, g.shape[0] // N_DEV, g.shape[1])

    for l in reversed(range(depth)):
        sh_m, sc_m, g_m, sh_f, sc_f, g_f = [mods[l, j] for j in range(6)]
        h, (z, a, lse), mix, x1, h2, gt, up, act, f = saved[l]
        i = l // 2
        dmod[l][5] = dg
        gw_down = _matmul("ffn_down_wgrad", "tn", act, dbr, [BF16])
        dgt, dup = _matmul("ffn_down_dgrad", "nt", dbr, w_down[l], [BF16, BF16], extras=[(gt, "mn"), (up, "mn")],
                           epilogue=_swiglu_bwd_epilogue)
        gw_gate = _matmul("ffn_in_wgrad", "tn", dgt, h2, [BF16])
        gw_up = _matmul("ffn_in_wgrad", "tn", dup, h2, [BF16])
        tok = exchange_start("ffn", l, ("gate", "up", "down"), l, [row_shards(gw_gate), row_shards(gw_up), row_shards(gw_down)])
        dh2 = _matmul("ffn_in_dgrad", "nn", [dgt, dup], [w_gate[l], w_up[l]], [BF16], after=tok)
        dx, dbr, dmod[l][3], dmod[l][4], d_norm_ffn[l], dg = _norm_mod_bwd(x1, dh2, norm_ffn[l][None], sc_f, dx, mix, g_m)
        dmod[l][2] = dg
        gw_out = _matmul("mix_out_wgrad", "tn", a, dbr, [BF16])
        da = _matmul("mix_out_dgrad", "nt", dbr, w_out[l], [BF16])
        if l % 2 == 0:
            dq, dk, dv = _attn_bwd(z, ct, st, da, a, lse, nh)
            du, dvg, d_sgu_w[i], dbb = _sgu_bwd(z, sgu_w[i], b_col[i], da, ng, u_blk, aw // gw)
            d_sgu_b[i] = dbb[:, :, 0]
            dz = jnp.concatenate([dq, dk, dv, du, dvg], axis=1)
            gw_in = _matmul("mix_in_wgrad", "tn", dz, h, [BF16])
            dh = _matmul("mix_in_dgrad", "nn", dz, w_in[l], [BF16])
        else:
            dz, d_conv_w[i] = _conv_bwd(z, conv_w_full[i], da)
            gw_in = _matmul("conv_in_wgrad", "tn", dz, h, [BF16])
            dh = _matmul("conv_in_dgrad", "nn", dz, w_in[l], [BF16])
        mix_group = ("mix", l, ("in", "out") if l % 2 == 0 else ("cin", "cout"), i, [row_shards(gw_in), row_shards(gw_out)])
        w_norm = norm_mix[l][None]
        if l > 0:
            w_norm = w_norm + exchange_start(*mix_group)
            f_prev, g_prev = saved[l - 1][8], mods[l - 1, 5]
            dx, dbr, dmod[l][0], dmod[l][1], d_norm_mix[l], dg = _norm_mod_bwd(stream[l], dh, w_norm, sc_m, dx, f_prev, g_prev)
        else:
            dx, dmod[l][0], dmod[l][1], d_norm_mix[l] = _norm_mod_bwd(stream[l], dh, w_norm, sc_m, dx)
    grad_x = dx[None]

    dmod_mine = jnp.stack([jnp.concatenate([v.reshape(d) for v in dmod[l]]) for l in range(depth)])
    small = [dmod_mine, jnp.concatenate(d_norm_mix), jnp.concatenate(d_norm_ffn), jnp.stack(d_sgu_w), jnp.stack(d_sgu_b),
             d_final, jnp.stack(d_conv_w)]
    slab, offs, _ = _pack_rows(small, width)
    gathered, token = _all_gather_small("gather_small_grads", slab)
    p_dmod, p_nmix, p_nffn, p_sguw, p_sgub, p_final, p_convw = _unpack_rows(gathered.reshape(N_DEV, -1, width), offs, width)
    mix_group[4][0] = mix_group[4][0] + token[0, 0].astype(BF16)
    p_dmod = p_dmod + exchange_start(*mix_group)

    outs = {}

    def update(name, pieces, w, m, v):
        shape = w.shape
        cdim = shape[-1]
        res = _adamw("adamw_" + name, pieces.reshape(pieces.shape[0], -1, cdim), w.reshape(-1, cdim),
                     m.reshape(-1, cdim), v.reshape(-1, cdim))
        outs[name] = [r.reshape(shape) for r in res]

    update("ada_b", p_dmod.reshape(N_DEV, depth, 6 * d), ada_b, m_ada_b, v_ada_b)
    update("norm_mix", p_nmix.reshape(N_DEV, depth, d), norm_mix, m_norm_mix, v_norm_mix)
    update("norm_ffn", p_nffn.reshape(N_DEV, depth, d), norm_ffn, m_norm_ffn, v_norm_ffn)
    update("sgu_w", p_sguw, sgu_w, m_sgu_w, v_sgu_w)
    update("sgu_b", p_sgub.reshape(N_DEV, 1, -1), sgu_b.reshape(1, -1), m_sgu_b.reshape(1, -1), v_sgu_b.reshape(1, -1))
    outs["sgu_b"] = [r.reshape(sgu_b.shape) for r in outs["sgu_b"]]
    update("final_norm", p_final.reshape(N_DEV, 1, d), final_norm[None], m_final_norm[None], v_final_norm[None])
    outs["final_norm"] = [r.reshape(final_norm.shape) for r in outs["final_norm"]]
    cw_mine = lax.dynamic_slice_in_dim(p_convw.reshape(N_DEV, n_odd, cwid, d), me * d8, d8, axis=3)
    update("conv_w", cw_mine, conv_w, m_conv_w, v_conv_w)

    dmod_cols = lax.dynamic_slice_in_dim(p_dmod.reshape(N_DEV, depth, 6 * d), me * n8, n8, axis=2)
    g_ada = _ada_wgrad(c_all, jnp.transpose(dmod_cols, (1, 0, 2)))
    update("ada_w", g_ada[None], ada_w, m_ada_w, v_ada_w)

    names = {"in": "ab_w_in", "out": "ab_w_out", "cin": "conv_w_in", "cout": "conv_w_out",
             "gate": "ffn_w_gate", "up": "ffn_w_up", "down": "ffn_w_down"}
    moments = {"in": (m_ab_w_in, v_ab_w_in), "out": (m_ab_w_out, v_ab_w_out), "cin": (m_conv_w_in, v_conv_w_in),
               "cout": (m_conv_w_out, v_conv_w_out), "gate": (m_ffn_w_gate, v_ffn_w_gate), "up": (m_ffn_w_up, v_ffn_w_up),
               "down": (m_ffn_w_down, v_ffn_w_down)}

    def update_big(k):
        pieces = lax.dynamic_update_slice(lands[k], jnp.stack(own[k])[None], (me, 0, 0, 0))
        if k in col_sharded:
            outs[names[k]] = _adamw_transposed("adamw_" + names[k], pieces, big[k], *moments[k])
        else:
            update(names[k], pieces, big[k], *moments[k])

    exchange_finish("ffn", g_ada)
    for k in ("gate", "up", "down", "cin", "cout"):
        update_big(k)
    done = sum(outs[n][0][(0,) * outs[n][0].ndim] for n in outs)
    exchange_finish("mix", done.reshape(1, 1))
    for k in ("in", "out"):
        update_big(k)

    order = ["ada_w", "ada_b", "norm_mix", "norm_ffn", "ab_w_in", "sgu_w", "sgu_b", "ab_w_out", "conv_w_in", "conv_w",
             "conv_w_out", "ffn_w_gate", "ffn_w_up", "ffn_w_down", "final_norm"]
    return (loss, grad_x, *[outs[n][0] for n in order], *[outs[n][1] for n in order],
            *[outs[n][2] for n in order], *[outs[n][3] for n in order])
```

```python
import functools
import math

import numpy as np
import jax
import jax.numpy as jnp
from jax import lax
from jax.experimental import pallas as pl
from jax.experimental.pallas import tpu as pltpu

F32, BF16 = jnp.float32, jnp.bfloat16
MESH = pl.DeviceIdType.MESH
N_DEV = 8
EPS = 1e-6
HEAD = 128
CHUNK = 128
DILATIONS = (1, 4, 16)
ATT_CHUNK = CHUNK * DILATIONS[-1]
ATT_BATCH = 8
ROPE_THETA = 500000.0
ROPE_DIM = HEAD // 4
NEG = -1e30
ADAM_LR, ADAM_B1, ADAM_B2, ADAM_EPS, ADAM_WD, ADAM_STEP = 0.001, 0.9, 0.999, 1e-08, 0.01, 10
VMEM_LIMIT = 56 * 1024 * 1024


def _cparams(*sem):
    return pltpu.CompilerParams(dimension_semantics=sem or None, vmem_limit_bytes=VMEM_LIMIT)


def _tile(n, pref, unit):
    t = (min(pref, n) // unit) * unit
    while t >= unit:
        if n % t == 0:
            return t
        t -= unit
    return n


def _dot(a, b, dims):
    return lax.dot_general(a, b, (dims, ((), ())), preferred_element_type=F32)


def _dot_nn(a, b):
    return _dot(a, b, ((1,), (0,)))


def _dot_nt(a, b):
    return _dot(a, b, ((1,), (1,)))


def _dot_tn(a, b):
    return _dot(a, b, ((0,), (0,)))


FULL_ROW = 2048
_TILE_DEFAULT = dict(tm=1024, tn=512, tk=2816)
_TILES = {
    "ffn_down": dict(tn=1024),
    "ffn_down_wgrad": dict(tm=1408, tn=1024),
    "ffn_in_wgrad": dict(tm=1408, tn=1024), "mix_in_wgrad": dict(tm=1408, tn=1024), "conv_in_wgrad": dict(tm=1536, tn=1024),
    "mix_in_dgrad": dict(tn=1024, tk=1408), "conv_in_dgrad": dict(tn=1024, tk=1536),
    "mix_out": dict(tm=512, tn=FULL_ROW),
    "mix_out_wgrad": dict(tn=1024), "mix_out_dgrad": dict(tn=1024),
}


def _matmul(name, mode, a, b, outs, *, extras=(), epilogue=None, after=None):
    a_list = list(a) if isinstance(a, (list, tuple)) else [a]
    b_list = list(b) if isinstance(b, (list, tuple)) else [b]
    na, nb = len(a_list), len(b_list)
    paired = na > 1
    assert na == nb if paired else na == 1
    nacc = 1 if paired else nb
    a0, b0 = a_list[0], b_list[0]
    if mode == "nn":
        (m, kk), (_, n) = a0.shape, b0.shape
    elif mode == "nt":
        (m, kk), (n, _) = a0.shape, b0.shape
    else:
        (kk, m), (_, n) = a0.shape, b0.shape
    if after is not None:
        inner = epilogue or (lambda *accs: accs)
        extras = list(extras) + [(jnp.zeros((1, n), F32) + after, "n")]
        epilogue = lambda *tiles: inner(*tiles[:-1])
    pref = {**_TILE_DEFAULT, **_TILES.get(name, {})}
    tm, tn, tk = _tile(m, pref["tm"], 128), _tile(n, pref["tn"], 128), _tile(kk, pref["tk"], 128)
    nk = kk // tk
    dotf = {"nn": _dot_nn, "nt": _dot_nt, "tn": _dot_tn}[mode]
    a_spec = pl.BlockSpec((tk, tm), lambda i, j, k: (k, i)) if mode == "tn" else pl.BlockSpec((tm, tk), lambda i, j, k: (i, k))
    b_spec = pl.BlockSpec((tn, tk), lambda i, j, k: (j, k)) if mode == "nt" else pl.BlockSpec((tk, tn), lambda i, j, k: (k, j))
    e_specs = [pl.BlockSpec((tm, tn), lambda i, j, k: (i, j)) if kind == "mn" else pl.BlockSpec((1, tn), lambda i, j, k: (0, j))
               for _, kind in extras]
    ne, no = len(extras), len(outs)
    epi = epilogue or (lambda *accs: accs)

    def body(*refs):
        a_refs, b_refs, rest = refs[:na], refs[na:na + nb], refs[na + nb:]
        e_refs, o_refs, acc_refs = rest[:ne], rest[ne:ne + no], rest[ne + no:]

        def products():
            if paired:
                p = dotf(a_refs[0][...], b_refs[0][...])
                for a_ref, b_ref in zip(a_refs[1:], b_refs[1:]):
                    p = p + dotf(a_ref[...], b_ref[...])
                return [p]
            av = a_refs[0][...]
            return [dotf(av, b_ref[...]) for b_ref in b_refs]

        def finish(accs):
            for o_ref, o in zip(o_refs, epi(*accs, *[r[...] for r in e_refs])):
                o_ref[...] = o.astype(o_ref.dtype)

        if nk == 1:
            finish(products())
            return
        k = pl.program_id(2)

        @pl.when(k == 0)
        def _():
            for acc_ref in acc_refs:
                acc_ref[...] = jnp.zeros_like(acc_ref)

        for acc_ref, p in zip(acc_refs, products()):
            acc_ref[...] += p

        @pl.when(k == nk - 1)
        def _():
            finish([acc_ref[...] for acc_ref in acc_refs])

    res = pl.pallas_call(
        body, name=name, grid=(m // tm, n // tn, nk),
        in_specs=[a_spec] * na + [b_spec] * nb + e_specs,
        out_specs=[pl.BlockSpec((tm, tn), lambda i, j, k: (i, j)) for _ in outs],
        out_shape=[jax.ShapeDtypeStruct((m, n), dt) for dt in outs],
        scratch_shapes=[pltpu.VMEM((tm, tn), F32)] * nacc if nk > 1 else [],
        compiler_params=_cparams("parallel", "parallel", "arbitrary"),
    )(*a_list, *b_list, *[e for e, _ in extras])
    return res[0] if no == 1 else res


def _norm_mod_fwd(x, w, sc, sh):
    s, d = x.shape
    tm = _tile(s, 512, 8)

    def body(x_ref, w_ref, sc_ref, sh_ref, h_ref):
        xv = x_ref[...]
        r = lax.rsqrt(jnp.mean(xv * xv, axis=-1, keepdims=True) + EPS)
        h_ref[...] = ((xv * r) * w_ref[...] * (1.0 + sc_ref[...]) + sh_ref[...]).astype(BF16)

    row = pl.BlockSpec((1, d), lambda i: (0, 0))
    return pl.pallas_call(
        body, name="norm_mod_fwd", grid=(s // tm,),
        in_specs=[pl.BlockSpec((tm, d), lambda i: (i, 0)), row, row, row],
        out_specs=pl.BlockSpec((tm, d), lambda i: (i, 0)),
        out_shape=jax.ShapeDtypeStruct((s, d), BF16),
        compiler_params=_cparams("parallel"),
    )(x, w, sc, sh)


def _gated_add_norm_epilogue(acc, res, g, w, sc, sh):
    xv = res + g * acc
    r = lax.rsqrt(jnp.mean(xv * xv, axis=-1, keepdims=True) + EPS)
    return xv, acc, (xv * r) * w * (1.0 + sc) + sh


def _colsum8(t):
    tm, d = t.shape
    return jnp.sum(t.reshape(tm // 8, 8, d), axis=0)


def _norm_mod_bwd(x, dh, w, sc, dres, branch=None, g=None):
    s, d = x.shape
    tm = _tile(s, 256, 8)
    nsteps = s // tm
    gated = branch is not None

    def body(*refs):
        if gated:
            x_ref, dh_ref, w_ref, sc_ref, dres_ref, br_ref, g_ref, dx_ref, dbr_ref, dsh_ref, dsc_ref, dw_ref, dg_ref, acc = refs
        else:
            x_ref, dh_ref, w_ref, sc_ref, dres_ref, dx_ref, dsh_ref, dsc_ref, dw_ref, acc = refs
        i = pl.program_id(0)

        @pl.when(i == 0)
        def _():
            acc[...] = jnp.zeros_like(acc)

        xv, dhv, wv, scv = x_ref[...], dh_ref[...].astype(F32), w_ref[...], sc_ref[...]
        r = lax.rsqrt(jnp.mean(xv * xv, axis=-1, keepdims=True) + EPS)
        xn = xv * r
        dxn = dhv * (wv * (1.0 + scv))
        dx = dres_ref[...] + r * (dxn - xn * jnp.mean(dxn * xn, axis=-1, keepdims=True))
        dx_ref[...] = dx
        acc[0] += _colsum8(dhv)
        acc[1] += _colsum8(dhv * xn)
        if gated:
            dbr_ref[...] = (dx * g_ref[...]).astype(BF16)
            acc[2] += _colsum8(dx * br_ref[...].astype(F32))

        @pl.when(i == nsteps - 1)
        def _():
            a0 = jnp.sum(acc[0], axis=0, keepdims=True)
            a1 = jnp.sum(acc[1], axis=0, keepdims=True)
            dsh_ref[...] = a0
            dsc_ref[...] = a1 * wv
            dw_ref[...] = a1 * (1.0 + scv)
            if gated:
                dg_ref[...] = jnp.sum(acc[2], axis=0, keepdims=True)

    big = pl.BlockSpec((tm, d), lambda i: (i, 0))
    row = pl.BlockSpec((1, d), lambda i: (0, 0))
    rowo = jax.ShapeDtypeStruct((1, d), F32)
    in_specs = [big, big, row, row, big] + ([big, row] if gated else [])
    out_specs = [big] + ([big] if gated else []) + [row, row, row] + ([row] if gated else [])
    out_shape = ([jax.ShapeDtypeStruct((s, d), F32)] + ([jax.ShapeDtypeStruct((s, d), BF16)] if gated else [])
                 + [rowo, rowo, rowo] + ([rowo] if gated else []))
    args = [x, dh, w, sc, dres] + ([branch, g] if gated else [])
    return pl.pallas_call(
        body, name="norm_mod_bwd_gated" if gated else "norm_mod_bwd", grid=(nsteps,),
        in_specs=in_specs, out_specs=out_specs, out_shape=out_shape,
        scratch_shapes=[pltpu.VMEM((3, 8, d), F32)],
        compiler_params=_cparams("arbitrary"),
    )(*args)


def _final_loss(x, target, w, branch, g):
    s, d = x.shape
    tm = _tile(s, 256, 8)
    nsteps = s // tm

    def body(x_ref, t_ref, w_ref, br_ref, g_ref, loss_ref, dx_ref, dbr_ref, dw_ref, dg_ref, acc):
        i = pl.program_id(0)

        @pl.when(i == 0)
        def _():
            acc[...] = jnp.zeros_like(acc)

        xv, wv = x_ref[...], w_ref[...]
        r = lax.rsqrt(jnp.mean(xv * xv, axis=-1, keepdims=True) + EPS)
        xn = xv * r
        err = xn * wv - t_ref[...]
        dy = err * (1.0 / d)
        dxn = dy * wv
        dx = r * (dxn - xn * jnp.mean(dxn * xn, axis=-1, keepdims=True))
        dx_ref[...] = dx
        dbr_ref[...] = (dx * g_ref[...]).astype(BF16)
        acc[0] += _colsum8(err * err)
        acc[1] += _colsum8(dy * xn)
        acc[2] += _colsum8(dx * br_ref[...].astype(F32))

        @pl.when(i == nsteps - 1)
        def _():
            loss_ref[...] = jnp.sum(jnp.sum(acc[0], axis=0, keepdims=True), axis=1, keepdims=True) * (0.5 / d)
            dw_ref[...] = jnp.sum(acc[1], axis=0, keepdims=True)
            dg_ref[...] = jnp.sum(acc[2], axis=0, keepdims=True)

    big = pl.BlockSpec((tm, d), lambda i: (i, 0))
    row = pl.BlockSpec((1, d), lambda i: (0, 0))
    rowo = jax.ShapeDtypeStruct((1, d), F32)
    return pl.pallas_call(
        body, name="final_loss", grid=(nsteps,),
        in_specs=[big, big, row, big, row],
        out_specs=[pl.BlockSpec((1, 1), lambda i: (0, 0)), big, big, row, row],
        out_shape=[jax.ShapeDtypeStruct((1, 1), F32), jax.ShapeDtypeStruct((s, d), F32),
                   jax.ShapeDtypeStruct((s, d), BF16), rowo, rowo],
        scratch_shapes=[pltpu.VMEM((3, 8, d), F32)],
        compiler_params=_cparams("arbitrary"),
    )(x, target, w, branch, g)


def _rope_tables(pos_col):
    s = pos_col.shape[0]
    tq = _tile(s, 1024, 8)
    half = ROPE_DIM // 2
    inv = np.float32(ROPE_THETA) ** (-np.arange(0, ROPE_DIM, 2, dtype=np.float32) / np.float32(ROPE_DIM))
    inv_row = jnp.asarray(np.tile(inv.astype(np.float32), HEAD // half)[None, :])

    def body(p_ref, inv_ref, ct_ref, st_ref):
        lane = lax.broadcasted_iota(jnp.int32, (tq, HEAD), 1)
        ang = p_ref[...].astype(F32) * inv_ref[...]
        cs, sn = jnp.cos(ang), jnp.sin(ang)
        ct_ref[...] = jnp.where(lane < ROPE_DIM, cs, 1.0)
        st_ref[...] = jnp.where(lane < half, -sn, jnp.where(lane < ROPE_DIM, sn, 0.0))

    blk = pl.BlockSpec((tq, HEAD), lambda i: (i, 0))
    return pl.pallas_call(
        body, name="rope_tables", grid=(s // tq,),
        in_specs=[pl.BlockSpec((tq, 1), lambda i: (i, 0)), pl.BlockSpec((1, HEAD), lambda i: (0, 0))],
        out_specs=[blk, blk],
        out_shape=[jax.ShapeDtypeStruct((s, HEAD), F32)] * 2,
        compiler_params=_cparams("parallel"),
    )(pos_col, inv_row)


def _swap_halves(x):
    lane = lax.broadcasted_iota(jnp.int32, x.shape, 1)
    half = ROPE_DIM // 2
    return jnp.where(lane < half, pltpu.roll(x, HEAD - half, 1), pltpu.roll(x, half, 1))


def _rope(x, ct, st):
    return x * ct + _swap_halves(x) * st


def _rope_t(dy, ct, st):
    return dy * ct - _swap_halves(dy) * st


def _band_bias(bias_ref):
    qi = lax.broadcasted_iota(jnp.int32, (CHUNK, 2 * CHUNK), 0)
    kj = lax.broadcasted_iota(jnp.int32, (CHUNK, 2 * CHUNK), 1)
    band = (kj >= qi) & (kj <= qi + CHUNK)
    bias_ref[0] = jnp.where(band, 0.0, NEG)
    bias_ref[1] = jnp.where(band & (kj >= CHUNK), 0.0, NEG)


def _rows(start, size, d):
    return pl.ds(start, size) if d == 1 else pl.ds(start, size, stride=d)


def _batch_units(ub, d, c):
    n_sb = ATT_BATCH // d
    out = []
    for sb, r in [(ub * n_sb + t, r) for t in range(n_sb) for r in range(d)]:
        base = pl.multiple_of(sb * (CHUNK * d), CHUNK)
        first = jnp.where((c == 0) & (sb == 0), 1, 0)
        out.append((_rows(base + r, CHUNK, d), _rows(base + (ATT_CHUNK - CHUNK * d + r), 2 * CHUNK, d), first))
    return out


def _for_batches(d, fn):
    def step(ub, carry):
        fn(ub)
        return carry
    lax.fori_loop(0, ATT_CHUNK // CHUNK // ATT_BATCH, step, 0)


def _bdot(a, b, ca, cb):
    return lax.dot_general(a, b, (((ca,), (cb,)), ((0,), (0,))), preferred_element_type=F32)


WIDE = DILATIONS[-1]
assert ATT_CHUNK == CHUNK * WIDE


def _deint(x):
    return jnp.swapaxes(x.reshape(CHUNK, WIDE, HEAD), 0, 1).reshape(ATT_CHUNK, HEAD)


def _reint(x):
    return jnp.swapaxes(x.reshape(WIDE, CHUNK, HEAD), 0, 1).reshape(ATT_CHUNK, HEAD)


def _blk(r):
    return pl.ds(r * CHUNK, CHUNK)


def _key_rows(ref, r):
    return jnp.concatenate([ref[_blk(r), :], ref[pl.ds(ATT_CHUNK + r * CHUNK, CHUNK), :]], axis=0)


def _attn_specs(nh, nc):
    cur = lambda off: pl.BlockSpec((ATT_CHUNK, HEAD), lambda h, c: (jnp.minimum(c, nc - 1), off + h))
    prev = lambda off: pl.BlockSpec((ATT_CHUNK, HEAD), lambda h, c: (jnp.maximum(c - 1, 0), off + h))
    tcur = pl.BlockSpec((ATT_CHUNK, HEAD), lambda h, c: (jnp.minimum(c, nc - 1), 0))
    tprev = pl.BlockSpec((ATT_CHUNK, HEAD), lambda h, c: (jnp.maximum(c - 1, 0), 0))
    return [cur(0), prev(nh), cur(nh), prev(2 * nh), cur(2 * nh), tcur, tcur, tprev, tprev]


def _attn_fwd(z, ct, st, nh):
    s = z.shape[0]
    nc = s // ATT_CHUNK
    scale = HEAD ** -0.5

    def body(q_ref, kp_ref, kc_ref, vp_ref, vc_ref, ctc, stc, ctp, stp, o_ref, lse_ref, qf, kf, vf, ob, lb, bias, q16, k16, v16):
        c = pl.program_id(1)
        qf[...] = _rope(q_ref[...].astype(F32), ctc[...], stc[...])
        kf[0:ATT_CHUNK] = _rope(kp_ref[...].astype(F32), ctp[...], stp[...])
        kf[ATT_CHUNK:] = _rope(kc_ref[...].astype(F32), ctc[...], stc[...])
        vf[0:ATT_CHUNK] = vp_ref[...].astype(F32)
        vf[ATT_CHUNK:] = vc_ref[...].astype(F32)
        _band_bias(bias)

        def softmax_units(q_b, k_b, v_b, bias_b):
            sc = _bdot(q_b, k_b, 2, 2) * scale + bias_b
            m = jnp.max(sc, axis=2, keepdims=True)
            p = jnp.exp(sc - m)
            l = jnp.sum(p, axis=2, keepdims=True)
            return _bdot(p.astype(BF16), v_b, 2, 1) / l, m + jnp.log(l)

        for b, d in enumerate(DILATIONS[:-1]):
            def batch(ub, b=b, d=d):
                units = _batch_units(ub, d, c)
                q_b = jnp.stack([qf[qr, :] for qr, _, _ in units]).astype(BF16)
                k_b = jnp.stack([kf[kr, :] for _, kr, _ in units]).astype(BF16)
                v_b = jnp.stack([vf[kr, :] for _, kr, _ in units]).astype(BF16)
                o, lse = softmax_units(q_b, k_b, v_b, jnp.stack([bias[first] for _, _, first in units]))
                for j, (qr, _, _) in enumerate(units):
                    ob[b, qr, :] = o[j]
                    lb[b, qr, :] = jnp.broadcast_to(lse[j], (CHUNK, HEAD))
            _for_batches(d, batch)

        b = len(DILATIONS) - 1
        q16[...] = _deint(qf[...]).astype(BF16)
        for half in range(2):
            rows = pl.ds(half * ATT_CHUNK, ATT_CHUNK)
            k16[rows, :] = _deint(kf[rows, :]).astype(BF16)
            v16[rows, :] = _deint(vf[rows, :]).astype(BF16)
        bias_w = bias[jnp.where(c == 0, 1, 0)][None]
        for ub in range(WIDE // ATT_BATCH):
            rs = [ub * ATT_BATCH + j for j in range(ATT_BATCH)]
            o, lse = softmax_units(jnp.stack([q16[_blk(r), :] for r in rs]), jnp.stack([_key_rows(k16, r) for r in rs]),
                                   jnp.stack([_key_rows(v16, r) for r in rs]), bias_w)
            for j, r in enumerate(rs):
                qf[_blk(r), :] = o[j]
                kf[_blk(r), :] = jnp.broadcast_to(lse[j], (CHUNK, HEAD))
        ob[b] = _reint(qf[...])
        lb[b] = _reint(kf[0:ATT_CHUNK])

        mx = jnp.maximum(jnp.maximum(lb[0], lb[1]), lb[2])
        e0, e1, e2 = jnp.exp(lb[0] - mx), jnp.exp(lb[1] - mx), jnp.exp(lb[2] - mx)
        den = e0 + e1 + e2
        o_ref[...] = ((e0 * ob[0] + e1 * ob[1] + e2 * ob[2]) / den).astype(BF16)
        lse_ref[...] = mx + jnp.log(den)

    blk = pl.BlockSpec((ATT_CHUNK, HEAD), lambda h, c: (c, h))
    return pl.pallas_call(
        body, name="attn_fwd", grid=(nh, nc),
        in_specs=_attn_specs(nh, nc), out_specs=[blk, blk],
        out_shape=[jax.ShapeDtypeStruct((s, nh * HEAD), BF16), jax.ShapeDtypeStruct((s, nh * HEAD), F32)],
        scratch_shapes=[pltpu.VMEM((ATT_CHUNK, HEAD), F32), pltpu.VMEM((2 * ATT_CHUNK, HEAD), F32),
                        pltpu.VMEM((2 * ATT_CHUNK, HEAD), F32), pltpu.VMEM((3, ATT_CHUNK, HEAD), F32),
                        pltpu.VMEM((3, ATT_CHUNK, HEAD), F32), pltpu.VMEM((2, CHUNK, 2 * CHUNK), F32),
                        pltpu.VMEM((ATT_CHUNK, HEAD), BF16), pltpu.VMEM((2 * ATT_CHUNK, HEAD), BF16),
                        pltpu.VMEM((2 * ATT_CHUNK, HEAD), BF16)],
        compiler_params=_cparams("parallel", "arbitrary"),
    )(z, z, z, z, z, ct, st, ct, st)


def _attn_bwd(z, ct, st, da, o, lse, nh):
    s = z.shape[0]
    nc = s // ATT_CHUNK
    scale = HEAD ** -0.5

    def body(q_ref, kp_ref, kc_ref, vp_ref, vc_ref, ctc, stc, ctp, stp, do_ref, o_ref, lse_ref,
             dq_ref, dk_ref, dv_ref, qf, kf, vf, dof, dbar, dqa, dkf, dvf, bias, q16, k16, v16, do16):
        c = pl.program_id(1)

        @pl.when(c == 0)
        def _():
            dkf[...] = jnp.zeros_like(dkf)
            dvf[...] = jnp.zeros_like(dvf)

        @pl.when(c > 0)
        def _():
            dkf[0:ATT_CHUNK] = dkf[ATT_CHUNK:]
            dvf[0:ATT_CHUNK] = dvf[ATT_CHUNK:]
            dkf[ATT_CHUNK:] = jnp.zeros((ATT_CHUNK, HEAD), F32)
            dvf[ATT_CHUNK:] = jnp.zeros((ATT_CHUNK, HEAD), F32)

        @pl.when(c < nc)
        def _():
            qf[...] = _rope(q_ref[...].astype(F32), ctc[...], stc[...])
            kf[0:ATT_CHUNK] = _rope(kp_ref[...].astype(F32), ctp[...], stp[...])
            kf[ATT_CHUNK:] = _rope(kc_ref[...].astype(F32), ctc[...], stc[...])
            vf[0:ATT_CHUNK] = vp_ref[...].astype(F32)
            vf[ATT_CHUNK:] = vc_ref[...].astype(F32)
            dov = do_ref[...].astype(F32)
            dof[...] = dov
            dbar[...] = jnp.broadcast_to(jnp.sum(dov * o_ref[...].astype(F32), axis=1, keepdims=True), (ATT_CHUNK, HEAD))
            dqa[...] = jnp.zeros_like(dqa)
            _band_bias(bias)

            def grads(q_b, k_b, v_b, do_b, lse_b, dbar_b, bias_b):
                sc = _bdot(q_b, k_b, 2, 2) * scale + bias_b
                p = jnp.exp(sc - jnp.concatenate([lse_b, lse_b], axis=2))
                dp = _bdot(do_b, v_b, 2, 2)
                ds = (p * (dp - jnp.concatenate([dbar_b, dbar_b], axis=2)) * scale).astype(BF16)
                return _bdot(ds, k_b, 2, 1), _bdot(ds, q_b, 1, 1), _bdot(p.astype(BF16), do_b, 1, 1)

            for d in DILATIONS[:-1]:
                def batch(ub, d=d):
                    units = _batch_units(ub, d, c)
                    dq, dk, dv = grads(jnp.stack([qf[qr, :] for qr, _, _ in units]).astype(BF16),
                                       jnp.stack([kf[kr, :] for _, kr, _ in units]).astype(BF16),
                                       jnp.stack([vf[kr, :] for _, kr, _ in units]).astype(BF16),
                                       jnp.stack([dof[qr, :] for qr, _, _ in units]).astype(BF16),
                                       jnp.stack([lse_ref[qr, :] for qr, _, _ in units]),
                                       jnp.stack([dbar[qr, :] for qr, _, _ in units]),
                                       jnp.stack([bias[first] for _, _, first in units]))
                    for j, (qr, kr, _) in enumerate(units):
                        dqa[qr, :] += dq[j]
                        dkf[kr, :] += dk[j]
                        dvf[kr, :] += dv[j]
                _for_batches(d, batch)

            q16[...] = _deint(qf[...]).astype(BF16)
            do16[...] = _deint(dof[...]).astype(BF16)
            for half in range(2):
                rows = pl.ds(half * ATT_CHUNK, ATT_CHUNK)
                k16[rows, :] = _deint(kf[rows, :]).astype(BF16)
                v16[rows, :] = _deint(vf[rows, :]).astype(BF16)
            dof[...] = _deint(lse_ref[...])
            dbar[...] = _deint(dbar[...])
            bias_w = bias[jnp.where(c == 0, 1, 0)][None]
            for ub in range(WIDE // ATT_BATCH):
                rs = [ub * ATT_BATCH + j for j in range(ATT_BATCH)]
                dq, dk, dv = grads(jnp.stack([q16[_blk(r), :] for r in rs]), jnp.stack([_key_rows(k16, r) for r in rs]),
                                   jnp.stack([_key_rows(v16, r) for r in rs]), jnp.stack([do16[_blk(r), :] for r in rs]),
                                   jnp.stack([dof[_blk(r), :] for r in rs]), jnp.stack([dbar[_blk(r), :] for r in rs]), bias_w)
                for j, r in enumerate(rs):
                    qf[_blk(r), :] = dq[j]
                    kf[_blk(r), :] = dk[j][0:CHUNK]
                    kf[pl.ds(ATT_CHUNK + r * CHUNK, CHUNK), :] = dk[j][CHUNK:]
                    vf[_blk(r), :] = dv[j][0:CHUNK]
                    vf[pl.ds(ATT_CHUNK + r * CHUNK, CHUNK), :] = dv[j][CHUNK:]
            dqa[...] += _reint(qf[...])
            for half in range(2):
                rows = pl.ds(half * ATT_CHUNK, ATT_CHUNK)
                dkf[rows, :] += _reint(kf[rows, :])
                dvf[rows, :] += _reint(vf[rows, :])
            dq_ref[...] = _rope_t(dqa[...], ctc[...], stc[...]).astype(BF16)

        @pl.when(c > 0)
        def _():
            dk_ref[...] = _rope_t(dkf[0:ATT_CHUNK], ctp[...], stp[...]).astype(BF16)
            dv_ref[...] = dvf[0:ATT_CHUNK].astype(BF16)

    cur = pl.BlockSpec((ATT_CHUNK, HEAD), lambda h, c: (jnp.minimum(c, nc - 1), h))
    late = pl.BlockSpec((ATT_CHUNK, HEAD), lambda h, c: (jnp.maximum(c - 1, 0), h))
    shp = jax.ShapeDtypeStruct((s, nh * HEAD), BF16)
    big = pltpu.VMEM((2 * ATT_CHUNK, HEAD), F32)
    one = pltpu.VMEM((ATT_CHUNK, HEAD), F32)
    one16, big16 = pltpu.VMEM((ATT_CHUNK, HEAD), BF16), pltpu.VMEM((2 * ATT_CHUNK, HEAD), BF16)
    return pl.pallas_call(
        body, name="attn_bwd", grid=(nh, nc + 1),
        in_specs=_attn_specs(nh, nc) + [cur, cur, cur], out_specs=[cur, late, late],
        out_shape=[shp, shp, shp],
        scratch_shapes=[one, big, big, one, one, one, big, big, pltpu.VMEM((2, CHUNK, 2 * CHUNK), F32),
                        one16, big16, big16, one16],
        compiler_params=_cparams("parallel", "arbitrary"),
    )(z, z, z, z, z, ct, st, ct, st, da, o, lse)


_GELU_K = math.sqrt(2.0 / math.pi)


def _gelu(x):
    return 0.5 * x * (1.0 + jnp.tanh(_GELU_K * (x + 0.044715 * x * x * x)))


def _gelu_and_grad(x):
    t = jnp.tanh(_GELU_K * (x + 0.044715 * x * x * x))
    g = 0.5 * x * (1.0 + t)
    dg = 0.5 * (1.0 + t) + 0.5 * x * (1.0 - t * t) * (_GELU_K * (1.0 + 3 * 0.044715 * x * x))
    return g, dg


def _tril(w):
    ti = lax.broadcasted_iota(jnp.int32, (CHUNK, CHUNK), 0)
    si = lax.broadcasted_iota(jnp.int32, (CHUNK, CHUNK), 1)
    return jnp.where(si <= ti, w, 0.0)


def _sgu_fwd(z, w_s, b_col, ng, u_blk):
    s = z.shape[0]
    gw = ng * HEAD
    tq = _tile(s, 1024, CHUNK)

    def body(u_ref, v_ref, w_ref, b_ref, o_ref):
        for g in range(ng):
            wg = _tril(w_ref[g]).astype(BF16)
            cols = slice(g * HEAD, (g + 1) * HEAD)
            for n in range(tq // CHUNK):
                rows = slice(n * CHUNK, (n + 1) * CHUNK)
                gv = _gelu(v_ref[rows, cols].astype(F32)).astype(BF16)
                mixed = _dot_nn(wg, gv) + b_ref[g]
                o_ref[rows, cols] = (_gelu(u_ref[rows, cols].astype(F32)) * mixed).astype(BF16)

    full = pl.BlockSpec((ng, CHUNK, CHUNK), lambda i: (0, 0, 0))
    return pl.pallas_call(
        body, name="sgu_fwd", grid=(s // tq,),
        in_specs=[pl.BlockSpec((tq, gw), lambda i: (i, u_blk)), pl.BlockSpec((tq, gw), lambda i: (i, u_blk + 1)), full, full],
        out_specs=pl.BlockSpec((tq, gw), lambda i: (i, 0)),
        out_shape=jax.ShapeDtypeStruct((s, gw), BF16),
        compiler_params=_cparams("parallel"),
    )(z, z, w_s, b_col)


def _sgu_bwd(z, w_s, b_col, da, ng, u_blk, da_blk):
    s = z.shape[0]
    gw = ng * HEAD
    tq = _tile(s, 1024, CHUNK)
    nsteps = s // tq

    def body(u_ref, v_ref, w_ref, b_ref, do_ref, du_ref, dv_ref, dw_ref, db_ref):
        i = pl.program_id(0)

        @pl.when(i == 0)
        def _():
            dw_ref[...] = jnp.zeros_like(dw_ref)
            db_ref[...] = jnp.zeros_like(db_ref)

        for g in range(ng):
            wg = _tril(w_ref[g]).astype(BF16)
            cols = slice(g * HEAD, (g + 1) * HEAD)
            dw_acc = jnp.zeros((CHUNK, CHUNK), F32)
            db_acc = jnp.zeros((CHUNK, 1), F32)
            for n in range(tq // CHUNK):
                rows = slice(n * CHUNK, (n + 1) * CHUNK)
                gu, dgu = _gelu_and_grad(u_ref[rows, cols].astype(F32))
                gv, dgv = _gelu_and_grad(v_ref[rows, cols].astype(F32))
                gvb = gv.astype(BF16)
                mixed = _dot_nn(wg, gvb) + b_ref[g]
                dout = do_ref[rows, cols].astype(F32)
                du_ref[rows, cols] = (dout * mixed * dgu).astype(BF16)
                dmix = dout * gu
                dmb = dmix.astype(BF16)
                dv_ref[rows, cols] = (_dot_tn(wg, dmb) * dgv).astype(BF16)
                dw_acc += _dot_nt(dmb, gvb)
                db_acc += jnp.sum(dmix, axis=1, keepdims=True)
            dw_ref[g] += _tril(dw_acc)
            db_ref[g] += jnp.broadcast_to(db_acc, (CHUNK, CHUNK))

    full = pl.BlockSpec((ng, CHUNK, CHUNK), lambda i: (0, 0, 0))
    out = pl.BlockSpec((tq, gw), lambda i: (i, 0))
    return pl.pallas_call(
        body, name="sgu_bwd", grid=(nsteps,),
        in_specs=[pl.BlockSpec((tq, gw), lambda i: (i, u_blk)), pl.BlockSpec((tq, gw), lambda i: (i, u_blk + 1)), full, full,
                  pl.BlockSpec((tq, gw), lambda i: (i, da_blk))],
        out_specs=[out, out, full, full],
        out_shape=[jax.ShapeDtypeStruct((s, gw), BF16)] * 2 + [jax.ShapeDtypeStruct((ng, CHUNK, CHUNK), F32)] * 2,
        compiler_params=_cparams("arbitrary"),
    )(z, z, w_s, b_col, da)


def _shift_down(y, halo, k):
    return pltpu.roll(jnp.concatenate([halo, y], axis=0), k, 0)[8:]


def _shift_up(y, halo, k):
    n = y.shape[0]
    return pltpu.roll(jnp.concatenate([y, halo], axis=0), n + 8 - k, 0)[:n]


def _conv_fwd(z, cw):
    s, d3 = z.shape
    d = d3 // 3
    tq = _tile(s, 256, 8)

    def body(z_ref, zh_ref, cw_ref, a_ref):
        i = pl.program_id(0)
        zv = z_ref[...].astype(F32)
        zh = jnp.where(i > 0, zh_ref[...].astype(F32), 0.0)
        y = zv[:, d:2 * d] * zv[:, 2 * d:]
        yh = zh[:, d:2 * d] * zh[:, 2 * d:]
        cwv = cw_ref[...]
        conv = cwv[0:1] * _shift_down(y, yh, 2) + cwv[1:2] * _shift_down(y, yh, 1) + cwv[2:3] * y
        a_ref[...] = (zv[:, :d] * conv).astype(BF16)

    return pl.pallas_call(
        body, name="conv_fwd", grid=(s // tq,),
        in_specs=[pl.BlockSpec((tq, d3), lambda i: (i, 0)),
                  pl.BlockSpec((8, d3), lambda i: (jnp.maximum(i * (tq // 8) - 1, 0), 0)),
                  pl.BlockSpec((3, d), lambda i: (0, 0))],
        out_specs=pl.BlockSpec((tq, d), lambda i: (i, 0)),
        out_shape=jax.ShapeDtypeStruct((s, d), BF16),
        compiler_params=_cparams("parallel"),
    )(z, z, cw)


def _conv_bwd(z, cw, da):
    s, d3 = z.shape
    d = d3 // 3
    tq = _tile(s, 128, 8)
    nsteps = s // tq
    nblk8 = s // 8

    def body(z_ref, zp_ref, zn_ref, da_ref, dan_ref, cw_ref, dz_ref, dcw_ref, acc):
        i = pl.program_id(0)

        @pl.when(i == 0)
        def _():
            acc[...] = jnp.zeros_like(acc)

        zv = z_ref[...].astype(F32)
        zp = jnp.where(i > 0, zp_ref[...].astype(F32), 0.0)
        zn = jnp.where(i < nsteps - 1, zn_ref[...].astype(F32), 0.0)
        dav = da_ref[...].astype(F32)
        dan = jnp.where(i < nsteps - 1, dan_ref[...].astype(F32), 0.0)
        gb, gc, hx = zv[:, :d], zv[:, d:2 * d], zv[:, 2 * d:]
        y = gc * hx
        yp = zp[:, d:2 * d] * zp[:, 2 * d:]
        cwv = cw_ref[...]
        y1, y2 = _shift_down(y, yp, 1), _shift_down(y, yp, 2)
        conv = cwv[0:1] * y2 + cwv[1:2] * y1 + cwv[2:3] * y
        dconv = dav * gb
        dconv_n = dan * zn[:, :d]
        dy = cwv[2:3] * dconv + cwv[1:2] * _shift_up(dconv, dconv_n, 1) + cwv[0:1] * _shift_up(dconv, dconv_n, 2)
        dz_ref[:, :d] = (dav * conv).astype(BF16)
        dz_ref[:, d:2 * d] = (dy * hx).astype(BF16)
        dz_ref[:, 2 * d:] = (dy * gc).astype(BF16)
        acc[0] += _colsum8(dconv * y2)
        acc[1] += _colsum8(dconv * y1)
        acc[2] += _colsum8(dconv * y)

        @pl.when(i == nsteps - 1)
        def _():
            for j in range(3):
                dcw_ref[j:j + 1, :] = jnp.sum(acc[j], axis=0, keepdims=True)

    return pl.pallas_call(
        body, name="conv_bwd", grid=(nsteps,),
        in_specs=[pl.BlockSpec((tq, d3), lambda i: (i, 0)),
                  pl.BlockSpec((8, d3), lambda i: (jnp.maximum(i * (tq // 8) - 1, 0), 0)),
                  pl.BlockSpec((8, d3), lambda i: (jnp.minimum((i + 1) * (tq // 8), nblk8 - 1), 0)),
                  pl.BlockSpec((tq, d), lambda i: (i, 0)),
                  pl.BlockSpec((8, d), lambda i: (jnp.minimum((i + 1) * (tq // 8), nblk8 - 1), 0)),
                  pl.BlockSpec((3, d), lambda i: (0, 0))],
        out_specs=[pl.BlockSpec((tq, d3), lambda i: (i, 0)), pl.BlockSpec((3, d), lambda i: (0, 0))],
        out_shape=[jax.ShapeDtypeStruct((s, d3), BF16), jax.ShapeDtypeStruct((3, d), F32)],
        scratch_shapes=[pltpu.VMEM((3, 8, d), F32)],
        compiler_params=_cparams("arbitrary"),
    )(z, z, z, da, da, cw)


def _swiglu_epilogue(gate, up):
    return gate, up, gate / (1.0 + jnp.exp(-gate)) * up


def _swiglu_bwd_epilogue(dact, gt, up):
    g, u = gt.astype(F32), up.astype(F32)
    sg = 1.0 / (1.0 + jnp.exp(-g))
    return dact * u * (sg * (1.0 + g * (1.0 - sg))), dact * (g * sg)


def _ada_fwd(c_all, ada_w):
    nl, d, n8 = ada_w.shape
    tn = _tile(n8, 768, 128)

    def body(c_ref, w_ref, o_ref):
        cv = c_ref[...]
        act = (cv / (1.0 + jnp.exp(-cv))).astype(BF16)
        o_ref[0] = _dot_nn(act, w_ref[0].astype(BF16))

    return pl.pallas_call(
        body, name="ada_fwd", grid=(nl, n8 // tn),
        in_specs=[pl.BlockSpec((N_DEV, d), lambda l, j: (0, 0)), pl.BlockSpec((1, d, tn), lambda l, j: (l, 0, j))],
        out_specs=pl.BlockSpec((1, N_DEV, tn), lambda l, j: (l, 0, j)),
        out_shape=jax.ShapeDtypeStruct((nl, N_DEV, n8), F32),
        compiler_params=_cparams("parallel", "parallel"),
    )(c_all, ada_w)


def _ada_wgrad(c_all, dmod_cols):
    nl, _, n8 = dmod_cols.shape
    d = c_all.shape[1]
    tn = _tile(n8, 768, 128)

    def body(c_ref, g_ref, o_ref):
        cv = c_ref[...]
        act = (cv / (1.0 + jnp.exp(-cv))).astype(BF16)
        o_ref[0] = _dot_tn(act, g_ref[0].astype(BF16))

    return pl.pallas_call(
        body, name="ada_wgrad", grid=(nl, n8 // tn),
        in_specs=[pl.BlockSpec((N_DEV, d), lambda l, j: (0, 0)), pl.BlockSpec((1, N_DEV, tn), lambda l, j: (l, 0, j))],
        out_specs=pl.BlockSpec((1, d, tn), lambda l, j: (l, 0, j)),
        out_shape=jax.ShapeDtypeStruct((nl, d, n8), F32),
        compiler_params=_cparams("parallel", "parallel"),
    )(c_all, dmod_cols)


def _adamw(name, pieces, w, m, v):
    npc, r, c = pieces.shape
    tr = _tile(r, max(8, (1 << 19) // c // 8 * 8), 8)
    bc1, bc2 = 1.0 - ADAM_B1 ** ADAM_STEP, 1.0 - ADAM_B2 ** ADAM_STEP

    def body(p_ref, w_ref, m_ref, v_ref, g_ref, d_ref, nm_ref, nv_ref):
        g = p_ref[0].astype(F32)
        for i in range(1, npc):
            g = g + p_ref[i].astype(F32)
        nm = ADAM_B1 * m_ref[...] + (1.0 - ADAM_B1) * g
        nv = ADAM_B2 * v_ref[...] + (1.0 - ADAM_B2) * (g * g)
        g_ref[...] = g
        nm_ref[...] = nm
        nv_ref[...] = nv
        d_ref[...] = -ADAM_LR * ((nm / bc1) / (jnp.sqrt(nv / bc2) + ADAM_EPS) + ADAM_WD * w_ref[...])

    blk = pl.BlockSpec((tr, c), lambda i: (i, 0))
    return pl.pallas_call(
        body, name=name, grid=(r // tr,),
        in_specs=[pl.BlockSpec((npc, tr, c), lambda i: (0, i, 0)), blk, blk, blk],
        out_specs=[blk] * 4, out_shape=[jax.ShapeDtypeStruct((r, c), F32)] * 4,
        compiler_params=_cparams("parallel"),
    )(pieces, w, m, v)


def _adamw_transposed(name, pieces, w, m, v):
    npc, nl, c, k = pieces.shape
    tk = _tile(k, 256, 128)
    bc1, bc2 = 1.0 - ADAM_B1 ** ADAM_STEP, 1.0 - ADAM_B2 ** ADAM_STEP

    def body(p_ref, w_ref, m_ref, v_ref, g_ref, d_ref, nm_ref, nv_ref):
        gt = p_ref[0].astype(F32)
        for i in range(1, npc):
            gt = gt + p_ref[i].astype(F32)
        g = gt.T
        nm = ADAM_B1 * m_ref[...] + (1.0 - ADAM_B1) * g
        nv = ADAM_B2 * v_ref[...] + (1.0 - ADAM_B2) * (g * g)
        g_ref[...] = g
        nm_ref[...] = nm
        nv_ref[...] = nv
        d_ref[...] = -ADAM_LR * ((nm / bc1) / (jnp.sqrt(nv / bc2) + ADAM_EPS) + ADAM_WD * w_ref[...])

    nk = k // tk
    blk = pl.BlockSpec((tk, c), lambda l, i: (l * nk + i, 0))
    res = pl.pallas_call(
        body, name=name, grid=(nl, nk),
        in_specs=[pl.BlockSpec((npc, None, c, tk), lambda l, i: (0, l, 0, i)), blk, blk, blk],
        out_specs=[blk] * 4, out_shape=[jax.ShapeDtypeStruct((nl * k, c), F32)] * 4,
        compiler_params=_cparams("parallel", "parallel"),
    )(pieces, w.reshape(nl * k, c), m.reshape(nl * k, c), v.reshape(nl * k, c))
    return [r.reshape(nl, k, c) for r in res]


def _place():
    x, y, c = lax.axis_index("x"), lax.axis_index("y"), lax.axis_index("c")
    return x, y, c


def _all_gather_small(name, x_shard):
    m_per, n = x_shard.shape

    def body(x_ref, out_ref, token_ref, send_sems, recv_sems, local_sem):
        token_ref[...] = jnp.zeros_like(token_ref)
        x, y, c = _place()
        me, sibling = (x, y, c), (x, y, 1 - c)
        chips = [(1 - x, y), (x, 1 - y), (1 - x, 1 - y)]

        def rows(px, py, pc):
            return out_ref.at[pl.ds((4 * px + 2 * py + pc) * m_per, m_per), :]

        def copy(k, block, to, src=None):
            return pltpu.make_async_remote_copy(
                src_ref=rows(*block) if src is None else src, dst_ref=rows(*block),
                send_sem=send_sems.at[k], recv_sem=recv_sems.at[k], device_id=to, device_id_type=MESH)

        mine = pltpu.make_async_copy(x_ref, rows(*me), local_sem)
        mine.start()
        first = [copy(0, me, sibling, src=x_ref)]
        first += [copy(1 + j, me, (*chip, c), src=x_ref) for j, chip in enumerate(chips)]
        for cp in first:
            cp.start()
        passed = [copy(4 + j, (*chip, c), sibling) for j, chip in enumerate(chips)]
        for j, chip in enumerate(chips):
            copy(1 + j, (*chip, c), me).wait_recv()
            passed[j].start()
        copy(0, sibling, me).wait_recv()
        for j, chip in enumerate(chips):
            copy(4 + j, (*chip, 1 - c), me).wait_recv()
        for cp in first + passed:
            cp.wait_send()
        mine.wait()

    vmem = pl.BlockSpec(memory_space=pltpu.VMEM)
    return pl.pallas_call(
        body, name=name,
        out_shape=[jax.ShapeDtypeStruct((N_DEV * m_per, n), x_shard.dtype), jax.ShapeDtypeStruct((8, HEAD), F32)],
        in_specs=[vmem], out_specs=[vmem, vmem],
        scratch_shapes=[pltpu.SemaphoreType.DMA((7,)), pltpu.SemaphoreType.DMA((7,)), pltpu.SemaphoreType.DMA],
        compiler_params=pltpu.CompilerParams(vmem_limit_bytes=VMEM_LIMIT),
    )(x_shard)


_HBM = pl.BlockSpec(memory_space=pltpu.HBM)
_SEM = pl.BlockSpec(memory_space=pltpu.SEMAPHORE)
_EFFECT = pltpu.SideEffectType.DATAFLOW_SIDE_EFFECTING


def _peers():
    x, y, c = _place()
    peers = []
    for k in range(1, N_DEV):
        px = 1 - x if k & 4 else x
        py = 1 - y if k & 2 else y
        pc = 1 - c if k & 1 else c
        peers.append(((px, py, pc), 4 * px + 2 * py + pc))
    return 4 * x + 2 * y + c, peers


def _push_start(name, srcs, lands, src_view, dst_view):
    na = len(srcs)
    n = na * (N_DEV - 1)

    def body(*refs):
        s_refs, l_refs = refs[:na], refs[na:2 * na]
        send_sems, recv_sems, token = refs[2 * na], refs[2 * na + 1], refs[-1]
        me, peers = _peers()
        for a in range(na):
            for k, (dev, idx) in enumerate(peers):
                pltpu.make_async_remote_copy(
                    src_ref=src_view(s_refs[a], idx), dst_ref=dst_view(l_refs[a], me),
                    send_sem=send_sems.at[a * (N_DEV - 1) + k], recv_sem=recv_sems.at[a * (N_DEV - 1) + k],
                    device_id=dev, device_id_type=MESH).start()
        token[...] = jnp.zeros_like(token)

    outs = pl.pallas_call(
        body, name=name,
        out_shape=(pltpu.SemaphoreType.DMA((n,)), pltpu.SemaphoreType.DMA((n,)),
                   *[pltpu.HBM(t.shape, t.dtype) for t in list(srcs) + list(lands)], jax.ShapeDtypeStruct((8, HEAD), F32)),
        in_specs=[_HBM] * (2 * na),
        out_specs=(_SEM, _SEM, *[_HBM] * (2 * na), pl.BlockSpec(memory_space=pltpu.VMEM)),
        input_output_aliases={i: 2 + i for i in range(2 * na)},
        compiler_params=pltpu.CompilerParams(has_side_effects=_EFFECT),
    )(*[pltpu.with_memory_space_constraint(t, pltpu.HBM) for t in list(srcs) + list(lands)])
    return outs[0], outs[1], list(outs[2:2 + na]), list(outs[2 + na:2 + 2 * na]), outs[-1]


def _push_wait(name, send_sems, recv_sems, srcs, lands, after, src_view, dst_view):
    na = len(srcs)

    def body(*refs):
        s_refs, l_refs = refs[:na], refs[na:2 * na]
        send_sems, recv_sems = refs[2 * na], refs[2 * na + 1]
        me, peers = _peers()
        for a in range(na):
            for k, (dev, idx) in enumerate(peers):
                cp = pltpu.make_async_remote_copy(
                    src_ref=src_view(s_refs[a], idx), dst_ref=dst_view(l_refs[a], idx),
                    send_sem=send_sems.at[a * (N_DEV - 1) + k], recv_sem=recv_sems.at[a * (N_DEV - 1) + k],
                    device_id=dev, device_id_type=MESH)
                cp.wait_send()
                cp.wait_recv()

    outs = pl.pallas_call(
        body, name=name,
        out_shape=[pltpu.HBM(t.shape, t.dtype) for t in list(srcs) + list(lands)],
        in_specs=[_HBM] * (2 * na) + [_SEM, _SEM, pl.BlockSpec(memory_space=pl.ANY)],
        out_specs=[_HBM] * (2 * na),
        input_output_aliases={i: i for i in range(2 * na)},
        compiler_params=pltpu.CompilerParams(has_side_effects=_EFFECT),
    )(*srcs, *lands, send_sems, recv_sems, after)
    return list(outs[na:])


def _gather_start(name, shards):
    lands = [lax.empty((N_DEV,) + t.shape, t.dtype) for t in shards]
    return _push_start(name, shards, lands, lambda ref, idx: ref, lambda ref, slot: ref.at[slot])


def _gather_wait(name, started, shards, after, me):
    send_sems, recv_sems, srcs, lands, _ = started
    lands = _push_wait(name, send_sems, recv_sems, srcs, lands, after, lambda ref, idx: ref, lambda ref, slot: ref.at[slot])
    return [lax.dynamic_update_index_in_dim(g, t, me, 0) for g, t in zip(lands, shards)]


def _pack_rows(parts, width):
    rows, offs, r = [], [], 0
    for p in parts:
        flat = p.reshape(-1).astype(F32)
        nr = -(-flat.shape[0] // (8 * width)) * 8
        rows.append(jnp.pad(flat, (0, nr * width - flat.shape[0])).reshape(nr, width))
        offs.append((r, flat.shape[0], p.shape))
        r += nr
    return jnp.concatenate(rows, axis=0), offs, r


def _unpack_rows(slab, offs, width):
    lead = slab.shape[:-2]
    out = []
    for r0, n, shape in offs:
        nr = -(-n // width)
        out.append(slab[..., r0:r0 + nr, :].reshape(lead + (nr * width,))[..., :n].reshape(lead + tuple(shape)))
    return out


def kernel(x, c, positions, ada_w, ada_b, norm_mix, norm_ffn, ab_w_in, sgu_w, sgu_b, ab_w_out, conv_w_in, conv_w, conv_w_out, ffn_w_gate, ffn_w_up, ffn_w_down, final_norm, loss_target, m_ada_w, m_ada_b, m_norm_mix, m_norm_ffn, m_ab_w_in, m_sgu_w, m_sgu_b, m_ab_w_out, m_conv_w_in, m_conv_w, m_conv_w_out, m_ffn_w_gate, m_ffn_w_up, m_ffn_w_down, m_final_norm, v_ada_w, v_ada_b, v_norm_mix, v_norm_ffn, v_ab_w_in, v_sgu_w, v_sgu_b, v_ab_w_out, v_conv_w_in, v_conv_w, v_conv_w_out, v_ffn_w_gate, v_ffn_w_up, v_ffn_w_down, v_final_norm):
    xi, yi, ci = _place()
    me = 4 * xi + 2 * yi + ci
    s, d = x.shape[1], x.shape[2]
    depth = ada_w.shape[0]
    n_even = ab_w_in.shape[0]
    nh_mix = d // HEAD
    nh = 3 * nh_mix // 4
    ng = nh_mix - nh
    aw, gw = nh * HEAD, ng * HEAD
    assert d <= FULL_ROW and s % ATT_CHUNK == 0
    x0 = x[0]
    target = loss_target[0]
    n_odd, cwid, d8 = conv_w.shape

    width = 512
    slab, offs, _ = _pack_rows([c, conv_w], width)
    gathered, _ = _all_gather_small("gather_cond", slab)
    c_parts, cw_parts = _unpack_rows(gathered.reshape(N_DEV, -1, width), offs, width)
    c_all = c_parts.reshape(N_DEV, d)
    conv_w_full = jnp.transpose(cw_parts, (1, 2, 0, 3)).reshape(n_odd, cwid, d)

    mod_cols = _ada_fwd(c_all, ada_w)
    n8 = mod_cols.shape[2]
    mod_all, token = _all_gather_small("gather_mod", mod_cols.reshape(depth * N_DEV, n8))
    mod_mine = lax.dynamic_index_in_dim(mod_all.reshape(N_DEV, depth, N_DEV, n8), me, axis=2, keepdims=False)

    def cols(w):
        return jnp.swapaxes(w, 0, 1)

    def mixer_weights(l):
        w_i, w_o = (ab_w_in, ab_w_out) if l % 2 == 0 else (conv_w_in, conv_w_out)
        return cols(w_i[l // 2]), w_o[l // 2]

    def ffn_weights(l):
        return [cols(ffn_w_gate[l]), cols(ffn_w_up[l]), ffn_w_down[l]]

    groups = [[mixer_weights(0)[0]], [mixer_weights(0)[1]], ffn_weights(0)]
    for l in range(1, depth):
        groups += [list(mixer_weights(l)), ffn_weights(l)]
    gathers, tok = [], token[0, 0]
    for n, ws in enumerate(groups):
        shards = [(w + tok).astype(BF16) for w in ws]
        started = _gather_start(f"gather_start_{n}", shards)
        gathers.append((started, shards))
        tok = started[4][0, 0]

    def weights_of_group(n, after):
        started, shards = gathers[n]
        return _gather_wait(f"gather_wait_{n}", started, shards, after, me)

    def plain_rows(g):
        return g.reshape(g.shape[0] * g.shape[1], g.shape[2])

    mod = jnp.transpose(mod_mine, (1, 0, 2)).reshape(depth, N_DEV * n8) + ada_b + tok
    mods = mod.reshape(depth, 6, 1, d)

    ct, st = _rope_tables(positions.reshape(s, 1))
    b_col = jnp.broadcast_to(sgu_b[..., None], sgu_b.shape + (CHUNK,))
    u_blk = 3 * aw // gw

    stream = [x0]
    saved = []
    w_in, w_out, w_gate, w_up, w_down = [[None] * depth for _ in range(5)]
    xcur = x0
    for l in range(depth):
        sh_m, sc_m, g_m, sh_f, sc_f, g_f = [mods[l, j] for j in range(6)]
        i = l // 2
        if l == 0:
            (g_in,) = weights_of_group(0, mod)
        else:
            g_in, g_out = weights_of_group(1 + 2 * l, xcur)
        w_in[l] = plain_rows(g_in)
        h = _norm_mod_fwd(xcur, norm_mix[l][None], sc_m, sh_m)
        if l % 2 == 0:
            z = _matmul("mix_in", "nt", h, w_in[l], [BF16])
            attn, lse = _attn_fwd(z, ct, st, nh)
            sgu = _sgu_fwd(z, sgu_w[i], b_col[i], ng, u_blk)
            a = jnp.concatenate([attn, sgu], axis=1)
            mixer_saved = (z, a, lse)
        else:
            z = _matmul("conv_in", "nt", h, w_in[l], [BF16])
            a = _conv_fwd(z, conv_w_full[i])
            mixer_saved = (z, a, None)
        if l == 0:
            (g_out,) = weights_of_group(1, a)
        w_out[l] = plain_rows(g_out)
        x1, mix, h2 = _matmul("mix_out", "nn", a, w_out[l], [F32, BF16, BF16],
                              extras=[(xcur, "mn"), (g_m, "n"), (norm_ffn[l][None], "n"), (sc_f, "n"), (sh_f, "n")],
                              epilogue=_gated_add_norm_epilogue)
        g_gate, g_up, g_down = weights_of_group(2 + 2 * l, x1)
        w_gate[l], w_up[l], w_down[l] = plain_rows(g_gate), plain_rows(g_up), plain_rows(g_down)
        gt, up, act = _matmul("ffn_in", "nt", h2, [w_gate[l], w_up[l]], [BF16, BF16, BF16], epilogue=_swiglu_epilogue)
        x2, f = _matmul("ffn_down", "nn", act, w_down[l], [F32, BF16], extras=[(x1, "mn"), (g_f, "n")],
                        epilogue=lambda acc, r, gv: (r + gv * acc, acc))
        saved.append((h, mixer_saved, mix, x1, h2, gt, up, act, f))
        stream.append(x2)
        xcur = x2

    f_last = saved[-1][8]
    loss_part, dx, dbr, d_final, dg = _final_loss(xcur, target, final_norm[None], f_last, mods[depth - 1, 5])
    loss = lax.psum(loss_part[0, 0], ("x", "y", "c"))

    dmod = [[None] * 6 for _ in range(depth)]
    d_norm_mix, d_norm_ffn = [None] * depth, [None] * depth
    d_sgu_w, d_sgu_b, d_conv_w = [None] * n_even, [None] * n_even, [None] * n_odd

    big = {"in": ab_w_in, "out": ab_w_out, "cin": conv_w_in, "cout": conv_w_out,
           "gate": ffn_w_gate, "up": ffn_w_up, "down": ffn_w_down}
    col_sharded = ("in", "cin", "gate", "up")
    lands = {k: lax.empty((N_DEV, w.shape[0]) + (w.shape[1:][::-1] if k in col_sharded else w.shape[1:]), BF16)
             for k, w in big.items()}
    own = {k: [None] * w.shape[0] for k, w in big.items()}
    pending = {"ffn": None, "mix": None}

    def exchange_finish(tag, after):
        (send_sems, recv_sems, srcs, lds, _), keys, li, layer = pending[tag]
        lds = _push_wait(f"exchange_wait_{tag}_{layer}", send_sems, recv_sems, srcs, lds, after,
                         lambda ref, idx: ref.at[idx], lambda ref, slot: ref.at[slot, li])
        for k, ld in zip(keys, lds):
            lands[k] = ld
        pending[tag] = None

    def exchange_start(tag, layer, keys, li, grads):
        if pending[tag] is not None:
            exchange_finish(tag, grads[0])
        for k, g in zip(keys, grads):
            own[k][li] = lax.dynamic_index_in_dim(g, me, 0, keepdims=False)
        started = _push_start(f"exchange_start_{tag}_{layer}", grads, [lands[k] for k in keys],
                              lambda ref, idx: ref.at[idx], lambda ref, slot: ref.at[slot, li])
        pending[tag] = (started, keys, li, layer)
        return started[4][0, 0]

    def row_shards(g):
        return g.reshape(N_DEV, g.shape[0] // N_DEV, g.shape[1])

    for l in reversed(range(depth)):
        sh_m, sc_m, g_m, sh_f, sc_f, g_f = [mods[l, j] for j in range(6)]
        h, (z, a, lse), mix, x1, h2, gt, up, act, f = saved[l]
        i = l // 2
        dmod[l][5] = dg
        gw_down = _matmul("ffn_down_wgrad", "tn", act, dbr, [BF16])
        dgt, dup = _matmul("ffn_down_dgrad", "nt", dbr, w_down[l], [BF16, BF16], extras=[(gt, "mn"), (up, "mn")],
                           epilogue=_swiglu_bwd_epilogue)
        gw_gate = _matmul("ffn_in_wgrad", "tn", dgt, h2, [BF16])
        gw_up = _matmul("ffn_in_wgrad", "tn", dup, h2, [BF16])
        tok = exchange_start("ffn", l, ("gate", "up", "down"), l, [row_shards(gw_gate), row_shards(gw_up), row_shards(gw_down)])
        dh2 = _matmul("ffn_in_dgrad", "nn", [dgt, dup], [w_gate[l], w_up[l]], [BF16], after=tok)
        dx, dbr, dmod[l][3], dmod[l][4], d_norm_ffn[l], dg = _norm_mod_bwd(x1, dh2, norm_ffn[l][None], sc_f, dx, mix, g_m)
        dmod[l][2] = dg
        gw_out = _matmul("mix_out_wgrad", "tn", a, dbr, [BF16])
        da = _matmul("mix_out_dgrad", "nt", dbr, w_out[l], [BF16])
        if l % 2 == 0:
            dq, dk, dv = _attn_bwd(z, ct, st, da, a, lse, nh)
            du, dvg, d_sgu_w[i], dbb = _sgu_bwd(z, sgu_w[i], b_col[i], da, ng, u_blk, aw // gw)
            d_sgu_b[i] = dbb[:, :, 0]
            dz = jnp.concatenate([dq, dk, dv, du, dvg], axis=1)
            gw_in = _matmul("mix_in_wgrad", "tn", dz, h, [BF16])
            dh = _matmul("mix_in_dgrad", "nn", dz, w_in[l], [BF16])
        else:
            dz, d_conv_w[i] = _conv_bwd(z, conv_w_full[i], da)
            gw_in = _matmul("conv_in_wgrad", "tn", dz, h, [BF16])
            dh = _matmul("conv_in_dgrad", "nn", dz, w_in[l], [BF16])
        mix_group = ("mix", l, ("in", "out") if l % 2 == 0 else ("cin", "cout"), i, [row_shards(gw_in), row_shards(gw_out)])
        w_norm = norm_mix[l][None]
        if l > 0:
            w_norm = w_norm + exchange_start(*mix_group)
            f_prev, g_prev = saved[l - 1][8], mods[l - 1, 5]
            dx, dbr, dmod[l][0], dmod[l][1], d_norm_mix[l], dg = _norm_mod_bwd(stream[l], dh, w_norm, sc_m, dx, f_prev, g_prev)
        else:
            dx, dmod[l][0], dmod[l][1], d_norm_mix[l] = _norm_mod_bwd(stream[l], dh, w_norm, sc_m, dx)
    grad_x = dx[None]

    dmod_mine = jnp.stack([jnp.concatenate([v.reshape(d) for v in dmod[l]]) for l in range(depth)])
    small = [dmod_mine, jnp.concatenate(d_norm_mix), jnp.concatenate(d_norm_ffn), jnp.stack(d_sgu_w), jnp.stack(d_sgu_b),
             d_final, jnp.stack(d_conv_w)]
    slab, offs, _ = _pack_rows(small, width)
    gathered, token = _all_gather_small("gather_small_grads", slab)
    p_dmod, p_nmix, p_nffn, p_sguw, p_sgub, p_final, p_convw = _unpack_rows(gathered.reshape(N_DEV, -1, width), offs, width)
    mix_group[4][0] = mix_group[4][0] + token[0, 0].astype(BF16)
    p_dmod = p_dmod + exchange_start(*mix_group)

    outs = {}

    def update(name, pieces, w, m, v):
        shape = w.shape
        cdim = shape[-1]
        res = _adamw("adamw_" + name, pieces.reshape(pieces.shape[0], -1, cdim), w.reshape(-1, cdim),
                     m.reshape(-1, cdim), v.reshape(-1, cdim))
        outs[name] = [r.reshape(shape) for r in res]

    update("ada_b", p_dmod.reshape(N_DEV, depth, 6 * d), ada_b, m_ada_b, v_ada_b)
    update("norm_mix", p_nmix.reshape(N_DEV, depth, d), norm_mix, m_norm_mix, v_norm_mix)
    update("norm_ffn", p_nffn.reshape(N_DEV, depth, d), norm_ffn, m_norm_ffn, v_norm_ffn)
    update("sgu_w", p_sguw, sgu_w, m_sgu_w, v_sgu_w)
    update("sgu_b", p_sgub.reshape(N_DEV, 1, -1), sgu_b.reshape(1, -1), m_sgu_b.reshape(1, -1), v_sgu_b.reshape(1, -1))
    outs["sgu_b"] = [r.reshape(sgu_b.shape) for r in outs["sgu_b"]]
    update("final_norm", p_final.reshape(N_DEV, 1, d), final_norm[None], m_final_norm[None], v_final_norm[None])
    outs["final_norm"] = [r.reshape(final_norm.shape) for r in outs["final_norm"]]
    cw_mine = lax.dynamic_slice_in_dim(p_convw.reshape(N_DEV, n_odd, cwid, d), me * d8, d8, axis=3)
    update("conv_w", cw_mine, conv_w, m_conv_w, v_conv_w)

    dmod_cols = lax.dynamic_slice_in_dim(p_dmod.reshape(N_DEV, depth, 6 * d), me * n8, n8, axis=2)
    g_ada = _ada_wgrad(c_all, jnp.transpose(dmod_cols, (1, 0, 2)))
    update("ada_w", g_ada[None], ada_w, m_ada_w, v_ada_w)

    names = {"in": "ab_w_in", "out": "ab_w_out", "cin": "conv_w_in", "cout": "conv_w_out",
             "gate": "ffn_w_gate", "up": "ffn_w_up", "down": "ffn_w_down"}
    moments = {"in": (m_ab_w_in, v_ab_w_in), "out": (m_ab_w_out, v_ab_w_out), "cin": (m_conv_w_in, v_conv_w_in),
               "cout": (m_conv_w_out, v_conv_w_out), "gate": (m_ffn_w_gate, v_ffn_w_gate), "up": (m_ffn_w_up, v_ffn_w_up),
               "down": (m_ffn_w_down, v_ffn_w_down)}

    def update_big(k):
        pieces = lax.dynamic_update_slice(lands[k], jnp.stack(own[k])[None], (me, 0, 0, 0))
        if k in col_sharded:
            outs[names[k]] = _adamw_transposed("adamw_" + names[k], pieces, big[k], *moments[k])
        else:
            update(names[k], pieces, big[k], *moments[k])

    exchange_finish("ffn", g_ada)
    for k in ("gate", "up", "down", "cin", "cout"):
        update_big(k)
    done = sum(outs[n][0][(0,) * outs[n][0].ndim] for n in outs)
    exchange_finish("mix", done.reshape(1, 1))
    for k in ("in", "out"):
        update_big(k)

    order = ["ada_w", "ada_b", "norm_mix", "norm_ffn", "ab_w_in", "sgu_w", "sgu_b", "ab_w_out", "conv_w_in", "conv_w",
             "conv_w_out", "ffn_w_gate", "ffn_w_up", "ffn_w_down", "final_norm"]
    return (loss, grad_x, *[outs[n][0] for n in order], *[outs[n][1] for n in order],
            *[outs[n][2] for n in order], *[outs[n][3] for n in order])
```

```python
import functools
import math

import numpy as np
import jax
import jax.numpy as jnp
from jax import lax
from jax.experimental import pallas as pl
from jax.experimental.pallas import tpu as pltpu

F32, BF16 = jnp.float32, jnp.bfloat16
MESH = pl.DeviceIdType.MESH
N_DEV = 8
EPS = 1e-6
HEAD = 128
CHUNK = 128
DILATIONS = (1, 4, 16)
ATT_CHUNK = CHUNK * DILATIONS[-1]
ATT_BATCH = 8
ROPE_THETA = 500000.0
ROPE_DIM = HEAD // 4
NEG = -1e30
ADAM_LR, ADAM_B1, ADAM_B2, ADAM_EPS, ADAM_WD, ADAM_STEP = 0.001, 0.9, 0.999, 1e-08, 0.01, 10
VMEM_LIMIT = 56 * 1024 * 1024


def _cparams(*sem):
    return pltpu.CompilerParams(dimension_semantics=sem or None, vmem_limit_bytes=VMEM_LIMIT)


def _tile(n, pref, unit):
    t = (min(pref, n) // unit) * unit
    while t >= unit:
        if n % t == 0:
            return t
        t -= unit
    return n


def _dot(a, b, dims):
    return lax.dot_general(a, b, (dims, ((), ())), preferred_element_type=F32)


def _dot_nn(a, b):
    return _dot(a, b, ((1,), (0,)))


def _dot_nt(a, b):
    return _dot(a, b, ((1,), (1,)))


def _dot_tn(a, b):
    return _dot(a, b, ((0,), (0,)))


FULL_ROW = 2048
_TILE_DEFAULT = dict(tm=1024, tn=512, tk=2816)
_TILES = {
    "ffn_down": dict(tn=1024),
    "ffn_down_wgrad": dict(tm=1408, tn=1024),
    "ffn_in_wgrad": dict(tm=1408, tn=1024), "mix_in_wgrad": dict(tm=1408, tn=1024), "conv_in_wgrad": dict(tm=1536, tn=1024),
    "mix_in_dgrad": dict(tn=1024, tk=1408), "conv_in_dgrad": dict(tn=1024, tk=1536),
    "mix_out": dict(tm=512, tn=FULL_ROW),
    "mix_out_wgrad": dict(tn=1024), "mix_out_dgrad": dict(tn=1024),
}


def _matmul(name, mode, a, b, outs, *, extras=(), epilogue=None, after=None):
    a_list = list(a) if isinstance(a, (list, tuple)) else [a]
    b_list = list(b) if isinstance(b, (list, tuple)) else [b]
    na, nb = len(a_list), len(b_list)
    paired = na > 1
    assert na == nb if paired else na == 1
    nacc = 1 if paired else nb
    a0, b0 = a_list[0], b_list[0]
    if mode == "nn":
        (m, kk), (_, n) = a0.shape, b0.shape
    elif mode == "nt":
        (m, kk), (n, _) = a0.shape, b0.shape
    else:
        (kk, m), (_, n) = a0.shape, b0.shape
    if after is not None:
        inner = epilogue or (lambda *accs: accs)
        extras = list(extras) + [(jnp.zeros((1, n), F32) + after, "n")]
        epilogue = lambda *tiles: inner(*tiles[:-1])
    pref = {**_TILE_DEFAULT, **_TILES.get(name, {})}
    tm, tn, tk = _tile(m, pref["tm"], 128), _tile(n, pref["tn"], 128), _tile(kk, pref["tk"], 128)
    nk = kk // tk
    dotf = {"nn": _dot_nn, "nt": _dot_nt, "tn": _dot_tn}[mode]
    a_spec = pl.BlockSpec((tk, tm), lambda i, j, k: (k, i)) if mode == "tn" else pl.BlockSpec((tm, tk), lambda i, j, k: (i, k))
    b_spec = pl.BlockSpec((tn, tk), lambda i, j, k: (j, k)) if mode == "nt" else pl.BlockSpec((tk, tn), lambda i, j, k: (k, j))
    e_specs = [pl.BlockSpec((tm, tn), lambda i, j, k: (i, j)) if kind == "mn" else pl.BlockSpec((1, tn), lambda i, j, k: (0, j))
               for _, kind in extras]
    ne, no = len(extras), len(outs)
    epi = epilogue or (lambda *accs: accs)

    def body(*refs):
        a_refs, b_refs, rest = refs[:na], refs[na:na + nb], refs[na + nb:]
        e_refs, o_refs, acc_refs = rest[:ne], rest[ne:ne + no], rest[ne + no:]

        def products():
            if paired:
                p = dotf(a_refs[0][...], b_refs[0][...])
                for a_ref, b_ref in zip(a_refs[1:], b_refs[1:]):
                    p = p + dotf(a_ref[...], b_ref[...])
                return [p]
            av = a_refs[0][...]
            return [dotf(av, b_ref[...]) for b_ref in b_refs]

        def finish(accs):
            for o_ref, o in zip(o_refs, epi(*accs, *[r[...] for r in e_refs])):
                o_ref[...] = o.astype(o_ref.dtype)

        if nk == 1:
            finish(products())
            return
        k = pl.program_id(2)

        @pl.when(k == 0)
        def _():
            for acc_ref in acc_refs:
                acc_ref[...] = jnp.zeros_like(acc_ref)

        for acc_ref, p in zip(acc_refs, products()):
            acc_ref[...] += p

        @pl.when(k == nk - 1)
        def _():
            finish([acc_ref[...] for acc_ref in acc_refs])

    res = pl.pallas_call(
        body, name=name, grid=(m // tm, n // tn, nk),
        in_specs=[a_spec] * na + [b_spec] * nb + e_specs,
        out_specs=[pl.BlockSpec((tm, tn), lambda i, j, k: (i, j)) for _ in outs],
        out_shape=[jax.ShapeDtypeStruct((m, n), dt) for dt in outs],
        scratch_shapes=[pltpu.VMEM((tm, tn), F32)] * nacc if nk > 1 else [],
        compiler_params=_cparams("parallel", "parallel", "arbitrary"),
    )(*a_list, *b_list, *[e for e, _ in extras])
    return res[0] if no == 1 else res


def _norm_mod_fwd(x, w, sc, sh):
    s, d = x.shape
    tm = _tile(s, 512, 8)

    def body(x_ref, w_ref, sc_ref, sh_ref, h_ref):
        xv = x_ref[...]
        r = lax.rsqrt(jnp.mean(xv * xv, axis=-1, keepdims=True) + EPS)
        h_ref[...] = ((xv * r) * w_ref[...] * (1.0 + sc_ref[...]) + sh_ref[...]).astype(BF16)

    row = pl.BlockSpec((1, d), lambda i: (0, 0))
    return pl.pallas_call(
        body, name="norm_mod_fwd", grid=(s // tm,),
        in_specs=[pl.BlockSpec((tm, d), lambda i: (i, 0)), row, row, row],
        out_specs=pl.BlockSpec((tm, d), lambda i: (i, 0)),
        out_shape=jax.ShapeDtypeStruct((s, d), BF16),
        compiler_params=_cparams("parallel"),
    )(x, w, sc, sh)


def _gated_add_norm_epilogue(acc, res, g, w, sc, sh):
    xv = res + g * acc
    r = lax.rsqrt(jnp.mean(xv * xv, axis=-1, keepdims=True) + EPS)
    return xv, acc, (xv * r) * w * (1.0 + sc) + sh


NORM_PARTS = 4


def _colsum8(t):
    tm, d = t.shape
    return jnp.sum(t.reshape(tm // 8, 8, d), axis=0)


def _norm_mod_bwd(x, dh, w, sc, dres, branch=None, g=None):
    s, d = x.shape
    tm = _tile(s, 256, 8)
    nsteps = s // tm
    gated = branch is not None

    def body(*refs):
        if gated:
            x_ref, dh_ref, w_ref, sc_ref, dres_ref, br_ref, g_ref, dx_ref, dbr_ref, dsh_ref, dsc_ref, dw_ref, dg_ref, acc = refs
        else:
            x_ref, dh_ref, w_ref, sc_ref, dres_ref, dx_ref, dsh_ref, dsc_ref, dw_ref, acc = refs
        i = pl.program_id(0)

        @pl.when(i == 0)
        def _():
            acc[...] = jnp.zeros_like(acc)

        wc = d // NORM_PARTS
        parts = [slice(p * wc, (p + 1) * wc) for p in range(NORM_PARTS)]
        s1 = jnp.zeros((tm, 1), F32)
        s2 = jnp.zeros((tm, 1), F32)
        for cols in parts:
            xv = x_ref[:, cols]
            wm = w_ref[:, cols] * (1.0 + sc_ref[:, cols])
            s1 = s1 + jnp.sum(xv * xv, axis=-1, keepdims=True)
            s2 = s2 + jnp.sum((dh_ref[:, cols].astype(F32) * wm) * xv, axis=-1, keepdims=True)
        r = lax.rsqrt(s1 * (1.0 / d) + EPS)
        m2 = s2 * r * (1.0 / d)
        for cols in parts:
            xn = x_ref[:, cols] * r
            dhv = dh_ref[:, cols].astype(F32)
            dx = dres_ref[:, cols] + r * (dhv * (w_ref[:, cols] * (1.0 + sc_ref[:, cols])) - xn * m2)
            dx_ref[:, cols] = dx
            acc[0, :, cols] += _colsum8(dhv)
            acc[1, :, cols] += _colsum8(dhv * xn)
            if gated:
                dbr_ref[:, cols] = (dx * g_ref[:, cols]).astype(BF16)
                acc[2, :, cols] += _colsum8(dx * br_ref[:, cols].astype(F32))

        @pl.when(i == nsteps - 1)
        def _():
            a0 = jnp.sum(acc[0], axis=0, keepdims=True)
            a1 = jnp.sum(acc[1], axis=0, keepdims=True)
            dsh_ref[...] = a0
            dsc_ref[...] = a1 * w_ref[...]
            dw_ref[...] = a1 * (1.0 + sc_ref[...])
            if gated:
                dg_ref[...] = jnp.sum(acc[2], axis=0, keepdims=True)

    big = pl.BlockSpec((tm, d), lambda i: (i, 0))
    row = pl.BlockSpec((1, d), lambda i: (0, 0))
    rowo = jax.ShapeDtypeStruct((1, d), F32)
    in_specs = [big, big, row, row, big] + ([big, row] if gated else [])
    out_specs = [big] + ([big] if gated else []) + [row, row, row] + ([row] if gated else [])
    out_shape = ([jax.ShapeDtypeStruct((s, d), F32)] + ([jax.ShapeDtypeStruct((s, d), BF16)] if gated else [])
                 + [rowo, rowo, rowo] + ([rowo] if gated else []))
    args = [x, dh, w, sc, dres] + ([branch, g] if gated else [])
    return pl.pallas_call(
        body, name="norm_mod_bwd_gated" if gated else "norm_mod_bwd", grid=(nsteps,),
        in_specs=in_specs, out_specs=out_specs, out_shape=out_shape,
        scratch_shapes=[pltpu.VMEM((3, 8, d), F32)],
        compiler_params=_cparams("arbitrary"),
    )(*args)


def _final_loss(x, target, w, branch, g):
    s, d = x.shape
    tm = _tile(s, 256, 8)
    nsteps = s // tm

    def body(x_ref, t_ref, w_ref, br_ref, g_ref, loss_ref, dx_ref, dbr_ref, dw_ref, dg_ref, acc):
        i = pl.program_id(0)

        @pl.when(i == 0)
        def _():
            acc[...] = jnp.zeros_like(acc)

        xv, wv = x_ref[...], w_ref[...]
        r = lax.rsqrt(jnp.mean(xv * xv, axis=-1, keepdims=True) + EPS)
        xn = xv * r
        err = xn * wv - t_ref[...]
        dy = err * (1.0 / d)
        dxn = dy * wv
        dx = r * (dxn - xn * jnp.mean(dxn * xn, axis=-1, keepdims=True))
        dx_ref[...] = dx
        dbr_ref[...] = (dx * g_ref[...]).astype(BF16)
        acc[0] += _colsum8(err * err)
        acc[1] += _colsum8(dy * xn)
        acc[2] += _colsum8(dx * br_ref[...].astype(F32))

        @pl.when(i == nsteps - 1)
        def _():
            loss_ref[...] = jnp.sum(jnp.sum(acc[0], axis=0, keepdims=True), axis=1, keepdims=True) * (0.5 / d)
            dw_ref[...] = jnp.sum(acc[1], axis=0, keepdims=True)
            dg_ref[...] = jnp.sum(acc[2], axis=0, keepdims=True)

    big = pl.BlockSpec((tm, d), lambda i: (i, 0))
    row = pl.BlockSpec((1, d), lambda i: (0, 0))
    rowo = jax.ShapeDtypeStruct((1, d), F32)
    return pl.pallas_call(
        body, name="final_loss", grid=(nsteps,),
        in_specs=[big, big, row, big, row],
        out_specs=[pl.BlockSpec((1, 1), lambda i: (0, 0)), big, big, row, row],
        out_shape=[jax.ShapeDtypeStruct((1, 1), F32), jax.ShapeDtypeStruct((s, d), F32),
                   jax.ShapeDtypeStruct((s, d), BF16), rowo, rowo],
        scratch_shapes=[pltpu.VMEM((3, 8, d), F32)],
        compiler_params=_cparams("arbitrary"),
    )(x, target, w, branch, g)


def _rope_tables(pos_col):
    s = pos_col.shape[0]
    tq = _tile(s, 1024, 8)
    half = ROPE_DIM // 2
    inv = np.float32(ROPE_THETA) ** (-np.arange(0, ROPE_DIM, 2, dtype=np.float32) / np.float32(ROPE_DIM))
    inv_row = jnp.asarray(np.tile(inv.astype(np.float32), HEAD // half)[None, :])

    def body(p_ref, inv_ref, ct_ref, st_ref):
        lane = lax.broadcasted_iota(jnp.int32, (tq, HEAD), 1)
        ang = p_ref[...].astype(F32) * inv_ref[...]
        cs, sn = jnp.cos(ang), jnp.sin(ang)
        ct_ref[...] = jnp.where(lane < ROPE_DIM, cs, 1.0)
        st_ref[...] = jnp.where(lane < half, -sn, jnp.where(lane < ROPE_DIM, sn, 0.0))

    blk = pl.BlockSpec((tq, HEAD), lambda i: (i, 0))
    return pl.pallas_call(
        body, name="rope_tables", grid=(s // tq,),
        in_specs=[pl.BlockSpec((tq, 1), lambda i: (i, 0)), pl.BlockSpec((1, HEAD), lambda i: (0, 0))],
        out_specs=[blk, blk],
        out_shape=[jax.ShapeDtypeStruct((s, HEAD), F32)] * 2,
        compiler_params=_cparams("parallel"),
    )(pos_col, inv_row)


def _swap_halves(x):
    lane = lax.broadcasted_iota(jnp.int32, x.shape, 1)
    half = ROPE_DIM // 2
    return jnp.where(lane < half, pltpu.roll(x, HEAD - half, 1), pltpu.roll(x, half, 1))


def _rope(x, ct, st):
    return x * ct + _swap_halves(x) * st


def _rope_t(dy, ct, st):
    return dy * ct - _swap_halves(dy) * st


def _band_bias(bias_ref):
    qi = lax.broadcasted_iota(jnp.int32, (CHUNK, 2 * CHUNK), 0)
    kj = lax.broadcasted_iota(jnp.int32, (CHUNK, 2 * CHUNK), 1)
    band = (kj >= qi) & (kj <= qi + CHUNK)
    bias_ref[0] = jnp.where(band, 0.0, NEG)
    bias_ref[1] = jnp.where(band & (kj >= CHUNK), 0.0, NEG)


def _rows(start, size, d):
    return pl.ds(start, size) if d == 1 else pl.ds(start, size, stride=d)


def _batch_units(ub, d, c):
    n_sb = ATT_BATCH // d
    out = []
    for sb, r in [(ub * n_sb + t, r) for t in range(n_sb) for r in range(d)]:
        base = pl.multiple_of(sb * (CHUNK * d), CHUNK)
        first = jnp.where((c == 0) & (sb == 0), 1, 0)
        out.append((_rows(base + r, CHUNK, d), _rows(base + (ATT_CHUNK - CHUNK * d + r), 2 * CHUNK, d), first))
    return out


def _for_batches(d, fn):
    def step(ub, carry):
        fn(ub)
        return carry
    lax.fori_loop(0, ATT_CHUNK // CHUNK // ATT_BATCH, step, 0)


def _bdot(a, b, ca, cb):
    return lax.dot_general(a, b, (((ca,), (cb,)), ((0,), (0,))), preferred_element_type=F32)


WIDE = DILATIONS[-1]
assert ATT_CHUNK == CHUNK * WIDE


def _deint(x):
    return jnp.swapaxes(x.reshape(CHUNK, WIDE, HEAD), 0, 1).reshape(ATT_CHUNK, HEAD)


def _reint(x):
    return jnp.swapaxes(x.reshape(WIDE, CHUNK, HEAD), 0, 1).reshape(ATT_CHUNK, HEAD)


def _blk(r):
    return pl.ds(r * CHUNK, CHUNK)


def _key_rows(ref, r):
    return jnp.concatenate([ref[_blk(r), :], ref[pl.ds(ATT_CHUNK + r * CHUNK, CHUNK), :]], axis=0)


def _attn_specs(nh, nc):
    cur = lambda off: pl.BlockSpec((ATT_CHUNK, HEAD), lambda h, c: (jnp.minimum(c, nc - 1), off + h))
    prev = lambda off: pl.BlockSpec((ATT_CHUNK, HEAD), lambda h, c: (jnp.maximum(c - 1, 0), off + h))
    tcur = pl.BlockSpec((ATT_CHUNK, HEAD), lambda h, c: (jnp.minimum(c, nc - 1), 0))
    tprev = pl.BlockSpec((ATT_CHUNK, HEAD), lambda h, c: (jnp.maximum(c - 1, 0), 0))
    return [cur(0), prev(nh), cur(nh), prev(2 * nh), cur(2 * nh), tcur, tcur, tprev, tprev]


def _attn_fwd(z, ct, st, nh):
    s = z.shape[0]
    nc = s // ATT_CHUNK
    scale = HEAD ** -0.5

    def body(q_ref, kp_ref, kc_ref, vp_ref, vc_ref, ctc, stc, ctp, stp, o_ref, lse_ref, qf, kf, vf, ob, lb, bias, q16, k16, v16):
        c = pl.program_id(1)
        qf[...] = _rope(q_ref[...].astype(F32), ctc[...], stc[...])
        kf[0:ATT_CHUNK] = _rope(kp_ref[...].astype(F32), ctp[...], stp[...])
        kf[ATT_CHUNK:] = _rope(kc_ref[...].astype(F32), ctc[...], stc[...])
        vf[0:ATT_CHUNK] = vp_ref[...].astype(F32)
        vf[ATT_CHUNK:] = vc_ref[...].astype(F32)
        _band_bias(bias)

        def softmax_units(q_b, k_b, v_b, bias_b):
            sc = _bdot(q_b, k_b, 2, 2) * scale + bias_b
            m = jnp.max(sc, axis=2, keepdims=True)
            p = jnp.exp(sc - m)
            l = jnp.sum(p, axis=2, keepdims=True)
            return _bdot(p.astype(BF16), v_b, 2, 1) / l, m + jnp.log(l)

        for b, d in enumerate(DILATIONS[:-1]):
            def batch(ub, b=b, d=d):
                units = _batch_units(ub, d, c)
                q_b = jnp.stack([qf[qr, :] for qr, _, _ in units]).astype(BF16)
                k_b = jnp.stack([kf[kr, :] for _, kr, _ in units]).astype(BF16)
                v_b = jnp.stack([vf[kr, :] for _, kr, _ in units]).astype(BF16)
                o, lse = softmax_units(q_b, k_b, v_b, jnp.stack([bias[first] for _, _, first in units]))
                for j, (qr, _, _) in enumerate(units):
                    ob[b, qr, :] = o[j]
                    lb[b, qr, :] = jnp.broadcast_to(lse[j], (CHUNK, HEAD))
            _for_batches(d, batch)

        b = len(DILATIONS) - 1
        q16[...] = _deint(qf[...]).astype(BF16)
        for half in range(2):
            rows = pl.ds(half * ATT_CHUNK, ATT_CHUNK)
            k16[rows, :] = _deint(kf[rows, :]).astype(BF16)
            v16[rows, :] = _deint(vf[rows, :]).astype(BF16)
        bias_w = bias[jnp.where(c == 0, 1, 0)][None]
        for ub in range(WIDE // ATT_BATCH):
            rs = [ub * ATT_BATCH + j for j in range(ATT_BATCH)]
            o, lse = softmax_units(jnp.stack([q16[_blk(r), :] for r in rs]), jnp.stack([_key_rows(k16, r) for r in rs]),
                                   jnp.stack([_key_rows(v16, r) for r in rs]), bias_w)
            for j, r in enumerate(rs):
                qf[_blk(r), :] = o[j]
                kf[_blk(r), :] = jnp.broadcast_to(lse[j], (CHUNK, HEAD))
        ob[b] = _reint(qf[...])
        lb[b] = _reint(kf[0:ATT_CHUNK])

        mx = jnp.maximum(jnp.maximum(lb[0], lb[1]), lb[2])
        e0, e1, e2 = jnp.exp(lb[0] - mx), jnp.exp(lb[1] - mx), jnp.exp(lb[2] - mx)
        den = e0 + e1 + e2
        o_ref[...] = ((e0 * ob[0] + e1 * ob[1] + e2 * ob[2]) / den).astype(BF16)
        lse_ref[...] = mx + jnp.log(den)

    blk = pl.BlockSpec((ATT_CHUNK, HEAD), lambda h, c: (c, h))
    return pl.pallas_call(
        body, name="attn_fwd", grid=(nh, nc),
        in_specs=_attn_specs(nh, nc), out_specs=[blk, blk],
        out_shape=[jax.ShapeDtypeStruct((s, nh * HEAD), BF16), jax.ShapeDtypeStruct((s, nh * HEAD), F32)],
        scratch_shapes=[pltpu.VMEM((ATT_CHUNK, HEAD), F32), pltpu.VMEM((2 * ATT_CHUNK, HEAD), F32),
                        pltpu.VMEM((2 * ATT_CHUNK, HEAD), F32), pltpu.VMEM((3, ATT_CHUNK, HEAD), F32),
                        pltpu.VMEM((3, ATT_CHUNK, HEAD), F32), pltpu.VMEM((2, CHUNK, 2 * CHUNK), F32),
                        pltpu.VMEM((ATT_CHUNK, HEAD), BF16), pltpu.VMEM((2 * ATT_CHUNK, HEAD), BF16),
                        pltpu.VMEM((2 * ATT_CHUNK, HEAD), BF16)],
        compiler_params=_cparams("parallel", "arbitrary"),
    )(z, z, z, z, z, ct, st, ct, st)


def _attn_bwd(z, ct, st, da, o, lse, nh):
    s = z.shape[0]
    nc = s // ATT_CHUNK
    scale = HEAD ** -0.5

    def body(q_ref, kp_ref, kc_ref, vp_ref, vc_ref, ctc, stc, ctp, stp, do_ref, o_ref, lse_ref,
             dq_ref, dk_ref, dv_ref, qf, kf, vf, dof, dbar, dqa, dkf, dvf, bias, q16, k16, v16, do16):
        c = pl.program_id(1)

        @pl.when(c == 0)
        def _():
            dkf[...] = jnp.zeros_like(dkf)
            dvf[...] = jnp.zeros_like(dvf)

        @pl.when(c > 0)
        def _():
            dkf[0:ATT_CHUNK] = dkf[ATT_CHUNK:]
            dvf[0:ATT_CHUNK] = dvf[ATT_CHUNK:]
            dkf[ATT_CHUNK:] = jnp.zeros((ATT_CHUNK, HEAD), F32)
            dvf[ATT_CHUNK:] = jnp.zeros((ATT_CHUNK, HEAD), F32)

        @pl.when(c < nc)
        def _():
            qf[...] = _rope(q_ref[...].astype(F32), ctc[...], stc[...])
            kf[0:ATT_CHUNK] = _rope(kp_ref[...].astype(F32), ctp[...], stp[...])
            kf[ATT_CHUNK:] = _rope(kc_ref[...].astype(F32), ctc[...], stc[...])
            vf[0:ATT_CHUNK] = vp_ref[...].astype(F32)
            vf[ATT_CHUNK:] = vc_ref[...].astype(F32)
            dov = do_ref[...].astype(F32)
            dof[...] = dov
            dbar[...] = jnp.broadcast_to(jnp.sum(dov * o_ref[...].astype(F32), axis=1, keepdims=True), (ATT_CHUNK, HEAD))
            dqa[...] = jnp.zeros_like(dqa)
            _band_bias(bias)

            def grads(q_b, k_b, v_b, do_b, lse_b, dbar_b, bias_b):
                sc = _bdot(q_b, k_b, 2, 2) * scale + bias_b
                p = jnp.exp(sc - jnp.concatenate([lse_b, lse_b], axis=2))
                dp = _bdot(do_b, v_b, 2, 2)
                ds = (p * (dp - jnp.concatenate([dbar_b, dbar_b], axis=2)) * scale).astype(BF16)
                return _bdot(ds, k_b, 2, 1), _bdot(ds, q_b, 1, 1), _bdot(p.astype(BF16), do_b, 1, 1)

            for d in DILATIONS[:-1]:
                def batch(ub, d=d):
                    units = _batch_units(ub, d, c)
                    dq, dk, dv = grads(jnp.stack([qf[qr, :] for qr, _, _ in units]).astype(BF16),
                                       jnp.stack([kf[kr, :] for _, kr, _ in units]).astype(BF16),
                                       jnp.stack([vf[kr, :] for _, kr, _ in units]).astype(BF16),
                                       jnp.stack([dof[qr, :] for qr, _, _ in units]).astype(BF16),
                                       jnp.stack([lse_ref[qr, :] for qr, _, _ in units]),
                                       jnp.stack([dbar[qr, :] for qr, _, _ in units]),
                                       jnp.stack([bias[first] for _, _, first in units]))
                    for j, (qr, kr, _) in enumerate(units):
                        dqa[qr, :] += dq[j]
                        dkf[kr, :] += dk[j]
                        dvf[kr, :] += dv[j]
                _for_batches(d, batch)

            q16[...] = _deint(qf[...]).astype(BF16)
            do16[...] = _deint(dof[...]).astype(BF16)
            for half in range(2):
                rows = pl.ds(half * ATT_CHUNK, ATT_CHUNK)
                k16[rows, :] = _deint(kf[rows, :]).astype(BF16)
                v16[rows, :] = _deint(vf[rows, :]).astype(BF16)
            dof[...] = _deint(lse_ref[...])
            dbar[...] = _deint(dbar[...])
            bias_w = bias[jnp.where(c == 0, 1, 0)][None]
            for ub in range(WIDE // ATT_BATCH):
                rs = [ub * ATT_BATCH + j for j in range(ATT_BATCH)]
                dq, dk, dv = grads(jnp.stack([q16[_blk(r), :] for r in rs]), jnp.stack([_key_rows(k16, r) for r in rs]),
                                   jnp.stack([_key_rows(v16, r) for r in rs]), jnp.stack([do16[_blk(r), :] for r in rs]),
                                   jnp.stack([dof[_blk(r), :] for r in rs]), jnp.stack([dbar[_blk(r), :] for r in rs]), bias_w)
                for j, r in enumerate(rs):
                    qf[_blk(r), :] = dq[j]
                    kf[_blk(r), :] = dk[j][0:CHUNK]
                    kf[pl.ds(ATT_CHUNK + r * CHUNK, CHUNK), :] = dk[j][CHUNK:]
                    vf[_blk(r), :] = dv[j][0:CHUNK]
                    vf[pl.ds(ATT_CHUNK + r * CHUNK, CHUNK), :] = dv[j][CHUNK:]
            dqa[...] += _reint(qf[...])
            for half in range(2):
                rows = pl.ds(half * ATT_CHUNK, ATT_CHUNK)
                dkf[rows, :] += _reint(kf[rows, :])
                dvf[rows, :] += _reint(vf[rows, :])
            dq_ref[...] = _rope_t(dqa[...], ctc[...], stc[...]).astype(BF16)

        @pl.when(c > 0)
        def _():
            dk_ref[...] = _rope_t(dkf[0:ATT_CHUNK], ctp[...], stp[...]).astype(BF16)
            dv_ref[...] = dvf[0:ATT_CHUNK].astype(BF16)

    cur = pl.BlockSpec((ATT_CHUNK, HEAD), lambda h, c: (jnp.minimum(c, nc - 1), h))
    late = pl.BlockSpec((ATT_CHUNK, HEAD), lambda h, c: (jnp.maximum(c - 1, 0), h))
    shp = jax.ShapeDtypeStruct((s, nh * HEAD), BF16)
    big = pltpu.VMEM((2 * ATT_CHUNK, HEAD), F32)
    one = pltpu.VMEM((ATT_CHUNK, HEAD), F32)
    one16, big16 = pltpu.VMEM((ATT_CHUNK, HEAD), BF16), pltpu.VMEM((2 * ATT_CHUNK, HEAD), BF16)
    return pl.pallas_call(
        body, name="attn_bwd", grid=(nh, nc + 1),
        in_specs=_attn_specs(nh, nc) + [cur, cur, cur], out_specs=[cur, late, late],
        out_shape=[shp, shp, shp],
        scratch_shapes=[one, big, big, one, one, one, big, big, pltpu.VMEM((2, CHUNK, 2 * CHUNK), F32),
                        one16, big16, big16, one16],
        compiler_params=_cparams("parallel", "arbitrary"),
    )(z, z, z, z, z, ct, st, ct, st, da, o, lse)


_GELU_K = math.sqrt(2.0 / math.pi)


def _gelu(x):
    return 0.5 * x * (1.0 + jnp.tanh(_GELU_K * (x + 0.044715 * x * x * x)))


def _gelu_and_grad(x):
    t = jnp.tanh(_GELU_K * (x + 0.044715 * x * x * x))
    g = 0.5 * x * (1.0 + t)
    dg = 0.5 * (1.0 + t) + 0.5 * x * (1.0 - t * t) * (_GELU_K * (1.0 + 3 * 0.044715 * x * x))
    return g, dg


def _tril(w):
    ti = lax.broadcasted_iota(jnp.int32, (CHUNK, CHUNK), 0)
    si = lax.broadcasted_iota(jnp.int32, (CHUNK, CHUNK), 1)
    return jnp.where(si <= ti, w, 0.0)


def _sgu_fwd(z, w_s, b_col, ng, u_blk):
    s = z.shape[0]
    gw = ng * HEAD
    tq = _tile(s, 1024, CHUNK)

    def body(u_ref, v_ref, w_ref, b_ref, o_ref):
        for g in range(ng):
            wg = _tril(w_ref[g]).astype(BF16)
            cols = slice(g * HEAD, (g + 1) * HEAD)
            for n in range(tq // CHUNK):
                rows = slice(n * CHUNK, (n + 1) * CHUNK)
                gv = _gelu(v_ref[rows, cols].astype(F32)).astype(BF16)
                mixed = _dot_nn(wg, gv) + b_ref[g]
                o_ref[rows, cols] = (_gelu(u_ref[rows, cols].astype(F32)) * mixed).astype(BF16)

    full = pl.BlockSpec((ng, CHUNK, CHUNK), lambda i: (0, 0, 0))
    return pl.pallas_call(
        body, name="sgu_fwd", grid=(s // tq,),
        in_specs=[pl.BlockSpec((tq, gw), lambda i: (i, u_blk)), pl.BlockSpec((tq, gw), lambda i: (i, u_blk + 1)), full, full],
        out_specs=pl.BlockSpec((tq, gw), lambda i: (i, 0)),
        out_shape=jax.ShapeDtypeStruct((s, gw), BF16),
        compiler_params=_cparams("parallel"),
    )(z, z, w_s, b_col)


def _sgu_bwd(z, w_s, b_col, da, ng, u_blk, da_blk):
    s = z.shape[0]
    gw = ng * HEAD
    tq = _tile(s, 1024, CHUNK)
    nsteps = s // tq

    def body(u_ref, v_ref, w_ref, b_ref, do_ref, du_ref, dv_ref, dw_ref, db_ref):
        i = pl.program_id(0)

        @pl.when(i == 0)
        def _():
            dw_ref[...] = jnp.zeros_like(dw_ref)
            db_ref[...] = jnp.zeros_like(db_ref)

        for g in range(ng):
            wg = _tril(w_ref[g]).astype(BF16)
            cols = slice(g * HEAD, (g + 1) * HEAD)
            dw_acc = jnp.zeros((CHUNK, CHUNK), F32)
            db_acc = jnp.zeros((CHUNK, 1), F32)
            for n in range(tq // CHUNK):
                rows = slice(n * CHUNK, (n + 1) * CHUNK)
                gu, dgu = _gelu_and_grad(u_ref[rows, cols].astype(F32))
                gv, dgv = _gelu_and_grad(v_ref[rows, cols].astype(F32))
                gvb = gv.astype(BF16)
                mixed = _dot_nn(wg, gvb) + b_ref[g]
                dout = do_ref[rows, cols].astype(F32)
                du_ref[rows, cols] = (dout * mixed * dgu).astype(BF16)
                dmix = dout * gu
                dmb = dmix.astype(BF16)
                dv_ref[rows, cols] = (_dot_tn(wg, dmb) * dgv).astype(BF16)
                dw_acc += _dot_nt(dmb, gvb)
                db_acc += jnp.sum(dmix, axis=1, keepdims=True)
            dw_ref[g] += _tril(dw_acc)
            db_ref[g] += jnp.broadcast_to(db_acc, (CHUNK, CHUNK))

    full = pl.BlockSpec((ng, CHUNK, CHUNK), lambda i: (0, 0, 0))
    out = pl.BlockSpec((tq, gw), lambda i: (i, 0))
    return pl.pallas_call(
        body, name="sgu_bwd", grid=(nsteps,),
        in_specs=[pl.BlockSpec((tq, gw), lambda i: (i, u_blk)), pl.BlockSpec((tq, gw), lambda i: (i, u_blk + 1)), full, full,
                  pl.BlockSpec((tq, gw), lambda i: (i, da_blk))],
        out_specs=[out, out, full, full],
        out_shape=[jax.ShapeDtypeStruct((s, gw), BF16)] * 2 + [jax.ShapeDtypeStruct((ng, CHUNK, CHUNK), F32)] * 2,
        compiler_params=_cparams("arbitrary"),
    )(z, z, w_s, b_col, da)


def _shift_down(y, halo, k):
    return pltpu.roll(jnp.concatenate([halo, y], axis=0), k, 0)[8:]


def _shift_up(y, halo, k):
    n = y.shape[0]
    return pltpu.roll(jnp.concatenate([y, halo], axis=0), n + 8 - k, 0)[:n]


def _conv_fwd(z, cw):
    s, d3 = z.shape
    d = d3 // 3
    tq = _tile(s, 256, 8)

    def body(z_ref, zh_ref, cw_ref, a_ref):
        i = pl.program_id(0)
        zv = z_ref[...].astype(F32)
        zh = jnp.where(i > 0, zh_ref[...].astype(F32), 0.0)
        y = zv[:, d:2 * d] * zv[:, 2 * d:]
        yh = zh[:, d:2 * d] * zh[:, 2 * d:]
        cwv = cw_ref[...]
        conv = cwv[0:1] * _shift_down(y, yh, 2) + cwv[1:2] * _shift_down(y, yh, 1) + cwv[2:3] * y
        a_ref[...] = (zv[:, :d] * conv).astype(BF16)

    return pl.pallas_call(
        body, name="conv_fwd", grid=(s // tq,),
        in_specs=[pl.BlockSpec((tq, d3), lambda i: (i, 0)),
                  pl.BlockSpec((8, d3), lambda i: (jnp.maximum(i * (tq // 8) - 1, 0), 0)),
                  pl.BlockSpec((3, d), lambda i: (0, 0))],
        out_specs=pl.BlockSpec((tq, d), lambda i: (i, 0)),
        out_shape=jax.ShapeDtypeStruct((s, d), BF16),
        compiler_params=_cparams("parallel"),
    )(z, z, cw)


def _conv_bwd(z, cw, da):
    s, d3 = z.shape
    d = d3 // 3
    tq = _tile(s, 128, 8)
    nsteps = s // tq
    nblk8 = s // 8

    def body(z_ref, zp_ref, zn_ref, da_ref, dan_ref, cw_ref, dz_ref, dcw_ref, acc):
        i = pl.program_id(0)

        @pl.when(i == 0)
        def _():
            acc[...] = jnp.zeros_like(acc)

        zv = z_ref[...].astype(F32)
        zp = jnp.where(i > 0, zp_ref[...].astype(F32), 0.0)
        zn = jnp.where(i < nsteps - 1, zn_ref[...].astype(F32), 0.0)
        dav = da_ref[...].astype(F32)
        dan = jnp.where(i < nsteps - 1, dan_ref[...].astype(F32), 0.0)
        gb, gc, hx = zv[:, :d], zv[:, d:2 * d], zv[:, 2 * d:]
        y = gc * hx
        yp = zp[:, d:2 * d] * zp[:, 2 * d:]
        cwv = cw_ref[...]
        y1, y2 = _shift_down(y, yp, 1), _shift_down(y, yp, 2)
        conv = cwv[0:1] * y2 + cwv[1:2] * y1 + cwv[2:3] * y
        dconv = dav * gb
        dconv_n = dan * zn[:, :d]
        dy = cwv[2:3] * dconv + cwv[1:2] * _shift_up(dconv, dconv_n, 1) + cwv[0:1] * _shift_up(dconv, dconv_n, 2)
        dz_ref[:, :d] = (dav * conv).astype(BF16)
        dz_ref[:, d:2 * d] = (dy * hx).astype(BF16)
        dz_ref[:, 2 * d:] = (dy * gc).astype(BF16)
        acc[0] += _colsum8(dconv * y2)
        acc[1] += _colsum8(dconv * y1)
        acc[2] += _colsum8(dconv * y)

        @pl.when(i == nsteps - 1)
        def _():
            for j in range(3):
                dcw_ref[j:j + 1, :] = jnp.sum(acc[j], axis=0, keepdims=True)

    return pl.pallas_call(
        body, name="conv_bwd", grid=(nsteps,),
        in_specs=[pl.BlockSpec((tq, d3), lambda i: (i, 0)),
                  pl.BlockSpec((8, d3), lambda i: (jnp.maximum(i * (tq // 8) - 1, 0), 0)),
                  pl.BlockSpec((8, d3), lambda i: (jnp.minimum((i + 1) * (tq // 8), nblk8 - 1), 0)),
                  pl.BlockSpec((tq, d), lambda i: (i, 0)),
                  pl.BlockSpec((8, d), lambda i: (jnp.minimum((i + 1) * (tq // 8), nblk8 - 1), 0)),
                  pl.BlockSpec((3, d), lambda i: (0, 0))],
        out_specs=[pl.BlockSpec((tq, d3), lambda i: (i, 0)), pl.BlockSpec((3, d), lambda i: (0, 0))],
        out_shape=[jax.ShapeDtypeStruct((s, d3), BF16), jax.ShapeDtypeStruct((3, d), F32)],
        scratch_shapes=[pltpu.VMEM((3, 8, d), F32)],
        compiler_params=_cparams("arbitrary"),
    )(z, z, z, da, da, cw)


def _swiglu_epilogue(gate, up):
    return gate, up, gate / (1.0 + jnp.exp(-gate)) * up


def _swiglu_bwd_epilogue(dact, gt, up):
    g, u = gt.astype(F32), up.astype(F32)
    sg = 1.0 / (1.0 + jnp.exp(-g))
    return dact * u * (sg * (1.0 + g * (1.0 - sg))), dact * (g * sg)


def _ada_fwd(c_all, ada_w):
    nl, d, n8 = ada_w.shape
    tn = _tile(n8, 768, 128)

    def body(c_ref, w_ref, o_ref):
        cv = c_ref[...]
        act = (cv / (1.0 + jnp.exp(-cv))).astype(BF16)
        o_ref[0] = _dot_nn(act, w_ref[0].astype(BF16))

    return pl.pallas_call(
        body, name="ada_fwd", grid=(nl, n8 // tn),
        in_specs=[pl.BlockSpec((N_DEV, d), lambda l, j: (0, 0)), pl.BlockSpec((1, d, tn), lambda l, j: (l, 0, j))],
        out_specs=pl.BlockSpec((1, N_DEV, tn), lambda l, j: (l, 0, j)),
        out_shape=jax.ShapeDtypeStruct((nl, N_DEV, n8), F32),
        compiler_params=_cparams("parallel", "parallel"),
    )(c_all, ada_w)


def _ada_wgrad(c_all, dmod_cols):
    nl, _, n8 = dmod_cols.shape
    d = c_all.shape[1]
    tn = _tile(n8, 768, 128)

    def body(c_ref, g_ref, o_ref):
        cv = c_ref[...]
        act = (cv / (1.0 + jnp.exp(-cv))).astype(BF16)
        o_ref[0] = _dot_tn(act, g_ref[0].astype(BF16))

    return pl.pallas_call(
        body, name="ada_wgrad", grid=(nl, n8 // tn),
        in_specs=[pl.BlockSpec((N_DEV, d), lambda l, j: (0, 0)), pl.BlockSpec((1, N_DEV, tn), lambda l, j: (l, 0, j))],
        out_specs=pl.BlockSpec((1, d, tn), lambda l, j: (l, 0, j)),
        out_shape=jax.ShapeDtypeStruct((nl, d, n8), F32),
        compiler_params=_cparams("parallel", "parallel"),
    )(c_all, dmod_cols)


def _adamw(name, pieces, w, m, v):
    npc, r, c = pieces.shape
    tr = _tile(r, max(8, (1 << 19) // c // 8 * 8), 8)
    bc1, bc2 = 1.0 - ADAM_B1 ** ADAM_STEP, 1.0 - ADAM_B2 ** ADAM_STEP

    def body(p_ref, w_ref, m_ref, v_ref, g_ref, d_ref, nm_ref, nv_ref):
        g = p_ref[0].astype(F32)
        for i in range(1, npc):
            g = g + p_ref[i].astype(F32)
        nm = ADAM_B1 * m_ref[...] + (1.0 - ADAM_B1) * g
        nv = ADAM_B2 * v_ref[...] + (1.0 - ADAM_B2) * (g * g)
        g_ref[...] = g
        nm_ref[...] = nm
        nv_ref[...] = nv
        d_ref[...] = -ADAM_LR * ((nm / bc1) / (jnp.sqrt(nv / bc2) + ADAM_EPS) + ADAM_WD * w_ref[...])

    blk = pl.BlockSpec((tr, c), lambda i: (i, 0))
    return pl.pallas_call(
        body, name=name, grid=(r // tr,),
        in_specs=[pl.BlockSpec((npc, tr, c), lambda i: (0, i, 0)), blk, blk, blk],
        out_specs=[blk] * 4, out_shape=[jax.ShapeDtypeStruct((r, c), F32)] * 4,
        compiler_params=_cparams("parallel"),
    )(pieces, w, m, v)


def _adamw_transposed(name, pieces, w, m, v):
    npc, nl, c, k = pieces.shape
    tk = _tile(k, 256, 128)
    bc1, bc2 = 1.0 - ADAM_B1 ** ADAM_STEP, 1.0 - ADAM_B2 ** ADAM_STEP

    def body(p_ref, w_ref, m_ref, v_ref, g_ref, d_ref, nm_ref, nv_ref):
        gt = p_ref[0].astype(F32)
        for i in range(1, npc):
            gt = gt + p_ref[i].astype(F32)
        g = gt.T
        nm = ADAM_B1 * m_ref[...] + (1.0 - ADAM_B1) * g
        nv = ADAM_B2 * v_ref[...] + (1.0 - ADAM_B2) * (g * g)
        g_ref[...] = g
        nm_ref[...] = nm
        nv_ref[...] = nv
        d_ref[...] = -ADAM_LR * ((nm / bc1) / (jnp.sqrt(nv / bc2) + ADAM_EPS) + ADAM_WD * w_ref[...])

    nk = k // tk
    blk = pl.BlockSpec((tk, c), lambda l, i: (l * nk + i, 0))
    res = pl.pallas_call(
        body, name=name, grid=(nl, nk),
        in_specs=[pl.BlockSpec((npc, None, c, tk), lambda l, i: (0, l, 0, i)), blk, blk, blk],
        out_specs=[blk] * 4, out_shape=[jax.ShapeDtypeStruct((nl * k, c), F32)] * 4,
        compiler_params=_cparams("parallel", "parallel"),
    )(pieces, w.reshape(nl * k, c), m.reshape(nl * k, c), v.reshape(nl * k, c))
    return [r.reshape(nl, k, c) for r in res]


def _place():
    x, y, c = lax.axis_index("x"), lax.axis_index("y"), lax.axis_index("c")
    return x, y, c


def _all_gather_small(name, x_shard):
    m_per, n = x_shard.shape

    def body(x_ref, out_ref, token_ref, send_sems, recv_sems, local_sem):
        token_ref[...] = jnp.zeros_like(token_ref)
        x, y, c = _place()
        me, sibling = (x, y, c), (x, y, 1 - c)
        chips = [(1 - x, y), (x, 1 - y), (1 - x, 1 - y)]

        def rows(px, py, pc):
            return out_ref.at[pl.ds((4 * px + 2 * py + pc) * m_per, m_per), :]

        def copy(k, block, to, src=None):
            return pltpu.make_async_remote_copy(
                src_ref=rows(*block) if src is None else src, dst_ref=rows(*block),
                send_sem=send_sems.at[k], recv_sem=recv_sems.at[k], device_id=to, device_id_type=MESH)

        mine = pltpu.make_async_copy(x_ref, rows(*me), local_sem)
        mine.start()
        first = [copy(0, me, sibling, src=x_ref)]
        first += [copy(1 + j, me, (*chip, c), src=x_ref) for j, chip in enumerate(chips)]
        for cp in first:
            cp.start()
        passed = [copy(4 + j, (*chip, c), sibling) for j, chip in enumerate(chips)]
        for j, chip in enumerate(chips):
            copy(1 + j, (*chip, c), me).wait_recv()
            passed[j].start()
        copy(0, sibling, me).wait_recv()
        for j, chip in enumerate(chips):
            copy(4 + j, (*chip, 1 - c), me).wait_recv()
        for cp in first + passed:
            cp.wait_send()
        mine.wait()

    vmem = pl.BlockSpec(memory_space=pltpu.VMEM)
    return pl.pallas_call(
        body, name=name,
        out_shape=[jax.ShapeDtypeStruct((N_DEV * m_per, n), x_shard.dtype), jax.ShapeDtypeStruct((8, HEAD), F32)],
        in_specs=[vmem], out_specs=[vmem, vmem],
        scratch_shapes=[pltpu.SemaphoreType.DMA((7,)), pltpu.SemaphoreType.DMA((7,)), pltpu.SemaphoreType.DMA],
        compiler_params=pltpu.CompilerParams(vmem_limit_bytes=VMEM_LIMIT),
    )(x_shard)


_HBM = pl.BlockSpec(memory_space=pltpu.HBM)
_SEM = pl.BlockSpec(memory_space=pltpu.SEMAPHORE)
_EFFECT = pltpu.SideEffectType.DATAFLOW_SIDE_EFFECTING


def _peers():
    x, y, c = _place()
    peers = []
    for k in range(1, N_DEV):
        px = 1 - x if k & 4 else x
        py = 1 - y if k & 2 else y
        pc = 1 - c if k & 1 else c
        peers.append(((px, py, pc), 4 * px + 2 * py + pc))
    return 4 * x + 2 * y + c, peers


def _push_start(name, srcs, lands, src_view, dst_view):
    na = len(srcs)
    n = na * (N_DEV - 1)

    def body(*refs):
        s_refs, l_refs = refs[:na], refs[na:2 * na]
        send_sems, recv_sems, token = refs[2 * na], refs[2 * na + 1], refs[-1]
        me, peers = _peers()
        for a in range(na):
            for k, (dev, idx) in enumerate(peers):
                pltpu.make_async_remote_copy(
                    src_ref=src_view(s_refs[a], idx), dst_ref=dst_view(l_refs[a], me),
                    send_sem=send_sems.at[a * (N_DEV - 1) + k], recv_sem=recv_sems.at[a * (N_DEV - 1) + k],
                    device_id=dev, device_id_type=MESH).start()
        token[...] = jnp.zeros_like(token)

    outs = pl.pallas_call(
        body, name=name,
        out_shape=(pltpu.SemaphoreType.DMA((n,)), pltpu.SemaphoreType.DMA((n,)),
                   *[pltpu.HBM(t.shape, t.dtype) for t in list(srcs) + list(lands)], jax.ShapeDtypeStruct((8, HEAD), F32)),
        in_specs=[_HBM] * (2 * na),
        out_specs=(_SEM, _SEM, *[_HBM] * (2 * na), pl.BlockSpec(memory_space=pltpu.VMEM)),
        input_output_aliases={i: 2 + i for i in range(2 * na)},
        compiler_params=pltpu.CompilerParams(has_side_effects=_EFFECT),
    )(*[pltpu.with_memory_space_constraint(t, pltpu.HBM) for t in list(srcs) + list(lands)])
    return outs[0], outs[1], list(outs[2:2 + na]), list(outs[2 + na:2 + 2 * na]), outs[-1]


def _push_wait(name, send_sems, recv_sems, srcs, lands, after, src_view, dst_view):
    na = len(srcs)

    def body(*refs):
        s_refs, l_refs = refs[:na], refs[na:2 * na]
        send_sems, recv_sems = refs[2 * na], refs[2 * na + 1]
        me, peers = _peers()
        for a in range(na):
            for k, (dev, idx) in enumerate(peers):
                cp = pltpu.make_async_remote_copy(
                    src_ref=src_view(s_refs[a], idx), dst_ref=dst_view(l_refs[a], idx),
                    send_sem=send_sems.at[a * (N_DEV - 1) + k], recv_sem=recv_sems.at[a * (N_DEV - 1) + k],
                    device_id=dev, device_id_type=MESH)
                cp.wait_send()
                cp.wait_recv()

    outs = pl.pallas_call(
        body, name=name,
        out_shape=[pltpu.HBM(t.shape, t.dtype) for t in list(srcs) + list(lands)],
        in_specs=[_HBM] * (2 * na) + [_SEM, _SEM, pl.BlockSpec(memory_space=pl.ANY)],
        out_specs=[_HBM] * (2 * na),
        input_output_aliases={i: i for i in range(2 * na)},
        compiler_params=pltpu.CompilerParams(has_side_effects=_EFFECT),
    )(*srcs, *lands, send_sems, recv_sems, after)
    return list(outs[na:])


def _gather_start(name, shards):
    lands = [lax.empty((N_DEV,) + t.shape, t.dtype) for t in shards]
    return _push_start(name, shards, lands, lambda ref, idx: ref, lambda ref, slot: ref.at[slot])


def _gather_wait(name, started, shards, after, me):
    send_sems, recv_sems, srcs, lands, _ = started
    lands = _push_wait(name, send_sems, recv_sems, srcs, lands, after, lambda ref, idx: ref, lambda ref, slot: ref.at[slot])
    return [lax.dynamic_update_index_in_dim(g, t, me, 0) for g, t in zip(lands, shards)]


def _pack_rows(parts, width):
    rows, offs, r = [], [], 0
    for p in parts:
        flat = p.reshape(-1).astype(F32)
        nr = -(-flat.shape[0] // (8 * width)) * 8
        rows.append(jnp.pad(flat, (0, nr * width - flat.shape[0])).reshape(nr, width))
        offs.append((r, flat.shape[0], p.shape))
        r += nr
    return jnp.concatenate(rows, axis=0), offs, r


def _unpack_rows(slab, offs, width):
    lead = slab.shape[:-2]
    out = []
    for r0, n, shape in offs:
        nr = -(-n // width)
        out.append(slab[..., r0:r0 + nr, :].reshape(lead + (nr * width,))[..., :n].reshape(lead + tuple(shape)))
    return out


def kernel(x, c, positions, ada_w, ada_b, norm_mix, norm_ffn, ab_w_in, sgu_w, sgu_b, ab_w_out, conv_w_in, conv_w, conv_w_out, ffn_w_gate, ffn_w_up, ffn_w_down, final_norm, loss_target, m_ada_w, m_ada_b, m_norm_mix, m_norm_ffn, m_ab_w_in, m_sgu_w, m_sgu_b, m_ab_w_out, m_conv_w_in, m_conv_w, m_conv_w_out, m_ffn_w_gate, m_ffn_w_up, m_ffn_w_down, m_final_norm, v_ada_w, v_ada_b, v_norm_mix, v_norm_ffn, v_ab_w_in, v_sgu_w, v_sgu_b, v_ab_w_out, v_conv_w_in, v_conv_w, v_conv_w_out, v_ffn_w_gate, v_ffn_w_up, v_ffn_w_down, v_final_norm):
    xi, yi, ci = _place()
    me = 4 * xi + 2 * yi + ci
    s, d = x.shape[1], x.shape[2]
    depth = ada_w.shape[0]
    n_even = ab_w_in.shape[0]
    nh_mix = d // HEAD
    nh = 3 * nh_mix // 4
    ng = nh_mix - nh
    aw, gw = nh * HEAD, ng * HEAD
    assert d <= FULL_ROW and s % ATT_CHUNK == 0
    x0 = x[0]
    target = loss_target[0]
    n_odd, cwid, d8 = conv_w.shape

    width = 512
    slab, offs, _ = _pack_rows([c, conv_w], width)
    gathered, _ = _all_gather_small("gather_cond", slab)
    c_parts, cw_parts = _unpack_rows(gathered.reshape(N_DEV, -1, width), offs, width)
    c_all = c_parts.reshape(N_DEV, d)
    conv_w_full = jnp.transpose(cw_parts, (1, 2, 0, 3)).reshape(n_odd, cwid, d)

    mod_cols = _ada_fwd(c_all, ada_w)
    n8 = mod_cols.shape[2]
    mod_all, token = _all_gather_small("gather_mod", mod_cols.reshape(depth * N_DEV, n8))
    mod_mine = lax.dynamic_index_in_dim(mod_all.reshape(N_DEV, depth, N_DEV, n8), me, axis=2, keepdims=False)

    def cols(w):
        return jnp.swapaxes(w, 0, 1)

    def mixer_weights(l):
        w_i, w_o = (ab_w_in, ab_w_out) if l % 2 == 0 else (conv_w_in, conv_w_out)
        return cols(w_i[l // 2]), w_o[l // 2]

    def ffn_weights(l):
        return [cols(ffn_w_gate[l]), cols(ffn_w_up[l]), ffn_w_down[l]]

    groups = [[mixer_weights(0)[0]], [mixer_weights(0)[1]], ffn_weights(0)]
    for l in range(1, depth):
        groups += [list(mixer_weights(l)), ffn_weights(l)]
    gathers, tok = [], token[0, 0]
    for n, ws in enumerate(groups):
        shards = [(w + tok).astype(BF16) for w in ws]
        started = _gather_start(f"gather_start_{n}", shards)
        gathers.append((started, shards))
        tok = started[4][0, 0]

    def weights_of_group(n, after):
        started, shards = gathers[n]
        return _gather_wait(f"gather_wait_{n}", started, shards, after, me)

    def plain_rows(g):
        return g.reshape(g.shape[0] * g.shape[1], g.shape[2])

    mod = jnp.transpose(mod_mine, (1, 0, 2)).reshape(depth, N_DEV * n8) + ada_b + tok
    mods = mod.reshape(depth, 6, 1, d)

    ct, st = _rope_tables(positions.reshape(s, 1))
    b_col = jnp.broadcast_to(sgu_b[..., None], sgu_b.shape + (CHUNK,))
    u_blk = 3 * aw // gw

    stream = [x0]
    saved = []
    w_in, w_out, w_gate, w_up, w_down = [[None] * depth for _ in range(5)]
    xcur = x0
    for l in range(depth):
        sh_m, sc_m, g_m, sh_f, sc_f, g_f = [mods[l, j] for j in range(6)]
        i = l // 2
        if l == 0:
            (g_in,) = weights_of_group(0, mod)
        else:
            g_in, g_out = weights_of_group(1 + 2 * l, xcur)
        w_in[l] = plain_rows(g_in)
        h = _norm_mod_fwd(xcur, norm_mix[l][None], sc_m, sh_m)
        if l % 2 == 0:
            z = _matmul("mix_in", "nt", h, w_in[l], [BF16])
            attn, lse = _attn_fwd(z, ct, st, nh)
            sgu = _sgu_fwd(z, sgu_w[i], b_col[i], ng, u_blk)
            a = jnp.concatenate([attn, sgu], axis=1)
            mixer_saved = (z, a, lse)
        else:
            z = _matmul("conv_in", "nt", h, w_in[l], [BF16])
            a = _conv_fwd(z, conv_w_full[i])
            mixer_saved = (z, a, None)
        if l == 0:
            (g_out,) = weights_of_group(1, a)
        w_out[l] = plain_rows(g_out)
        x1, mix, h2 = _matmul("mix_out", "nn", a, w_out[l], [F32, BF16, BF16],
                              extras=[(xcur, "mn"), (g_m, "n"), (norm_ffn[l][None], "n"), (sc_f, "n"), (sh_f, "n")],
                              epilogue=_gated_add_norm_epilogue)
        g_gate, g_up, g_down = weights_of_group(2 + 2 * l, x1)
        w_gate[l], w_up[l], w_down[l] = plain_rows(g_gate), plain_rows(g_up), plain_rows(g_down)
        gt, up, act = _matmul("ffn_in", "nt", h2, [w_gate[l], w_up[l]], [BF16, BF16, BF16], epilogue=_swiglu_epilogue)
        x2, f = _matmul("ffn_down", "nn", act, w_down[l], [F32, BF16], extras=[(x1, "mn"), (g_f, "n")],
                        epilogue=lambda acc, r, gv: (r + gv * acc, acc))
        saved.append((h, mixer_saved, mix, x1, h2, gt, up, act, f))
        stream.append(x2)
        xcur = x2

    f_last = saved[-1][8]
    loss_part, dx, dbr, d_final, dg = _final_loss(xcur, target, final_norm[None], f_last, mods[depth - 1, 5])
    loss = lax.psum(loss_part[0, 0], ("x", "y", "c"))

    dmod = [[None] * 6 for _ in range(depth)]
    d_norm_mix, d_norm_ffn = [None] * depth, [None] * depth
    d_sgu_w, d_sgu_b, d_conv_w = [None] * n_even, [None] * n_even, [None] * n_odd

    big = {"in": ab_w_in, "out": ab_w_out, "cin": conv_w_in, "cout": conv_w_out,
           "gate": ffn_w_gate, "up": ffn_w_up, "down": ffn_w_down}
    col_sharded = ("in", "cin", "gate", "up")
    lands = {k: lax.empty((N_DEV, w.shape[0]) + (w.shape[1:][::-1] if k in col_sharded else w.shape[1:]), BF16)
             for k, w in big.items()}
    own = {k: [None] * w.shape[0] for k, w in big.items()}
    pending = {"ffn": None, "mix": None}

    def exchange_finish(tag, after):
        (send_sems, recv_sems, srcs, lds, _), keys, li, layer = pending[tag]
        lds = _push_wait(f"exchange_wait_{tag}_{layer}", send_sems, recv_sems, srcs, lds, after,
                         lambda ref, idx: ref.at[idx], lambda ref, slot: ref.at[slot, li])
        for k, ld in zip(keys, lds):
            lands[k] = ld
        pending[tag] = None

    def exchange_start(tag, layer, keys, li, grads):
        if pending[tag] is not None:
            exchange_finish(tag, grads[0])
        for k, g in zip(keys, grads):
            own[k][li] = lax.dynamic_index_in_dim(g, me, 0, keepdims=False)
        started = _push_start(f"exchange_start_{tag}_{layer}", grads, [lands[k] for k in keys],
                              lambda ref, idx: ref.at[idx], lambda ref, slot: ref.at[slot, li])
        pending[tag] = (started, keys, li, layer)
        return started[4][0, 0]

    def row_shards(g):
        return g.reshape(N_DEV, g.shape[0] // N_DEV, g.shape[1])

    for l in reversed(range(depth)):
        sh_m, sc_m, g_m, sh_f, sc_f, g_f = [mods[l, j] for j in range(6)]
        h, (z, a, lse), mix, x1, h2, gt, up, act, f = saved[l]
        i = l // 2
        dmod[l][5] = dg
        gw_down = _matmul("ffn_down_wgrad", "tn", act, dbr, [BF16])
        dgt, dup = _matmul("ffn_down_dgrad", "nt", dbr, w_down[l], [BF16, BF16], extras=[(gt, "mn"), (up, "mn")],
                           epilogue=_swiglu_bwd_epilogue)
        gw_gate = _matmul("ffn_in_wgrad", "tn", dgt, h2, [BF16])
        gw_up = _matmul("ffn_in_wgrad", "tn", dup, h2, [BF16])
        tok = exchange_start("ffn", l, ("gate", "up", "down"), l, [row_shards(gw_gate), row_shards(gw_up), row_shards(gw_down)])
        dh2 = _matmul("ffn_in_dgrad", "nn", [dgt, dup], [w_gate[l], w_up[l]], [BF16], after=tok)
        dx, dbr, dmod[l][3], dmod[l][4], d_norm_ffn[l], dg = _norm_mod_bwd(x1, dh2, norm_ffn[l][None], sc_f, dx, mix, g_m)
        dmod[l][2] = dg
        gw_out = _matmul("mix_out_wgrad", "tn", a, dbr, [BF16])
        da = _matmul("mix_out_dgrad", "nt", dbr, w_out[l], [BF16])
        if l % 2 == 0:
            dq, dk, dv = _attn_bwd(z, ct, st, da, a, lse, nh)
            du, dvg, d_sgu_w[i], dbb = _sgu_bwd(z, sgu_w[i], b_col[i], da, ng, u_blk, aw // gw)
            d_sgu_b[i] = dbb[:, :, 0]
            dz = jnp.concatenate([dq, dk, dv, du, dvg], axis=1)
            gw_in = _matmul("mix_in_wgrad", "tn", dz, h, [BF16])
            dh = _matmul("mix_in_dgrad", "nn", dz, w_in[l], [BF16])
        else:
            dz, d_conv_w[i] = _conv_bwd(z, conv_w_full[i], da)
            gw_in = _matmul("conv_in_wgrad", "tn", dz, h, [BF16])
            dh = _matmul("conv_in_dgrad", "nn", dz, w_in[l], [BF16])
        mix_group = ("mix", l, ("in", "out") if l % 2 == 0 else ("cin", "cout"), i, [row_shards(gw_in), row_shards(gw_out)])
        w_norm = norm_mix[l][None]
        if l > 0:
            w_norm = w_norm + exchange_start(*mix_group)
            f_prev, g_prev = saved[l - 1][8], mods[l - 1, 5]
            dx, dbr, dmod[l][0], dmod[l][1], d_norm_mix[l], dg = _norm_mod_bwd(stream[l], dh, w_norm, sc_m, dx, f_prev, g_prev)
        else:
            dx, dmod[l][0], dmod[l][1], d_norm_mix[l] = _norm_mod_bwd(stream[l], dh, w_norm, sc_m, dx)
    grad_x = dx[None]

    dmod_mine = jnp.stack([jnp.concatenate([v.reshape(d) for v in dmod[l]]) for l in range(depth)])
    small = [dmod_mine, jnp.concatenate(d_norm_mix), jnp.concatenate(d_norm_ffn), jnp.stack(d_sgu_w), jnp.stack(d_sgu_b),
             d_final, jnp.stack(d_conv_w)]
    slab, offs, _ = _pack_rows(small, width)
    gathered, token = _all_gather_small("gather_small_grads", slab)
    p_dmod, p_nmix, p_nffn, p_sguw, p_sgub, p_final, p_convw = _unpack_rows(gathered.reshape(N_DEV, -1, width), offs, width)
    mix_group[4][0] = mix_group[4][0] + token[0, 0].astype(BF16)
    p_dmod = p_dmod + exchange_start(*mix_group)

    outs = {}

    def update(name, pieces, w, m, v):
        shape = w.shape
        cdim = shape[-1]
        res = _adamw("adamw_" + name, pieces.reshape(pieces.shape[0], -1, cdim), w.reshape(-1, cdim),
                     m.reshape(-1, cdim), v.reshape(-1, cdim))
        outs[name] = [r.reshape(shape) for r in res]

    update("ada_b", p_dmod.reshape(N_DEV, depth, 6 * d), ada_b, m_ada_b, v_ada_b)
    update("norm_mix", p_nmix.reshape(N_DEV, depth, d), norm_mix, m_norm_mix, v_norm_mix)
    update("norm_ffn", p_nffn.reshape(N_DEV, depth, d), norm_ffn, m_norm_ffn, v_norm_ffn)
    update("sgu_w", p_sguw, sgu_w, m_sgu_w, v_sgu_w)
    update("sgu_b", p_sgub.reshape(N_DEV, 1, -1), sgu_b.reshape(1, -1), m_sgu_b.reshape(1, -1), v_sgu_b.reshape(1, -1))
    outs["sgu_b"] = [r.reshape(sgu_b.shape) for r in outs["sgu_b"]]
    update("final_norm", p_final.reshape(N_DEV, 1, d), final_norm[None], m_final_norm[None], v_final_norm[None])
    outs["final_norm"] = [r.reshape(final_norm.shape) for r in outs["final_norm"]]
    cw_mine = lax.dynamic_slice_in_dim(p_convw.reshape(N_DEV, n_odd, cwid, d), me * d8, d8, axis=3)
    update("conv_w", cw_mine, conv_w, m_conv_w, v_conv_w)

    dmod_cols = lax.dynamic_slice_in_dim(p_dmod.reshape(N_DEV, depth, 6 * d), me * n8, n8, axis=2)
    g_ada = _ada_wgrad(c_all, jnp.transpose(dmod_cols, (1, 0, 2)))
    update("ada_w", g_ada[None], ada_w, m_ada_w, v_ada_w)

    names = {"in": "ab_w_in", "out": "ab_w_out", "cin": "conv_w_in", "cout": "conv_w_out",
             "gate": "ffn_w_gate", "up": "ffn_w_up", "down": "ffn_w_down"}
    moments = {"in": (m_ab_w_in, v_ab_w_in), "out": (m_ab_w_out, v_ab_w_out), "cin": (m_conv_w_in, v_conv_w_in),
               "cout": (m_conv_w_out, v_conv_w_out), "gate": (m_ffn_w_gate, v_ffn_w_gate), "up": (m_ffn_w_up, v_ffn_w_up),
               "down": (m_ffn_w_down, v_ffn_w_down)}

    def update_big(k):
        pieces = lax.dynamic_update_slice(lands[k], jnp.stack(own[k])[None], (me, 0, 0, 0))
        if k in col_sharded:
            outs[names[k]] = _adamw_transposed("adamw_" + names[k], pieces, big[k], *moments[k])
        else:
            update(names[k], pieces, big[k], *moments[k])

    exchange_finish("ffn", g_ada)
    for k in ("gate", "up", "down", "cin", "cout"):
        update_big(k)
    done = sum(outs[n][0][(0,) * outs[n][0].ndim] for n in outs)
    exchange_finish("mix", done.reshape(1, 1))
    for k in ("in", "out"):
        update_big(k)

    order = ["ada_w", "ada_b", "norm_mix", "norm_ffn", "ab_w_in", "sgu_w", "sgu_b", "ab_w_out", "conv_w_in", "conv_w",
             "conv_w_out", "ffn_w_gate", "ffn_w_up", "ffn_w_down", "final_norm"]
    return (loss, grad_x, *[outs[n][0] for n in order], *[outs[n][1] for n in order],
            *[outs[n][2] for n in order], *[outs[n][3] for n in order])
```

```python
import functools
import math

import numpy as np
import jax
import jax.numpy as jnp
from jax import lax
from jax.experimental import pallas as pl
from jax.experimental.pallas import tpu as pltpu

F32, BF16 = jnp.float32, jnp.bfloat16
MESH = pl.DeviceIdType.MESH
N_DEV = 8
EPS = 1e-6
HEAD = 128
CHUNK = 128
DILATIONS = (1, 4, 16)
ATT_CHUNK = CHUNK * DILATIONS[-1]
ATT_BATCH = 8
ROPE_THETA = 500000.0
ROPE_DIM = HEAD // 4
NEG = -1e30
ADAM_LR, ADAM_B1, ADAM_B2, ADAM_EPS, ADAM_WD, ADAM_STEP = 0.001, 0.9, 0.999, 1e-08, 0.01, 10
VMEM_LIMIT = 56 * 1024 * 1024


def _cparams(*sem):
    return pltpu.CompilerParams(dimension_semantics=sem or None, vmem_limit_bytes=VMEM_LIMIT)


def _tile(n, pref, unit):
    t = (min(pref, n) // unit) * unit
    while t >= unit:
        if n % t == 0:
            return t
        t -= unit
    return n


def _dot(a, b, dims):
    return lax.dot_general(a, b, (dims, ((), ())), preferred_element_type=F32)


def _dot_nn(a, b):
    return _dot(a, b, ((1,), (0,)))


def _dot_nt(a, b):
    return _dot(a, b, ((1,), (1,)))


def _dot_tn(a, b):
    return _dot(a, b, ((0,), (0,)))


FULL_ROW = 2048
_TILE_DEFAULT = dict(tm=1024, tn=512, tk=2816)
_TILES = {
    "ffn_down": dict(tn=1024),
    "ffn_down_wgrad": dict(tm=1408, tn=1024),
    "ffn_in_wgrad": dict(tm=1408, tn=1024), "mix_in_wgrad": dict(tm=1408, tn=1024), "conv_in_wgrad": dict(tm=1536, tn=1024),
    "mix_in_dgrad": dict(tn=1024, tk=1408), "conv_in_dgrad": dict(tn=1024, tk=1536),
    "mix_out": dict(tm=512, tn=FULL_ROW),
    "mix_out_wgrad": dict(tn=1024), "mix_out_dgrad": dict(tn=1024),
}


def _matmul(name, mode, a, b, outs, *, extras=(), epilogue=None, after=None):
    a_list = list(a) if isinstance(a, (list, tuple)) else [a]
    b_list = list(b) if isinstance(b, (list, tuple)) else [b]
    na, nb = len(a_list), len(b_list)
    paired = na > 1
    assert na == nb if paired else na == 1
    nacc = 1 if paired else nb
    a0, b0 = a_list[0], b_list[0]
    if mode == "nn":
        (m, kk), (_, n) = a0.shape, b0.shape
    elif mode == "nt":
        (m, kk), (n, _) = a0.shape, b0.shape
    else:
        (kk, m), (_, n) = a0.shape, b0.shape
    if after is not None:
        inner = epilogue or (lambda *accs: accs)
        extras = list(extras) + [(jnp.zeros((1, n), F32) + after, "n")]
        epilogue = lambda *tiles: inner(*tiles[:-1])
    pref = {**_TILE_DEFAULT, **_TILES.get(name, {})}
    tm, tn, tk = _tile(m, pref["tm"], 128), _tile(n, pref["tn"], 128), _tile(kk, pref["tk"], 128)
    nk = kk // tk
    dotf = {"nn": _dot_nn, "nt": _dot_nt, "tn": _dot_tn}[mode]
    a_spec = pl.BlockSpec((tk, tm), lambda i, j, k: (k, i)) if mode == "tn" else pl.BlockSpec((tm, tk), lambda i, j, k: (i, k))
    b_spec = pl.BlockSpec((tn, tk), lambda i, j, k: (j, k)) if mode == "nt" else pl.BlockSpec((tk, tn), lambda i, j, k: (k, j))
    e_specs = [pl.BlockSpec((tm, tn), lambda i, j, k: (i, j)) if kind == "mn" else pl.BlockSpec((1, tn), lambda i, j, k: (0, j))
               for _, kind in extras]
    ne, no = len(extras), len(outs)
    epi = epilogue or (lambda *accs: accs)

    def body(*refs):
        a_refs, b_refs, rest = refs[:na], refs[na:na + nb], refs[na + nb:]
        e_refs, o_refs, acc_refs = rest[:ne], rest[ne:ne + no], rest[ne + no:]

        def products():
            if paired:
                p = dotf(a_refs[0][...], b_refs[0][...])
                for a_ref, b_ref in zip(a_refs[1:], b_refs[1:]):
                    p = p + dotf(a_ref[...], b_ref[...])
                return [p]
            av = a_refs[0][...]
            return [dotf(av, b_ref[...]) for b_ref in b_refs]

        def finish(accs):
            for o_ref, o in zip(o_refs, epi(*accs, *[r[...] for r in e_refs])):
                o_ref[...] = o.astype(o_ref.dtype)

        if nk == 1:
            finish(products())
            return
        k = pl.program_id(2)

        @pl.when(k == 0)
        def _():
            for acc_ref in acc_refs:
                acc_ref[...] = jnp.zeros_like(acc_ref)

        for acc_ref, p in zip(acc_refs, products()):
            acc_ref[...] += p

        @pl.when(k == nk - 1)
        def _():
            finish([acc_ref[...] for acc_ref in acc_refs])

    res = pl.pallas_call(
        body, name=name, grid=(m // tm, n // tn, nk),
        in_specs=[a_spec] * na + [b_spec] * nb + e_specs,
        out_specs=[pl.BlockSpec((tm, tn), lambda i, j, k: (i, j)) for _ in outs],
        out_shape=[jax.ShapeDtypeStruct((m, n), dt) for dt in outs],
        scratch_shapes=[pltpu.VMEM((tm, tn), F32)] * nacc if nk > 1 else [],
        compiler_params=_cparams("parallel", "parallel", "arbitrary"),
    )(*a_list, *b_list, *[e for e, _ in extras])
    return res[0] if no == 1 else res


def _norm_mod_fwd(x, w, sc, sh):
    s, d = x.shape
    tm = _tile(s, 512, 8)

    def body(x_ref, w_ref, sc_ref, sh_ref, h_ref):
        wc = d // NORM_PARTS
        parts = [slice(p * wc, (p + 1) * wc) for p in range(NORM_PARTS)]
        s1 = jnp.zeros((tm, 1), F32)
        for cols in parts:
            xv = x_ref[:, cols]
            s1 = s1 + jnp.sum(xv * xv, axis=-1, keepdims=True)
        r = lax.rsqrt(s1 * (1.0 / d) + EPS)
        for cols in parts:
            h_ref[:, cols] = ((x_ref[:, cols] * r) * w_ref[:, cols] * (1.0 + sc_ref[:, cols]) + sh_ref[:, cols]).astype(BF16)

    row = pl.BlockSpec((1, d), lambda i: (0, 0))
    return pl.pallas_call(
        body, name="norm_mod_fwd", grid=(s // tm,),
        in_specs=[pl.BlockSpec((tm, d), lambda i: (i, 0)), row, row, row],
        out_specs=pl.BlockSpec((tm, d), lambda i: (i, 0)),
        out_shape=jax.ShapeDtypeStruct((s, d), BF16),
        compiler_params=_cparams("parallel"),
    )(x, w, sc, sh)


def _gated_add_norm_epilogue(acc, res, g, w, sc, sh):
    xv = res + g * acc
    r = lax.rsqrt(jnp.mean(xv * xv, axis=-1, keepdims=True) + EPS)
    return xv, acc, (xv * r) * w * (1.0 + sc) + sh


NORM_PARTS = 4


def _colsum8(t):
    tm, d = t.shape
    return jnp.sum(t.reshape(tm // 8, 8, d), axis=0)


def _norm_mod_bwd(x, dh, w, sc, dres, branch=None, g=None):
    s, d = x.shape
    tm = _tile(s, 256, 8)
    nsteps = s // tm
    gated = branch is not None

    def body(*refs):
        if gated:
            x_ref, dh_ref, w_ref, sc_ref, dres_ref, br_ref, g_ref, dx_ref, dbr_ref, dsh_ref, dsc_ref, dw_ref, dg_ref, acc = refs
        else:
            x_ref, dh_ref, w_ref, sc_ref, dres_ref, dx_ref, dsh_ref, dsc_ref, dw_ref, acc = refs
        i = pl.program_id(0)

        @pl.when(i == 0)
        def _():
            acc[...] = jnp.zeros_like(acc)

        wc = d // NORM_PARTS
        parts = [slice(p * wc, (p + 1) * wc) for p in range(NORM_PARTS)]
        s1 = jnp.zeros((tm, 1), F32)
        s2 = jnp.zeros((tm, 1), F32)
        for cols in parts:
            xv = x_ref[:, cols]
            wm = w_ref[:, cols] * (1.0 + sc_ref[:, cols])
            s1 = s1 + jnp.sum(xv * xv, axis=-1, keepdims=True)
            s2 = s2 + jnp.sum((dh_ref[:, cols].astype(F32) * wm) * xv, axis=-1, keepdims=True)
        r = lax.rsqrt(s1 * (1.0 / d) + EPS)
        m2 = s2 * r * (1.0 / d)
        for cols in parts:
            xn = x_ref[:, cols] * r
            dhv = dh_ref[:, cols].astype(F32)
            dx = dres_ref[:, cols] + r * (dhv * (w_ref[:, cols] * (1.0 + sc_ref[:, cols])) - xn * m2)
            dx_ref[:, cols] = dx
            acc[0, :, cols] += _colsum8(dhv)
            acc[1, :, cols] += _colsum8(dhv * xn)
            if gated:
                dbr_ref[:, cols] = (dx * g_ref[:, cols]).astype(BF16)
                acc[2, :, cols] += _colsum8(dx * br_ref[:, cols].astype(F32))

        @pl.when(i == nsteps - 1)
        def _():
            a0 = jnp.sum(acc[0], axis=0, keepdims=True)
            a1 = jnp.sum(acc[1], axis=0, keepdims=True)
            dsh_ref[...] = a0
            dsc_ref[...] = a1 * w_ref[...]
            dw_ref[...] = a1 * (1.0 + sc_ref[...])
            if gated:
                dg_ref[...] = jnp.sum(acc[2], axis=0, keepdims=True)

    big = pl.BlockSpec((tm, d), lambda i: (i, 0))
    row = pl.BlockSpec((1, d), lambda i: (0, 0))
    rowo = jax.ShapeDtypeStruct((1, d), F32)
    in_specs = [big, big, row, row, big] + ([big, row] if gated else [])
    out_specs = [big] + ([big] if gated else []) + [row, row, row] + ([row] if gated else [])
    out_shape = ([jax.ShapeDtypeStruct((s, d), F32)] + ([jax.ShapeDtypeStruct((s, d), BF16)] if gated else [])
                 + [rowo, rowo, rowo] + ([rowo] if gated else []))
    args = [x, dh, w, sc, dres] + ([branch, g] if gated else [])
    return pl.pallas_call(
        body, name="norm_mod_bwd_gated" if gated else "norm_mod_bwd", grid=(nsteps,),
        in_specs=in_specs, out_specs=out_specs, out_shape=out_shape,
        scratch_shapes=[pltpu.VMEM((3, 8, d), F32)],
        compiler_params=_cparams("arbitrary"),
    )(*args)


def _final_loss(x, target, w, branch, g):
    s, d = x.shape
    tm = _tile(s, 256, 8)
    nsteps = s // tm

    def body(x_ref, t_ref, w_ref, br_ref, g_ref, loss_ref, dx_ref, dbr_ref, dw_ref, dg_ref, acc):
        i = pl.program_id(0)

        @pl.when(i == 0)
        def _():
            acc[...] = jnp.zeros_like(acc)

        wc = d // NORM_PARTS
        parts = [slice(p * wc, (p + 1) * wc) for p in range(NORM_PARTS)]
        s1 = jnp.zeros((tm, 1), F32)
        for cols in parts:
            xv = x_ref[:, cols]
            s1 = s1 + jnp.sum(xv * xv, axis=-1, keepdims=True)
        r = lax.rsqrt(s1 * (1.0 / d) + EPS)
        s2 = jnp.zeros((tm, 1), F32)
        for cols in parts:
            xn = x_ref[:, cols] * r
            wv = w_ref[:, cols]
            s2 = s2 + jnp.sum(((xn * wv - t_ref[:, cols]) * (1.0 / d) * wv) * xn, axis=-1, keepdims=True)
        m2 = s2 * (1.0 / d)
        for cols in parts:
            xn = x_ref[:, cols] * r
            wv = w_ref[:, cols]
            err = xn * wv - t_ref[:, cols]
            dy = err * (1.0 / d)
            dx = r * (dy * wv - xn * m2)
            dx_ref[:, cols] = dx
            dbr_ref[:, cols] = (dx * g_ref[:, cols]).astype(BF16)
            acc[0, :, cols] += _colsum8(err * err)
            acc[1, :, cols] += _colsum8(dy * xn)
            acc[2, :, cols] += _colsum8(dx * br_ref[:, cols].astype(F32))

        @pl.when(i == nsteps - 1)
        def _():
            loss_ref[...] = jnp.sum(jnp.sum(acc[0], axis=0, keepdims=True), axis=1, keepdims=True) * (0.5 / d)
            dw_ref[...] = jnp.sum(acc[1], axis=0, keepdims=True)
            dg_ref[...] = jnp.sum(acc[2], axis=0, keepdims=True)

    big = pl.BlockSpec((tm, d), lambda i: (i, 0))
    row = pl.BlockSpec((1, d), lambda i: (0, 0))
    rowo = jax.ShapeDtypeStruct((1, d), F32)
    return pl.pallas_call(
        body, name="final_loss", grid=(nsteps,),
        in_specs=[big, big, row, big, row],
        out_specs=[pl.BlockSpec((1, 1), lambda i: (0, 0)), big, big, row, row],
        out_shape=[jax.ShapeDtypeStruct((1, 1), F32), jax.ShapeDtypeStruct((s, d), F32),
                   jax.ShapeDtypeStruct((s, d), BF16), rowo, rowo],
        scratch_shapes=[pltpu.VMEM((3, 8, d), F32)],
        compiler_params=_cparams("arbitrary"),
    )(x, target, w, branch, g)


def _rope_tables(pos_col):
    s = pos_col.shape[0]
    tq = _tile(s, 1024, 8)
    half = ROPE_DIM // 2
    inv = np.float32(ROPE_THETA) ** (-np.arange(0, ROPE_DIM, 2, dtype=np.float32) / np.float32(ROPE_DIM))
    inv_row = jnp.asarray(np.tile(inv.astype(np.float32), HEAD // half)[None, :])

    def body(p_ref, inv_ref, ct_ref, st_ref):
        lane = lax.broadcasted_iota(jnp.int32, (tq, HEAD), 1)
        ang = p_ref[...].astype(F32) * inv_ref[...]
        cs, sn = jnp.cos(ang), jnp.sin(ang)
        ct_ref[...] = jnp.where(lane < ROPE_DIM, cs, 1.0)
        st_ref[...] = jnp.where(lane < half, -sn, jnp.where(lane < ROPE_DIM, sn, 0.0))

    blk = pl.BlockSpec((tq, HEAD), lambda i: (i, 0))
    return pl.pallas_call(
        body, name="rope_tables", grid=(s // tq,),
        in_specs=[pl.BlockSpec((tq, 1), lambda i: (i, 0)), pl.BlockSpec((1, HEAD), lambda i: (0, 0))],
        out_specs=[blk, blk],
        out_shape=[jax.ShapeDtypeStruct((s, HEAD), F32)] * 2,
        compiler_params=_cparams("parallel"),
    )(pos_col, inv_row)


def _swap_halves(x):
    lane = lax.broadcasted_iota(jnp.int32, x.shape, 1)
    half = ROPE_DIM // 2
    return jnp.where(lane < half, pltpu.roll(x, HEAD - half, 1), pltpu.roll(x, half, 1))


def _rope(x, ct, st):
    return x * ct + _swap_halves(x) * st


def _rope_t(dy, ct, st):
    return dy * ct - _swap_halves(dy) * st


def _band_bias(bias_ref):
    qi = lax.broadcasted_iota(jnp.int32, (CHUNK, 2 * CHUNK), 0)
    kj = lax.broadcasted_iota(jnp.int32, (CHUNK, 2 * CHUNK), 1)
    band = (kj >= qi) & (kj <= qi + CHUNK)
    bias_ref[0] = jnp.where(band, 0.0, NEG)
    bias_ref[1] = jnp.where(band & (kj >= CHUNK), 0.0, NEG)


def _rows(start, size, d):
    return pl.ds(start, size) if d == 1 else pl.ds(start, size, stride=d)


def _batch_units(ub, d, c):
    n_sb = ATT_BATCH // d
    out = []
    for sb, r in [(ub * n_sb + t, r) for t in range(n_sb) for r in range(d)]:
        base = pl.multiple_of(sb * (CHUNK * d), CHUNK)
        first = jnp.where((c == 0) & (sb == 0), 1, 0)
        out.append((_rows(base + r, CHUNK, d), _rows(base + (ATT_CHUNK - CHUNK * d + r), 2 * CHUNK, d), first))
    return out


def _for_batches(d, fn):
    def step(ub, carry):
        fn(ub)
        return carry
    lax.fori_loop(0, ATT_CHUNK // CHUNK // ATT_BATCH, step, 0)


def _bdot(a, b, ca, cb):
    return lax.dot_general(a, b, (((ca,), (cb,)), ((0,), (0,))), preferred_element_type=F32)


WIDE = DILATIONS[-1]
assert ATT_CHUNK == CHUNK * WIDE


def _deint(x):
    return jnp.swapaxes(x.reshape(CHUNK, WIDE, HEAD), 0, 1).reshape(ATT_CHUNK, HEAD)


def _reint(x):
    return jnp.swapaxes(x.reshape(WIDE, CHUNK, HEAD), 0, 1).reshape(ATT_CHUNK, HEAD)


def _blk(r):
    return pl.ds(r * CHUNK, CHUNK)


def _key_rows(ref, r):
    return jnp.concatenate([ref[_blk(r), :], ref[pl.ds(ATT_CHUNK + r * CHUNK, CHUNK), :]], axis=0)


def _attn_specs(nh, nc):
    cur = lambda off: pl.BlockSpec((ATT_CHUNK, HEAD), lambda h, c: (jnp.minimum(c, nc - 1), off + h))
    prev = lambda off: pl.BlockSpec((ATT_CHUNK, HEAD), lambda h, c: (jnp.maximum(c - 1, 0), off + h))
    tcur = pl.BlockSpec((ATT_CHUNK, HEAD), lambda h, c: (jnp.minimum(c, nc - 1), 0))
    tprev = pl.BlockSpec((ATT_CHUNK, HEAD), lambda h, c: (jnp.maximum(c - 1, 0), 0))
    return [cur(0), prev(nh), cur(nh), prev(2 * nh), cur(2 * nh), tcur, tcur, tprev, tprev]


def _attn_fwd(z, ct, st, nh):
    s = z.shape[0]
    nc = s // ATT_CHUNK
    scale = HEAD ** -0.5

    def body(q_ref, kp_ref, kc_ref, vp_ref, vc_ref, ctc, stc, ctp, stp, o_ref, lse_ref, qf, kf, vf, ob, lb, bias, q16, k16, v16):
        c = pl.program_id(1)
        qf[...] = _rope(q_ref[...].astype(F32), ctc[...], stc[...])
        kf[0:ATT_CHUNK] = _rope(kp_ref[...].astype(F32), ctp[...], stp[...])
        kf[ATT_CHUNK:] = _rope(kc_ref[...].astype(F32), ctc[...], stc[...])
        vf[0:ATT_CHUNK] = vp_ref[...].astype(F32)
        vf[ATT_CHUNK:] = vc_ref[...].astype(F32)
        _band_bias(bias)

        def softmax_units(q_b, k_b, v_b, bias_b):
            sc = _bdot(q_b, k_b, 2, 2) * scale + bias_b
            m = jnp.max(sc, axis=2, keepdims=True)
            p = jnp.exp(sc - m)
            l = jnp.sum(p, axis=2, keepdims=True)
            return _bdot(p.astype(BF16), v_b, 2, 1) / l, m + jnp.log(l)

        for b, d in enumerate(DILATIONS[:-1]):
            def batch(ub, b=b, d=d):
                units = _batch_units(ub, d, c)
                q_b = jnp.stack([qf[qr, :] for qr, _, _ in units]).astype(BF16)
                k_b = jnp.stack([kf[kr, :] for _, kr, _ in units]).astype(BF16)
                v_b = jnp.stack([vf[kr, :] for _, kr, _ in units]).astype(BF16)
                o, lse = softmax_units(q_b, k_b, v_b, jnp.stack([bias[first] for _, _, first in units]))
                for j, (qr, _, _) in enumerate(units):
                    ob[b, qr, :] = o[j]
                    lb[b, qr, :] = jnp.broadcast_to(lse[j], (CHUNK, HEAD))
            _for_batches(d, batch)

        b = len(DILATIONS) - 1
        q16[...] = _deint(qf[...]).astype(BF16)
        for half in range(2):
            rows = pl.ds(half * ATT_CHUNK, ATT_CHUNK)
            k16[rows, :] = _deint(kf[rows, :]).astype(BF16)
            v16[rows, :] = _deint(vf[rows, :]).astype(BF16)
        bias_w = bias[jnp.where(c == 0, 1, 0)][None]
        for ub in range(WIDE // ATT_BATCH):
            rs = [ub * ATT_BATCH + j for j in range(ATT_BATCH)]
            o, lse = softmax_units(jnp.stack([q16[_blk(r), :] for r in rs]), jnp.stack([_key_rows(k16, r) for r in rs]),
                                   jnp.stack([_key_rows(v16, r) for r in rs]), bias_w)
            for j, r in enumerate(rs):
                qf[_blk(r), :] = o[j]
                kf[_blk(r), :] = jnp.broadcast_to(lse[j], (CHUNK, HEAD))
        ob[b] = _reint(qf[...])
        lb[b] = _reint(kf[0:ATT_CHUNK])

        mx = jnp.maximum(jnp.maximum(lb[0], lb[1]), lb[2])
        e0, e1, e2 = jnp.exp(lb[0] - mx), jnp.exp(lb[1] - mx), jnp.exp(lb[2] - mx)
        den = e0 + e1 + e2
        o_ref[...] = ((e0 * ob[0] + e1 * ob[1] + e2 * ob[2]) / den).astype(BF16)
        lse_ref[...] = mx + jnp.log(den)

    blk = pl.BlockSpec((ATT_CHUNK, HEAD), lambda h, c: (c, h))
    return pl.pallas_call(
        body, name="attn_fwd", grid=(nh, nc),
        in_specs=_attn_specs(nh, nc), out_specs=[blk, blk],
        out_shape=[jax.ShapeDtypeStruct((s, nh * HEAD), BF16), jax.ShapeDtypeStruct((s, nh * HEAD), F32)],
        scratch_shapes=[pltpu.VMEM((ATT_CHUNK, HEAD), F32), pltpu.VMEM((2 * ATT_CHUNK, HEAD), F32),
                        pltpu.VMEM((2 * ATT_CHUNK, HEAD), F32), pltpu.VMEM((3, ATT_CHUNK, HEAD), F32),
                        pltpu.VMEM((3, ATT_CHUNK, HEAD), F32), pltpu.VMEM((2, CHUNK, 2 * CHUNK), F32),
                        pltpu.VMEM((ATT_CHUNK, HEAD), BF16), pltpu.VMEM((2 * ATT_CHUNK, HEAD), BF16),
                        pltpu.VMEM((2 * ATT_CHUNK, HEAD), BF16)],
        compiler_params=_cparams("parallel", "arbitrary"),
    )(z, z, z, z, z, ct, st, ct, st)


def _attn_bwd(z, ct, st, da, o, lse, nh):
    s = z.shape[0]
    nc = s // ATT_CHUNK
    scale = HEAD ** -0.5

    def body(q_ref, kp_ref, kc_ref, vp_ref, vc_ref, ctc, stc, ctp, stp, do_ref, o_ref, lse_ref,
             dq_ref, dk_ref, dv_ref, qf, kf, vf, dof, dbar, dqa, dkf, dvf, bias, q16, k16, v16, do16):
        c = pl.program_id(1)

        @pl.when(c == 0)
        def _():
            dkf[...] = jnp.zeros_like(dkf)
            dvf[...] = jnp.zeros_like(dvf)

        @pl.when(c > 0)
        def _():
            dkf[0:ATT_CHUNK] = dkf[ATT_CHUNK:]
            dvf[0:ATT_CHUNK] = dvf[ATT_CHUNK:]
            dkf[ATT_CHUNK:] = jnp.zeros((ATT_CHUNK, HEAD), F32)
            dvf[ATT_CHUNK:] = jnp.zeros((ATT_CHUNK, HEAD), F32)

        @pl.when(c < nc)
        def _():
            qf[...] = _rope(q_ref[...].astype(F32), ctc[...], stc[...])
            kf[0:ATT_CHUNK] = _rope(kp_ref[...].astype(F32), ctp[...], stp[...])
            kf[ATT_CHUNK:] = _rope(kc_ref[...].astype(F32), ctc[...], stc[...])
            vf[0:ATT_CHUNK] = vp_ref[...].astype(F32)
            vf[ATT_CHUNK:] = vc_ref[...].astype(F32)
            dov = do_ref[...].astype(F32)
            dof[...] = dov
            dbar[...] = jnp.broadcast_to(jnp.sum(dov * o_ref[...].astype(F32), axis=1, keepdims=True), (ATT_CHUNK, HEAD))
            dqa[...] = jnp.zeros_like(dqa)
            _band_bias(bias)

            def grads(q_b, k_b, v_b, do_b, lse_b, dbar_b, bias_b):
                sc = _bdot(q_b, k_b, 2, 2) * scale + bias_b
                p = jnp.exp(sc - jnp.concatenate([lse_b, lse_b], axis=2))
                dp = _bdot(do_b, v_b, 2, 2)
                ds = (p * (dp - jnp.concatenate([dbar_b, dbar_b], axis=2)) * scale).astype(BF16)
                return _bdot(ds, k_b, 2, 1), _bdot(ds, q_b, 1, 1), _bdot(p.astype(BF16), do_b, 1, 1)

            for d in DILATIONS[:-1]:
                def batch(ub, d=d):
                    units = _batch_units(ub, d, c)
                    dq, dk, dv = grads(jnp.stack([qf[qr, :] for qr, _, _ in units]).astype(BF16),
                                       jnp.stack([kf[kr, :] for _, kr, _ in units]).astype(BF16),
                                       jnp.stack([vf[kr, :] for _, kr, _ in units]).astype(BF16),
                                       jnp.stack([dof[qr, :] for qr, _, _ in units]).astype(BF16),
                                       jnp.stack([lse_ref[qr, :] for qr, _, _ in units]),
                                       jnp.stack([dbar[qr, :] for qr, _, _ in units]),
                                       jnp.stack([bias[first] for _, _, first in units]))
                    for j, (qr, kr, _) in enumerate(units):
                        dqa[qr, :] += dq[j]
                        dkf[kr, :] += dk[j]
                        dvf[kr, :] += dv[j]
                _for_batches(d, batch)

            q16[...] = _deint(qf[...]).astype(BF16)
            do16[...] = _deint(dof[...]).astype(BF16)
            for half in range(2):
                rows = pl.ds(half * ATT_CHUNK, ATT_CHUNK)
                k16[rows, :] = _deint(kf[rows, :]).astype(BF16)
                v16[rows, :] = _deint(vf[rows, :]).astype(BF16)
            dof[...] = _deint(lse_ref[...])
            dbar[...] = _deint(dbar[...])
            bias_w = bias[jnp.where(c == 0, 1, 0)][None]
            for ub in range(WIDE // ATT_BATCH):
                rs = [ub * ATT_BATCH + j for j in range(ATT_BATCH)]
                dq, dk, dv = grads(jnp.stack([q16[_blk(r), :] for r in rs]), jnp.stack([_key_rows(k16, r) for r in rs]),
                                   jnp.stack([_key_rows(v16, r) for r in rs]), jnp.stack([do16[_blk(r), :] for r in rs]),
                                   jnp.stack([dof[_blk(r), :] for r in rs]), jnp.stack([dbar[_blk(r), :] for r in rs]), bias_w)
                for j, r in enumerate(rs):
                    qf[_blk(r), :] = dq[j]
                    kf[_blk(r), :] = dk[j][0:CHUNK]
                    kf[pl.ds(ATT_CHUNK + r * CHUNK, CHUNK), :] = dk[j][CHUNK:]
                    vf[_blk(r), :] = dv[j][0:CHUNK]
                    vf[pl.ds(ATT_CHUNK + r * CHUNK, CHUNK), :] = dv[j][CHUNK:]
            dqa[...] += _reint(qf[...])
            for half in range(2):
                rows = pl.ds(half * ATT_CHUNK, ATT_CHUNK)
                dkf[rows, :] += _reint(kf[rows, :])
                dvf[rows, :] += _reint(vf[rows, :])
            dq_ref[...] = _rope_t(dqa[...], ctc[...], stc[...]).astype(BF16)

        @pl.when(c > 0)
        def _():
            dk_ref[...] = _rope_t(dkf[0:ATT_CHUNK], ctp[...], stp[...]).astype(BF16)
            dv_ref[...] = dvf[0:ATT_CHUNK].astype(BF16)

    cur = pl.BlockSpec((ATT_CHUNK, HEAD), lambda h, c: (jnp.minimum(c, nc - 1), h))
    late = pl.BlockSpec((ATT_CHUNK, HEAD), lambda h, c: (jnp.maximum(c - 1, 0), h))
    shp = jax.ShapeDtypeStruct((s, nh * HEAD), BF16)
    big = pltpu.VMEM((2 * ATT_CHUNK, HEAD), F32)
    one = pltpu.VMEM((ATT_CHUNK, HEAD), F32)
    one16, big16 = pltpu.VMEM((ATT_CHUNK, HEAD), BF16), pltpu.VMEM((2 * ATT_CHUNK, HEAD), BF16)
    return pl.pallas_call(
        body, name="attn_bwd", grid=(nh, nc + 1),
        in_specs=_attn_specs(nh, nc) + [cur, cur, cur], out_specs=[cur, late, late],
        out_shape=[shp, shp, shp],
        scratch_shapes=[one, big, big, one, one, one, big, big, pltpu.VMEM((2, CHUNK, 2 * CHUNK), F32),
                        one16, big16, big16, one16],
        compiler_params=_cparams("parallel", "arbitrary"),
    )(z, z, z, z, z, ct, st, ct, st, da, o, lse)


_GELU_K = math.sqrt(2.0 / math.pi)


def _gelu(x):
    return 0.5 * x * (1.0 + jnp.tanh(_GELU_K * (x + 0.044715 * x * x * x)))


def _gelu_and_grad(x):
    t = jnp.tanh(_GELU_K * (x + 0.044715 * x * x * x))
    g = 0.5 * x * (1.0 + t)
    dg = 0.5 * (1.0 + t) + 0.5 * x * (1.0 - t * t) * (_GELU_K * (1.0 + 3 * 0.044715 * x * x))
    return g, dg


def _tril(w):
    ti = lax.broadcasted_iota(jnp.int32, (CHUNK, CHUNK), 0)
    si = lax.broadcasted_iota(jnp.int32, (CHUNK, CHUNK), 1)
    return jnp.where(si <= ti, w, 0.0)


def _sgu_fwd(z, w_s, b_col, ng, u_blk):
    s = z.shape[0]
    gw = ng * HEAD
    tq = _tile(s, 1024, CHUNK)

    def body(u_ref, v_ref, w_ref, b_ref, o_ref):
        for g in range(ng):
            wg = _tril(w_ref[g]).astype(BF16)
            cols = slice(g * HEAD, (g + 1) * HEAD)
            for n in range(tq // CHUNK):
                rows = slice(n * CHUNK, (n + 1) * CHUNK)
                gv = _gelu(v_ref[rows, cols].astype(F32)).astype(BF16)
                mixed = _dot_nn(wg, gv) + b_ref[g]
                o_ref[rows, cols] = (_gelu(u_ref[rows, cols].astype(F32)) * mixed).astype(BF16)

    full = pl.BlockSpec((ng, CHUNK, CHUNK), lambda i: (0, 0, 0))
    return pl.pallas_call(
        body, name="sgu_fwd", grid=(s // tq,),
        in_specs=[pl.BlockSpec((tq, gw), lambda i: (i, u_blk)), pl.BlockSpec((tq, gw), lambda i: (i, u_blk + 1)), full, full],
        out_specs=pl.BlockSpec((tq, gw), lambda i: (i, 0)),
        out_shape=jax.ShapeDtypeStruct((s, gw), BF16),
        compiler_params=_cparams("parallel"),
    )(z, z, w_s, b_col)


def _sgu_bwd(z, w_s, b_col, da, ng, u_blk, da_blk):
    s = z.shape[0]
    gw = ng * HEAD
    tq = _tile(s, 1024, CHUNK)
    nsteps = s // tq

    def body(u_ref, v_ref, w_ref, b_ref, do_ref, du_ref, dv_ref, dw_ref, db_ref):
        i = pl.program_id(0)

        @pl.when(i == 0)
        def _():
            dw_ref[...] = jnp.zeros_like(dw_ref)
            db_ref[...] = jnp.zeros_like(db_ref)

        for g in range(ng):
            wg = _tril(w_ref[g]).astype(BF16)
            cols = slice(g * HEAD, (g + 1) * HEAD)
            dw_acc = jnp.zeros((CHUNK, CHUNK), F32)
            db_acc = jnp.zeros((CHUNK, 1), F32)
            for n in range(tq // CHUNK):
                rows = slice(n * CHUNK, (n + 1) * CHUNK)
                gu, dgu = _gelu_and_grad(u_ref[rows, cols].astype(F32))
                gv, dgv = _gelu_and_grad(v_ref[rows, cols].astype(F32))
                gvb = gv.astype(BF16)
                mixed = _dot_nn(wg, gvb) + b_ref[g]
                dout = do_ref[rows, cols].astype(F32)
                du_ref[rows, cols] = (dout * mixed * dgu).astype(BF16)
                dmix = dout * gu
                dmb = dmix.astype(BF16)
                dv_ref[rows, cols] = (_dot_tn(wg, dmb) * dgv).astype(BF16)
                dw_acc += _dot_nt(dmb, gvb)
                db_acc += jnp.sum(dmix, axis=1, keepdims=True)
            dw_ref[g] += _tril(dw_acc)
            db_ref[g] += jnp.broadcast_to(db_acc, (CHUNK, CHUNK))

    full = pl.BlockSpec((ng, CHUNK, CHUNK), lambda i: (0, 0, 0))
    out = pl.BlockSpec((tq, gw), lambda i: (i, 0))
    return pl.pallas_call(
        body, name="sgu_bwd", grid=(nsteps,),
        in_specs=[pl.BlockSpec((tq, gw), lambda i: (i, u_blk)), pl.BlockSpec((tq, gw), lambda i: (i, u_blk + 1)), full, full,
                  pl.BlockSpec((tq, gw), lambda i: (i, da_blk))],
        out_specs=[out, out, full, full],
        out_shape=[jax.ShapeDtypeStruct((s, gw), BF16)] * 2 + [jax.ShapeDtypeStruct((ng, CHUNK, CHUNK), F32)] * 2,
        compiler_params=_cparams("arbitrary"),
    )(z, z, w_s, b_col, da)


def _shift_down(y, halo, k):
    return pltpu.roll(jnp.concatenate([halo, y], axis=0), k, 0)[8:]


def _shift_up(y, halo, k):
    n = y.shape[0]
    return pltpu.roll(jnp.concatenate([y, halo], axis=0), n + 8 - k, 0)[:n]


def _conv_fwd(z, cw):
    s, d3 = z.shape
    d = d3 // 3
    tq = _tile(s, 256, 8)

    def body(z_ref, zh_ref, cw_ref, a_ref):
        i = pl.program_id(0)
        zv = z_ref[...].astype(F32)
        zh = jnp.where(i > 0, zh_ref[...].astype(F32), 0.0)
        y = zv[:, d:2 * d] * zv[:, 2 * d:]
        yh = zh[:, d:2 * d] * zh[:, 2 * d:]
        cwv = cw_ref[...]
        conv = cwv[0:1] * _shift_down(y, yh, 2) + cwv[1:2] * _shift_down(y, yh, 1) + cwv[2:3] * y
        a_ref[...] = (zv[:, :d] * conv).astype(BF16)

    return pl.pallas_call(
        body, name="conv_fwd", grid=(s // tq,),
        in_specs=[pl.BlockSpec((tq, d3), lambda i: (i, 0)),
                  pl.BlockSpec((8, d3), lambda i: (jnp.maximum(i * (tq // 8) - 1, 0), 0)),
                  pl.BlockSpec((3, d), lambda i: (0, 0))],
        out_specs=pl.BlockSpec((tq, d), lambda i: (i, 0)),
        out_shape=jax.ShapeDtypeStruct((s, d), BF16),
        compiler_params=_cparams("parallel"),
    )(z, z, cw)


def _conv_bwd(z, cw, da):
    s, d3 = z.shape
    d = d3 // 3
    tq = _tile(s, 128, 8)
    nsteps = s // tq
    nblk8 = s // 8

    def body(z_ref, zp_ref, zn_ref, da_ref, dan_ref, cw_ref, dz_ref, dcw_ref, acc):
        i = pl.program_id(0)

        @pl.when(i == 0)
        def _():
            acc[...] = jnp.zeros_like(acc)

        zv = z_ref[...].astype(F32)
        zp = jnp.where(i > 0, zp_ref[...].astype(F32), 0.0)
        zn = jnp.where(i < nsteps - 1, zn_ref[...].astype(F32), 0.0)
        dav = da_ref[...].astype(F32)
        dan = jnp.where(i < nsteps - 1, dan_ref[...].astype(F32), 0.0)
        gb, gc, hx = zv[:, :d], zv[:, d:2 * d], zv[:, 2 * d:]
        y = gc * hx
        yp = zp[:, d:2 * d] * zp[:, 2 * d:]
        cwv = cw_ref[...]
        y1, y2 = _shift_down(y, yp, 1), _shift_down(y, yp, 2)
        conv = cwv[0:1] * y2 + cwv[1:2] * y1 + cwv[2:3] * y
        dconv = dav * gb
        dconv_n = dan * zn[:, :d]
        dy = cwv[2:3] * dconv + cwv[1:2] * _shift_up(dconv, dconv_n, 1) + cwv[0:1] * _shift_up(dconv, dconv_n, 2)
        dz_ref[:, :d] = (dav * conv).astype(BF16)
        dz_ref[:, d:2 * d] = (dy * hx).astype(BF16)
        dz_ref[:, 2 * d:] = (dy * gc).astype(BF16)
        acc[0] += _colsum8(dconv * y2)
        acc[1] += _colsum8(dconv * y1)
        acc[2] += _colsum8(dconv * y)

        @pl.when(i == nsteps - 1)
        def _():
            for j in range(3):
                dcw_ref[j:j + 1, :] = jnp.sum(acc[j], axis=0, keepdims=True)

    return pl.pallas_call(
        body, name="conv_bwd", grid=(nsteps,),
        in_specs=[pl.BlockSpec((tq, d3), lambda i: (i, 0)),
                  pl.BlockSpec((8, d3), lambda i: (jnp.maximum(i * (tq // 8) - 1, 0), 0)),
                  pl.BlockSpec((8, d3), lambda i: (jnp.minimum((i + 1) * (tq // 8), nblk8 - 1), 0)),
                  pl.BlockSpec((tq, d), lambda i: (i, 0)),
                  pl.BlockSpec((8, d), lambda i: (jnp.minimum((i + 1) * (tq // 8), nblk8 - 1), 0)),
                  pl.BlockSpec((3, d), lambda i: (0, 0))],
        out_specs=[pl.BlockSpec((tq, d3), lambda i: (i, 0)), pl.BlockSpec((3, d), lambda i: (0, 0))],
        out_shape=[jax.ShapeDtypeStruct((s, d3), BF16), jax.ShapeDtypeStruct((3, d), F32)],
        scratch_shapes=[pltpu.VMEM((3, 8, d), F32)],
        compiler_params=_cparams("arbitrary"),
    )(z, z, z, da, da, cw)


def _swiglu_epilogue(gate, up):
    return gate, up, gate / (1.0 + jnp.exp(-gate)) * up


def _swiglu_bwd_epilogue(dact, gt, up):
    g, u = gt.astype(F32), up.astype(F32)
    sg = 1.0 / (1.0 + jnp.exp(-g))
    return dact * u * (sg * (1.0 + g * (1.0 - sg))), dact * (g * sg)


def _ada_fwd(c_all, ada_w):
    nl, d, n8 = ada_w.shape
    tn = _tile(n8, 768, 128)

    def body(c_ref, w_ref, o_ref):
        cv = c_ref[...]
        act = (cv / (1.0 + jnp.exp(-cv))).astype(BF16)
        o_ref[0] = _dot_nn(act, w_ref[0].astype(BF16))

    return pl.pallas_call(
        body, name="ada_fwd", grid=(nl, n8 // tn),
        in_specs=[pl.BlockSpec((N_DEV, d), lambda l, j: (0, 0)), pl.BlockSpec((1, d, tn), lambda l, j: (l, 0, j))],
        out_specs=pl.BlockSpec((1, N_DEV, tn), lambda l, j: (l, 0, j)),
        out_shape=jax.ShapeDtypeStruct((nl, N_DEV, n8), F32),
        compiler_params=_cparams("parallel", "parallel"),
    )(c_all, ada_w)


def _ada_wgrad(c_all, dmod_cols):
    nl, _, n8 = dmod_cols.shape
    d = c_all.shape[1]
    tn = _tile(n8, 768, 128)

    def body(c_ref, g_ref, o_ref):
        cv = c_ref[...]
        act = (cv / (1.0 + jnp.exp(-cv))).astype(BF16)
        o_ref[0] = _dot_tn(act, g_ref[0].astype(BF16))

    return pl.pallas_call(
        body, name="ada_wgrad", grid=(nl, n8 // tn),
        in_specs=[pl.BlockSpec((N_DEV, d), lambda l, j: (0, 0)), pl.BlockSpec((1, N_DEV, tn), lambda l, j: (l, 0, j))],
        out_specs=pl.BlockSpec((1, d, tn), lambda l, j: (l, 0, j)),
        out_shape=jax.ShapeDtypeStruct((nl, d, n8), F32),
        compiler_params=_cparams("parallel", "parallel"),
    )(c_all, dmod_cols)


def _adamw(name, pieces, w, m, v):
    npc, r, c = pieces.shape
    tr = _tile(r, max(8, (1 << 19) // c // 8 * 8), 8)
    bc1, bc2 = 1.0 - ADAM_B1 ** ADAM_STEP, 1.0 - ADAM_B2 ** ADAM_STEP

    def body(p_ref, w_ref, m_ref, v_ref, g_ref, d_ref, nm_ref, nv_ref):
        g = p_ref[0].astype(F32)
        for i in range(1, npc):
            g = g + p_ref[i].astype(F32)
        nm = ADAM_B1 * m_ref[...] + (1.0 - ADAM_B1) * g
        nv = ADAM_B2 * v_ref[...] + (1.0 - ADAM_B2) * (g * g)
        g_ref[...] = g
        nm_ref[...] = nm
        nv_ref[...] = nv
        d_ref[...] = -ADAM_LR * ((nm / bc1) / (jnp.sqrt(nv / bc2) + ADAM_EPS) + ADAM_WD * w_ref[...])

    blk = pl.BlockSpec((tr, c), lambda i: (i, 0))
    return pl.pallas_call(
        body, name=name, grid=(r // tr,),
        in_specs=[pl.BlockSpec((npc, tr, c), lambda i: (0, i, 0)), blk, blk, blk],
        out_specs=[blk] * 4, out_shape=[jax.ShapeDtypeStruct((r, c), F32)] * 4,
        compiler_params=_cparams("parallel"),
    )(pieces, w, m, v)


def _adamw_transposed(name, pieces, w, m, v):
    npc, nl, c, k = pieces.shape
    tk = _tile(k, 256, 128)
    bc1, bc2 = 1.0 - ADAM_B1 ** ADAM_STEP, 1.0 - ADAM_B2 ** ADAM_STEP

    def body(p_ref, w_ref, m_ref, v_ref, g_ref, d_ref, nm_ref, nv_ref):
        gt = p_ref[0].astype(F32)
        for i in range(1, npc):
            gt = gt + p_ref[i].astype(F32)
        g = gt.T
        nm = ADAM_B1 * m_ref[...] + (1.0 - ADAM_B1) * g
        nv = ADAM_B2 * v_ref[...] + (1.0 - ADAM_B2) * (g * g)
        g_ref[...] = g
        nm_ref[...] = nm
        nv_ref[...] = nv
        d_ref[...] = -ADAM_LR * ((nm / bc1) / (jnp.sqrt(nv / bc2) + ADAM_EPS) + ADAM_WD * w_ref[...])

    nk = k // tk
    blk = pl.BlockSpec((tk, c), lambda l, i: (l * nk + i, 0))
    res = pl.pallas_call(
        body, name=name, grid=(nl, nk),
        in_specs=[pl.BlockSpec((npc, None, c, tk), lambda l, i: (0, l, 0, i)), blk, blk, blk],
        out_specs=[blk] * 4, out_shape=[jax.ShapeDtypeStruct((nl * k, c), F32)] * 4,
        compiler_params=_cparams("parallel", "parallel"),
    )(pieces, w.reshape(nl * k, c), m.reshape(nl * k, c), v.reshape(nl * k, c))
    return [r.reshape(nl, k, c) for r in res]


def _place():
    x, y, c = lax.axis_index("x"), lax.axis_index("y"), lax.axis_index("c")
    return x, y, c


def _all_gather_small(name, x_shard):
    m_per, n = x_shard.shape

    def body(x_ref, out_ref, token_ref, send_sems, recv_sems, local_sem):
        token_ref[...] = jnp.zeros_like(token_ref)
        x, y, c = _place()
        me, sibling = (x, y, c), (x, y, 1 - c)
        chips = [(1 - x, y), (x, 1 - y), (1 - x, 1 - y)]

        def rows(px, py, pc):
            return out_ref.at[pl.ds((4 * px + 2 * py + pc) * m_per, m_per), :]

        def copy(k, block, to, src=None):
            return pltpu.make_async_remote_copy(
                src_ref=rows(*block) if src is None else src, dst_ref=rows(*block),
                send_sem=send_sems.at[k], recv_sem=recv_sems.at[k], device_id=to, device_id_type=MESH)

        mine = pltpu.make_async_copy(x_ref, rows(*me), local_sem)
        mine.start()
        first = [copy(0, me, sibling, src=x_ref)]
        first += [copy(1 + j, me, (*chip, c), src=x_ref) for j, chip in enumerate(chips)]
        for cp in first:
            cp.start()
        passed = [copy(4 + j, (*chip, c), sibling) for j, chip in enumerate(chips)]
        for j, chip in enumerate(chips):
            copy(1 + j, (*chip, c), me).wait_recv()
            passed[j].start()
        copy(0, sibling, me).wait_recv()
        for j, chip in enumerate(chips):
            copy(4 + j, (*chip, 1 - c), me).wait_recv()
        for cp in first + passed:
            cp.wait_send()
        mine.wait()

    vmem = pl.BlockSpec(memory_space=pltpu.VMEM)
    return pl.pallas_call(
        body, name=name,
        out_shape=[jax.ShapeDtypeStruct((N_DEV * m_per, n), x_shard.dtype), jax.ShapeDtypeStruct((8, HEAD), F32)],
        in_specs=[vmem], out_specs=[vmem, vmem],
        scratch_shapes=[pltpu.SemaphoreType.DMA((7,)), pltpu.SemaphoreType.DMA((7,)), pltpu.SemaphoreType.DMA],
        compiler_params=pltpu.CompilerParams(vmem_limit_bytes=VMEM_LIMIT),
    )(x_shard)


_HBM = pl.BlockSpec(memory_space=pltpu.HBM)
_SEM = pl.BlockSpec(memory_space=pltpu.SEMAPHORE)
_EFFECT = pltpu.SideEffectType.DATAFLOW_SIDE_EFFECTING


def _peers():
    x, y, c = _place()
    peers = []
    for k in range(1, N_DEV):
        px = 1 - x if k & 4 else x
        py = 1 - y if k & 2 else y
        pc = 1 - c if k & 1 else c
        peers.append(((px, py, pc), 4 * px + 2 * py + pc))
    return 4 * x + 2 * y + c, peers


def _push_start(name, srcs, lands, src_view, dst_view):
    na = len(srcs)
    n = na * (N_DEV - 1)

    def body(*refs):
        s_refs, l_refs = refs[:na], refs[na:2 * na]
        send_sems, recv_sems, token = refs[2 * na], refs[2 * na + 1], refs[-1]
        me, peers = _peers()
        for a in range(na):
            for k, (dev, idx) in enumerate(peers):
                pltpu.make_async_remote_copy(
                    src_ref=src_view(s_refs[a], idx), dst_ref=dst_view(l_refs[a], me),
                    send_sem=send_sems.at[a * (N_DEV - 1) + k], recv_sem=recv_sems.at[a * (N_DEV - 1) + k],
                    device_id=dev, device_id_type=MESH).start()
        token[...] = jnp.zeros_like(token)

    outs = pl.pallas_call(
        body, name=name,
        out_shape=(pltpu.SemaphoreType.DMA((n,)), pltpu.SemaphoreType.DMA((n,)),
                   *[pltpu.HBM(t.shape, t.dtype) for t in list(srcs) + list(lands)], jax.ShapeDtypeStruct((8, HEAD), F32)),
        in_specs=[_HBM] * (2 * na),
        out_specs=(_SEM, _SEM, *[_HBM] * (2 * na), pl.BlockSpec(memory_space=pltpu.VMEM)),
        input_output_aliases={i: 2 + i for i in range(2 * na)},
        compiler_params=pltpu.CompilerParams(has_side_effects=_EFFECT),
    )(*[pltpu.with_memory_space_constraint(t, pltpu.HBM) for t in list(srcs) + list(lands)])
    return outs[0], outs[1], list(outs[2:2 + na]), list(outs[2 + na:2 + 2 * na]), outs[-1]


def _push_wait(name, send_sems, recv_sems, srcs, lands, after, src_view, dst_view):
    na = len(srcs)

    def body(*refs):
        s_refs, l_refs = refs[:na], refs[na:2 * na]
        send_sems, recv_sems = refs[2 * na], refs[2 * na + 1]
        me, peers = _peers()
        for a in range(na):
            for k, (dev, idx) in enumerate(peers):
                cp = pltpu.make_async_remote_copy(
                    src_ref=src_view(s_refs[a], idx), dst_ref=dst_view(l_refs[a], idx),
                    send_sem=send_sems.at[a * (N_DEV - 1) + k], recv_sem=recv_sems.at[a * (N_DEV - 1) + k],
                    device_id=dev, device_id_type=MESH)
                cp.wait_send()
                cp.wait_recv()

    outs = pl.pallas_call(
        body, name=name,
        out_shape=[pltpu.HBM(t.shape, t.dtype) for t in list(srcs) + list(lands)],
        in_specs=[_HBM] * (2 * na) + [_SEM, _SEM, pl.BlockSpec(memory_space=pl.ANY)],
        out_specs=[_HBM] * (2 * na),
        input_output_aliases={i: i for i in range(2 * na)},
        compiler_params=pltpu.CompilerParams(has_side_effects=_EFFECT),
    )(*srcs, *lands, send_sems, recv_sems, after)
    return list(outs[na:])


def _gather_start(name, shards):
    lands = [lax.empty((N_DEV,) + t.shape, t.dtype) for t in shards]
    return _push_start(name, shards, lands, lambda ref, idx: ref, lambda ref, slot: ref.at[slot])


def _gather_wait(name, started, shards, after, me):
    send_sems, recv_sems, srcs, lands, _ = started
    lands = _push_wait(name, send_sems, recv_sems, srcs, lands, after, lambda ref, idx: ref, lambda ref, slot: ref.at[slot])
    return [lax.dynamic_update_index_in_dim(g, t, me, 0) for g, t in zip(lands, shards)]


def _pack_rows(parts, width):
    rows, offs, r = [], [], 0
    for p in parts:
        flat = p.reshape(-1).astype(F32)
        nr = -(-flat.shape[0] // (8 * width)) * 8
        rows.append(jnp.pad(flat, (0, nr * width - flat.shape[0])).reshape(nr, width))
        offs.append((r, flat.shape[0], p.shape))
        r += nr
    return jnp.concatenate(rows, axis=0), offs, r


def _unpack_rows(slab, offs, width):
    lead = slab.shape[:-2]
    out = []
    for r0, n, shape in offs:
        nr = -(-n // width)
        out.append(slab[..., r0:r0 + nr, :].reshape(lead + (nr * width,))[..., :n].reshape(lead + tuple(shape)))
    return out


def kernel(x, c, positions, ada_w, ada_b, norm_mix, norm_ffn, ab_w_in, sgu_w, sgu_b, ab_w_out, conv_w_in, conv_w, conv_w_out, ffn_w_gate, ffn_w_up, ffn_w_down, final_norm, loss_target, m_ada_w, m_ada_b, m_norm_mix, m_norm_ffn, m_ab_w_in, m_sgu_w, m_sgu_b, m_ab_w_out, m_conv_w_in, m_conv_w, m_conv_w_out, m_ffn_w_gate, m_ffn_w_up, m_ffn_w_down, m_final_norm, v_ada_w, v_ada_b, v_norm_mix, v_norm_ffn, v_ab_w_in, v_sgu_w, v_sgu_b, v_ab_w_out, v_conv_w_in, v_conv_w, v_conv_w_out, v_ffn_w_gate, v_ffn_w_up, v_ffn_w_down, v_final_norm):
    xi, yi, ci = _place()
    me = 4 * xi + 2 * yi + ci
    s, d = x.shape[1], x.shape[2]
    depth = ada_w.shape[0]
    n_even = ab_w_in.shape[0]
    nh_mix = d // HEAD
    nh = 3 * nh_mix // 4
    ng = nh_mix - nh
    aw, gw = nh * HEAD, ng * HEAD
    assert d <= FULL_ROW and s % ATT_CHUNK == 0
    x0 = x[0]
    target = loss_target[0]
    n_odd, cwid, d8 = conv_w.shape

    width = 512
    slab, offs, _ = _pack_rows([c, conv_w], width)
    gathered, _ = _all_gather_small("gather_cond", slab)
    c_parts, cw_parts = _unpack_rows(gathered.reshape(N_DEV, -1, width), offs, width)
    c_all = c_parts.reshape(N_DEV, d)
    conv_w_full = jnp.transpose(cw_parts, (1, 2, 0, 3)).reshape(n_odd, cwid, d)

    mod_cols = _ada_fwd(c_all, ada_w)
    n8 = mod_cols.shape[2]
    mod_all, token = _all_gather_small("gather_mod", mod_cols.reshape(depth * N_DEV, n8))
    mod_mine = lax.dynamic_index_in_dim(mod_all.reshape(N_DEV, depth, N_DEV, n8), me, axis=2, keepdims=False)

    def cols(w):
        return jnp.swapaxes(w, 0, 1)

    def mixer_weights(l):
        w_i, w_o = (ab_w_in, ab_w_out) if l % 2 == 0 else (conv_w_in, conv_w_out)
        return cols(w_i[l // 2]), w_o[l // 2]

    def ffn_weights(l):
        return [cols(ffn_w_gate[l]), cols(ffn_w_up[l]), ffn_w_down[l]]

    groups = [[mixer_weights(0)[0]], [mixer_weights(0)[1]], ffn_weights(0)]
    for l in range(1, depth):
        groups += [list(mixer_weights(l)), ffn_weights(l)]
    gathers, tok = [], token[0, 0]
    for n, ws in enumerate(groups):
        shards = [(w + tok).astype(BF16) for w in ws]
        started = _gather_start(f"gather_start_{n}", shards)
        gathers.append((started, shards))
        tok = started[4][0, 0]

    def weights_of_group(n, after):
        started, shards = gathers[n]
        return _gather_wait(f"gather_wait_{n}", started, shards, after, me)

    def plain_rows(g):
        return g.reshape(g.shape[0] * g.shape[1], g.shape[2])

    mod = jnp.transpose(mod_mine, (1, 0, 2)).reshape(depth, N_DEV * n8) + ada_b + tok
    mods = mod.reshape(depth, 6, 1, d)

    ct, st = _rope_tables(positions.reshape(s, 1))
    b_col = jnp.broadcast_to(sgu_b[..., None], sgu_b.shape + (CHUNK,))
    u_blk = 3 * aw // gw

    stream = [x0]
    saved = []
    w_in, w_out, w_gate, w_up, w_down = [[None] * depth for _ in range(5)]
    xcur = x0
    for l in range(depth):
        sh_m, sc_m, g_m, sh_f, sc_f, g_f = [mods[l, j] for j in range(6)]
        i = l // 2
        if l == 0:
            (g_in,) = weights_of_group(0, mod)
        else:
            g_in, g_out = weights_of_group(1 + 2 * l, xcur)
        w_in[l] = plain_rows(g_in)
        h = _norm_mod_fwd(xcur, norm_mix[l][None], sc_m, sh_m)
        if l % 2 == 0:
            z = _matmul("mix_in", "nt", h, w_in[l], [BF16])
            attn, lse = _attn_fwd(z, ct, st, nh)
            sgu = _sgu_fwd(z, sgu_w[i], b_col[i], ng, u_blk)
            a = jnp.concatenate([attn, sgu], axis=1)
            mixer_saved = (z, a, lse)
        else:
            z = _matmul("conv_in", "nt", h, w_in[l], [BF16])
            a = _conv_fwd(z, conv_w_full[i])
            mixer_saved = (z, a, None)
        if l == 0:
            (g_out,) = weights_of_group(1, a)
        w_out[l] = plain_rows(g_out)
        x1, mix, h2 = _matmul("mix_out", "nn", a, w_out[l], [F32, BF16, BF16],
                              extras=[(xcur, "mn"), (g_m, "n"), (norm_ffn[l][None], "n"), (sc_f, "n"), (sh_f, "n")],
                              epilogue=_gated_add_norm_epilogue)
        g_gate, g_up, g_down = weights_of_group(2 + 2 * l, x1)
        w_gate[l], w_up[l], w_down[l] = plain_rows(g_gate), plain_rows(g_up), plain_rows(g_down)
        gt, up, act = _matmul("ffn_in", "nt", h2, [w_gate[l], w_up[l]], [BF16, BF16, BF16], epilogue=_swiglu_epilogue)
        x2, f = _matmul("ffn_down", "nn", act, w_down[l], [F32, BF16], extras=[(x1, "mn"), (g_f, "n")],
                        epilogue=lambda acc, r, gv: (r + gv * acc, acc))
        saved.append((h, mixer_saved, mix, x1, h2, gt, up, act, f))
        stream.append(x2)
        xcur = x2

    f_last = saved[-1][8]
    loss_part, dx, dbr, d_final, dg = _final_loss(xcur, target, final_norm[None], f_last, mods[depth - 1, 5])
    loss = lax.psum(loss_part[0, 0], ("x", "y", "c"))

    dmod = [[None] * 6 for _ in range(depth)]
    d_norm_mix, d_norm_ffn = [None] * depth, [None] * depth
    d_sgu_w, d_sgu_b, d_conv_w = [None] * n_even, [None] * n_even, [None] * n_odd

    big = {"in": ab_w_in, "out": ab_w_out, "cin": conv_w_in, "cout": conv_w_out,
           "gate": ffn_w_gate, "up": ffn_w_up, "down": ffn_w_down}
    col_sharded = ("in", "cin", "gate", "up")
    lands = {k: lax.empty((N_DEV, w.shape[0]) + (w.shape[1:][::-1] if k in col_sharded else w.shape[1:]), BF16)
             for k, w in big.items()}
    own = {k: [None] * w.shape[0] for k, w in big.items()}
    pending = {"ffn": None, "mix": None}

    def exchange_finish(tag, after):
        (send_sems, recv_sems, srcs, lds, _), keys, li, layer = pending[tag]
        lds = _push_wait(f"exchange_wait_{tag}_{layer}", send_sems, recv_sems, srcs, lds, after,
                         lambda ref, idx: ref.at[idx], lambda ref, slot: ref.at[slot, li])
        for k, ld in zip(keys, lds):
            lands[k] = ld
        pending[tag] = None

    def exchange_start(tag, layer, keys, li, grads):
        if pending[tag] is not None:
            exchange_finish(tag, grads[0])
        for k, g in zip(keys, grads):
            own[k][li] = lax.dynamic_index_in_dim(g, me, 0, keepdims=False)
        started = _push_start(f"exchange_start_{tag}_{layer}", grads, [lands[k] for k in keys],
                              lambda ref, idx: ref.at[idx], lambda ref, slot: ref.at[slot, li])
        pending[tag] = (started, keys, li, layer)
        return started[4][0, 0]

    def row_shards(g):
        return g.reshape(N_DEV, g.shape[0] // N_DEV, g.shape[1])

    for l in reversed(range(depth)):
        sh_m, sc_m, g_m, sh_f, sc_f, g_f = [mods[l, j] for j in range(6)]
        h, (z, a, lse), mix, x1, h2, gt, up, act, f = saved[l]
        i = l // 2
        dmod[l][5] = dg
        gw_down = _matmul("ffn_down_wgrad", "tn", act, dbr, [BF16])
        dgt, dup = _matmul("ffn_down_dgrad", "nt", dbr, w_down[l], [BF16, BF16], extras=[(gt, "mn"), (up, "mn")],
                           epilogue=_swiglu_bwd_epilogue)
        gw_gate = _matmul("ffn_in_wgrad", "tn", dgt, h2, [BF16])
        gw_up = _matmul("ffn_in_wgrad", "tn", dup, h2, [BF16])
        tok = exchange_start("ffn", l, ("gate", "up", "down"), l, [row_shards(gw_gate), row_shards(gw_up), row_shards(gw_down)])
        dh2 = _matmul("ffn_in_dgrad", "nn", [dgt, dup], [w_gate[l], w_up[l]], [BF16], after=tok)
        dx, dbr, dmod[l][3], dmod[l][4], d_norm_ffn[l], dg = _norm_mod_bwd(x1, dh2, norm_ffn[l][None], sc_f, dx, mix, g_m)
        dmod[l][2] = dg
        gw_out = _matmul("mix_out_wgrad", "tn", a, dbr, [BF16])
        da = _matmul("mix_out_dgrad", "nt", dbr, w_out[l], [BF16])
        if l % 2 == 0:
            dq, dk, dv = _attn_bwd(z, ct, st, da, a, lse, nh)
            du, dvg, d_sgu_w[i], dbb = _sgu_bwd(z, sgu_w[i], b_col[i], da, ng, u_blk, aw // gw)
            d_sgu_b[i] = dbb[:, :, 0]
            dz = jnp.concatenate([dq, dk, dv, du, dvg], axis=1)
            gw_in = _matmul("mix_in_wgrad", "tn", dz, h, [BF16])
            dh = _matmul("mix_in_dgrad", "nn", dz, w_in[l], [BF16])
        else:
            dz, d_conv_w[i] = _conv_bwd(z, conv_w_full[i], da)
            gw_in = _matmul("conv_in_wgrad", "tn", dz, h, [BF16])
            dh = _matmul("conv_in_dgrad", "nn", dz, w_in[l], [BF16])
        mix_group = ("mix", l, ("in", "out") if l % 2 == 0 else ("cin", "cout"), i, [row_shards(gw_in), row_shards(gw_out)])
        w_norm = norm_mix[l][None]
        if l > 0:
            w_norm = w_norm + exchange_start(*mix_group)
            f_prev, g_prev = saved[l - 1][8], mods[l - 1, 5]
            dx, dbr, dmod[l][0], dmod[l][1], d_norm_mix[l], dg = _norm_mod_bwd(stream[l], dh, w_norm, sc_m, dx, f_prev, g_prev)
        else:
            dx, dmod[l][0], dmod[l][1], d_norm_mix[l] = _norm_mod_bwd(stream[l], dh, w_norm, sc_m, dx)
    grad_x = dx[None]

    dmod_mine = jnp.stack([jnp.concatenate([v.reshape(d) for v in dmod[l]]) for l in range(depth)])
    small = [dmod_mine, jnp.concatenate(d_norm_mix), jnp.concatenate(d_norm_ffn), jnp.stack(d_sgu_w), jnp.stack(d_sgu_b),
             d_final, jnp.stack(d_conv_w)]
    slab, offs, _ = _pack_rows(small, width)
    gathered, token = _all_gather_small("gather_small_grads", slab)
    p_dmod, p_nmix, p_nffn, p_sguw, p_sgub, p_final, p_convw = _unpack_rows(gathered.reshape(N_DEV, -1, width), offs, width)
    mix_group[4][0] = mix_group[4][0] + token[0, 0].astype(BF16)
    p_dmod = p_dmod + exchange_start(*mix_group)

    outs = {}

    def update(name, pieces, w, m, v):
        shape = w.shape
        cdim = shape[-1]
        res = _adamw("adamw_" + name, pieces.reshape(pieces.shape[0], -1, cdim), w.reshape(-1, cdim),
                     m.reshape(-1, cdim), v.reshape(-1, cdim))
        outs[name] = [r.reshape(shape) for r in res]

    update("ada_b", p_dmod.reshape(N_DEV, depth, 6 * d), ada_b, m_ada_b, v_ada_b)
    update("norm_mix", p_nmix.reshape(N_DEV, depth, d), norm_mix, m_norm_mix, v_norm_mix)
    update("norm_ffn", p_nffn.reshape(N_DEV, depth, d), norm_ffn, m_norm_ffn, v_norm_ffn)
    update("sgu_w", p_sguw, sgu_w, m_sgu_w, v_sgu_w)
    update("sgu_b", p_sgub.reshape(N_DEV, 1, -1), sgu_b.reshape(1, -1), m_sgu_b.reshape(1, -1), v_sgu_b.reshape(1, -1))
    outs["sgu_b"] = [r.reshape(sgu_b.shape) for r in outs["sgu_b"]]
    update("final_norm", p_final.reshape(N_DEV, 1, d), final_norm[None], m_final_norm[None], v_final_norm[None])
    outs["final_norm"] = [r.reshape(final_norm.shape) for r in outs["final_norm"]]
    cw_mine = lax.dynamic_slice_in_dim(p_convw.reshape(N_DEV, n_odd, cwid, d), me * d8, d8, axis=3)
    update("conv_w", cw_mine, conv_w, m_conv_w, v_conv_w)

    dmod_cols = lax.dynamic_slice_in_dim(p_dmod.reshape(N_DEV, depth, 6 * d), me * n8, n8, axis=2)
    g_ada = _ada_wgrad(c_all, jnp.transpose(dmod_cols, (1, 0, 2)))
    update("ada_w", g_ada[None], ada_w, m_ada_w, v_ada_w)

    names = {"in": "ab_w_in", "out": "ab_w_out", "cin": "conv_w_in", "cout": "conv_w_out",
             "gate": "ffn_w_gate", "up": "ffn_w_up", "down": "ffn_w_down"}
    moments = {"in": (m_ab_w_in, v_ab_w_in), "out": (m_ab_w_out, v_ab_w_out), "cin": (m_conv_w_in, v_conv_w_in),
               "cout": (m_conv_w_out, v_conv_w_out), "gate": (m_ffn_w_gate, v_ffn_w_gate), "up": (m_ffn_w_up, v_ffn_w_up),
               "down": (m_ffn_w_down, v_ffn_w_down)}

    def update_big(k):
        pieces = lax.dynamic_update_slice(lands[k], jnp.stack(own[k])[None], (me, 0, 0, 0))
        if k in col_sharded:
            outs[names[k]] = _adamw_transposed("adamw_" + names[k], pieces, big[k], *moments[k])
        else:
            update(names[k], pieces, big[k], *moments[k])

    exchange_finish("ffn", g_ada)
    for k in ("gate", "up", "down", "cin", "cout"):
        update_big(k)
    done = sum(outs[n][0][(0,) * outs[n][0].ndim] for n in outs)
    exchange_finish("mix", done.reshape(1, 1))
    for k in ("in", "out"):
        update_big(k)

    order = ["ada_w", "ada_b", "norm_mix", "norm_ffn", "ab_w_in", "sgu_w", "sgu_b", "ab_w_out", "conv_w_in", "conv_w",
             "conv_w_out", "ffn_w_gate", "ffn_w_up", "ffn_w_down", "final_norm"]
    return (loss, grad_x, *[outs[n][0] for n in order], *[outs[n][1] for n in order],
            *[outs[n][2] for n in order], *[outs[n][3] for n in order])
```
